```python
import jax
import jax.numpy as jnp
from jax import lax
import numpy as np

D_MODEL = 2048
BATCH = 8
SEQ = 2048
DEPTH = 2

GRID_W = 64
CTX_LEN = 256
HEAD_DIM = 128
ROPE_THETA = 10000.0
EPS = 1e-6
Q_BLOCK = 128
N_MOD = 9
D_FF = 5632

CONV_CHANNELS = D_MODEL // 2
GQA_HEADS = (D_MODEL // 2) // HEAD_DIM
GQA_KV_HEADS = 2
GQA_GROUP = GQA_HEADS // GQA_KV_HEADS
HYB_IN = 3 * CONV_CHANNELS + (GQA_HEADS + 2 * GQA_KV_HEADS) * HEAD_DIM
HYB_OUT = CONV_CHANNELS + GQA_HEADS * HEAD_DIM

MLA_HEADS = D_MODEL // HEAD_DIM
MLA_Q_RANK = 768
MLA_KV_RANK = 512
MLA_NOPE = 128
MLA_ROPE = 64
MLA_V = 128
MLA_DOWN = MLA_Q_RANK + MLA_KV_RANK + MLA_ROPE
MLA_SCALE = (MLA_NOPE + MLA_ROPE) ** -0.5

N_EVEN = (DEPTH + 1) // 2
N_ODD = DEPTH // 2

kernel_name = 'hybrid_conv_gqa_mla_macaron_prefix_dit'


def rms_norm(x, g):
    xf = x.astype(jnp.float32)
    y = xf * lax.rsqrt(jnp.mean(xf * xf, axis=-1, keepdims=True) + EPS)
    return (y * g.astype(jnp.float32)).astype(x.dtype)


def modulated_norm(h, g, shift, scale):
    return rms_norm(h, g) * (1 + scale) + shift


def swiglu(h, w_gate, w_up, w_down):
    return (jax.nn.silu(h @ w_gate) * (h @ w_up)) @ w_down


def axial_rope_tables(n_tok, dim, dtype):
    n_rows = n_tok // GRID_W
    row = jnp.repeat(jnp.arange(n_rows), GRID_W).astype(jnp.float32)
    col = jnp.tile(jnp.arange(GRID_W), n_rows).astype(jnp.float32)
    half = dim // 2
    inv = 1.0 / (ROPE_THETA ** (jnp.arange(0, half, 2, dtype=jnp.float32) / half))
    ang = jnp.concatenate([row[:, None] * inv, col[:, None] * inv], axis=-1)
    return jnp.cos(ang).astype(dtype), jnp.sin(ang).astype(dtype)


def apply_rope(x, cos, sin):
    xp = x.reshape(*x.shape[:-1], x.shape[-1] // 2, 2)
    x0, x1 = xp[..., 0], xp[..., 1]
    return jnp.stack([x0 * cos - x1 * sin, x0 * sin + x1 * cos], axis=-1).reshape(x.shape)


def short_conv3(u, w):
    up = jnp.pad(u, ((0, 0), (1, 1), (0, 0)))
    return up[:, :-2] * w[0] + up[:, 1:-1] * w[1] + up[:, 2:] * w[2]


def sweep_query_blocks(fn, *qs):
    b, t = qs[0].shape[:2]
    nb = t // Q_BLOCK
    blocks = tuple(jnp.moveaxis(q.reshape(b, nb, Q_BLOCK, *q.shape[2:]), 1, 0) for q in qs)
    out = lax.map(lambda qb: fn(*qb), blocks)
    out = jnp.moveaxis(out, 0, 1)
    return out.reshape(b, t, *out.shape[3:])


def gqa_attend(q, k, v):
    s = jnp.einsum('bqgrd,bkgd->bgrqk', q, k).astype(jnp.float32) * (HEAD_DIM ** -0.5)
    p = jax.nn.softmax(s, axis=-1).astype(v.dtype)
    return jnp.einsum('bgrqk,bkgd->bqgrd', p, v)


def mla_attend(qn, qr, kn, kr, v):
    s = (jnp.einsum('bqhd,bkhd->bhqk', qn, kn)
         + jnp.einsum('bqhr,bkr->bhqk', qr, kr)).astype(jnp.float32) * MLA_SCALE
    p = jax.nn.softmax(s, axis=-1).astype(v.dtype)
    return jnp.einsum('bhqk,bkhd->bqhd', p, v)


def conv_attn_mixer(hc, hl, w_in, conv_w, q_norm, k_norm, w_out, cos, sin, need_ctx):
    splits = [CONV_CHANNELS, 2 * CONV_CHANNELS, 3 * CONV_CHANNELS,
              3 * CONV_CHANNELS + GQA_HEADS * HEAD_DIM,
              3 * CONV_CHANNELS + (GQA_HEADS + GQA_KV_HEADS) * HEAD_DIM]

    def project(h):
        b, t = h.shape[:2]
        gate_b, gate_c, u, q, k, v = jnp.split(h @ w_in, splits, axis=-1)
        conv_out = gate_b * short_conv3(gate_c * u, conv_w)
        q = rms_norm(q.reshape(b, t, GQA_KV_HEADS, GQA_GROUP, HEAD_DIM), q_norm)
        k = rms_norm(k.reshape(b, t, GQA_KV_HEADS, HEAD_DIM), k_norm)
        v = v.reshape(b, t, GQA_KV_HEADS, HEAD_DIM)
        return conv_out, q, k, v

    b, t = hl.shape[:2]
    conv_l, ql, kl, vl = project(hl)
    ql = apply_rope(ql, cos[:, None, None], sin[:, None, None])
    kl = apply_rope(kl, cos[:, None], sin[:, None])
    conv_c, qc, kc, vc = project(hc)
    k_all = jnp.concatenate([kc, kl], axis=1)
    v_all = jnp.concatenate([vc, vl], axis=1)
    att_l = sweep_query_blocks(lambda qb: gqa_attend(qb, k_all, v_all), ql)
    out_l = jnp.concatenate([conv_l, att_l.reshape(b, t, -1)], axis=-1) @ w_out
    out_c = None
    if need_ctx:
        att_c = gqa_attend(qc, kc, vc)
        out_c = jnp.concatenate([conv_c, att_c.reshape(hc.shape[0], hc.shape[1], -1)], axis=-1) @ w_out
    return out_c, out_l


def mla_mixer(hc, hl, w_down, q_norm, kv_norm, w_uq, w_ukv, w_o, cos, sin, need_ctx):
    def project(h):
        b, t = h.shape[:2]
        cq, ckv, kr = jnp.split(h @ w_down, [MLA_Q_RANK, MLA_Q_RANK + MLA_KV_RANK], axis=-1)
        q = (rms_norm(cq, q_norm) @ w_uq).reshape(b, t, MLA_HEADS, MLA_NOPE + MLA_ROPE)
        kv = (rms_norm(ckv, kv_norm) @ w_ukv).reshape(b, t, MLA_HEADS, MLA_NOPE + MLA_V)
        qn, qr = jnp.split(q, [MLA_NOPE], axis=-1)
        kn, v = jnp.split(kv, [MLA_NOPE], axis=-1)
        return qn, qr, kn, kr, v

    b, t = hl.shape[:2]
    qnl, qrl, knl, krl, vl = project(hl)
    qrl = apply_rope(qrl, cos[:, None], sin[:, None])
    krl = apply_rope(krl, cos, sin)
    qnc, qrc, knc, krc, vc = project(hc)
    kn_all = jnp.concatenate([knc, knl], axis=1)
    kr_all = jnp.concatenate([krc, krl], axis=1)
    v_all = jnp.concatenate([vc, vl], axis=1)
    att_l = sweep_query_blocks(lambda qn_b, qr_b: mla_attend(qn_b, qr_b, kn_all, kr_all, v_all), qnl, qrl)
    out_l = att_l.reshape(b, t, -1) @ w_o
    out_c = None
    if need_ctx:
        out_c = mla_attend(qnc, qrc, knc, krc, vc).reshape(hc.shape[0], hc.shape[1], -1) @ w_o
    return out_c, out_l


def setup_inputs(seed: int = 0) -> dict:
    key = jax.random.key(seed)
    ks = iter(jax.random.split(key, 32))
    f32 = jnp.float32

    def nrm(shape, fan_in, g=1.0):
        return g * fan_in ** -0.5 * jax.random.normal(next(ks), shape, f32)

    def gain(shape):
        return 1.0 + 0.05 * jax.random.normal(next(ks), shape, f32)

    return {
        'x': jax.random.normal(next(ks), (BATCH, SEQ, D_MODEL), f32),
        'c': jax.random.normal(next(ks), (BATCH, D_MODEL), f32),
        'ctx': jax.random.normal(next(ks), (BATCH, CTX_LEN, D_MODEL), f32),
        'c_ctx': jax.random.normal(next(ks), (D_MODEL,), f32),
        'mod_w': nrm((DEPTH, D_MODEL, N_MOD * D_MODEL), D_MODEL, 0.5),
        'mod_b': 0.02 * jax.random.normal(next(ks), (DEPTH, N_MOD * D_MODEL), f32),
        'norm_ffn1': gain((DEPTH, D_MODEL)),
        'norm_mix': gain((DEPTH, D_MODEL)),
        'norm_ffn2': gain((DEPTH, D_MODEL)),
        'ffn1_w_gate': nrm((DEPTH, D_MODEL, D_FF), D_MODEL),
        'ffn1_w_up': nrm((DEPTH, D_MODEL, D_FF), D_MODEL),
        'ffn1_w_down': nrm((DEPTH, D_FF, D_MODEL), D_FF),
        'ffn2_w_gate': nrm((DEPTH, D_MODEL, D_FF), D_MODEL),
        'ffn2_w_up': nrm((DEPTH, D_MODEL, D_FF), D_MODEL),
        'ffn2_w_down': nrm((DEPTH, D_FF, D_MODEL), D_FF),
        'hyb_w_in': nrm((N_EVEN, D_MODEL, HYB_IN), D_MODEL),
        'hyb_conv_w': nrm((N_EVEN, 3, CONV_CHANNELS), 3),
        'hyb_q_norm': gain((N_EVEN, HEAD_DIM)),
        'hyb_k_norm': gain((N_EVEN, HEAD_DIM)),
        'hyb_w_out': nrm((N_EVEN, HYB_OUT, D_MODEL), HYB_OUT),
        'mla_w_down': nrm((N_ODD, D_MODEL, MLA_DOWN), D_MODEL),
        'mla_q_norm': gain((N_ODD, MLA_Q_RANK)),
        'mla_kv_norm': gain((N_ODD, MLA_KV_RANK)),
        'mla_w_uq': nrm((N_ODD, MLA_Q_RANK, MLA_HEADS * (MLA_NOPE + MLA_ROPE)), MLA_Q_RANK),
        'mla_w_ukv': nrm((N_ODD, MLA_KV_RANK, MLA_HEADS * (MLA_NOPE + MLA_V)), MLA_KV_RANK),
        'mla_w_o': nrm((N_ODD, MLA_HEADS * MLA_V, D_MODEL), MLA_HEADS * MLA_V),
        'final_norm': gain((D_MODEL,)),
    }


def reference(x, c, ctx, c_ctx, mod_w, mod_b, norm_ffn1, norm_mix, norm_ffn2,
              ffn1_w_gate, ffn1_w_up, ffn1_w_down, ffn2_w_gate, ffn2_w_up, ffn2_w_down,
              hyb_w_in, hyb_conv_w, hyb_q_norm, hyb_k_norm, hyb_w_out,
              mla_w_down, mla_q_norm, mla_kv_norm, mla_w_uq, mla_w_ukv, mla_w_o, final_norm):
    n_lat = x.shape[1]
    cos_a, sin_a = axial_rope_tables(n_lat, HEAD_DIM, x.dtype)
    cos_m, sin_m = axial_rope_tables(n_lat, MLA_ROPE, x.dtype)
    silu_c = jax.nn.silu(c)
    silu_cc = jax.nn.silu(c_ctx)
    xl, xc = x, ctx
    for layer in range(DEPTH):
        need_ctx = layer < DEPTH - 1
        m_l = jnp.split((silu_c @ mod_w[layer] + mod_b[layer])[:, None, :], N_MOD, axis=-1)
        m_c = jnp.split(silu_cc @ mod_w[layer] + mod_b[layer], N_MOD, axis=-1)
        f1 = (ffn1_w_gate[layer], ffn1_w_up[layer], ffn1_w_down[layer])
        f2 = (ffn2_w_gate[layer], ffn2_w_up[layer], ffn2_w_down[layer])

        xl = xl + 0.5 * m_l[2] * swiglu(modulated_norm(xl, norm_ffn1[layer], m_l[0], m_l[1]), *f1)
        xc = xc + 0.5 * m_c[2] * swiglu(modulated_norm(xc, norm_ffn1[layer], m_c[0], m_c[1]), *f1)

        hl = modulated_norm(xl, norm_mix[layer], m_l[3], m_l[4])
        hc = modulated_norm(xc, norm_mix[layer], m_c[3], m_c[4])
        i = layer // 2
        if layer % 2 == 0:
            mc, ml = conv_attn_mixer(hc, hl, hyb_w_in[i], hyb_conv_w[i], hyb_q_norm[i], hyb_k_norm[i],
                                     hyb_w_out[i], cos_a, sin_a, need_ctx)
        else:
            mc, ml = mla_mixer(hc, hl, mla_w_down[i], mla_q_norm[i], mla_kv_norm[i], mla_w_uq[i],
                               mla_w_ukv[i], mla_w_o[i], cos_m, sin_m, need_ctx)
        xl = xl + m_l[5] * ml

        xl = xl + 0.5 * m_l[8] * swiglu(modulated_norm(xl, norm_ffn2[layer], m_l[6], m_l[7]), *f2)
        if need_ctx:
            xc = xc + m_c[5] * mc
            xc = xc + 0.5 * m_c[8] * swiglu(modulated_norm(xc, norm_ffn2[layer], m_c[6], m_c[7]), *f2)
    return rms_norm(xl, final_norm)
```

```python
import functools

import numpy as np
import jax
import jax.numpy as jnp
from jax import lax
from jax.experimental import pallas as pl
from jax.experimental.pallas import tpu as pltpu

HEAD_DIM = 128
GRID_W = 64
ROPE_THETA = 10000.0
EPS = 1e-6
N_MOD = 9
GQA_KV_HEADS = 2
MLA_NOPE = 128
MLA_ROPE = 64
MLA_V = 128
LANES = 128
VMEM_LIMIT = 56 * 1024 * 1024
MOD_ROWS = 16

F32 = jnp.float32
BF16 = jnp.bfloat16


def _cparams(*sem):
    return pltpu.CompilerParams(dimension_semantics=sem, vmem_limit_bytes=VMEM_LIMIT)


def _resident(shape):
    nd = len(shape)
    return pl.BlockSpec(shape, lambda *_: (0,) * nd, pipeline_mode=pl.Buffered(1))


def _rms(x, g):
    return x * lax.rsqrt(jnp.mean(x * x, axis=-1, keepdims=True) + EPS) * g


def _modnorm(x, g, shift, scale):
    return _rms(x, g) * (1.0 + scale) + shift


def _rope(x, cos, sin):
    return x * cos + pltpu.roll(x, LANES // 2, axis=1) * sin


def _dot(a, b):
    return jnp.dot(a, b, preferred_element_type=F32)


def _dot_t(a, b):
    return lax.dot_general(a, b, (((1,), (1,)), ((), ())), preferred_element_type=F32)


def _mod_kernel(c_ref, w_ref, b_ref, o_ref):
    c = c_ref[...]
    s = (c * jax.nn.sigmoid(c)).astype(BF16)
    o_ref[0] = _dot(s, w_ref[0].astype(BF16)) + b_ref[0]


def _mod_call(cvec, mod_w, mod_b):
    depth, d, n = mod_w.shape
    tn = min(1024, n)
    return pl.pallas_call(
        _mod_kernel,
        grid=(depth, n // tn),
        in_specs=[
            pl.BlockSpec((MOD_ROWS, d), lambda l, j: (0, 0)),
            pl.BlockSpec((1, d, tn), lambda l, j: (l, 0, j)),
            pl.BlockSpec((1, 1, tn), lambda l, j: (l, 0, j)),
        ],
        out_specs=pl.BlockSpec((1, MOD_ROWS, tn), lambda l, j: (l, 0, j)),
        out_shape=jax.ShapeDtypeStruct((depth, MOD_ROWS, n), F32),
        compiler_params=_cparams("parallel", "parallel"),
        name="adaln_mod",
    )(cvec, mod_w, mod_b.reshape(depth, 1, n))


def _ffn_kernel(x_ref, mod_ref, g_ref, wg_ref, wu_ref, wd_ref, *rest, m0, final):
    if final:
        fn_ref, o_ref, xn_ref, acc_ref = rest
    else:
        o_ref, xn_ref, acc_ref = rest
    j = pl.program_id(2)

    @pl.when(j == 0)
    def _():
        xn_ref[...] = _modnorm(x_ref[0], g_ref[...], mod_ref[0, m0:m0 + 1, :],
                               mod_ref[0, m0 + 1:m0 + 2, :]).astype(BF16)
        acc_ref[...] = jnp.zeros_like(acc_ref)

    xn = xn_ref[...]
    hg = _dot(xn, wg_ref[...])
    hu = _dot(xn, wu_ref[...])
    a = (hg * jax.nn.sigmoid(hg) * hu).astype(BF16)
    acc_ref[...] += _dot(a, wd_ref[...])

    @pl.when(j == pl.num_programs(2) - 1)
    def _():
        y = x_ref[0] + 0.5 * mod_ref[0, m0 + 2:m0 + 3, :] * acc_ref[...]
        if final:
            y = _rms(y, fn_ref[...])
        o_ref[0] = y


def _ffn_call(x, nb, mod, g, wg, wu, wd, m0, final_g=None):
    _, t, d = x.shape
    dff = wg.shape[1]
    tm = min(512, t)
    tf = min(512, dff)
    final = final_g is not None
    in_specs = [
        pl.BlockSpec((1, tm, d), lambda b, i, j: (b, i, 0)),
        pl.BlockSpec((1, N_MOD, d), lambda b, i, j: (b, 0, 0)),
        pl.BlockSpec((1, d), lambda b, i, j: (0, 0)),
        pl.BlockSpec((d, tf), lambda b, i, j: (0, j)),
        pl.BlockSpec((d, tf), lambda b, i, j: (0, j)),
        pl.BlockSpec((tf, d), lambda b, i, j: (j, 0)),
    ]
    args = [x, mod, g.reshape(1, d), wg, wu, wd]
    if final:
        in_specs.append(pl.BlockSpec((1, d), lambda b, i, j: (0, 0)))
        args.append(final_g.reshape(1, d))
    return pl.pallas_call(
        functools.partial(_ffn_kernel, m0=m0, final=final),
        grid=(nb, t // tm, dff // tf),
        in_specs=in_specs,
        out_specs=pl.BlockSpec((1, tm, d), lambda b, i, j: (b, i, 0)),
        out_shape=jax.ShapeDtypeStruct((nb, t, d), F32),
        scratch_shapes=[pltpu.VMEM((tm, d), BF16), pltpu.VMEM((tm, d), F32)],
        compiler_params=_cparams("parallel", "parallel", "arbitrary"),
        name="swiglu_half_step",
    )(*args)


def _hyb_in_kernel(x_ref, mod_ref, g_ref, wgb_ref, wgc_ref, wu_ref, wq_ref, wk_ref, wv_ref,
                   qg_ref, kg_ref, cos_ref, sin_ref,
                   gb_ref, z_ref, q_ref, k_ref, v_ref):
    xn = _modnorm(x_ref[0], g_ref[...], mod_ref[0, 3:4, :], mod_ref[0, 4:5, :]).astype(BF16)
    gb_ref[0] = _dot(xn, wgb_ref[...]).astype(BF16)
    z_ref[0] = (_dot(xn, wgc_ref[...]) * _dot(xn, wu_ref[...])).astype(BF16)
    v_ref[0] = _dot(xn, wv_ref[...]).astype(BF16)
    cos = cos_ref[0]
    sin = sin_ref[0]
    q = _dot(xn, wq_ref[...])
    q_scale = HEAD_DIM ** -0.5
    for h in range(q.shape[1] // HEAD_DIM):
        sl = slice(h * HEAD_DIM, (h + 1) * HEAD_DIM)
        q_ref[0, :, sl] = (_rope(_rms(q[:, sl], qg_ref[...]), cos, sin) * q_scale).astype(BF16)
    k = _dot(xn, wk_ref[...])
    for h in range(k.shape[1] // HEAD_DIM):
        sl = slice(h * HEAD_DIM, (h + 1) * HEAD_DIM)
        k_ref[0, :, sl] = _rope(_rms(k[:, sl], kg_ref[...]), cos, sin).astype(BF16)


def _hyb_in_call(x, mod, g, wgb, wgc, wu, wq, wk, wv, qg, kg, cos2, sin2):
    nb, t, d = x.shape
    tm = min(512, t)
    cc, nq, nkv = wgb.shape[1], wq.shape[1], wk.shape[1]
    row = lambda n: pl.BlockSpec((1, tm, n), lambda b, i: (b, i, 0))
    table = pl.BlockSpec((1, tm, LANES), lambda b, i: (jnp.where(b == nb - 1, 1, 0), i, 0))
    return pl.pallas_call(
        _hyb_in_kernel,
        grid=(nb, t // tm),
        in_specs=[
            row(d),
            pl.BlockSpec((1, N_MOD, d), lambda b, i: (b, 0, 0)),
            _resident((1, d)),
            _resident(wgb.shape), _resident(wgc.shape), _resident(wu.shape),
            _resident(wq.shape), _resident(wk.shape), _resident(wv.shape),
            _resident((1, HEAD_DIM)), _resident((1, HEAD_DIM)),
            table, table,
        ],
        out_specs=[row(cc), row(cc), row(nq), row(nkv), row(nkv)],
        out_shape=[jax.ShapeDtypeStruct((nb, t, n), BF16) for n in (cc, cc, nq, nkv, nkv)],
        compiler_params=_cparams("parallel", "parallel"),
        name="hyb_in_proj",
    )(x, mod, g.reshape(1, d), wgb, wgc, wu, wq, wk, wv,
      qg.reshape(1, HEAD_DIM), kg.reshape(1, HEAD_DIM), cos2, sin2)


def _softmax_pv(s_list, v_list):
    m = s_list[0].max(axis=-1, keepdims=True)
    for s in s_list[1:]:
        m = jnp.maximum(m, s.max(axis=-1, keepdims=True))
    l = 0.0
    o = 0.0
    for s, v in zip(s_list, v_list):
        p = jnp.exp(s - m)
        l = l + p.sum(axis=-1, keepdims=True)
        o = o + _dot(p.astype(BF16), v)
    return o / l


def _gqa_kernel(*refs, has_lat):
    if has_lat:
        q_ref, kc_ref, vc_ref, kl_ref, vl_ref, o_ref = refs
    else:
        q_ref, kc_ref, vc_ref, o_ref = refs
    for h in range(q_ref.shape[2] // HEAD_DIM):
        sl = slice(h * HEAD_DIM, (h + 1) * HEAD_DIM)
        qh = q_ref[0, :, sl]
        s_list = [_dot_t(qh, kc_ref[0])]
        v_list = [vc_ref[0]]
        if has_lat:
            s_list.append(_dot_t(qh, kl_ref[0]))
            v_list.append(vl_ref[0])
        o_ref[0, :, sl] = _softmax_pv(s_list, v_list).astype(BF16)


def _gqa_latent_call(q, k, v, b_lat, tc):
    nb, t, nq = q.shape
    gw = nq // GQA_KV_HEADS
    tq = min(512, t)
    ctx = pl.BlockSpec((1, tc, HEAD_DIM), lambda b, g, i: (b_lat, b, g))
    lat = pl.BlockSpec((1, t, HEAD_DIM), lambda b, g, i: (b, 0, g))
    qo = pl.BlockSpec((1, tq, gw), lambda b, g, i: (b, i, g))
    return pl.pallas_call(
        functools.partial(_gqa_kernel, has_lat=True),
        grid=(b_lat, GQA_KV_HEADS, t // tq),
        in_specs=[qo, ctx, ctx, lat, lat],
        out_specs=qo,
        out_shape=jax.ShapeDtypeStruct((nb, t, nq), BF16),
        compiler_params=_cparams("parallel", "parallel", "parallel"),
        name="gqa_latent",
    )(q, k, v, k, v)


def _gqa_ctx_kernel(q_ref, kc_ref, vc_ref, att_hbm_ref, o_ref):
    del att_hbm_ref
    _gqa_kernel(q_ref, kc_ref, vc_ref, o_ref, has_lat=False)


def _gqa_ctx_call(q, k, v, att, b_lat, tc):
    nb, t, nq = q.shape
    gw = nq // GQA_KV_HEADS
    ctx = pl.BlockSpec((1, tc, HEAD_DIM), lambda b, g: (b_lat, b, g))
    qo = pl.BlockSpec((1, tc, gw), lambda b, g: (b_lat, b, g))
    return pl.pallas_call(
        _gqa_ctx_kernel,
        grid=(t // tc, GQA_KV_HEADS),
        in_specs=[qo, ctx, ctx, pl.BlockSpec(memory_space=pl.ANY)],
        out_specs=qo,
        out_shape=jax.ShapeDtypeStruct(att.shape, att.dtype),
        input_output_aliases={3: 0},
        compiler_params=_cparams("parallel", "parallel"),
        name="gqa_context",
    )(q, k, v, att)


def _hyb_out_kernel(x_ref, mod_ref, gb_ref, z_ref, zp_ref, zn_ref, att_ref, cw_ref, w_ref,
                    o_ref, lhs_ref, *, seq_lat, seq_ctx, b_lat):
    b, i, n = pl.program_id(0), pl.program_id(1), pl.program_id(2)
    tm, cc = z_ref.shape[1], z_ref.shape[2]

    @pl.when(n == 0)
    def _():
        z = z_ref[0].astype(F32)
        rows = lax.broadcasted_iota(jnp.int32, (tm, cc), 0)
        seq = jnp.where(b == b_lat, seq_ctx, seq_lat)
        pos = (i * tm + rows) & (seq - 1)
        z_prev = jnp.where(rows == 0, zp_ref[0, 7:8, :].astype(F32), pltpu.roll(z, 1, axis=0))
        z_next = jnp.where(rows == tm - 1, zn_ref[0, 0:1, :].astype(F32),
                           pltpu.roll(z, tm - 1, axis=0))
        z_prev = jnp.where(pos == 0, 0.0, z_prev)
        z_next = jnp.where(pos == seq - 1, 0.0, z_next)
        conv = z_prev * cw_ref[0:1, :] + z * cw_ref[1:2, :] + z_next * cw_ref[2:3, :]
        lhs_ref[:, :cc] = (gb_ref[0].astype(F32) * conv).astype(BF16)
        lhs_ref[:, cc:] = att_ref[0]

    o_ref[0] = x_ref[0] + mod_ref[0, 5:6, :] * _dot(lhs_ref[...], w_ref[...])


def _hyb_out_call(x, mod, gb, z, att, conv_w, w_out, b_lat, tc):
    nb, t, d = x.shape
    cc = gb.shape[2]
    tm = min(512, t)
    tn = min(512, d)
    hb = tm // 8
    row = lambda n: pl.BlockSpec((1, tm, n), lambda b, i, j: (b, i, 0))
    return pl.pallas_call(
        functools.partial(_hyb_out_kernel, seq_lat=t, seq_ctx=tc, b_lat=b_lat),
        grid=(nb, t // tm, d // tn),
        in_specs=[
            pl.BlockSpec((1, tm, tn), lambda b, i, j: (b, i, j)),
            pl.BlockSpec((1, N_MOD, tn), lambda b, i, j: (b, 0, j)),
            row(cc), row(cc),
            pl.BlockSpec((1, 8, cc), lambda b, i, j: (b, jnp.maximum(i * hb - 1, 0), 0)),
            pl.BlockSpec((1, 8, cc), lambda b, i, j: (b, jnp.minimum((i + 1) * hb, t // 8 - 1), 0)),
            row(att.shape[2]),
            pl.BlockSpec(conv_w.shape, lambda b, i, j: (0, 0)),
            pl.BlockSpec((w_out.shape[0], tn), lambda b, i, j: (0, j)),
        ],
        out_specs=pl.BlockSpec((1, tm, tn), lambda b, i, j: (b, i, j)),
        out_shape=jax.ShapeDtypeStruct((nb, t, d), F32),
        scratch_shapes=[pltpu.VMEM((tm, w_out.shape[0]), BF16)],
        compiler_params=_cparams("parallel", "parallel", "arbitrary"),
        name="hyb_out_proj",
    )(x, mod, gb, z, z, z, att, conv_w, w_out)


def _mla_in_kernel(x_ref, mod_ref, g_ref, wd_ref, qg_ref, kvg_ref, wuq_ref, wukv_ref,
                   cos_ref, sin_ref, qn_ref, qr_ref, kn_ref, v_ref, kr_ref, *, q_rank, kv_rank):
    xn = _modnorm(x_ref[0], g_ref[...], mod_ref[0, 3:4, :], mod_ref[0, 4:5, :]).astype(BF16)
    d = _dot(xn, wd_ref[...])
    cos = cos_ref[0]
    sin = sin_ref[0]
    n_nope = qn_ref.shape[2]
    q_scale = (MLA_NOPE + MLA_ROPE) ** -0.5
    q = _dot(_rms(d[:, :q_rank], qg_ref[...]).astype(BF16), wuq_ref[...])
    qn_ref[0] = (q[:, :n_nope] * q_scale).astype(BF16)
    for j in range(qr_ref.shape[2] // LANES):
        qr_ref[0, :, j * LANES:(j + 1) * LANES] = (
            _rope(q[:, n_nope + j * LANES:n_nope + (j + 1) * LANES], cos, sin) * q_scale).astype(BF16)
    kv = _dot(_rms(d[:, q_rank:q_rank + kv_rank], kvg_ref[...]).astype(BF16), wukv_ref[...])
    kn_ref[0] = kv[:, :n_nope].astype(BF16)
    v_ref[0] = kv[:, n_nope:].astype(BF16)
    kr = _rope(d[:, q_rank + kv_rank:], cos, sin)
    first = (lax.broadcasted_iota(jnp.int32, kr.shape, 1) & (MLA_ROPE // 2)) == 0
    kr_ref[0, :, :LANES] = jnp.where(first, kr, 0.0).astype(BF16)
    kr_ref[0, :, LANES:] = jnp.where(first, 0.0, kr).astype(BF16)


def _mla_in_call(x, mod, g, wd, qg, kvg, wuq, wukv, cos2, sin2):
    nb, t, d = x.shape
    tm = min(256, t)
    q_rank, kv_rank = qg.shape[0], kvg.shape[0]
    n_nope = wukv.shape[1] // 2
    n_rope = wuq.shape[1] - n_nope
    row = lambda n: pl.BlockSpec((1, tm, n), lambda b, i: (b, i, 0))
    table = pl.BlockSpec((1, tm, LANES), lambda b, i: (jnp.where(b == nb - 1, 1, 0), i, 0))
    widths = (n_nope, n_rope, n_nope, n_nope, 2 * LANES)
    return pl.pallas_call(
        functools.partial(_mla_in_kernel, q_rank=q_rank, kv_rank=kv_rank),
        grid=(nb, t // tm),
        in_specs=[
            row(d),
            pl.BlockSpec((1, N_MOD, d), lambda b, i: (b, 0, 0)),
            _resident((1, d)),
            _resident(wd.shape), _resident((1, q_rank)), _resident((1, kv_rank)),
            _resident(wuq.shape), _resident(wukv.shape),
            table, table,
        ],
        out_specs=[row(n) for n in widths],
        out_shape=[jax.ShapeDtypeStruct((nb, t, n), BF16) for n in widths],
        compiler_params=_cparams("parallel", "parallel"),
        name="mla_in_proj",
    )(x, mod, g.reshape(1, d), wd, qg.reshape(1, q_rank), kvg.reshape(1, kv_rank), wuq, wukv,
      cos2, sin2)


def _mla_attn_kernel(qn_ref, qr_ref, knc_ref, krc_ref, vc_ref, knl_ref, krl_ref, vl_ref,
                     o_ref, k_ref, v_ref):
    tc = knc_ref.shape[1]

    @pl.when(pl.program_id(2) == 0)
    def _():
        k_ref[:tc, :LANES] = knc_ref[0]
        k_ref[:tc, LANES:] = krc_ref[0]
        k_ref[tc:, :LANES] = knl_ref[0]
        k_ref[tc:, LANES:] = krl_ref[0]
        v_ref[:tc, :] = vc_ref[0]
        v_ref[tc:, :] = vl_ref[0]

    q = jnp.concatenate([qn_ref[0], qr_ref[0]], axis=1)
    o_ref[0] = _softmax_pv([_dot_t(q, k_ref[...])], [v_ref[...]]).astype(BF16)


def _mla_attn_call(qn, qr, kn, v, kr, b_lat, tc):
    nb, t, n_nope = qn.shape
    heads = n_nope // MLA_NOPE
    tq = min(512, t)
    qspec = pl.BlockSpec((1, tq, LANES), lambda b, h, i: (b, i, h))
    qrspec = pl.BlockSpec((1, tq, LANES), lambda b, h, i: (b, i, h // 2))
    ctx = lambda f: pl.BlockSpec((1, tc, LANES), lambda b, h, i: (b_lat, b, f(h)))
    lat = lambda f: pl.BlockSpec((1, t, LANES), lambda b, h, i: (b, 0, f(h)))
    same = lambda h: h
    parity = lambda h: h % 2
    return pl.pallas_call(
        _mla_attn_kernel,
        grid=(b_lat, heads, t // tq),
        in_specs=[qspec, qrspec, ctx(same), ctx(parity), ctx(same), lat(same), lat(parity), lat(same)],
        out_specs=qspec,
        out_shape=jax.ShapeDtypeStruct((b_lat, t, n_nope), BF16),
        scratch_shapes=[pltpu.VMEM((tc + t, 2 * LANES), BF16), pltpu.VMEM((tc + t, LANES), BF16)],
        compiler_params=_cparams("parallel", "parallel", "arbitrary"),
        name="mla_attention",
    )(qn, qr, kn, kr, v, kn, kr, v)


def _mla_out_kernel(x_ref, mod_ref, att_ref, w_ref, o_ref):
    o_ref[0] = x_ref[0] + mod_ref[0, 5:6, :] * _dot(att_ref[0], w_ref[...])


def _mla_out_call(x, mod, att, w_o):
    nb, t, k = att.shape
    d = w_o.shape[1]
    tm = min(512, t)
    tn = min(512, d)
    return pl.pallas_call(
        _mla_out_kernel,
        grid=(nb, t // tm, d // tn),
        in_specs=[
            pl.BlockSpec((1, tm, tn), lambda b, i, j: (b, i, j)),
            pl.BlockSpec((1, N_MOD, tn), lambda b, i, j: (b, 0, j)),
            pl.BlockSpec((1, tm, k), lambda b, i, j: (b, i, 0)),
            pl.BlockSpec((k, tn), lambda b, i, j: (0, j)),
        ],
        out_specs=pl.BlockSpec((1, tm, tn), lambda b, i, j: (b, i, j)),
        out_shape=jax.ShapeDtypeStruct((nb, t, d), F32),
        compiler_params=_cparams("parallel", "parallel", "parallel"),
        name="mla_out_proj",
    )(x, mod, att, w_o)


def _rope_angles(n_tok, dim):
    n_rows = n_tok // GRID_W
    row = jnp.repeat(jnp.arange(n_rows), GRID_W).astype(F32)
    col = jnp.tile(jnp.arange(GRID_W), n_rows).astype(F32)
    half = dim // 2
    inv = 1.0 / (ROPE_THETA ** (jnp.arange(0, half, 2, dtype=F32) / half))
    return jnp.concatenate([row[:, None] * inv, col[:, None] * inv], axis=-1)


def _rope_tables(n_tok, dim):
    ang = _rope_angles(n_tok, dim)
    reps = LANES // dim
    cos = jnp.concatenate([jnp.cos(ang)] * (2 * reps), axis=-1)
    sin = jnp.concatenate([-jnp.sin(ang)] * reps + [jnp.sin(ang)] * reps, axis=-1)
    return (jnp.stack([cos, jnp.ones_like(cos)]), jnp.stack([sin, jnp.zeros_like(sin)]))


def _deinterleave(n):
    return np.concatenate([np.arange(0, n, 2), np.arange(1, n, 2)])


def kernel(x, c, ctx, c_ctx, mod_w, mod_b, norm_ffn1, norm_mix, norm_ffn2,
           ffn1_w_gate, ffn1_w_up, ffn1_w_down, ffn2_w_gate, ffn2_w_up, ffn2_w_down,
           hyb_w_in, hyb_conv_w, hyb_q_norm, hyb_k_norm, hyb_w_out,
           mla_w_down, mla_q_norm, mla_kv_norm, mla_w_uq, mla_w_ukv, mla_w_o, final_norm):
    b_lat, t, d = x.shape
    tc = ctx.shape[1]
    depth = mod_w.shape[0]
    assert depth == 2 and ctx.shape[0] * tc == t and b_lat + 1 <= MOD_ROWS
    assert t & (t - 1) == 0 and tc & (tc - 1) == 0 and t % GRID_W == 0
    nb = b_lat + 1

    xs = jnp.concatenate([x, ctx.reshape(1, t, d)], axis=0)
    cvec = jnp.concatenate([c, c_ctx[None], jnp.zeros((MOD_ROWS - nb, d), F32)], axis=0)
    mods = _mod_call(cvec, mod_w, mod_b).reshape(depth, MOD_ROWS, N_MOD, d)
    cast = lambda w: w.astype(BF16)

    mod = mods[0]
    xs = _ffn_call(xs, nb, mod, norm_ffn1[0], cast(ffn1_w_gate[0]), cast(ffn1_w_up[0]),
                   cast(ffn1_w_down[0]), 0)
    cc = d // 2
    n_q = cc
    n_kv = GQA_KV_HEADS * HEAD_DIM
    w_in = hyb_w_in[0]
    perm = _deinterleave(HEAD_DIM)
    head_perm = lambda n: (np.arange(n).reshape(-1, HEAD_DIM)[:, perm]).reshape(-1)
    wq = w_in[:, 3 * cc:3 * cc + n_q][:, head_perm(n_q)]
    wk = w_in[:, 3 * cc + n_q:3 * cc + n_q + n_kv][:, head_perm(n_kv)]
    wv = w_in[:, 3 * cc + n_q + n_kv:]
    cos_a, sin_a = _rope_tables(t, HEAD_DIM)
    gb, z, q, k, v = _hyb_in_call(
        xs, mod, norm_mix[0], cast(w_in[:, :cc]), cast(w_in[:, cc:2 * cc]), cast(w_in[:, 2 * cc:3 * cc]),
        cast(wq), cast(wk), cast(wv), hyb_q_norm[0][perm], hyb_k_norm[0][perm], cos_a, sin_a)
    att = _gqa_latent_call(q, k, v, b_lat, tc)
    att = _gqa_ctx_call(q, k, v, att, b_lat, tc)
    xs = _hyb_out_call(xs, mod, gb, z, att, hyb_conv_w[0], cast(hyb_w_out[0]), b_lat, tc)
    xs = _ffn_call(xs, nb, mod, norm_ffn2[0], cast(ffn2_w_gate[0]), cast(ffn2_w_up[0]),
                   cast(ffn2_w_down[0]), 6)

    mod = mods[1]
    xs = _ffn_call(xs, nb, mod, norm_ffn1[1], cast(ffn1_w_gate[1]), cast(ffn1_w_up[1]),
                   cast(ffn1_w_down[1]), 0)
    heads = d // HEAD_DIM
    q_rank, kv_rank = mla_q_norm.shape[1], mla_kv_norm.shape[1]
    qk = MLA_NOPE + MLA_ROPE
    ev, od = np.arange(0, MLA_ROPE, 2), np.arange(1, MLA_ROPE, 2)
    nope_cols = (np.arange(heads)[:, None] * qk + np.arange(MLA_NOPE)[None]).reshape(-1)
    rope_cols = []
    for j in range(heads // 2):
        a0, b0 = 2 * j * qk + MLA_NOPE, (2 * j + 1) * qk + MLA_NOPE
        rope_cols += [a0 + ev, b0 + ev, a0 + od, b0 + od]
    wuq = mla_w_uq[0][:, np.concatenate([nope_cols] + rope_cols)]
    kvw = MLA_NOPE + MLA_V
    kn_cols = (np.arange(heads)[:, None] * kvw + np.arange(MLA_NOPE)[None]).reshape(-1)
    wukv = mla_w_ukv[0][:, np.concatenate([kn_cols, kn_cols + MLA_NOPE])]
    kr0 = q_rank + kv_rank
    wdn = mla_w_down[0][:, np.concatenate([np.arange(kr0), kr0 + ev, kr0 + ev, kr0 + od, kr0 + od])]
    cos_m, sin_m = _rope_tables(t, MLA_ROPE)
    qn, qr, kn, vv, kr = _mla_in_call(xs, mod, norm_mix[1], cast(wdn), mla_q_norm[0], mla_kv_norm[0],
                                      cast(wuq), cast(wukv), cos_m, sin_m)
    att = _mla_attn_call(qn, qr, kn, vv, kr, b_lat, tc)
    xs = _mla_out_call(xs, mod, att, cast(mla_w_o[0]))
    return _ffn_call(xs, b_lat, mod, norm_ffn2[1], cast(ffn2_w_gate[1]), cast(ffn2_w_up[1]),
                     cast(ffn2_w_down[1]), 6, final_g=final_norm)
```

```python
import functools

import numpy as np
import jax
import jax.numpy as jnp
from jax import lax
from jax.experimental import pallas as pl
from jax.experimental.pallas import tpu as pltpu

HEAD_DIM = 128
GRID_W = 64
ROPE_THETA = 10000.0
EPS = 1e-6
N_MOD = 9
GQA_KV_HEADS = 2
MLA_NOPE = 128
MLA_ROPE = 64
MLA_V = 128
LANES = 128
VMEM_LIMIT = 56 * 1024 * 1024
MOD_ROWS = 16
FFN_ROW_CHUNKS = 4
ATTN_ROW_CHUNK = 256
LOG2E = 1.4426950408889634
CAST_BLOCK_BYTES = 4 * 1024 * 1024

F32 = jnp.float32
BF16 = jnp.bfloat16


def _cparams(*sem):
    return pltpu.CompilerParams(dimension_semantics=sem, vmem_limit_bytes=VMEM_LIMIT)


def _resident(shape):
    nd = len(shape)
    return pl.BlockSpec(shape, lambda *_: (0,) * nd, pipeline_mode=pl.Buffered(1))


def _rms(x, g):
    return x * lax.rsqrt(jnp.mean(x * x, axis=-1, keepdims=True) + EPS) * g


def _modnorm(x, g, shift, scale):
    return _rms(x, g) * (1.0 + scale) + shift


def _rope(x, cos, sin):
    return x * cos + pltpu.roll(x, LANES // 2, axis=1) * sin


def _dot(a, b):
    return jnp.dot(a, b, preferred_element_type=F32)


def _dot_t(a, b):
    return lax.dot_general(a, b, (((1,), (1,)), ((), ())), preferred_element_type=F32)


def _mod_kernel(c_ref, w_ref, b_ref, o_ref):
    c = c_ref[...]
    s = (c * jax.nn.sigmoid(c)).astype(BF16)
    o_ref[0] = _dot(s, w_ref[0].astype(BF16)) + b_ref[0]


def _mod_call(cvec, mod_w, mod_b):
    depth, d, n = mod_w.shape
    tn = min(1024, n)
    return pl.pallas_call(
        _mod_kernel,
        grid=(depth, n // tn),
        in_specs=[
            pl.BlockSpec((MOD_ROWS, d), lambda l, j: (0, 0)),
            pl.BlockSpec((1, d, tn), lambda l, j: (l, 0, j)),
            pl.BlockSpec((1, 1, tn), lambda l, j: (l, 0, j)),
        ],
        out_specs=pl.BlockSpec((1, MOD_ROWS, tn), lambda l, j: (l, 0, j)),
        out_shape=jax.ShapeDtypeStruct((depth, MOD_ROWS, n), F32),
        compiler_params=_cparams("parallel", "parallel"),
        name="adaln_mod",
    )(cvec, mod_w, mod_b.reshape(depth, 1, n))


def _swiglu_act(xn, wg_ref, wu_ref):
    hg = _dot(xn, wg_ref[...])
    hu = _dot(xn, wu_ref[...])
    return (hg * jax.nn.sigmoid(hg) * hu).astype(BF16)


def _ffn_kernel(x_ref, xt_ref, mod_ref, modt_ref, g_ref, wg_ref, wu_ref, wd_ref, *rest,
                m0, final, n_ff, filled):
    rest = list(rest)
    fn_ref = rest.pop(0) if final else None
    if filled:
        rest.pop(0)
    o_ref, xn_ref, h_ref = rest[:3]
    ss_ref = rest[3] if final else None
    j = pl.program_id(2)
    tm = xn_ref.shape[0]
    tf = wg_ref.shape[1]

    @pl.when(j == 0)
    def _():
        rs = tm // FFN_ROW_CHUNKS
        for r in range(FFN_ROW_CHUNKS):
            rows = slice(r * rs, (r + 1) * rs)
            xn = _modnorm(x_ref[0, rows, :], g_ref[...], mod_ref[0, m0:m0 + 1, :],
                          mod_ref[0, m0 + 1:m0 + 2, :]).astype(BF16)
            xn_ref[rows, :] = xn
            h_ref[rows, :tf] = _swiglu_act(xn, wg_ref, wu_ref)

    @pl.when((j > 0) & (j < n_ff))
    def _():
        col = pl.multiple_of(j * tf, tf)
        h_ref[:, pl.ds(col, tf)] = _swiglu_act(xn_ref[...], wg_ref, wu_ref)

    @pl.when(j >= n_ff)
    def _():
        y = xt_ref[0] + 0.5 * modt_ref[0, m0 + 2:m0 + 3, :] * _dot(h_ref[...], wd_ref[...])
        if final:
            part = jnp.sum(y * y, axis=-1, keepdims=True)
            ss_ref[...] = jnp.where(j == n_ff, part, ss_ref[...] + part)
            tn = wd_ref.shape[1]
            col = pl.multiple_of((j - n_ff) * tn, tn)
            o_ref[0, :, pl.ds(col, tn)] = y
        else:
            o_ref[0] = y

    if final:
        @pl.when(j == pl.num_programs(2) - 1)
        def _():
            d = o_ref.shape[2]
            o_ref[0] = o_ref[0] * lax.rsqrt(ss_ref[...] / d + EPS) * fn_ref[...]


def _ffn_call(x, nb, mod, g, wg, wu, wd, layer, m0, final_g=None, row0=0, out_nb=None, fill=None):
    _, t, d = x.shape
    dff = wg.shape[2]
    tm = min(512, t)
    tf = min(512, dff)
    tn = min(512, d)
    n_ff = dff // tf
    final = final_g is not None
    up = lambda b, i, j: (layer, 0, jnp.minimum(j, n_ff - 1))
    down = lambda j: jnp.maximum(j - n_ff, 0)
    in_specs = [
        pl.BlockSpec((1, tm, d), lambda b, i, j: (b, i, 0)),
        pl.BlockSpec((1, tm, tn), lambda b, i, j: (b, i, down(j))),
        pl.BlockSpec((1, N_MOD, d), lambda b, i, j: (b + row0, 0, 0)),
        pl.BlockSpec((1, N_MOD, tn), lambda b, i, j: (b + row0, 0, down(j))),
        pl.BlockSpec((1, d), lambda b, i, j: (0, 0)),
        pl.BlockSpec((None, d, tf), up),
        pl.BlockSpec((None, d, tf), up),
        pl.BlockSpec((None, dff, tn), lambda b, i, j: (layer, 0, down(j))),
    ]
    args = [x, x, mod, mod, g.reshape(1, d), wg, wu, wd]
    scratch = [pltpu.VMEM((tm, d), BF16), pltpu.VMEM((tm, dff), BF16)]
    if final:
        in_specs.append(pl.BlockSpec((1, d), lambda b, i, j: (0, 0)))
        args.append(final_g.reshape(1, d))
        out_spec = pl.BlockSpec((1, tm, d), lambda b, i, j: (b + row0, i, 0))
        scratch.append(pltpu.VMEM((tm, 1), F32))
    else:
        out_spec = pl.BlockSpec((1, tm, tn), lambda b, i, j: (b + row0, i, down(j)))
    aliases = {}
    if fill is not None:
        aliases = {len(args): 0}
        in_specs.append(pl.BlockSpec(memory_space=pl.ANY))
        args.append(fill)
    return pl.pallas_call(
        functools.partial(_ffn_kernel, m0=m0, final=final, n_ff=n_ff, filled=fill is not None),
        grid=(nb, t // tm, n_ff + d // tn),
        in_specs=in_specs,
        out_specs=out_spec,
        out_shape=jax.ShapeDtypeStruct((out_nb or nb, t, d), F32),
        scratch_shapes=scratch,
        input_output_aliases=aliases,
        compiler_params=_cparams("parallel", "parallel", "arbitrary"),
        name="swiglu_half_step",
    )(*args)


def _cast_kernel(w_ref, o_ref):
    o_ref[...] = w_ref[...].astype(BF16)


def _cast_call(w):
    depth, r, c = w.shape
    tr = max(n for n in range(8, r + 1, 8) if r % n == 0 and 4 * c * n <= CAST_BLOCK_BYTES)
    spec = pl.BlockSpec((1, tr, c), lambda l, i: (l, i, 0))
    return pl.pallas_call(
        _cast_kernel,
        grid=(depth, r // tr),
        in_specs=[spec],
        out_specs=spec,
        out_shape=jax.ShapeDtypeStruct(w.shape, BF16),
        compiler_params=_cparams("parallel", "parallel"),
        name="weight_cast",
    )(w)


def _hyb_in_kernel(x_ref, mod_ref, g_ref, wgb_ref, wgc_ref, wu_ref, wq_ref, wk_ref, wv_ref,
                   qg_ref, kg_ref, cos_ref, sin_ref,
                   gb_ref, z_ref, q_ref, k_ref, v_ref):
    xn = _modnorm(x_ref[0], g_ref[...], mod_ref[0, 3:4, :], mod_ref[0, 4:5, :]).astype(BF16)
    gb_ref[0] = _dot(xn, wgb_ref[...]).astype(BF16)
    z_ref[0] = (_dot(xn, wgc_ref[...]) * _dot(xn, wu_ref[...])).astype(BF16)
    v_ref[0] = _dot(xn, wv_ref[...]).astype(BF16)
    cos = cos_ref[0]
    sin = sin_ref[0]
    q = _dot(xn, wq_ref[...])
    q_scale = HEAD_DIM ** -0.5 * LOG2E
    for h in range(q.shape[1] // HEAD_DIM):
        sl = slice(h * HEAD_DIM, (h + 1) * HEAD_DIM)
        q_ref[0, :, sl] = (_rope(_rms(q[:, sl], qg_ref[...]), cos, sin) * q_scale).astype(BF16)
    k = _dot(xn, wk_ref[...])
    for h in range(k.shape[1] // HEAD_DIM):
        sl = slice(h * HEAD_DIM, (h + 1) * HEAD_DIM)
        k_ref[0, :, sl] = _rope(_rms(k[:, sl], kg_ref[...]), cos, sin).astype(BF16)


def _hyb_in_call(x, mod, g, wgb, wgc, wu, wq, wk, wv, qg, kg, cos2, sin2):
    nb, t, d = x.shape
    tm = min(512, t)
    cc, nq, nkv = wgb.shape[1], wq.shape[1], wk.shape[1]
    row = lambda n: pl.BlockSpec((1, tm, n), lambda b, i: (b, i, 0))
    table = pl.BlockSpec((1, tm, LANES), lambda b, i: (jnp.where(b == nb - 1, 1, 0), i, 0))
    return pl.pallas_call(
        _hyb_in_kernel,
        grid=(nb, t // tm),
        in_specs=[
            row(d),
            pl.BlockSpec((1, N_MOD, d), lambda b, i: (b, 0, 0)),
            _resident((1, d)),
            _resident(wgb.shape), _resident(wgc.shape), _resident(wu.shape),
            _resident(wq.shape), _resident(wk.shape), _resident(wv.shape),
            _resident((1, HEAD_DIM)), _resident((1, HEAD_DIM)),
            table, table,
        ],
        out_specs=[row(cc), row(cc), row(nq), row(nkv), row(nkv)],
        out_shape=[jax.ShapeDtypeStruct((nb, t, n), BF16) for n in (cc, cc, nq, nkv, nkv)],
        compiler_params=_cparams("parallel", "parallel"),
        name="hyb_in_proj",
    )(x, mod, g.reshape(1, d), wgb, wgc, wu, wq, wk, wv,
      qg.reshape(1, HEAD_DIM), kg.reshape(1, HEAD_DIM), cos2, sin2)


def _softmax_pv(q, kv_list):
    s_list = [_dot_t(q, k) for k, _ in kv_list]
    m = s_list[0].max(axis=-1, keepdims=True)
    for s in s_list[1:]:
        m = jnp.maximum(m, s.max(axis=-1, keepdims=True))
    l = 0.0
    o = 0.0
    for s, (_, v) in zip(s_list, kv_list):
        p = jnp.exp2(s - m)
        l = l + p.sum(axis=-1, keepdims=True)
        o = o + _dot(p.astype(BF16), v)
    return o / l


def _row_chunks(n):
    c = min(ATTN_ROW_CHUNK, n)
    return [slice(r * c, (r + 1) * c) for r in range(n // c)]


def _gqa_kernel(*refs, has_lat):
    if has_lat:
        q_ref, kc_ref, vc_ref, kl_ref, vl_ref, o_ref = refs
    else:
        q_ref, kc_ref, vc_ref, o_ref = refs
    kv_list = [(kc_ref[0], vc_ref[0])]
    if has_lat:
        kv_list.append((kl_ref[0], vl_ref[0]))
    for h in range(q_ref.shape[2] // HEAD_DIM):
        sl = slice(h * HEAD_DIM, (h + 1) * HEAD_DIM)
        for rows in _row_chunks(q_ref.shape[1]):
            o_ref[0, rows, sl] = _softmax_pv(q_ref[0, rows, sl], kv_list).astype(BF16)


def _gqa_latent_call(q, k, v, b_lat, tc):
    nb, t, nq = q.shape
    gw = nq // GQA_KV_HEADS
    tq = min(512, t)
    ctx = pl.BlockSpec((1, tc, HEAD_DIM), lambda b, g, i: (b_lat, b, g))
    lat = pl.BlockSpec((1, t, HEAD_DIM), lambda b, g, i: (b, 0, g))
    qo = pl.BlockSpec((1, tq, gw), lambda b, g, i: (b, i, g))
    return pl.pallas_call(
        functools.partial(_gqa_kernel, has_lat=True),
        grid=(b_lat, GQA_KV_HEADS, t // tq),
        in_specs=[qo, ctx, ctx, lat, lat],
        out_specs=qo,
        out_shape=jax.ShapeDtypeStruct((nb, t, nq), BF16),
        compiler_params=_cparams("parallel", "parallel", "parallel"),
        name="gqa_latent",
    )(q, k, v, k, v)


def _gqa_ctx_kernel(q_ref, kc_ref, vc_ref, att_hbm_ref, o_ref):
    del att_hbm_ref
    _gqa_kernel(q_ref, kc_ref, vc_ref, o_ref, has_lat=False)


def _gqa_ctx_call(q, k, v, att, b_lat, tc):
    nb, t, nq = q.shape
    gw = nq // GQA_KV_HEADS
    ctx = pl.BlockSpec((1, tc, HEAD_DIM), lambda b, g: (b_lat, b, g))
    qo = pl.BlockSpec((1, tc, gw), lambda b, g: (b_lat, b, g))
    return pl.pallas_call(
        _gqa_ctx_kernel,
        grid=(t // tc, GQA_KV_HEADS),
        in_specs=[qo, ctx, ctx, pl.BlockSpec(memory_space=pl.ANY)],
        out_specs=qo,
        out_shape=jax.ShapeDtypeStruct(att.shape, att.dtype),
        input_output_aliases={3: 0},
        compiler_params=_cparams("parallel", "parallel"),
        name="gqa_context",
    )(q, k, v, att)


def _hyb_out_kernel(x_ref, mod_ref, gb_ref, z_ref, zp_ref, zn_ref, att_ref, cw_ref, w_ref,
                    o_ref, *, seq_lat, seq_ctx, b_lat):
    b, i = pl.program_id(0), pl.program_id(1)
    tm, cc = z_ref.shape[1], z_ref.shape[2]
    acc = _dot(att_ref[0], w_ref[cc:, :])
    z = z_ref[0].astype(F32)
    rows = lax.broadcasted_iota(jnp.int32, (tm, cc), 0)
    seq = jnp.where(b == b_lat, seq_ctx, seq_lat)
    pos = (i * tm + rows) & (seq - 1)
    z_prev = jnp.where(rows == 0, zp_ref[0, 7:8, :].astype(F32), pltpu.roll(z, 1, axis=0))
    z_next = jnp.where(rows == tm - 1, zn_ref[0, 0:1, :].astype(F32),
                       pltpu.roll(z, tm - 1, axis=0))
    z_prev = jnp.where(pos == 0, 0.0, z_prev)
    z_next = jnp.where(pos == seq - 1, 0.0, z_next)
    conv = z_prev * cw_ref[0:1, :] + z * cw_ref[1:2, :] + z_next * cw_ref[2:3, :]
    acc = acc + _dot((gb_ref[0].astype(F32) * conv).astype(BF16), w_ref[:cc, :])
    o_ref[0] = x_ref[0] + mod_ref[0, 5:6, :] * acc


def _hyb_out_call(x, mod, gb, z, att, conv_w, w_out, b_lat, tc):
    nb, t, d = x.shape
    cc = gb.shape[2]
    tm = min(512, t)
    hb = tm // 8
    row = lambda n: pl.BlockSpec((1, tm, n), lambda b, i: (b, i, 0))
    return pl.pallas_call(
        functools.partial(_hyb_out_kernel, seq_lat=t, seq_ctx=tc, b_lat=b_lat),
        grid=(nb, t // tm),
        in_specs=[
            row(d),
            pl.BlockSpec((1, N_MOD, d), lambda b, i: (b, 0, 0)),
            row(cc), row(cc),
            pl.BlockSpec((1, 8, cc), lambda b, i: (b, jnp.maximum(i * hb - 1, 0), 0)),
            pl.BlockSpec((1, 8, cc), lambda b, i: (b, jnp.minimum((i + 1) * hb, t // 8 - 1), 0)),
            row(att.shape[2]),
            _resident(conv_w.shape),
            _resident(w_out.shape),
        ],
        out_specs=row(d),
        out_shape=jax.ShapeDtypeStruct((nb, t, d), F32),
        compiler_params=_cparams("parallel", "parallel"),
        name="hyb_out_proj",
    )(x, mod, gb, z, z, z, att, conv_w, w_out)


def _mla_in_kernel(x_ref, mod_ref, g_ref, wd_ref, qg_ref, kvg_ref, wuq_ref, wukv_ref,
                   cos_ref, sin_ref, qn_ref, qr_ref, kn_ref, v_ref, kr_ref, *, q_rank, kv_rank):
    xn = _modnorm(x_ref[0], g_ref[...], mod_ref[0, 3:4, :], mod_ref[0, 4:5, :]).astype(BF16)
    d = _dot(xn, wd_ref[...])
    cos = cos_ref[0]
    sin = sin_ref[0]
    n_nope = qn_ref.shape[2]
    q_scale = (MLA_NOPE + MLA_ROPE) ** -0.5 * LOG2E
    q = _dot(_rms(d[:, :q_rank], qg_ref[...]).astype(BF16), wuq_ref[...])
    qn_ref[0] = (q[:, :n_nope] * q_scale).astype(BF16)
    for j in range(qr_ref.shape[2] // LANES):
        qr_ref[0, :, j * LANES:(j + 1) * LANES] = (
            _rope(q[:, n_nope + j * LANES:n_nope + (j + 1) * LANES], cos, sin) * q_scale).astype(BF16)
    kv = _dot(_rms(d[:, q_rank:q_rank + kv_rank], kvg_ref[...]).astype(BF16), wukv_ref[...])
    kn_ref[0] = kv[:, :n_nope].astype(BF16)
    v_ref[0] = kv[:, n_nope:].astype(BF16)
    kr = _rope(d[:, q_rank + kv_rank:], cos, sin)
    first = (lax.broadcasted_iota(jnp.int32, kr.shape, 1) & (MLA_ROPE // 2)) == 0
    kr_ref[0, :, :LANES] = jnp.where(first, kr, 0.0).astype(BF16)
    kr_ref[0, :, LANES:] = jnp.where(first, 0.0, kr).astype(BF16)


def _mla_in_call(x, mod, g, wd, qg, kvg, wuq, wukv, cos2, sin2):
    nb, t, d = x.shape
    tm = min(256, t)
    q_rank, kv_rank = qg.shape[0], kvg.shape[0]
    n_nope = wukv.shape[1] // 2
    n_rope = wuq.shape[1] - n_nope
    row = lambda n: pl.BlockSpec((1, tm, n), lambda b, i: (b, i, 0))
    table = pl.BlockSpec((1, tm, LANES), lambda b, i: (jnp.where(b == nb - 1, 1, 0), i, 0))
    widths = (n_nope, n_rope, n_nope, n_nope, 2 * LANES)
    return pl.pallas_call(
        functools.partial(_mla_in_kernel, q_rank=q_rank, kv_rank=kv_rank),
        grid=(nb, t // tm),
        in_specs=[
            row(d),
            pl.BlockSpec((1, N_MOD, d), lambda b, i: (b, 0, 0)),
            _resident((1, d)),
            _resident(wd.shape), _resident((1, q_rank)), _resident((1, kv_rank)),
            _resident(wuq.shape), _resident(wukv.shape),
            table, table,
        ],
        out_specs=[row(n) for n in widths],
        out_shape=[jax.ShapeDtypeStruct((nb, t, n), BF16) for n in widths],
        compiler_params=_cparams("parallel", "parallel"),
        name="mla_in_proj",
    )(x, mod, g.reshape(1, d), wd, qg.reshape(1, q_rank), kvg.reshape(1, kv_rank), wuq, wukv,
      cos2, sin2)


def _mla_attn_kernel(qn_ref, qr_ref, knc_ref, krc_ref, vc_ref, knl_ref, krl_ref, vl_ref,
                     o_ref, k_ref, v_ref):
    tc = knc_ref.shape[1]

    @pl.when(pl.program_id(2) == 0)
    def _():
        k_ref[:tc, :LANES] = knc_ref[0]
        k_ref[:tc, LANES:] = krc_ref[0]
        k_ref[tc:, :LANES] = knl_ref[0]
        k_ref[tc:, LANES:] = krl_ref[0]
        v_ref[:tc, :] = vc_ref[0]
        v_ref[tc:, :] = vl_ref[0]

    kv_list = [(k_ref[...], v_ref[...])]
    for rows in _row_chunks(qn_ref.shape[1]):
        q = jnp.concatenate([qn_ref[0, rows, :], qr_ref[0, rows, :]], axis=1)
        o_ref[0, rows, :] = _softmax_pv(q, kv_list).astype(BF16)


def _mla_attn_call(qn, qr, kn, v, kr, b_lat, tc):
    nb, t, n_nope = qn.shape
    heads = n_nope // MLA_NOPE
    tq = min(1024, t)
    qspec = pl.BlockSpec((1, tq, LANES), lambda b, h, i: (b, i, h))
    qrspec = pl.BlockSpec((1, tq, LANES), lambda b, h, i: (b, i, h // 2))
    ctx = lambda f: pl.BlockSpec((1, tc, LANES), lambda b, h, i: (b_lat, b, f(h)))
    lat = lambda f: pl.BlockSpec((1, t, LANES), lambda b, h, i: (b, 0, f(h)))
    same = lambda h: h
    parity = lambda h: h % 2
    return pl.pallas_call(
        _mla_attn_kernel,
        grid=(b_lat, heads, t // tq),
        in_specs=[qspec, qrspec, ctx(same), ctx(parity), ctx(same), lat(same), lat(parity), lat(same)],
        out_specs=qspec,
        out_shape=jax.ShapeDtypeStruct((b_lat, t, n_nope), BF16),
        scratch_shapes=[pltpu.VMEM((tc + t, 2 * LANES), BF16), pltpu.VMEM((tc + t, LANES), BF16)],
        compiler_params=_cparams("parallel", "parallel", "arbitrary"),
        name="mla_attention",
    )(qn, qr, kn, kr, v, kn, kr, v)


def _mla_out_kernel(x_ref, mod_ref, att_ref, w_ref, o_ref):
    o_ref[0] = x_ref[0] + mod_ref[0, 5:6, :] * _dot(att_ref[0], w_ref[...])


def _mla_out_call(x, mod, att, w_o):
    nb, t, k = att.shape
    d = w_o.shape[1]
    tm = min(512, t)
    row = lambda n: pl.BlockSpec((1, tm, n), lambda b, i: (b, i, 0))
    return pl.pallas_call(
        _mla_out_kernel,
        grid=(nb, t // tm),
        in_specs=[row(d), pl.BlockSpec((1, N_MOD, d), lambda b, i: (b, 0, 0)), row(k),
                  _resident(w_o.shape)],
        out_specs=row(d),
        out_shape=jax.ShapeDtypeStruct((nb, t, d), F32),
        compiler_params=_cparams("parallel", "parallel"),
        name="mla_out_proj",
    )(x, mod, att, w_o)


def _rope_angles(n_tok, dim):
    n_rows = n_tok // GRID_W
    row = jnp.repeat(jnp.arange(n_rows), GRID_W).astype(F32)
    col = jnp.tile(jnp.arange(GRID_W), n_rows).astype(F32)
    half = dim // 2
    inv = 1.0 / (ROPE_THETA ** (jnp.arange(0, half, 2, dtype=F32) / half))
    return jnp.concatenate([row[:, None] * inv, col[:, None] * inv], axis=-1)


def _rope_tables(n_tok, dim):
    ang = _rope_angles(n_tok, dim)
    reps = LANES // dim
    cos = jnp.concatenate([jnp.cos(ang)] * (2 * reps), axis=-1)
    sin = jnp.concatenate([-jnp.sin(ang)] * reps + [jnp.sin(ang)] * reps, axis=-1)
    return (jnp.stack([cos, jnp.ones_like(cos)]), jnp.stack([sin, jnp.zeros_like(sin)]))


def _deinterleave(n):
    return np.concatenate([np.arange(0, n, 2), np.arange(1, n, 2)])


def kernel(x, c, ctx, c_ctx, mod_w, mod_b, norm_ffn1, norm_mix, norm_ffn2,
           ffn1_w_gate, ffn1_w_up, ffn1_w_down, ffn2_w_gate, ffn2_w_up, ffn2_w_down,
           hyb_w_in, hyb_conv_w, hyb_q_norm, hyb_k_norm, hyb_w_out,
           mla_w_down, mla_q_norm, mla_kv_norm, mla_w_uq, mla_w_ukv, mla_w_o, final_norm):
    b_lat, t, d = x.shape
    tc = ctx.shape[1]
    depth = mod_w.shape[0]
    assert depth == 2 and ctx.shape[0] * tc == t and b_lat + 1 <= MOD_ROWS
    assert t & (t - 1) == 0 and tc & (tc - 1) == 0 and t % GRID_W == 0
    nb = b_lat + 1

    cvec = jnp.concatenate([c, c_ctx[None], jnp.zeros((MOD_ROWS - nb, d), F32)], axis=0)
    mods = _mod_call(cvec, mod_w, mod_b).reshape(depth, MOD_ROWS, N_MOD, d)
    cast = lambda w: w.astype(BF16)
    f1 = [_cast_call(w) for w in (ffn1_w_gate, ffn1_w_up, ffn1_w_down)]
    f2 = [_cast_call(w) for w in (ffn2_w_gate, ffn2_w_up, ffn2_w_down)]

    mod = mods[0]
    xs = _ffn_call(x, b_lat, mod, norm_ffn1[0], *f1, 0, 0, out_nb=nb)
    xs = _ffn_call(ctx.reshape(1, t, d), 1, mod, norm_ffn1[0], *f1, 0, 0, row0=b_lat, out_nb=nb, fill=xs)
    cc = d // 2
    n_q = cc
    n_kv = GQA_KV_HEADS * HEAD_DIM
    w_in = hyb_w_in[0]
    perm = _deinterleave(HEAD_DIM)
    head_perm = lambda n: (np.arange(n).reshape(-1, HEAD_DIM)[:, perm]).reshape(-1)
    wq = w_in[:, 3 * cc:3 * cc + n_q][:, head_perm(n_q)]
    wk = w_in[:, 3 * cc + n_q:3 * cc + n_q + n_kv][:, head_perm(n_kv)]
    wv = w_in[:, 3 * cc + n_q + n_kv:]
    cos_a, sin_a = _rope_tables(t, HEAD_DIM)
    gb, z, q, k, v = _hyb_in_call(
        xs, mod, norm_mix[0], cast(w_in[:, :cc]), cast(w_in[:, cc:2 * cc]), cast(w_in[:, 2 * cc:3 * cc]),
        cast(wq), cast(wk), cast(wv), hyb_q_norm[0][perm], hyb_k_norm[0][perm], cos_a, sin_a)
    att = _gqa_latent_call(q, k, v, b_lat, tc)
    att = _gqa_ctx_call(q, k, v, att, b_lat, tc)
    xs = _hyb_out_call(xs, mod, gb, z, att, hyb_conv_w[0], cast(hyb_w_out[0]), b_lat, tc)
    xs = _ffn_call(xs, nb, mod, norm_ffn2[0], *f2, 0, 6)

    mod = mods[1]
    xs = _ffn_call(xs, nb, mod, norm_ffn1[1], *f1, 1, 0)
    heads = d // HEAD_DIM
    q_rank, kv_rank = mla_q_norm.shape[1], mla_kv_norm.shape[1]
    qk = MLA_NOPE + MLA_ROPE
    ev, od = np.arange(0, MLA_ROPE, 2), np.arange(1, MLA_ROPE, 2)
    nope_cols = (np.arange(heads)[:, None] * qk + np.arange(MLA_NOPE)[None]).reshape(-1)
    rope_cols = []
    for j in range(heads // 2):
        a0, b0 = 2 * j * qk + MLA_NOPE, (2 * j + 1) * qk + MLA_NOPE
        rope_cols += [a0 + ev, b0 + ev, a0 + od, b0 + od]
    wuq = mla_w_uq[0][:, np.concatenate([nope_cols] + rope_cols)]
    kvw = MLA_NOPE + MLA_V
    kn_cols = (np.arange(heads)[:, None] * kvw + np.arange(MLA_NOPE)[None]).reshape(-1)
    wukv = mla_w_ukv[0][:, np.concatenate([kn_cols, kn_cols + MLA_NOPE])]
    kr0 = q_rank + kv_rank
    wdn = mla_w_down[0][:, np.concatenate([np.arange(kr0), kr0 + ev, kr0 + ev, kr0 + od, kr0 + od])]
    cos_m, sin_m = _rope_tables(t, MLA_ROPE)
    qn, qr, kn, vv, kr = _mla_in_call(xs, mod, norm_mix[1], cast(wdn), mla_q_norm[0], mla_kv_norm[0],
                                      cast(wuq), cast(wukv), cos_m, sin_m)
    att = _mla_attn_call(qn, qr, kn, vv, kr, b_lat, tc)
    xs = _mla_out_call(xs, mod, att, cast(mla_w_o[0]))
    return _ffn_call(xs, b_lat, mod, norm_ffn2[1], *f2, 1, 6, final_g=final_norm)
```

```python
import functools

import numpy as np
import jax
import jax.numpy as jnp
from jax import lax
from jax.experimental import pallas as pl
from jax.experimental.pallas import tpu as pltpu

HEAD_DIM = 128
GRID_W = 64
ROPE_THETA = 10000.0
EPS = 1e-6
N_MOD = 9
GQA_KV_HEADS = 2
MLA_NOPE = 128
MLA_ROPE = 64
MLA_V = 128
LANES = 128
VMEM_LIMIT = 56 * 1024 * 1024
MOD_ROWS = 16
FFN_ROW_CHUNKS = 4
FFN_X_ADVANCE_STEPS = (2, 4)
GQA_ROW_CHUNK = 512
MLA_ROW_CHUNK = 256
LOG2E = 1.4426950408889634
CAST_BLOCK_BYTES = 4 * 1024 * 1024

F32 = jnp.float32
BF16 = jnp.bfloat16


def _cparams(*sem):
    return pltpu.CompilerParams(dimension_semantics=sem, vmem_limit_bytes=VMEM_LIMIT)


def _resident(shape):
    nd = len(shape)
    return pl.BlockSpec(shape, lambda *_: (0,) * nd, pipeline_mode=pl.Buffered(1))


def _rms(x, g):
    return x * lax.rsqrt(jnp.mean(x * x, axis=-1, keepdims=True) + EPS) * g


def _modnorm(x, g, shift, scale):
    return _rms(x, g) * (1.0 + scale) + shift


def _rope(x, cos, sin):
    return x * cos + pltpu.roll(x, LANES // 2, axis=1) * sin


def _dot(a, b):
    return jnp.dot(a, b, preferred_element_type=F32)


def _dot_t(a, b):
    return lax.dot_general(a, b, (((1,), (1,)), ((), ())), preferred_element_type=F32)


def _mod_kernel(c_ref, w_ref, b_ref, o_ref):
    c = c_ref[...]
    s = (c * jax.nn.sigmoid(c)).astype(BF16)
    o_ref[0] = _dot(s, w_ref[0].astype(BF16)) + b_ref[0]


def _mod_call(cvec, mod_w, mod_b):
    depth, d, n = mod_w.shape
    tn = min(1024, n)
    return pl.pallas_call(
        _mod_kernel,
        grid=(depth, n // tn),
        in_specs=[
            pl.BlockSpec((MOD_ROWS, d), lambda l, j: (0, 0)),
            pl.BlockSpec((1, d, tn), lambda l, j: (l, 0, j)),
            pl.BlockSpec((1, 1, tn), lambda l, j: (l, 0, j)),
        ],
        out_specs=pl.BlockSpec((1, MOD_ROWS, tn), lambda l, j: (l, 0, j)),
        out_shape=jax.ShapeDtypeStruct((depth, MOD_ROWS, n), F32),
        compiler_params=_cparams("parallel", "parallel"),
        name="adaln_mod",
    )(cvec, mod_w, mod_b.reshape(depth, 1, n))


def _swiglu_act(xn, wg_ref, wu_ref):
    hg = _dot(xn, wg_ref[...])
    hu = _dot(xn, wu_ref[...])
    return (hg * jax.nn.sigmoid(hg) * hu).astype(BF16)


def _ffn_kernel(xlo_ref, xhi_ref, xt_ref, mod_ref, modt_ref, g_ref, wg_ref, wu_ref, wd_ref, *rest,
                m0, final, n_ff, filled):
    rest = list(rest)
    fn_ref = rest.pop(0) if final else None
    if filled:
        rest.pop(0)
    o_ref, xn_ref, h_ref = rest[:3]
    ss_ref = rest[3] if final else None
    j = pl.program_id(2)
    tm = xn_ref.shape[0]
    tf = wg_ref.shape[1]

    @pl.when(j == 0)
    def _():
        rs = tm // FFN_ROW_CHUNKS
        per_half = FFN_ROW_CHUNKS // 2
        for r in range(FFN_ROW_CHUNKS):
            src = (xlo_ref, xhi_ref)[r // per_half]
            lr = r % per_half
            rows = slice(r * rs, (r + 1) * rs)
            xn = _modnorm(src[0, lr * rs:(lr + 1) * rs, :], g_ref[...], mod_ref[0, m0:m0 + 1, :],
                          mod_ref[0, m0 + 1:m0 + 2, :]).astype(BF16)
            xn_ref[rows, :] = xn
            h_ref[rows, :tf] = _swiglu_act(xn, wg_ref, wu_ref)

    @pl.when((j > 0) & (j < n_ff))
    def _():
        col = pl.multiple_of(j * tf, tf)
        h_ref[:, pl.ds(col, tf)] = _swiglu_act(xn_ref[...], wg_ref, wu_ref)

    @pl.when(j >= n_ff)
    def _():
        y = xt_ref[0] + 0.5 * modt_ref[0, m0 + 2:m0 + 3, :] * _dot(h_ref[...], wd_ref[...])
        if final:
            part = jnp.sum(y * y, axis=-1, keepdims=True)
            ss_ref[...] = jnp.where(j == n_ff, part, ss_ref[...] + part)
            tn = wd_ref.shape[1]
            col = pl.multiple_of((j - n_ff) * tn, tn)
            o_ref[0, :, pl.ds(col, tn)] = y
        else:
            o_ref[0] = y

    if final:
        @pl.when(j == pl.num_programs(2) - 1)
        def _():
            d = o_ref.shape[2]
            o_ref[0] = o_ref[0] * lax.rsqrt(ss_ref[...] / d + EPS) * fn_ref[...]


def _ffn_call(x, nb, mod, g, wg, wu, wd, layer, m0, final_g=None, row0=0, out_nb=None, fill=None):
    _, t, d = x.shape
    dff = wg.shape[2]
    final = final_g is not None
    tm, tf, tn = (512, 512, 512) if final else (1024, 256, 256)
    tm, tf, tn = min(tm, t), min(tf, dff), min(tn, d)
    n_ff = dff // tf
    nt = t // tm
    up = lambda b, i, j: (layer, 0, jnp.minimum(j, n_ff - 1))
    down = lambda j: jnp.maximum(j - n_ff, 0)

    def x_half(half, start):
        def index(b, i, j):
            lin = jnp.minimum(b * nt + i + (j >= start).astype(jnp.int32), nb * nt - 1)
            return (lin // nt, 2 * (lin % nt) + half, 0)
        return pl.BlockSpec((1, tm // 2, d), index)

    in_specs = [
        x_half(0, FFN_X_ADVANCE_STEPS[0]),
        x_half(1, FFN_X_ADVANCE_STEPS[1]),
        pl.BlockSpec((1, tm, tn), lambda b, i, j: (b, i, down(j))),
        pl.BlockSpec((1, N_MOD, d), lambda b, i, j: (b + row0, 0, 0)),
        pl.BlockSpec((1, N_MOD, tn), lambda b, i, j: (b + row0, 0, down(j))),
        pl.BlockSpec((1, d), lambda b, i, j: (0, 0)),
        pl.BlockSpec((None, d, tf), up),
        pl.BlockSpec((None, d, tf), up),
        pl.BlockSpec((None, dff, tn), lambda b, i, j: (layer, 0, down(j))),
    ]
    args = [x, x, x, mod, mod, g.reshape(1, d), wg, wu, wd]
    scratch = [pltpu.VMEM((tm, d), BF16), pltpu.VMEM((tm, dff), BF16)]
    if final:
        in_specs.append(pl.BlockSpec((1, d), lambda b, i, j: (0, 0)))
        args.append(final_g.reshape(1, d))
        out_spec = pl.BlockSpec((1, tm, d), lambda b, i, j: (b + row0, i, 0))
        scratch.append(pltpu.VMEM((tm, 1), F32))
    else:
        out_spec = pl.BlockSpec((1, tm, tn), lambda b, i, j: (b + row0, i, down(j)))
    aliases = {}
    if fill is not None:
        aliases = {len(args): 0}
        in_specs.append(pl.BlockSpec(memory_space=pl.ANY))
        args.append(fill)
    return pl.pallas_call(
        functools.partial(_ffn_kernel, m0=m0, final=final, n_ff=n_ff, filled=fill is not None),
        grid=(nb, t // tm, n_ff + d // tn),
        in_specs=in_specs,
        out_specs=out_spec,
        out_shape=jax.ShapeDtypeStruct((out_nb or nb, t, d), F32),
        scratch_shapes=scratch,
        input_output_aliases=aliases,
        compiler_params=_cparams("parallel", "parallel", "arbitrary"),
        name="swiglu_half_step",
    )(*args)


def _cast_kernel(w_ref, o_ref):
    o_ref[...] = w_ref[...].astype(BF16)


def _cast_call(w):
    depth, r, c = w.shape
    tr = max(n for n in range(8, r + 1, 8) if r % n == 0 and 4 * c * n <= CAST_BLOCK_BYTES)
    spec = pl.BlockSpec((1, tr, c), lambda l, i: (l, i, 0))
    return pl.pallas_call(
        _cast_kernel,
        grid=(depth, r // tr),
        in_specs=[spec],
        out_specs=spec,
        out_shape=jax.ShapeDtypeStruct(w.shape, BF16),
        compiler_params=_cparams("parallel", "parallel"),
        name="weight_cast",
    )(w)


def _hyb_in_kernel(x_ref, mod_ref, g_ref, wgb_ref, wgc_ref, wu_ref, wq_ref, wk_ref, wv_ref,
                   qg_ref, kg_ref, cos_ref, sin_ref,
                   gb_ref, z_ref, q_ref, k_ref, v_ref):
    xn = _modnorm(x_ref[0], g_ref[...], mod_ref[0, 3:4, :], mod_ref[0, 4:5, :]).astype(BF16)
    gb_ref[0] = _dot(xn, wgb_ref[...]).astype(BF16)
    z_ref[0] = (_dot(xn, wgc_ref[...]) * _dot(xn, wu_ref[...])).astype(BF16)
    v_ref[0] = _dot(xn, wv_ref[...]).astype(BF16)
    cos = cos_ref[0]
    sin = sin_ref[0]
    q = _dot(xn, wq_ref[...])
    q_scale = HEAD_DIM ** -0.5 * LOG2E
    for h in range(q.shape[1] // HEAD_DIM):
        sl = slice(h * HEAD_DIM, (h + 1) * HEAD_DIM)
        q_ref[0, :, sl] = (_rope(_rms(q[:, sl], qg_ref[...]), cos, sin) * q_scale).astype(BF16)
    k = _dot(xn, wk_ref[...])
    for h in range(k.shape[1] // HEAD_DIM):
        sl = slice(h * HEAD_DIM, (h + 1) * HEAD_DIM)
        k_ref[0, :, sl] = _rope(_rms(k[:, sl], kg_ref[...]), cos, sin).astype(BF16)


def _hyb_in_call(x, mod, g, wgb, wgc, wu, wq, wk, wv, qg, kg, cos2, sin2):
    nb, t, d = x.shape
    tm = min(512, t)
    cc, nq, nkv = wgb.shape[1], wq.shape[1], wk.shape[1]
    row = lambda n: pl.BlockSpec((1, tm, n), lambda b, i: (b, i, 0))
    table = pl.BlockSpec((1, tm, LANES), lambda b, i: (jnp.where(b == nb - 1, 1, 0), i, 0))
    return pl.pallas_call(
        _hyb_in_kernel,
        grid=(nb, t // tm),
        in_specs=[
            row(d),
            pl.BlockSpec((1, N_MOD, d), lambda b, i: (b, 0, 0)),
            _resident((1, d)),
            _resident(wgb.shape), _resident(wgc.shape), _resident(wu.shape),
            _resident(wq.shape), _resident(wk.shape), _resident(wv.shape),
            _resident((1, HEAD_DIM)), _resident((1, HEAD_DIM)),
            table, table,
        ],
        out_specs=[row(cc), row(cc), row(nq), row(nkv), row(nkv)],
        out_shape=[jax.ShapeDtypeStruct((nb, t, n), BF16) for n in (cc, cc, nq, nkv, nkv)],
        compiler_params=_cparams("parallel", "parallel"),
        name="hyb_in_proj",
    )(x, mod, g.reshape(1, d), wgb, wgc, wu, wq, wk, wv,
      qg.reshape(1, HEAD_DIM), kg.reshape(1, HEAD_DIM), cos2, sin2)


def _softmax_pv(q, kv_list):
    s_list = [_dot_t(q, k) for k, _ in kv_list]
    m = s_list[0].max(axis=-1, keepdims=True)
    for s in s_list[1:]:
        m = jnp.maximum(m, s.max(axis=-1, keepdims=True))
    l = 0.0
    o = 0.0
    for s, (_, v) in zip(s_list, kv_list):
        p = jnp.exp2(s - m)
        l = l + p.sum(axis=-1, keepdims=True)
        o = o + _dot(p.astype(BF16), v)
    return o / l


def _row_chunks(n, c):
    c = min(c, n)
    return [slice(r * c, (r + 1) * c) for r in range(n // c)]


def _gqa_kernel(*refs, has_lat):
    if has_lat:
        q_ref, kc_ref, vc_ref, kl_ref, vl_ref, o_ref = refs
    else:
        q_ref, kc_ref, vc_ref, o_ref = refs
    kv_list = [(kc_ref[0], vc_ref[0])]
    if has_lat:
        kv_list.append((kl_ref[0], vl_ref[0]))
    for h in range(q_ref.shape[2] // HEAD_DIM):
        sl = slice(h * HEAD_DIM, (h + 1) * HEAD_DIM)
        for rows in _row_chunks(q_ref.shape[1], GQA_ROW_CHUNK):
            o_ref[0, rows, sl] = _softmax_pv(q_ref[0, rows, sl], kv_list).astype(BF16)


def _gqa_latent_call(q, k, v, b_lat, tc):
    nb, t, nq = q.shape
    gw = nq // GQA_KV_HEADS
    tq = min(512, t)
    ctx = pl.BlockSpec((1, tc, HEAD_DIM), lambda b, g, i: (b_lat, b, g))
    lat = pl.BlockSpec((1, t, HEAD_DIM), lambda b, g, i: (b, 0, g))
    qo = pl.BlockSpec((1, tq, gw), lambda b, g, i: (b, i, g))
    return pl.pallas_call(
        functools.partial(_gqa_kernel, has_lat=True),
        grid=(b_lat, GQA_KV_HEADS, t // tq),
        in_specs=[qo, ctx, ctx, lat, lat],
        out_specs=qo,
        out_shape=jax.ShapeDtypeStruct((nb, t, nq), BF16),
        compiler_params=_cparams("parallel", "parallel", "parallel"),
        name="gqa_latent",
    )(q, k, v, k, v)


def _gqa_ctx_kernel(q_ref, kc_ref, vc_ref, att_hbm_ref, o_ref):
    del att_hbm_ref
    _gqa_kernel(q_ref, kc_ref, vc_ref, o_ref, has_lat=False)


def _gqa_ctx_call(q, k, v, att, b_lat, tc):
    nb, t, nq = q.shape
    gw = nq // GQA_KV_HEADS
    ctx = pl.BlockSpec((1, tc, HEAD_DIM), lambda b, g: (b_lat, b, g))
    qo = pl.BlockSpec((1, tc, gw), lambda b, g: (b_lat, b, g))
    return pl.pallas_call(
        _gqa_ctx_kernel,
        grid=(t // tc, GQA_KV_HEADS),
        in_specs=[qo, ctx, ctx, pl.BlockSpec(memory_space=pl.ANY)],
        out_specs=qo,
        out_shape=jax.ShapeDtypeStruct(att.shape, att.dtype),
        input_output_aliases={3: 0},
        compiler_params=_cparams("parallel", "parallel"),
        name="gqa_context",
    )(q, k, v, att)


def _hyb_out_kernel(x_ref, mod_ref, gb_ref, z_ref, zp_ref, zn_ref, att_ref, cw_ref, w_ref,
                    o_ref, *, seq_lat, seq_ctx, b_lat):
    b, i = pl.program_id(0), pl.program_id(1)
    tm, cc = z_ref.shape[1], z_ref.shape[2]
    acc = _dot(att_ref[0], w_ref[cc:, :])
    z = z_ref[0].astype(F32)
    rows = lax.broadcasted_iota(jnp.int32, (tm, cc), 0)
    seq = jnp.where(b == b_lat, seq_ctx, seq_lat)
    pos = (i * tm + rows) & (seq - 1)
    z_prev = jnp.where(rows == 0, zp_ref[0, 7:8, :].astype(F32), pltpu.roll(z, 1, axis=0))
    z_next = jnp.where(rows == tm - 1, zn_ref[0, 0:1, :].astype(F32),
                       pltpu.roll(z, tm - 1, axis=0))
    z_prev = jnp.where(pos == 0, 0.0, z_prev)
    z_next = jnp.where(pos == seq - 1, 0.0, z_next)
    conv = z_prev * cw_ref[0:1, :] + z * cw_ref[1:2, :] + z_next * cw_ref[2:3, :]
    acc = acc + _dot((gb_ref[0].astype(F32) * conv).astype(BF16), w_ref[:cc, :])
    o_ref[0] = x_ref[0] + mod_ref[0, 5:6, :] * acc


def _hyb_out_call(x, mod, gb, z, att, conv_w, w_out, b_lat, tc):
    nb, t, d = x.shape
    cc = gb.shape[2]
    tm = min(512, t)
    hb = tm // 8
    row = lambda n: pl.BlockSpec((1, tm, n), lambda b, i: (b, i, 0))
    return pl.pallas_call(
        functools.partial(_hyb_out_kernel, seq_lat=t, seq_ctx=tc, b_lat=b_lat),
        grid=(nb, t // tm),
        in_specs=[
            row(d),
            pl.BlockSpec((1, N_MOD, d), lambda b, i: (b, 0, 0)),
            row(cc), row(cc),
            pl.BlockSpec((1, 8, cc), lambda b, i: (b, jnp.maximum(i * hb - 1, 0), 0)),
            pl.BlockSpec((1, 8, cc), lambda b, i: (b, jnp.minimum((i + 1) * hb, t // 8 - 1), 0)),
            row(att.shape[2]),
            _resident(conv_w.shape),
            _resident(w_out.shape),
        ],
        out_specs=row(d),
        out_shape=jax.ShapeDtypeStruct((nb, t, d), F32),
        compiler_params=_cparams("parallel", "parallel"),
        name="hyb_out_proj",
    )(x, mod, gb, z, z, z, att, conv_w, w_out)


def _mla_in_kernel(x_ref, mod_ref, g_ref, wd_ref, qg_ref, kvg_ref, wuq_ref, wukv_ref,
                   cos_ref, sin_ref, qn_ref, qr_ref, kn_ref, v_ref, kr_ref, *, q_rank, kv_rank):
    xn = _modnorm(x_ref[0], g_ref[...], mod_ref[0, 3:4, :], mod_ref[0, 4:5, :]).astype(BF16)
    d = _dot(xn, wd_ref[...])
    cos = cos_ref[0]
    sin = sin_ref[0]
    n_nope = qn_ref.shape[2]
    q_scale = (MLA_NOPE + MLA_ROPE) ** -0.5 * LOG2E
    q = _dot(_rms(d[:, :q_rank], qg_ref[...]).astype(BF16), wuq_ref[...])
    qn_ref[0] = (q[:, :n_nope] * q_scale).astype(BF16)
    for j in range(qr_ref.shape[2] // LANES):
        qr_ref[0, :, j * LANES:(j + 1) * LANES] = (
            _rope(q[:, n_nope + j * LANES:n_nope + (j + 1) * LANES], cos, sin) * q_scale).astype(BF16)
    kv = _dot(_rms(d[:, q_rank:q_rank + kv_rank], kvg_ref[...]).astype(BF16), wukv_ref[...])
    kn_ref[0] = kv[:, :n_nope].astype(BF16)
    v_ref[0] = kv[:, n_nope:].astype(BF16)
    kr = _rope(d[:, q_rank + kv_rank:], cos, sin)
    first = (lax.broadcasted_iota(jnp.int32, kr.shape, 1) & (MLA_ROPE // 2)) == 0
    kr_ref[0, :, :LANES] = jnp.where(first, kr, 0.0).astype(BF16)
    kr_ref[0, :, LANES:] = jnp.where(first, 0.0, kr).astype(BF16)


def _mla_in_call(x, mod, g, wd, qg, kvg, wuq, wukv, cos2, sin2):
    nb, t, d = x.shape
    tm = min(256, t)
    q_rank, kv_rank = qg.shape[0], kvg.shape[0]
    n_nope = wukv.shape[1] // 2
    n_rope = wuq.shape[1] - n_nope
    row = lambda n: pl.BlockSpec((1, tm, n), lambda b, i: (b, i, 0))
    table = pl.BlockSpec((1, tm, LANES), lambda b, i: (jnp.where(b == nb - 1, 1, 0), i, 0))
    widths = (n_nope, n_rope, n_nope, n_nope, 2 * LANES)
    return pl.pallas_call(
        functools.partial(_mla_in_kernel, q_rank=q_rank, kv_rank=kv_rank),
        grid=(nb, t // tm),
        in_specs=[
            row(d),
            pl.BlockSpec((1, N_MOD, d), lambda b, i: (b, 0, 0)),
            _resident((1, d)),
            _resident(wd.shape), _resident((1, q_rank)), _resident((1, kv_rank)),
            _resident(wuq.shape), _resident(wukv.shape),
            table, table,
        ],
        out_specs=[row(n) for n in widths],
        out_shape=[jax.ShapeDtypeStruct((nb, t, n), BF16) for n in widths],
        compiler_params=_cparams("parallel", "parallel"),
        name="mla_in_proj",
    )(x, mod, g.reshape(1, d), wd, qg.reshape(1, q_rank), kvg.reshape(1, kv_rank), wuq, wukv,
      cos2, sin2)


def _mla_attn_kernel(qn_ref, qr_ref, knc_ref, krc_ref, vc_ref, knl_ref, krl_ref, vl_ref,
                     o_ref, k_ref, v_ref):
    tc = knc_ref.shape[1]

    @pl.when(pl.program_id(2) == 0)
    def _():
        k_ref[:tc, :LANES] = knc_ref[0]
        k_ref[:tc, LANES:] = krc_ref[0]
        k_ref[tc:, :LANES] = knl_ref[0]
        k_ref[tc:, LANES:] = krl_ref[0]
        v_ref[:tc, :] = vc_ref[0]
        v_ref[tc:, :] = vl_ref[0]

    kv_list = [(k_ref[...], v_ref[...])]
    for rows in _row_chunks(qn_ref.shape[1], MLA_ROW_CHUNK):
        q = jnp.concatenate([qn_ref[0, rows, :], qr_ref[0, rows, :]], axis=1)
        o_ref[0, rows, :] = _softmax_pv(q, kv_list).astype(BF16)


def _mla_attn_call(qn, qr, kn, v, kr, b_lat, tc):
    nb, t, n_nope = qn.shape
    heads = n_nope // MLA_NOPE
    tq = min(1024, t)
    qspec = pl.BlockSpec((1, tq, LANES), lambda b, h, i: (b, i, h))
    qrspec = pl.BlockSpec((1, tq, LANES), lambda b, h, i: (b, i, h // 2))
    ctx = lambda f: pl.BlockSpec((1, tc, LANES), lambda b, h, i: (b_lat, b, f(h)))
    lat = lambda f: pl.BlockSpec((1, t, LANES), lambda b, h, i: (b, 0, f(h)))
    same = lambda h: h
    parity = lambda h: h % 2
    return pl.pallas_call(
        _mla_attn_kernel,
        grid=(b_lat, heads, t // tq),
        in_specs=[qspec, qrspec, ctx(same), ctx(parity), ctx(same), lat(same), lat(parity), lat(same)],
        out_specs=qspec,
        out_shape=jax.ShapeDtypeStruct((b_lat, t, n_nope), BF16),
        scratch_shapes=[pltpu.VMEM((tc + t, 2 * LANES), BF16), pltpu.VMEM((tc + t, LANES), BF16)],
        compiler_params=_cparams("parallel", "parallel", "arbitrary"),
        name="mla_attention",
    )(qn, qr, kn, kr, v, kn, kr, v)


def _mla_out_kernel(x_ref, mod_ref, att_ref, w_ref, o_ref):
    o_ref[0] = x_ref[0] + mod_ref[0, 5:6, :] * _dot(att_ref[0], w_ref[...])


def _mla_out_call(x, mod, att, w_o):
    nb, t, k = att.shape
    d = w_o.shape[1]
    tm = min(512, t)
    row = lambda n: pl.BlockSpec((1, tm, n), lambda b, i: (b, i, 0))
    return pl.pallas_call(
        _mla_out_kernel,
        grid=(nb, t // tm),
        in_specs=[row(d), pl.BlockSpec((1, N_MOD, d), lambda b, i: (b, 0, 0)), row(k),
                  _resident(w_o.shape)],
        out_specs=row(d),
        out_shape=jax.ShapeDtypeStruct((nb, t, d), F32),
        compiler_params=_cparams("parallel", "parallel"),
        name="mla_out_proj",
    )(x, mod, att, w_o)


def _rope_angles(n_tok, dim):
    n_rows = n_tok // GRID_W
    row = jnp.repeat(jnp.arange(n_rows), GRID_W).astype(F32)
    col = jnp.tile(jnp.arange(GRID_W), n_rows).astype(F32)
    half = dim // 2
    inv = 1.0 / (ROPE_THETA ** (jnp.arange(0, half, 2, dtype=F32) / half))
    return jnp.concatenate([row[:, None] * inv, col[:, None] * inv], axis=-1)


def _rope_tables(n_tok, dim):
    ang = _rope_angles(n_tok, dim)
    reps = LANES // dim
    cos = jnp.concatenate([jnp.cos(ang)] * (2 * reps), axis=-1)
    sin = jnp.concatenate([-jnp.sin(ang)] * reps + [jnp.sin(ang)] * reps, axis=-1)
    return (jnp.stack([cos, jnp.ones_like(cos)]), jnp.stack([sin, jnp.zeros_like(sin)]))


def _deinterleave(n):
    return np.concatenate([np.arange(0, n, 2), np.arange(1, n, 2)])


def kernel(x, c, ctx, c_ctx, mod_w, mod_b, norm_ffn1, norm_mix, norm_ffn2,
           ffn1_w_gate, ffn1_w_up, ffn1_w_down, ffn2_w_gate, ffn2_w_up, ffn2_w_down,
           hyb_w_in, hyb_conv_w, hyb_q_norm, hyb_k_norm, hyb_w_out,
           mla_w_down, mla_q_norm, mla_kv_norm, mla_w_uq, mla_w_ukv, mla_w_o, final_norm):
    b_lat, t, d = x.shape
    tc = ctx.shape[1]
    depth = mod_w.shape[0]
    assert depth == 2 and ctx.shape[0] * tc == t and b_lat + 1 <= MOD_ROWS
    assert t & (t - 1) == 0 and tc & (tc - 1) == 0 and t % GRID_W == 0
    nb = b_lat + 1

    cvec = jnp.concatenate([c, c_ctx[None], jnp.zeros((MOD_ROWS - nb, d), F32)], axis=0)
    mods = _mod_call(cvec, mod_w, mod_b).reshape(depth, MOD_ROWS, N_MOD, d)
    cast = lambda w: w.astype(BF16)
    f1 = [_cast_call(w) for w in (ffn1_w_gate, ffn1_w_up, ffn1_w_down)]
    f2 = [_cast_call(w) for w in (ffn2_w_gate, ffn2_w_up, ffn2_w_down)]

    mod = mods[0]
    xs = _ffn_call(x, b_lat, mod, norm_ffn1[0], *f1, 0, 0, out_nb=nb)
    xs = _ffn_call(ctx.reshape(1, t, d), 1, mod, norm_ffn1[0], *f1, 0, 0, row0=b_lat, out_nb=nb, fill=xs)
    cc = d // 2
    n_q = cc
    n_kv = GQA_KV_HEADS * HEAD_DIM
    w_in = hyb_w_in[0]
    perm = _deinterleave(HEAD_DIM)
    head_perm = lambda n: (np.arange(n).reshape(-1, HEAD_DIM)[:, perm]).reshape(-1)
    wq = w_in[:, 3 * cc:3 * cc + n_q][:, head_perm(n_q)]
    wk = w_in[:, 3 * cc + n_q:3 * cc + n_q + n_kv][:, head_perm(n_kv)]
    wv = w_in[:, 3 * cc + n_q + n_kv:]
    cos_a, sin_a = _rope_tables(t, HEAD_DIM)
    gb, z, q, k, v = _hyb_in_call(
        xs, mod, norm_mix[0], cast(w_in[:, :cc]), cast(w_in[:, cc:2 * cc]), cast(w_in[:, 2 * cc:3 * cc]),
        cast(wq), cast(wk), cast(wv), hyb_q_norm[0][perm], hyb_k_norm[0][perm], cos_a, sin_a)
    att = _gqa_latent_call(q, k, v, b_lat, tc)
    att = _gqa_ctx_call(q, k, v, att, b_lat, tc)
    xs = _hyb_out_call(xs, mod, gb, z, att, hyb_conv_w[0], cast(hyb_w_out[0]), b_lat, tc)
    xs = _ffn_call(xs, nb, mod, norm_ffn2[0], *f2, 0, 6)

    mod = mods[1]
    xs = _ffn_call(xs, nb, mod, norm_ffn1[1], *f1, 1, 0)
    heads = d // HEAD_DIM
    q_rank, kv_rank = mla_q_norm.shape[1], mla_kv_norm.shape[1]
    qk = MLA_NOPE + MLA_ROPE
    ev, od = np.arange(0, MLA_ROPE, 2), np.arange(1, MLA_ROPE, 2)
    nope_cols = (np.arange(heads)[:, None] * qk + np.arange(MLA_NOPE)[None]).reshape(-1)
    rope_cols = []
    for j in range(heads // 2):
        a0, b0 = 2 * j * qk + MLA_NOPE, (2 * j + 1) * qk + MLA_NOPE
        rope_cols += [a0 + ev, b0 + ev, a0 + od, b0 + od]
    wuq = mla_w_uq[0][:, np.concatenate([nope_cols] + rope_cols)]
    kvw = MLA_NOPE + MLA_V
    kn_cols = (np.arange(heads)[:, None] * kvw + np.arange(MLA_NOPE)[None]).reshape(-1)
    wukv = mla_w_ukv[0][:, np.concatenate([kn_cols, kn_cols + MLA_NOPE])]
    kr0 = q_rank + kv_rank
    wdn = mla_w_down[0][:, np.concatenate([np.arange(kr0), kr0 + ev, kr0 + ev, kr0 + od, kr0 + od])]
    cos_m, sin_m = _rope_tables(t, MLA_ROPE)
    qn, qr, kn, vv, kr = _mla_in_call(xs, mod, norm_mix[1], cast(wdn), mla_q_norm[0], mla_kv_norm[0],
                                      cast(wuq), cast(wukv), cos_m, sin_m)
    att = _mla_attn_call(qn, qr, kn, vv, kr, b_lat, tc)
    xs = _mla_out_call(xs, mod, att, cast(mla_w_o[0]))
    return _ffn_call(xs, b_lat, mod, norm_ffn2[1], *f2, 1, 6, final_g=final_norm)
```

```python
import functools

import numpy as np
import jax
import jax.numpy as jnp
from jax import lax
from jax.experimental import pallas as pl
from jax.experimental.pallas import tpu as pltpu

HEAD_DIM = 128
GRID_W = 64
ROPE_THETA = 10000.0
EPS = 1e-6
N_MOD = 9
GQA_KV_HEADS = 2
MLA_NOPE = 128
MLA_ROPE = 64
MLA_V = 128
LANES = 128
VMEM_LIMIT = 56 * 1024 * 1024
MOD_ROWS = 16
FFN_ROW_CHUNKS = 4
FFN_X_ADVANCE_STEPS = (2, 4)
FFN_ACC_TILE = 512
FFN_TWO_PHASE_TILE = 256
GQA_ROW_CHUNK = 512
MLA_ROW_CHUNK = 256
LOG2E = 1.4426950408889634
CAST_BLOCK_BYTES = 4 * 1024 * 1024

F32 = jnp.float32
BF16 = jnp.bfloat16


def _cparams(*sem):
    return pltpu.CompilerParams(dimension_semantics=sem, vmem_limit_bytes=VMEM_LIMIT)


def _resident(shape):
    nd = len(shape)
    return pl.BlockSpec(shape, lambda *_: (0,) * nd, pipeline_mode=pl.Buffered(1))


def _rms(x, g):
    return x * lax.rsqrt(jnp.mean(x * x, axis=-1, keepdims=True) + EPS) * g


def _modnorm(x, g, shift, scale):
    return _rms(x, g) * (1.0 + scale) + shift


def _rope(x, cos, sin):
    return x * cos + pltpu.roll(x, LANES // 2, axis=1) * sin


def _dot(a, b):
    return jnp.dot(a, b, preferred_element_type=F32)


def _dot_t(a, b):
    return lax.dot_general(a, b, (((1,), (1,)), ((), ())), preferred_element_type=F32)


def _mod_kernel(c_ref, w_ref, b_ref, o_ref):
    c = c_ref[...]
    s = (c * jax.nn.sigmoid(c)).astype(BF16)
    o_ref[0] = _dot(s, w_ref[0].astype(BF16)) + b_ref[0]


def _mod_call(cvec, mod_w, mod_b):
    depth, d, n = mod_w.shape
    tn = min(1024, n)
    return pl.pallas_call(
        _mod_kernel,
        grid=(depth, n // tn),
        in_specs=[
            pl.BlockSpec((MOD_ROWS, d), lambda l, j: (0, 0)),
            pl.BlockSpec((1, d, tn), lambda l, j: (l, 0, j)),
            pl.BlockSpec((1, 1, tn), lambda l, j: (l, 0, j)),
        ],
        out_specs=pl.BlockSpec((1, MOD_ROWS, tn), lambda l, j: (l, 0, j)),
        out_shape=jax.ShapeDtypeStruct((depth, MOD_ROWS, n), F32),
        compiler_params=_cparams("parallel", "parallel"),
        name="adaln_mod",
    )(cvec, mod_w, mod_b.reshape(depth, 1, n))


def _swiglu_act(xn, wg_ref, wu_ref):
    hg = _dot(xn, wg_ref[...])
    hu = _dot(xn, wu_ref[...])
    return (hg * jax.nn.sigmoid(hg) * hu).astype(BF16)


def _ffn_kernel(xlo_ref, xhi_ref, xt_ref, mod_ref, modt_ref, g_ref, wg_ref, wu_ref, wd_ref, *rest,
                m0, final, n_ff, filled):
    rest = list(rest)
    fn_ref = rest.pop(0) if final else None
    if filled:
        rest.pop(0)
    o_ref, xn_ref, h_ref = rest[:3]
    ss_ref = rest[3] if final else None
    j = pl.program_id(2)
    tm = xn_ref.shape[0]
    tf = wg_ref.shape[1]

    @pl.when(j == 0)
    def _():
        rs = tm // FFN_ROW_CHUNKS
        per_half = FFN_ROW_CHUNKS // 2
        for r in range(FFN_ROW_CHUNKS):
            src = (xlo_ref, xhi_ref)[r // per_half]
            lr = r % per_half
            rows = slice(r * rs, (r + 1) * rs)
            xn = _modnorm(src[0, lr * rs:(lr + 1) * rs, :], g_ref[...], mod_ref[0, m0:m0 + 1, :],
                          mod_ref[0, m0 + 1:m0 + 2, :]).astype(BF16)
            xn_ref[rows, :] = xn
            h_ref[rows, :tf] = _swiglu_act(xn, wg_ref, wu_ref)

    @pl.when((j > 0) & (j < n_ff))
    def _():
        col = pl.multiple_of(j * tf, tf)
        h_ref[:, pl.ds(col, tf)] = _swiglu_act(xn_ref[...], wg_ref, wu_ref)

    @pl.when(j >= n_ff)
    def _():
        y = xt_ref[0] + 0.5 * modt_ref[0, m0 + 2:m0 + 3, :] * _dot(h_ref[...], wd_ref[...])
        if final:
            part = jnp.sum(y * y, axis=-1, keepdims=True)
            ss_ref[...] = jnp.where(j == n_ff, part, ss_ref[...] + part)
            tn = wd_ref.shape[1]
            col = pl.multiple_of((j - n_ff) * tn, tn)
            o_ref[0, :, pl.ds(col, tn)] = y
        else:
            o_ref[0] = y

    if final:
        @pl.when(j == pl.num_programs(2) - 1)
        def _():
            d = o_ref.shape[2]
            o_ref[0] = o_ref[0] * lax.rsqrt(ss_ref[...] / d + EPS) * fn_ref[...]


def _x_half_spec(half, start, nb, nt, tm, d):
    def index(b, i, j):
        lin = jnp.minimum(b * nt + i + (j >= start).astype(jnp.int32), nb * nt - 1)
        return (lin // nt, 2 * (lin % nt) + half, 0)
    return pl.BlockSpec((1, tm // 2, d), index)


def _ffn_call(x, nb, mod, g, wg, wu, wd, layer, m0, final_g=None, row0=0, out_nb=None, fill=None):
    _, t, d = x.shape
    _, n_ff, _, tf = wg.shape
    _, n_out, dff, tn = wd.shape
    final = final_g is not None
    tm = min(512 if final else 1024, t)
    nt = t // tm
    up = lambda b, i, j: (layer, jnp.minimum(j, n_ff - 1), 0, 0)
    down = lambda j: jnp.maximum(j - n_ff, 0)

    in_specs = [
        _x_half_spec(0, FFN_X_ADVANCE_STEPS[0], nb, nt, tm, d),
        _x_half_spec(1, FFN_X_ADVANCE_STEPS[1], nb, nt, tm, d),
        pl.BlockSpec((1, tm, tn), lambda b, i, j: (b, i, down(j))),
        pl.BlockSpec((1, N_MOD, d), lambda b, i, j: (b + row0, 0, 0)),
        pl.BlockSpec((1, N_MOD, tn), lambda b, i, j: (b + row0, 0, down(j))),
        pl.BlockSpec((1, d), lambda b, i, j: (0, 0)),
        pl.BlockSpec((None, None, d, tf), up),
        pl.BlockSpec((None, None, d, tf), up),
        pl.BlockSpec((None, None, dff, tn), lambda b, i, j: (layer, down(j), 0, 0)),
    ]
    args = [x, x, x, mod, mod, g.reshape(1, d), wg, wu, wd]
    scratch = [pltpu.VMEM((tm, d), BF16), pltpu.VMEM((tm, dff), BF16)]
    if final:
        in_specs.append(pl.BlockSpec((1, d), lambda b, i, j: (0, 0)))
        args.append(final_g.reshape(1, d))
        out_spec = pl.BlockSpec((1, tm, d), lambda b, i, j: (b + row0, i, 0))
        scratch.append(pltpu.VMEM((tm, 1), F32))
    else:
        out_spec = pl.BlockSpec((1, tm, tn), lambda b, i, j: (b + row0, i, down(j)))
    aliases = {}
    if fill is not None:
        aliases = {len(args): 0}
        in_specs.append(pl.BlockSpec(memory_space=pl.ANY))
        args.append(fill)
    return pl.pallas_call(
        functools.partial(_ffn_kernel, m0=m0, final=final, n_ff=n_ff, filled=fill is not None),
        grid=(nb, t // tm, n_ff + n_out),
        in_specs=in_specs,
        out_specs=out_spec,
        out_shape=jax.ShapeDtypeStruct((out_nb or nb, t, d), F32),
        scratch_shapes=scratch,
        input_output_aliases=aliases,
        compiler_params=_cparams("parallel", "parallel", "arbitrary"),
        name="swiglu_two_phase",
    )(*args)


def _ffn_acc_kernel(xlo_ref, xhi_ref, mod_ref, g_ref, wg_ref, wu_ref, wd_ref, *rest,
                    m0, final, filled):
    rest = list(rest)
    fn_ref = rest.pop(0) if final else None
    if filled:
        rest.pop(0)
    o_ref, xn_ref = rest
    j = pl.program_id(2)
    tm = xn_ref.shape[0]
    half_gate = 0.5 * mod_ref[0, m0 + 2:m0 + 3, :]

    @pl.when(j == 0)
    def _():
        rs = tm // FFN_ROW_CHUNKS
        per_half = FFN_ROW_CHUNKS // 2
        for r in range(FFN_ROW_CHUNKS):
            src = (xlo_ref, xhi_ref)[r // per_half]
            lr = r % per_half
            rows = slice(r * rs, (r + 1) * rs)
            x = src[0, lr * rs:(lr + 1) * rs, :]
            xn = _modnorm(x, g_ref[...], mod_ref[0, m0:m0 + 1, :],
                          mod_ref[0, m0 + 1:m0 + 2, :]).astype(BF16)
            xn_ref[rows, :] = xn
            o_ref[0, rows, :] = x + half_gate * _dot(_swiglu_act(xn, wg_ref, wu_ref), wd_ref[...])

    @pl.when(j > 0)
    def _():
        o_ref[0] += half_gate * _dot(_swiglu_act(xn_ref[...], wg_ref, wu_ref), wd_ref[...])

    if final:
        @pl.when(j == pl.num_programs(2) - 1)
        def _():
            o_ref[0] = _rms(o_ref[0], fn_ref[...])


def _ffn_acc_call(x, nb, mod, g, wg, wu, wd, layer, m0, final_g=None, row0=0, out_nb=None,
                  fill=None):
    _, t, d = x.shape
    _, n_ff, _, tf = wg.shape
    final = final_g is not None
    tm = min(512, t)
    nt = t // tm
    in_specs = [
        _x_half_spec(0, FFN_X_ADVANCE_STEPS[0], nb, nt, tm, d),
        _x_half_spec(1, FFN_X_ADVANCE_STEPS[1], nb, nt, tm, d),
        pl.BlockSpec((1, N_MOD, d), lambda b, i, j: (b + row0, 0, 0)),
        pl.BlockSpec((1, d), lambda b, i, j: (0, 0)),
        pl.BlockSpec((None, None, d, tf), lambda b, i, j: (layer, j, 0, 0)),
        pl.BlockSpec((None, None, d, tf), lambda b, i, j: (layer, j, 0, 0)),
        pl.BlockSpec((None, None, tf, d), lambda b, i, j: (layer, 0, j, 0)),
    ]
    args = [x, x, mod, g.reshape(1, d), wg, wu, wd]
    if final:
        in_specs.append(pl.BlockSpec((1, d), lambda b, i, j: (0, 0)))
        args.append(final_g.reshape(1, d))
    aliases = {}
    if fill is not None:
        aliases = {len(args): 0}
        in_specs.append(pl.BlockSpec(memory_space=pl.ANY))
        args.append(fill)
    return pl.pallas_call(
        functools.partial(_ffn_acc_kernel, m0=m0, final=final, filled=fill is not None),
        grid=(nb, nt, n_ff),
        in_specs=in_specs,
        out_specs=pl.BlockSpec((1, tm, d), lambda b, i, j: (b + row0, i, 0)),
        out_shape=jax.ShapeDtypeStruct((out_nb or nb, t, d), F32),
        scratch_shapes=[pltpu.VMEM((tm, d), BF16)],
        input_output_aliases=aliases,
        compiler_params=_cparams("parallel", "parallel", "arbitrary"),
        name="swiglu_accumulate",
    )(*args)


def _cast_kernel(w_ref, o_ref):
    cb = o_ref.shape[3]
    for n in range(o_ref.shape[1]):
        o_ref[0, n] = w_ref[0, :, n * cb:(n + 1) * cb].astype(BF16)


def _cast_call(w, col_block):
    depth, r, c = w.shape
    nblk = c // col_block
    tr = max(n for n in range(16, r + 1, 16) if r % n == 0 and 4 * c * n <= CAST_BLOCK_BYTES)
    return pl.pallas_call(
        _cast_kernel,
        grid=(depth, r // tr),
        in_specs=[pl.BlockSpec((1, tr, c), lambda l, i: (l, i, 0))],
        out_specs=pl.BlockSpec((1, nblk, tr, col_block), lambda l, i: (l, 0, i, 0)),
        out_shape=jax.ShapeDtypeStruct((depth, nblk, r, col_block), BF16),
        compiler_params=_cparams("parallel", "parallel"),
        name="weight_cast",
    )(w)


def _hyb_in_kernel(x_ref, mod_ref, g_ref, wgb_ref, wgc_ref, wu_ref, wq_ref, wk_ref, wv_ref,
                   qg_ref, kg_ref, cos_ref, sin_ref,
                   gb_ref, z_ref, q_ref, k_ref, v_ref):
    xn = _modnorm(x_ref[0], g_ref[...], mod_ref[0, 3:4, :], mod_ref[0, 4:5, :]).astype(BF16)
    gb_ref[0] = _dot(xn, wgb_ref[...]).astype(BF16)
    z_ref[0] = (_dot(xn, wgc_ref[...]) * _dot(xn, wu_ref[...])).astype(BF16)
    v_ref[0] = _dot(xn, wv_ref[...]).astype(BF16)
    cos = cos_ref[0]
    sin = sin_ref[0]
    q = _dot(xn, wq_ref[...])
    q_scale = HEAD_DIM ** -0.5 * LOG2E
    for h in range(q.shape[1] // HEAD_DIM):
        sl = slice(h * HEAD_DIM, (h + 1) * HEAD_DIM)
        q_ref[0, :, sl] = (_rope(_rms(q[:, sl], qg_ref[...]), cos, sin) * q_scale).astype(BF16)
    k = _dot(xn, wk_ref[...])
    for h in range(k.shape[1] // HEAD_DIM):
        sl = slice(h * HEAD_DIM, (h + 1) * HEAD_DIM)
        k_ref[0, :, sl] = _rope(_rms(k[:, sl], kg_ref[...]), cos, sin).astype(BF16)


def _hyb_in_call(x, mod, g, wgb, wgc, wu, wq, wk, wv, qg, kg, cos2, sin2):
    nb, t, d = x.shape
    tm = min(512, t)
    cc, nq, nkv = wgb.shape[1], wq.shape[1], wk.shape[1]
    row = lambda n: pl.BlockSpec((1, tm, n), lambda b, i: (b, i, 0))
    table = pl.BlockSpec((1, tm, LANES), lambda b, i: (jnp.where(b == nb - 1, 1, 0), i, 0))
    return pl.pallas_call(
        _hyb_in_kernel,
        grid=(nb, t // tm),
        in_specs=[
            row(d),
            pl.BlockSpec((1, N_MOD, d), lambda b, i: (b, 0, 0)),
            _resident((1, d)),
            _resident(wgb.shape), _resident(wgc.shape), _resident(wu.shape),
            _resident(wq.shape), _resident(wk.shape), _resident(wv.shape),
            _resident((1, HEAD_DIM)), _resident((1, HEAD_DIM)),
            table, table,
        ],
        out_specs=[row(cc), row(cc), row(nq), row(nkv), row(nkv)],
        out_shape=[jax.ShapeDtypeStruct((nb, t, n), BF16) for n in (cc, cc, nq, nkv, nkv)],
        compiler_params=_cparams("parallel", "parallel"),
        name="hyb_in_proj",
    )(x, mod, g.reshape(1, d), wgb, wgc, wu, wq, wk, wv,
      qg.reshape(1, HEAD_DIM), kg.reshape(1, HEAD_DIM), cos2, sin2)


def _softmax_pv(q, kv_list):
    s_list = [_dot_t(q, k) for k, _ in kv_list]
    m = s_list[0].max(axis=-1, keepdims=True)
    for s in s_list[1:]:
        m = jnp.maximum(m, s.max(axis=-1, keepdims=True))
    l = 0.0
    o = 0.0
    for s, (_, v) in zip(s_list, kv_list):
        p = jnp.exp2(s - m)
        l = l + p.sum(axis=-1, keepdims=True)
        o = o + _dot(p.astype(BF16), v)
    return o / l


def _row_chunks(n, c):
    c = min(c, n)
    return [slice(r * c, (r + 1) * c) for r in range(n // c)]


def _gqa_kernel(*refs, has_lat):
    if has_lat:
        q_ref, kc_ref, vc_ref, kl_ref, vl_ref, o_ref = refs
    else:
        q_ref, kc_ref, vc_ref, o_ref = refs
    kv_list = [(kc_ref[0], vc_ref[0])]
    if has_lat:
        kv_list.append((kl_ref[0], vl_ref[0]))
    for h in range(q_ref.shape[2] // HEAD_DIM):
        sl = slice(h * HEAD_DIM, (h + 1) * HEAD_DIM)
        for rows in _row_chunks(q_ref.shape[1], GQA_ROW_CHUNK):
            o_ref[0, rows, sl] = _softmax_pv(q_ref[0, rows, sl], kv_list).astype(BF16)


def _gqa_latent_call(q, k, v, b_lat, tc):
    nb, t, nq = q.shape
    gw = nq // GQA_KV_HEADS
    tq = min(512, t)
    ctx = pl.BlockSpec((1, tc, HEAD_DIM), lambda b, g, i: (b_lat, b, g))
    lat = pl.BlockSpec((1, t, HEAD_DIM), lambda b, g, i: (b, 0, g))
    qo = pl.BlockSpec((1, tq, gw), lambda b, g, i: (b, i, g))
    return pl.pallas_call(
        functools.partial(_gqa_kernel, has_lat=True),
        grid=(b_lat, GQA_KV_HEADS, t // tq),
        in_specs=[qo, ctx, ctx, lat, lat],
        out_specs=qo,
        out_shape=jax.ShapeDtypeStruct((nb, t, nq), BF16),
        compiler_params=_cparams("parallel", "parallel", "parallel"),
        name="gqa_latent",
    )(q, k, v, k, v)


def _gqa_ctx_kernel(q_ref, kc_ref, vc_ref, att_hbm_ref, o_ref):
    del att_hbm_ref
    _gqa_kernel(q_ref, kc_ref, vc_ref, o_ref, has_lat=False)


def _gqa_ctx_call(q, k, v, att, b_lat, tc):
    nb, t, nq = q.shape
    gw = nq // GQA_KV_HEADS
    ctx = pl.BlockSpec((1, tc, HEAD_DIM), lambda b, g: (b_lat, b, g))
    qo = pl.BlockSpec((1, tc, gw), lambda b, g: (b_lat, b, g))
    return pl.pallas_call(
        _gqa_ctx_kernel,
        grid=(t // tc, GQA_KV_HEADS),
        in_specs=[qo, ctx, ctx, pl.BlockSpec(memory_space=pl.ANY)],
        out_specs=qo,
        out_shape=jax.ShapeDtypeStruct(att.shape, att.dtype),
        input_output_aliases={3: 0},
        compiler_params=_cparams("parallel", "parallel"),
        name="gqa_context",
    )(q, k, v, att)


def _hyb_out_kernel(x_ref, mod_ref, gb_ref, z_ref, zp_ref, zn_ref, att_ref, cw_ref, w_ref,
                    o_ref, *, seq_lat, seq_ctx, b_lat):
    b, i = pl.program_id(0), pl.program_id(1)
    tm, cc = z_ref.shape[1], z_ref.shape[2]
    acc = _dot(att_ref[0], w_ref[cc:, :])
    z = z_ref[0].astype(F32)
    rows = lax.broadcasted_iota(jnp.int32, (tm, cc), 0)
    seq = jnp.where(b == b_lat, seq_ctx, seq_lat)
    pos = (i * tm + rows) & (seq - 1)
    z_prev = jnp.where(rows == 0, zp_ref[0, 7:8, :].astype(F32), pltpu.roll(z, 1, axis=0))
    z_next = jnp.where(rows == tm - 1, zn_ref[0, 0:1, :].astype(F32),
                       pltpu.roll(z, tm - 1, axis=0))
    z_prev = jnp.where(pos == 0, 0.0, z_prev)
    z_next = jnp.where(pos == seq - 1, 0.0, z_next)
    conv = z_prev * cw_ref[0:1, :] + z * cw_ref[1:2, :] + z_next * cw_ref[2:3, :]
    acc = acc + _dot((gb_ref[0].astype(F32) * conv).astype(BF16), w_ref[:cc, :])
    o_ref[0] = x_ref[0] + mod_ref[0, 5:6, :] * acc


def _hyb_out_call(x, mod, gb, z, att, conv_w, w_out, b_lat, tc):
    nb, t, d = x.shape
    cc = gb.shape[2]
    tm = min(512, t)
    hb = tm // 8
    row = lambda n: pl.BlockSpec((1, tm, n), lambda b, i: (b, i, 0))
    return pl.pallas_call(
        functools.partial(_hyb_out_kernel, seq_lat=t, seq_ctx=tc, b_lat=b_lat),
        grid=(nb, t // tm),
        in_specs=[
            row(d),
            pl.BlockSpec((1, N_MOD, d), lambda b, i: (b, 0, 0)),
            row(cc), row(cc),
            pl.BlockSpec((1, 8, cc), lambda b, i: (b, jnp.maximum(i * hb - 1, 0), 0)),
            pl.BlockSpec((1, 8, cc), lambda b, i: (b, jnp.minimum((i + 1) * hb, t // 8 - 1), 0)),
            row(att.shape[2]),
            _resident(conv_w.shape),
            _resident(w_out.shape),
        ],
        out_specs=row(d),
        out_shape=jax.ShapeDtypeStruct((nb, t, d), F32),
        compiler_params=_cparams("parallel", "parallel"),
        name="hyb_out_proj",
    )(x, mod, gb, z, z, z, att, conv_w, w_out)


def _mla_in_kernel(x_ref, mod_ref, g_ref, wd_ref, qg_ref, kvg_ref, wuq_ref, wukv_ref,
                   cos_ref, sin_ref, qn_ref, qr_ref, kn_ref, v_ref, kr_ref, *, q_rank, kv_rank):
    xn = _modnorm(x_ref[0], g_ref[...], mod_ref[0, 3:4, :], mod_ref[0, 4:5, :]).astype(BF16)
    d = _dot(xn, wd_ref[...])
    cos = cos_ref[0]
    sin = sin_ref[0]
    n_nope = qn_ref.shape[2]
    q_scale = (MLA_NOPE + MLA_ROPE) ** -0.5 * LOG2E
    q = _dot(_rms(d[:, :q_rank], qg_ref[...]).astype(BF16), wuq_ref[...])
    qn_ref[0] = (q[:, :n_nope] * q_scale).astype(BF16)
    for j in range(qr_ref.shape[2] // LANES):
        qr_ref[0, :, j * LANES:(j + 1) * LANES] = (
            _rope(q[:, n_nope + j * LANES:n_nope + (j + 1) * LANES], cos, sin) * q_scale).astype(BF16)
    kv = _dot(_rms(d[:, q_rank:q_rank + kv_rank], kvg_ref[...]).astype(BF16), wukv_ref[...])
    kn_ref[0] = kv[:, :n_nope].astype(BF16)
    v_ref[0] = kv[:, n_nope:].astype(BF16)
    kr = _rope(d[:, q_rank + kv_rank:], cos, sin)
    first = (lax.broadcasted_iota(jnp.int32, kr.shape, 1) & (MLA_ROPE // 2)) == 0
    kr_ref[0, :, :LANES] = jnp.where(first, kr, 0.0).astype(BF16)
    kr_ref[0, :, LANES:] = jnp.where(first, 0.0, kr).astype(BF16)


def _mla_in_call(x, mod, g, wd, qg, kvg, wuq, wukv, cos2, sin2):
    nb, t, d = x.shape
    tm = min(256, t)
    q_rank, kv_rank = qg.shape[0], kvg.shape[0]
    n_nope = wukv.shape[1] // 2
    n_rope = wuq.shape[1] - n_nope
    row = lambda n: pl.BlockSpec((1, tm, n), lambda b, i: (b, i, 0))
    table = pl.BlockSpec((1, tm, LANES), lambda b, i: (jnp.where(b == nb - 1, 1, 0), i, 0))
    widths = (n_nope, n_rope, n_nope, n_nope, 2 * LANES)
    return pl.pallas_call(
        functools.partial(_mla_in_kernel, q_rank=q_rank, kv_rank=kv_rank),
        grid=(nb, t // tm),
        in_specs=[
            row(d),
            pl.BlockSpec((1, N_MOD, d), lambda b, i: (b, 0, 0)),
            _resident((1, d)),
            _resident(wd.shape), _resident((1, q_rank)), _resident((1, kv_rank)),
            _resident(wuq.shape), _resident(wukv.shape),
            table, table,
        ],
        out_specs=[row(n) for n in widths],
        out_shape=[jax.ShapeDtypeStruct((nb, t, n), BF16) for n in widths],
        compiler_params=_cparams("parallel", "parallel"),
        name="mla_in_proj",
    )(x, mod, g.reshape(1, d), wd, qg.reshape(1, q_rank), kvg.reshape(1, kv_rank), wuq, wukv,
      cos2, sin2)


def _mla_attn_kernel(qn_ref, qr_ref, knc_ref, krc_ref, vc_ref, knl_ref, krl_ref, vl_ref,
                     o_ref, k_ref, v_ref):
    tc = knc_ref.shape[1]

    @pl.when(pl.program_id(2) == 0)
    def _():
        k_ref[:tc, :LANES] = knc_ref[0]
        k_ref[:tc, LANES:] = krc_ref[0]
        k_ref[tc:, :LANES] = knl_ref[0]
        k_ref[tc:, LANES:] = krl_ref[0]
        v_ref[:tc, :] = vc_ref[0]
        v_ref[tc:, :] = vl_ref[0]

    kv_list = [(k_ref[...], v_ref[...])]
    for rows in _row_chunks(qn_ref.shape[1], MLA_ROW_CHUNK):
        q = jnp.concatenate([qn_ref[0, rows, :], qr_ref[0, rows, :]], axis=1)
        o_ref[0, rows, :] = _softmax_pv(q, kv_list).astype(BF16)


def _mla_attn_call(qn, qr, kn, v, kr, b_lat, tc):
    nb, t, n_nope = qn.shape
    heads = n_nope // MLA_NOPE
    tq = min(1024, t)
    qspec = pl.BlockSpec((1, tq, LANES), lambda b, h, i: (b, i, h))
    qrspec = pl.BlockSpec((1, tq, LANES), lambda b, h, i: (b, i, h // 2))
    ctx = lambda f: pl.BlockSpec((1, tc, LANES), lambda b, h, i: (b_lat, b, f(h)))
    lat = lambda f: pl.BlockSpec((1, t, LANES), lambda b, h, i: (b, 0, f(h)))
    same = lambda h: h
    parity = lambda h: h % 2
    return pl.pallas_call(
        _mla_attn_kernel,
        grid=(b_lat, heads, t // tq),
        in_specs=[qspec, qrspec, ctx(same), ctx(parity), ctx(same), lat(same), lat(parity), lat(same)],
        out_specs=qspec,
        out_shape=jax.ShapeDtypeStruct((b_lat, t, n_nope), BF16),
        scratch_shapes=[pltpu.VMEM((tc + t, 2 * LANES), BF16), pltpu.VMEM((tc + t, LANES), BF16)],
        compiler_params=_cparams("parallel", "parallel", "arbitrary"),
        name="mla_attention",
    )(qn, qr, kn, kr, v, kn, kr, v)


def _mla_out_kernel(x_ref, mod_ref, att_ref, w_ref, o_ref):
    o_ref[0] = x_ref[0] + mod_ref[0, 5:6, :] * _dot(att_ref[0], w_ref[...])


def _mla_out_call(x, mod, att, w_o):
    nb, t, k = att.shape
    d = w_o.shape[1]
    tm = min(512, t)
    row = lambda n: pl.BlockSpec((1, tm, n), lambda b, i: (b, i, 0))
    return pl.pallas_call(
        _mla_out_kernel,
        grid=(nb, t // tm),
        in_specs=[row(d), pl.BlockSpec((1, N_MOD, d), lambda b, i: (b, 0, 0)), row(k),
                  _resident(w_o.shape)],
        out_specs=row(d),
        out_shape=jax.ShapeDtypeStruct((nb, t, d), F32),
        compiler_params=_cparams("parallel", "parallel"),
        name="mla_out_proj",
    )(x, mod, att, w_o)


def _rope_angles(n_tok, dim):
    n_rows = n_tok // GRID_W
    row = jnp.repeat(jnp.arange(n_rows), GRID_W).astype(F32)
    col = jnp.tile(jnp.arange(GRID_W), n_rows).astype(F32)
    half = dim // 2
    inv = 1.0 / (ROPE_THETA ** (jnp.arange(0, half, 2, dtype=F32) / half))
    return jnp.concatenate([row[:, None] * inv, col[:, None] * inv], axis=-1)


def _rope_tables(n_tok, dim):
    ang = _rope_angles(n_tok, dim)
    reps = LANES // dim
    cos = jnp.concatenate([jnp.cos(ang)] * (2 * reps), axis=-1)
    sin = jnp.concatenate([-jnp.sin(ang)] * reps + [jnp.sin(ang)] * reps, axis=-1)
    return (jnp.stack([cos, jnp.ones_like(cos)]), jnp.stack([sin, jnp.zeros_like(sin)]))


def _deinterleave(n):
    return np.concatenate([np.arange(0, n, 2), np.arange(1, n, 2)])


def kernel(x, c, ctx, c_ctx, mod_w, mod_b, norm_ffn1, norm_mix, norm_ffn2,
           ffn1_w_gate, ffn1_w_up, ffn1_w_down, ffn2_w_gate, ffn2_w_up, ffn2_w_down,
           hyb_w_in, hyb_conv_w, hyb_q_norm, hyb_k_norm, hyb_w_out,
           mla_w_down, mla_q_norm, mla_kv_norm, mla_w_uq, mla_w_ukv, mla_w_o, final_norm):
    b_lat, t, d = x.shape
    tc = ctx.shape[1]
    depth = mod_w.shape[0]
    assert depth == 2 and ctx.shape[0] * tc == t and b_lat + 1 <= MOD_ROWS
    assert t & (t - 1) == 0 and tc & (tc - 1) == 0 and t % GRID_W == 0
    nb = b_lat + 1

    cvec = jnp.concatenate([c, c_ctx[None], jnp.zeros((MOD_ROWS - nb, d), F32)], axis=0)
    mods = _mod_call(cvec, mod_w, mod_b).reshape(depth, MOD_ROWS, N_MOD, d)
    cast = lambda w: w.astype(BF16)
    f1 = [_cast_call(ffn1_w_gate, FFN_ACC_TILE), _cast_call(ffn1_w_up, FFN_ACC_TILE),
          _cast_call(ffn1_w_down, d)]
    f2 = [_cast_call(w, FFN_TWO_PHASE_TILE) for w in (ffn2_w_gate, ffn2_w_up, ffn2_w_down)]

    mod = mods[0]
    xs = _ffn_acc_call(x, b_lat, mod, norm_ffn1[0], *f1, 0, 0, out_nb=nb)
    xs = _ffn_acc_call(ctx.reshape(1, t, d), 1, mod, norm_ffn1[0], *f1, 0, 0, row0=b_lat, out_nb=nb,
                       fill=xs)
    cc = d // 2
    n_q = cc
    n_kv = GQA_KV_HEADS * HEAD_DIM
    w_in = hyb_w_in[0]
    perm = _deinterleave(HEAD_DIM)
    head_perm = lambda n: (np.arange(n).reshape(-1, HEAD_DIM)[:, perm]).reshape(-1)
    wq = w_in[:, 3 * cc:3 * cc + n_q][:, head_perm(n_q)]
    wk = w_in[:, 3 * cc + n_q:3 * cc + n_q + n_kv][:, head_perm(n_kv)]
    wv = w_in[:, 3 * cc + n_q + n_kv:]
    cos_a, sin_a = _rope_tables(t, HEAD_DIM)
    gb, z, q, k, v = _hyb_in_call(
        xs, mod, norm_mix[0], cast(w_in[:, :cc]), cast(w_in[:, cc:2 * cc]), cast(w_in[:, 2 * cc:3 * cc]),
        cast(wq), cast(wk), cast(wv), hyb_q_norm[0][perm], hyb_k_norm[0][perm], cos_a, sin_a)
    att = _gqa_latent_call(q, k, v, b_lat, tc)
    att = _gqa_ctx_call(q, k, v, att, b_lat, tc)
    xs = _hyb_out_call(xs, mod, gb, z, att, hyb_conv_w[0], cast(hyb_w_out[0]), b_lat, tc)
    xs = _ffn_call(xs, nb, mod, norm_ffn2[0], *f2, 0, 6)

    mod = mods[1]
    xs = _ffn_acc_call(xs, nb, mod, norm_ffn1[1], *f1, 1, 0)
    heads = d // HEAD_DIM
    q_rank, kv_rank = mla_q_norm.shape[1], mla_kv_norm.shape[1]
    qk = MLA_NOPE + MLA_ROPE
    ev, od = np.arange(0, MLA_ROPE, 2), np.arange(1, MLA_ROPE, 2)
    nope_cols = (np.arange(heads)[:, None] * qk + np.arange(MLA_NOPE)[None]).reshape(-1)
    rope_cols = []
    for j in range(heads // 2):
        a0, b0 = 2 * j * qk + MLA_NOPE, (2 * j + 1) * qk + MLA_NOPE
        rope_cols += [a0 + ev, b0 + ev, a0 + od, b0 + od]
    wuq = mla_w_uq[0][:, np.concatenate([nope_cols] + rope_cols)]
    kvw = MLA_NOPE + MLA_V
    kn_cols = (np.arange(heads)[:, None] * kvw + np.arange(MLA_NOPE)[None]).reshape(-1)
    wukv = mla_w_ukv[0][:, np.concatenate([kn_cols, kn_cols + MLA_NOPE])]
    kr0 = q_rank + kv_rank
    wdn = mla_w_down[0][:, np.concatenate([np.arange(kr0), kr0 + ev, kr0 + ev, kr0 + od, kr0 + od])]
    cos_m, sin_m = _rope_tables(t, MLA_ROPE)
    qn, qr, kn, vv, kr = _mla_in_call(xs, mod, norm_mix[1], cast(wdn), mla_q_norm[0], mla_kv_norm[0],
                                      cast(wuq), cast(wukv), cos_m, sin_m)
    att = _mla_attn_call(qn, qr, kn, vv, kr, b_lat, tc)
    xs = _mla_out_call(xs, mod, att, cast(mla_w_o[0]))
    return _ffn_call(xs, b_lat, mod, norm_ffn2[1], *f2, 1, 6, final_g=final_norm)
```

```python
import functools

import numpy as np
import jax
import jax.numpy as jnp
from jax import lax
from jax.experimental import pallas as pl
from jax.experimental.pallas import tpu as pltpu

HEAD_DIM = 128
GRID_W = 64
ROPE_THETA = 10000.0
EPS = 1e-6
N_MOD = 9
GQA_KV_HEADS = 2
MLA_NOPE = 128
MLA_ROPE = 64
MLA_V = 128
LANES = 128
VMEM_LIMIT = 56 * 1024 * 1024
MOD_ROWS = 16
FFN_ROW_CHUNKS = 4
FFN_TILE = 512
GQA_ROW_CHUNK = 512
MLA_ROW_CHUNK = 256
LOG2E = 1.4426950408889634
CAST_BLOCK_BYTES = 4 * 1024 * 1024

F32 = jnp.float32
BF16 = jnp.bfloat16


def _cparams(*sem):
    return pltpu.CompilerParams(dimension_semantics=sem, vmem_limit_bytes=VMEM_LIMIT)


def _resident(shape):
    nd = len(shape)
    return pl.BlockSpec(shape, lambda *_: (0,) * nd, pipeline_mode=pl.Buffered(1))


def _rms(x, g):
    return x * lax.rsqrt(jnp.mean(x * x, axis=-1, keepdims=True) + EPS) * g


def _modnorm(x, g, shift, scale):
    return _rms(x, g) * (1.0 + scale) + shift


def _rope(x, cos, sin):
    return x * cos + pltpu.roll(x, LANES // 2, axis=1) * sin


def _dot(a, b):
    return jnp.dot(a, b, preferred_element_type=F32)


def _dot_t(a, b):
    return lax.dot_general(a, b, (((1,), (1,)), ((), ())), preferred_element_type=F32)


def _mod_kernel(c_ref, w_ref, b_ref, o_ref):
    c = c_ref[...]
    s = (c * jax.nn.sigmoid(c)).astype(BF16)
    o_ref[0] = _dot(s, w_ref[0].astype(BF16)) + b_ref[0]


def _mod_call(cvec, mod_w, mod_b):
    depth, d, n = mod_w.shape
    tn = min(1024, n)
    return pl.pallas_call(
        _mod_kernel,
        grid=(depth, n // tn),
        in_specs=[
            pl.BlockSpec((MOD_ROWS, d), lambda l, j: (0, 0)),
            pl.BlockSpec((1, d, tn), lambda l, j: (l, 0, j)),
            pl.BlockSpec((1, 1, tn), lambda l, j: (l, 0, j)),
        ],
        out_specs=pl.BlockSpec((1, MOD_ROWS, tn), lambda l, j: (l, 0, j)),
        out_shape=jax.ShapeDtypeStruct((depth, MOD_ROWS, n), F32),
        compiler_params=_cparams("parallel", "parallel"),
        name="adaln_mod",
    )(cvec, mod_w, mod_b.reshape(depth, 1, n))


def _swiglu_act(xn, wg_ref, wu_ref):
    hg = _dot(xn, wg_ref[...])
    hu = _dot(xn, wu_ref[...])
    return (hg * jax.nn.sigmoid(hg) * hu).astype(BF16)


def _ffn_kernel(x_ref, mod_ref, g_ref, wg_hbm, wu_hbm, wd_hbm, *rest, layer, m0, final, with_ctx):
    rest = list(rest)
    fn_ref = rest.pop(0) if final else None
    ctx_ref = rest.pop(0) if with_ctx else None
    o_ref, xn_ref, wg_buf, wu_buf, wd_buf, sem = rest
    is_ctx = pl.program_id(0) == pl.num_programs(0) - 1
    n_ff = wg_hbm.shape[1]
    tm = xn_ref.shape[0]
    step = pl.program_id(0) * pl.num_programs(1) + pl.program_id(1)
    backward = step % 2 == 1
    half_gate = 0.5 * mod_ref[0, m0 + 2:m0 + 3, :]

    def copies(j, slot):
        blk = jnp.where(backward, n_ff - 1 - j, j)
        pairs = ((wg_hbm, wg_buf), (wu_hbm, wu_buf), (wd_hbm, wd_buf))
        return [pltpu.make_async_copy(hbm.at[layer, blk], buf.at[slot], sem.at[k, slot])
                for k, (hbm, buf) in enumerate(pairs)]

    @pl.when(step == 0)
    def _():
        for c in copies(0, 0):
            c.start()
        for c in copies(0, 0):
            c.wait()

    for j in range(n_ff):
        slot = j % 2
        if j + 1 < n_ff:
            for c in copies(j + 1, 1 - slot):
                c.start()
        if j > 0:
            for c in copies(j, slot):
                c.wait()
        wg, wu, wd = wg_buf.at[slot], wu_buf.at[slot], wd_buf.at[slot]
        if j == 0:
            rs = tm // FFN_ROW_CHUNKS
            for r in range(FFN_ROW_CHUNKS):
                rows = slice(r * rs, (r + 1) * rs)
                x = x_ref[0, rows, :]
                if with_ctx:
                    x = jnp.where(is_ctx, ctx_ref[0, rows, :], x)
                xn = _modnorm(x, g_ref[...], mod_ref[0, m0:m0 + 1, :],
                              mod_ref[0, m0 + 1:m0 + 2, :]).astype(BF16)
                xn_ref[rows, :] = xn
                o_ref[0, rows, :] = x + half_gate * _dot(_swiglu_act(xn, wg, wu), wd[...])
        else:
            o_ref[0] += half_gate * _dot(_swiglu_act(xn_ref[...], wg, wu), wd[...])

    if final:
        o_ref[0] = _rms(o_ref[0], fn_ref[...])


def _ffn_call(x, nb, mod, g, wg, wu, wd, layer, m0, final_g=None, ctx=None):
    _, t, d = x.shape
    _, n_ff, _, tf = wg.shape
    assert n_ff % 2 == 1
    final = final_g is not None
    tm = min(512, t)
    nt = t // tm
    if ctx is None:
        x_spec = pl.BlockSpec((1, tm, d), lambda b, i: (b, i, 0))
    else:
        x_spec = pl.BlockSpec((1, tm, d), lambda b, i: (jnp.minimum(b, nb - 2),
                                                         jnp.where(b < nb - 1, i, nt - 1), 0))
    in_specs = [
        x_spec,
        pl.BlockSpec((1, N_MOD, d), lambda b, i: (b, 0, 0)),
        pl.BlockSpec((1, d), lambda b, i: (0, 0)),
        pl.BlockSpec(memory_space=pl.ANY),
        pl.BlockSpec(memory_space=pl.ANY),
        pl.BlockSpec(memory_space=pl.ANY),
    ]
    args = [x, mod, g.reshape(1, d), wg, wu, wd]
    if final:
        in_specs.append(pl.BlockSpec((1, d), lambda b, i: (0, 0)))
        args.append(final_g.reshape(1, d))
    if ctx is not None:
        in_specs.append(pl.BlockSpec((1, tm, d), lambda b, i: (0, jnp.where(b < nb - 1, 0, i), 0)))
        args.append(ctx)
    return pl.pallas_call(
        functools.partial(_ffn_kernel, layer=layer, m0=m0, final=final, with_ctx=ctx is not None),
        grid=(nb, nt),
        in_specs=in_specs,
        out_specs=pl.BlockSpec((1, tm, d), lambda b, i: (b, i, 0)),
        out_shape=jax.ShapeDtypeStruct((nb, t, d), F32),
        scratch_shapes=[
            pltpu.VMEM((tm, d), BF16),
            pltpu.VMEM((2, d, tf), BF16), pltpu.VMEM((2, d, tf), BF16), pltpu.VMEM((2, tf, d), BF16),
            pltpu.SemaphoreType.DMA((3, 2)),
        ],
        compiler_params=_cparams("arbitrary", "arbitrary"),
        name="swiglu_half_step",
    )(*args)


def _cast_kernel(w_ref, o_ref):
    cb = o_ref.shape[3]
    for n in range(o_ref.shape[1]):
        o_ref[0, n] = w_ref[0, :, n * cb:(n + 1) * cb].astype(BF16)


def _cast_call(w, col_block):
    depth, r, c = w.shape
    nblk = c // col_block
    tr = max(n for n in range(16, r + 1, 16) if r % n == 0 and 4 * c * n <= CAST_BLOCK_BYTES)
    return pl.pallas_call(
        _cast_kernel,
        grid=(depth, r // tr),
        in_specs=[pl.BlockSpec((1, tr, c), lambda l, i: (l, i, 0))],
        out_specs=pl.BlockSpec((1, nblk, tr, col_block), lambda l, i: (l, 0, i, 0)),
        out_shape=jax.ShapeDtypeStruct((depth, nblk, r, col_block), BF16),
        compiler_params=_cparams("parallel", "parallel"),
        name="weight_cast",
    )(w)


def _ffn_weights(w_gate, w_up, w_down):
    depth, d, dff = w_gate.shape
    tf = min(FFN_TILE, dff)
    return (_cast_call(w_gate, tf), _cast_call(w_up, tf),
            _cast_call(w_down, d).reshape(depth, dff // tf, tf, d))


def _hyb_in_kernel(x_ref, mod_ref, g_ref, wgb_ref, wgc_ref, wu_ref, wq_ref, wk_ref, wv_ref,
                   qg_ref, kg_ref, cos_ref, sin_ref,
                   gb_ref, z_ref, q_ref, k_ref, v_ref):
    xn = _modnorm(x_ref[0], g_ref[...], mod_ref[0, 3:4, :], mod_ref[0, 4:5, :]).astype(BF16)
    gb_ref[0] = _dot(xn, wgb_ref[...]).astype(BF16)
    z_ref[0] = (_dot(xn, wgc_ref[...]) * _dot(xn, wu_ref[...])).astype(BF16)
    v_ref[0] = _dot(xn, wv_ref[...]).astype(BF16)
    cos = cos_ref[0]
    sin = sin_ref[0]
    q = _dot(xn, wq_ref[...])
    q_scale = HEAD_DIM ** -0.5 * LOG2E
    for h in range(q.shape[1] // HEAD_DIM):
        sl = slice(h * HEAD_DIM, (h + 1) * HEAD_DIM)
        q_ref[0, :, sl] = (_rope(_rms(q[:, sl], qg_ref[...]), cos, sin) * q_scale).astype(BF16)
    k = _dot(xn, wk_ref[...])
    for h in range(k.shape[1] // HEAD_DIM):
        sl = slice(h * HEAD_DIM, (h + 1) * HEAD_DIM)
        k_ref[0, :, sl] = _rope(_rms(k[:, sl], kg_ref[...]), cos, sin).astype(BF16)


def _hyb_in_call(x, mod, g, wgb, wgc, wu, wq, wk, wv, qg, kg, cos2, sin2):
    nb, t, d = x.shape
    tm = min(512, t)
    cc, nq, nkv = wgb.shape[1], wq.shape[1], wk.shape[1]
    row = lambda n: pl.BlockSpec((1, tm, n), lambda b, i: (b, i, 0))
    table = pl.BlockSpec((1, tm, LANES), lambda b, i: (jnp.where(b == nb - 1, 1, 0), i, 0))
    return pl.pallas_call(
        _hyb_in_kernel,
        grid=(nb, t // tm),
        in_specs=[
            row(d),
            pl.BlockSpec((1, N_MOD, d), lambda b, i: (b, 0, 0)),
            _resident((1, d)),
            _resident(wgb.shape), _resident(wgc.shape), _resident(wu.shape),
            _resident(wq.shape), _resident(wk.shape), _resident(wv.shape),
            _resident((1, HEAD_DIM)), _resident((1, HEAD_DIM)),
            table, table,
        ],
        out_specs=[row(cc), row(cc), row(nq), row(nkv), row(nkv)],
        out_shape=[jax.ShapeDtypeStruct((nb, t, n), BF16) for n in (cc, cc, nq, nkv, nkv)],
        compiler_params=_cparams("parallel", "parallel"),
        name="hyb_in_proj",
    )(x, mod, g.reshape(1, d), wgb, wgc, wu, wq, wk, wv,
      qg.reshape(1, HEAD_DIM), kg.reshape(1, HEAD_DIM), cos2, sin2)


def _softmax_pv(q, kv_list):
    s_list = [_dot_t(q, k) for k, _ in kv_list]
    m = s_list[0].max(axis=-1, keepdims=True)
    for s in s_list[1:]:
        m = jnp.maximum(m, s.max(axis=-1, keepdims=True))
    l = 0.0
    o = 0.0
    for s, (_, v) in zip(s_list, kv_list):
        p = jnp.exp2(s - m)
        l = l + p.sum(axis=-1, keepdims=True)
        o = o + _dot(p.astype(BF16), v)
    return o / l


def _row_chunks(n, c):
    c = min(c, n)
    return [slice(r * c, (r + 1) * c) for r in range(n // c)]


def _gqa_kernel(*refs, has_lat):
    if has_lat:
        q_ref, kc_ref, vc_ref, kl_ref, vl_ref, o_ref = refs
    else:
        q_ref, kc_ref, vc_ref, o_ref = refs
    kv_list = [(kc_ref[0], vc_ref[0])]
    if has_lat:
        kv_list.append((kl_ref[0], vl_ref[0]))
    for h in range(q_ref.shape[2] // HEAD_DIM):
        sl = slice(h * HEAD_DIM, (h + 1) * HEAD_DIM)
        for rows in _row_chunks(q_ref.shape[1], GQA_ROW_CHUNK):
            o_ref[0, rows, sl] = _softmax_pv(q_ref[0, rows, sl], kv_list).astype(BF16)


def _gqa_latent_call(q, k, v, b_lat, tc):
    nb, t, nq = q.shape
    gw = nq // GQA_KV_HEADS
    tq = min(512, t)
    ctx = pl.BlockSpec((1, tc, HEAD_DIM), lambda b, g, i: (b_lat, b, g))
    lat = pl.BlockSpec((1, t, HEAD_DIM), lambda b, g, i: (b, 0, g))
    qo = pl.BlockSpec((1, tq, gw), lambda b, g, i: (b, i, g))
    return pl.pallas_call(
        functools.partial(_gqa_kernel, has_lat=True),
        grid=(b_lat, GQA_KV_HEADS, t // tq),
        in_specs=[qo, ctx, ctx, lat, lat],
        out_specs=qo,
        out_shape=jax.ShapeDtypeStruct((b_lat, t, nq), BF16),
        compiler_params=_cparams("parallel", "parallel", "parallel"),
        name="gqa_latent",
    )(q, k, v, k, v)


def _gqa_ctx_call(q, k, v, b_lat, tc):
    nb, t, nq = q.shape
    gw = nq // GQA_KV_HEADS
    ctx = pl.BlockSpec((1, tc, HEAD_DIM), lambda b, g: (b_lat, b, g))
    return pl.pallas_call(
        functools.partial(_gqa_kernel, has_lat=False),
        grid=(t // tc, GQA_KV_HEADS),
        in_specs=[pl.BlockSpec((1, tc, gw), lambda b, g: (b_lat, b, g)), ctx, ctx],
        out_specs=pl.BlockSpec((1, tc, gw), lambda b, g: (0, b, g)),
        out_shape=jax.ShapeDtypeStruct((1, t, nq), BF16),
        compiler_params=_cparams("parallel", "parallel"),
        name="gqa_context",
    )(q, k, v)


def _hyb_out_kernel(x_ref, mod_ref, gb_ref, z_ref, zp_ref, zn_ref, att_ref, attc_ref, cw_ref, w_ref,
                    o_ref, *, seq_lat, seq_ctx, b_lat):
    b, i = pl.program_id(0), pl.program_id(1)
    tm, cc = z_ref.shape[1], z_ref.shape[2]
    acc = _dot(jnp.where(b == b_lat, attc_ref[0], att_ref[0]), w_ref[cc:, :])
    z = z_ref[0].astype(F32)
    rows = lax.broadcasted_iota(jnp.int32, (tm, cc), 0)
    seq = jnp.where(b == b_lat, seq_ctx, seq_lat)
    pos = (i * tm + rows) & (seq - 1)
    z_prev = jnp.where(rows == 0, zp_ref[0, 7:8, :].astype(F32), pltpu.roll(z, 1, axis=0))
    z_next = jnp.where(rows == tm - 1, zn_ref[0, 0:1, :].astype(F32),
                       pltpu.roll(z, tm - 1, axis=0))
    z_prev = jnp.where(pos == 0, 0.0, z_prev)
    z_next = jnp.where(pos == seq - 1, 0.0, z_next)
    conv = z_prev * cw_ref[0:1, :] + z * cw_ref[1:2, :] + z_next * cw_ref[2:3, :]
    acc = acc + _dot((gb_ref[0].astype(F32) * conv).astype(BF16), w_ref[:cc, :])
    o_ref[0] = x_ref[0] + mod_ref[0, 5:6, :] * acc


def _hyb_out_call(x, mod, gb, z, att, att_c, conv_w, w_out, b_lat, tc):
    nb, t, d = x.shape
    cc = gb.shape[2]
    tm = min(512, t)
    nt = t // tm
    hb = tm // 8
    row = lambda n: pl.BlockSpec((1, tm, n), lambda b, i: (b, i, 0))
    nq = att.shape[2]
    att_spec = pl.BlockSpec((1, tm, nq), lambda b, i: (jnp.minimum(b, b_lat - 1),
                                                       jnp.where(b < b_lat, i, nt - 1), 0))
    attc_spec = pl.BlockSpec((1, tm, nq), lambda b, i: (0, jnp.where(b < b_lat, 0, i), 0))
    return pl.pallas_call(
        functools.partial(_hyb_out_kernel, seq_lat=t, seq_ctx=tc, b_lat=b_lat),
        grid=(nb, t // tm),
        in_specs=[
            row(d),
            pl.BlockSpec((1, N_MOD, d), lambda b, i: (b, 0, 0)),
            row(cc), row(cc),
            pl.BlockSpec((1, 8, cc), lambda b, i: (b, jnp.maximum(i * hb - 1, 0), 0)),
            pl.BlockSpec((1, 8, cc), lambda b, i: (b, jnp.minimum((i + 1) * hb, t // 8 - 1), 0)),
            att_spec, attc_spec,
            _resident(conv_w.shape),
            _resident(w_out.shape),
        ],
        out_specs=row(d),
        out_shape=jax.ShapeDtypeStruct((nb, t, d), F32),
        compiler_params=_cparams("parallel", "parallel"),
        name="hyb_out_proj",
    )(x, mod, gb, z, z, z, att, att_c, conv_w, w_out)


def _mla_in_kernel(x_ref, mod_ref, g_ref, wd_ref, qg_ref, kvg_ref, wuq_ref, wukv_ref,
                   cos_ref, sin_ref, qn_ref, qr_ref, kn_ref, v_ref, kr_ref, *, q_rank, kv_rank):
    xn = _modnorm(x_ref[0], g_ref[...], mod_ref[0, 3:4, :], mod_ref[0, 4:5, :]).astype(BF16)
    d = _dot(xn, wd_ref[...])
    cos = cos_ref[0]
    sin = sin_ref[0]
    n_nope = qn_ref.shape[2]
    q_scale = (MLA_NOPE + MLA_ROPE) ** -0.5 * LOG2E
    q = _dot(_rms(d[:, :q_rank], qg_ref[...]).astype(BF16), wuq_ref[...])
    qn_ref[0] = (q[:, :n_nope] * q_scale).astype(BF16)
    for j in range(qr_ref.shape[2] // LANES):
        qr_ref[0, :, j * LANES:(j + 1) * LANES] = (
            _rope(q[:, n_nope + j * LANES:n_nope + (j + 1) * LANES], cos, sin) * q_scale).astype(BF16)
    kv = _dot(_rms(d[:, q_rank:q_rank + kv_rank], kvg_ref[...]).astype(BF16), wukv_ref[...])
    kn_ref[0] = kv[:, :n_nope].astype(BF16)
    v_ref[0] = kv[:, n_nope:].astype(BF16)
    kr = _rope(d[:, q_rank + kv_rank:], cos, sin)
    first = (lax.broadcasted_iota(jnp.int32, kr.shape, 1) & (MLA_ROPE // 2)) == 0
    kr_ref[0, :, :LANES] = jnp.where(first, kr, 0.0).astype(BF16)
    kr_ref[0, :, LANES:] = jnp.where(first, 0.0, kr).astype(BF16)


def _mla_in_call(x, mod, g, wd, qg, kvg, wuq, wukv, cos2, sin2):
    nb, t, d = x.shape
    tm = min(256, t)
    q_rank, kv_rank = qg.shape[0], kvg.shape[0]
    n_nope = wukv.shape[1] // 2
    n_rope = wuq.shape[1] - n_nope
    row = lambda n: pl.BlockSpec((1, tm, n), lambda b, i: (b, i, 0))
    table = pl.BlockSpec((1, tm, LANES), lambda b, i: (jnp.where(b == nb - 1, 1, 0), i, 0))
    widths = (n_nope, n_rope, n_nope, n_nope, 2 * LANES)
    return pl.pallas_call(
        functools.partial(_mla_in_kernel, q_rank=q_rank, kv_rank=kv_rank),
        grid=(nb, t // tm),
        in_specs=[
            row(d),
            pl.BlockSpec((1, N_MOD, d), lambda b, i: (b, 0, 0)),
            _resident((1, d)),
            _resident(wd.shape), _resident((1, q_rank)), _resident((1, kv_rank)),
            _resident(wuq.shape), _resident(wukv.shape),
            table, table,
        ],
        out_specs=[row(n) for n in widths],
        out_shape=[jax.ShapeDtypeStruct((nb, t, n), BF16) for n in widths],
        compiler_params=_cparams("parallel", "parallel"),
        name="mla_in_proj",
    )(x, mod, g.reshape(1, d), wd, qg.reshape(1, q_rank), kvg.reshape(1, kv_rank), wuq, wukv,
      cos2, sin2)


def _mla_attn_kernel(qn_ref, qr_ref, knc_ref, krc_ref, vc_ref, knl_ref, krl_ref, vl_ref,
                     o_ref, k_ref, v_ref):
    tc = knc_ref.shape[1]

    @pl.when(pl.program_id(2) == 0)
    def _():
        k_ref[:tc, :LANES] = knc_ref[0]
        k_ref[:tc, LANES:] = krc_ref[0]
        k_ref[tc:, :LANES] = knl_ref[0]
        k_ref[tc:, LANES:] = krl_ref[0]
        v_ref[:tc, :] = vc_ref[0]
        v_ref[tc:, :] = vl_ref[0]

    kv_list = [(k_ref[...], v_ref[...])]
    for rows in _row_chunks(qn_ref.shape[1], MLA_ROW_CHUNK):
        q = jnp.concatenate([qn_ref[0, rows, :], qr_ref[0, rows, :]], axis=1)
        o_ref[0, rows, :] = _softmax_pv(q, kv_list).astype(BF16)


def _mla_attn_call(qn, qr, kn, v, kr, b_lat, tc):
    nb, t, n_nope = qn.shape
    heads = n_nope // MLA_NOPE
    tq = min(2048, t)
    qspec = pl.BlockSpec((1, tq, LANES), lambda b, h, i: (b, i, h))
    qrspec = pl.BlockSpec((1, tq, LANES), lambda b, h, i: (b, i, h // 2))
    ctx = lambda f: pl.BlockSpec((1, tc, LANES), lambda b, h, i: (b_lat, b, f(h)))
    lat = lambda f: pl.BlockSpec((1, t, LANES), lambda b, h, i: (b, 0, f(h)))
    same = lambda h: h
    parity = lambda h: h % 2
    return pl.pallas_call(
        _mla_attn_kernel,
        grid=(b_lat, heads, t // tq),
        in_specs=[qspec, qrspec, ctx(same), ctx(parity), ctx(same), lat(same), lat(parity), lat(same)],
        out_specs=qspec,
        out_shape=jax.ShapeDtypeStruct((b_lat, t, n_nope), BF16),
        scratch_shapes=[pltpu.VMEM((tc + t, 2 * LANES), BF16), pltpu.VMEM((tc + t, LANES), BF16)],
        compiler_params=_cparams("parallel", "parallel", "arbitrary"),
        name="mla_attention",
    )(qn, qr, kn, kr, v, kn, kr, v)


def _mla_out_kernel(x_ref, mod_ref, att_ref, w_ref, o_ref):
    o_ref[0] = x_ref[0] + mod_ref[0, 5:6, :] * _dot(att_ref[0], w_ref[...])


def _mla_out_call(x, mod, att, w_o):
    nb, t, k = att.shape
    d = w_o.shape[1]
    tm = min(512, t)
    row = lambda n: pl.BlockSpec((1, tm, n), lambda b, i: (b, i, 0))
    return pl.pallas_call(
        _mla_out_kernel,
        grid=(nb, t // tm),
        in_specs=[row(d), pl.BlockSpec((1, N_MOD, d), lambda b, i: (b, 0, 0)), row(k),
                  _resident(w_o.shape)],
        out_specs=row(d),
        out_shape=jax.ShapeDtypeStruct((nb, t, d), F32),
        compiler_params=_cparams("parallel", "parallel"),
        name="mla_out_proj",
    )(x, mod, att, w_o)


def _rope_angles(n_tok, dim):
    n_rows = n_tok // GRID_W
    row = jnp.repeat(jnp.arange(n_rows), GRID_W).astype(F32)
    col = jnp.tile(jnp.arange(GRID_W), n_rows).astype(F32)
    half = dim // 2
    inv = 1.0 / (ROPE_THETA ** (jnp.arange(0, half, 2, dtype=F32) / half))
    return jnp.concatenate([row[:, None] * inv, col[:, None] * inv], axis=-1)


def _rope_tables(n_tok, dim):
    ang = _rope_angles(n_tok, dim)
    reps = LANES // dim
    cos = jnp.concatenate([jnp.cos(ang)] * (2 * reps), axis=-1)
    sin = jnp.concatenate([-jnp.sin(ang)] * reps + [jnp.sin(ang)] * reps, axis=-1)
    return (jnp.stack([cos, jnp.ones_like(cos)]), jnp.stack([sin, jnp.zeros_like(sin)]))


def _deinterleave(n):
    return np.concatenate([np.arange(0, n, 2), np.arange(1, n, 2)])


def kernel(x, c, ctx, c_ctx, mod_w, mod_b, norm_ffn1, norm_mix, norm_ffn2,
           ffn1_w_gate, ffn1_w_up, ffn1_w_down, ffn2_w_gate, ffn2_w_up, ffn2_w_down,
           hyb_w_in, hyb_conv_w, hyb_q_norm, hyb_k_norm, hyb_w_out,
           mla_w_down, mla_q_norm, mla_kv_norm, mla_w_uq, mla_w_ukv, mla_w_o, final_norm):
    b_lat, t, d = x.shape
    tc = ctx.shape[1]
    depth = mod_w.shape[0]
    assert depth == 2 and ctx.shape[0] * tc == t and b_lat + 1 <= MOD_ROWS
    assert t & (t - 1) == 0 and tc & (tc - 1) == 0 and t % GRID_W == 0
    nb = b_lat + 1

    cvec = jnp.concatenate([c, c_ctx[None], jnp.zeros((MOD_ROWS - nb, d), F32)], axis=0)
    mods = _mod_call(cvec, mod_w, mod_b).reshape(depth, MOD_ROWS, N_MOD, d)
    cast = lambda w: w.astype(BF16)
    f1 = _ffn_weights(ffn1_w_gate, ffn1_w_up, ffn1_w_down)
    f2 = _ffn_weights(ffn2_w_gate, ffn2_w_up, ffn2_w_down)

    mod = mods[0]
    xs = _ffn_call(x, nb, mod, norm_ffn1[0], *f1, 0, 0, ctx=ctx.reshape(1, t, d))
    cc = d // 2
    n_q = cc
    n_kv = GQA_KV_HEADS * HEAD_DIM
    w_in = hyb_w_in[0]
    perm = _deinterleave(HEAD_DIM)
    head_perm = lambda n: (np.arange(n).reshape(-1, HEAD_DIM)[:, perm]).reshape(-1)
    wq = w_in[:, 3 * cc:3 * cc + n_q][:, head_perm(n_q)]
    wk = w_in[:, 3 * cc + n_q:3 * cc + n_q + n_kv][:, head_perm(n_kv)]
    wv = w_in[:, 3 * cc + n_q + n_kv:]
    cos_a, sin_a = _rope_tables(t, HEAD_DIM)
    gb, z, q, k, v = _hyb_in_call(
        xs, mod, norm_mix[0], cast(w_in[:, :cc]), cast(w_in[:, cc:2 * cc]), cast(w_in[:, 2 * cc:3 * cc]),
        cast(wq), cast(wk), cast(wv), hyb_q_norm[0][perm], hyb_k_norm[0][perm], cos_a, sin_a)
    att = _gqa_latent_call(q, k, v, b_lat, tc)
    att_c = _gqa_ctx_call(q, k, v, b_lat, tc)
    xs = _hyb_out_call(xs, mod, gb, z, att, att_c, hyb_conv_w[0], cast(hyb_w_out[0]), b_lat, tc)
    xs = _ffn_call(xs, nb, mod, norm_ffn2[0], *f2, 0, 6)

    mod = mods[1]
    xs = _ffn_call(xs, nb, mod, norm_ffn1[1], *f1, 1, 0)
    heads = d // HEAD_DIM
    q_rank, kv_rank = mla_q_norm.shape[1], mla_kv_norm.shape[1]
    qk = MLA_NOPE + MLA_ROPE
    ev, od = np.arange(0, MLA_ROPE, 2), np.arange(1, MLA_ROPE, 2)
    nope_cols = (np.arange(heads)[:, None] * qk + np.arange(MLA_NOPE)[None]).reshape(-1)
    rope_cols = []
    for j in range(heads // 2):
        a0, b0 = 2 * j * qk + MLA_NOPE, (2 * j + 1) * qk + MLA_NOPE
        rope_cols += [a0 + ev, b0 + ev, a0 + od, b0 + od]
    wuq = mla_w_uq[0][:, np.concatenate([nope_cols] + rope_cols)]
    kvw = MLA_NOPE + MLA_V
    kn_cols = (np.arange(heads)[:, None] * kvw + np.arange(MLA_NOPE)[None]).reshape(-1)
    wukv = mla_w_ukv[0][:, np.concatenate([kn_cols, kn_cols + MLA_NOPE])]
    kr0 = q_rank + kv_rank
    wdn = mla_w_down[0][:, np.concatenate([np.arange(kr0), kr0 + ev, kr0 + ev, kr0 + od, kr0 + od])]
    cos_m, sin_m = _rope_tables(t, MLA_ROPE)
    qn, qr, kn, vv, kr = _mla_in_call(xs, mod, norm_mix[1], cast(wdn), mla_q_norm[0], mla_kv_norm[0],
                                      cast(wuq), cast(wukv), cos_m, sin_m)
    att = _mla_attn_call(qn, qr, kn, vv, kr, b_lat, tc)
    xs = _mla_out_call(xs, mod, att, cast(mla_w_o[0]))
    return _ffn_call(xs, b_lat, mod, norm_ffn2[1], *f2, 1, 6, final_g=final_norm)
```

```python
import functools

import numpy as np
import jax
import jax.numpy as jnp
from jax import lax
from jax.experimental import pallas as pl
from jax.experimental.pallas import tpu as pltpu

HEAD_DIM = 128
GRID_W = 64
ROPE_THETA = 10000.0
EPS = 1e-6
N_MOD = 9
GQA_KV_HEADS = 2
MLA_NOPE = 128
MLA_ROPE = 64
MLA_V = 128
LANES = 128
VMEM_LIMIT = 56 * 1024 * 1024
MOD_ROWS = 16
FFN_ROW_CHUNKS = 4
FFN_TILE = 512
GQA_ROW_CHUNK = 512
MLA_ROW_CHUNK = 256
ATTN_SCORE_LOOKAHEAD = 1
LOG2E = 1.4426950408889634
CAST_BLOCK_BYTES = 4 * 1024 * 1024

F32 = jnp.float32
BF16 = jnp.bfloat16


def _cparams(*sem):
    return pltpu.CompilerParams(dimension_semantics=sem, vmem_limit_bytes=VMEM_LIMIT)


def _resident(shape):
    nd = len(shape)
    return pl.BlockSpec(shape, lambda *_: (0,) * nd, pipeline_mode=pl.Buffered(1))


def _rms(x, g):
    return x * lax.rsqrt(jnp.mean(x * x, axis=-1, keepdims=True) + EPS) * g


def _modnorm(x, g, shift, scale):
    return _rms(x, g) * (1.0 + scale) + shift


def _rope(x, cos, sin):
    return x * cos + pltpu.roll(x, LANES // 2, axis=1) * sin


def _dot(a, b):
    return jnp.dot(a, b, preferred_element_type=F32)


def _dot_t(a, b):
    return lax.dot_general(a, b, (((1,), (1,)), ((), ())), preferred_element_type=F32)


def _mod_kernel(c_ref, w_ref, b_ref, o_ref):
    c = c_ref[...]
    s = (c * jax.nn.sigmoid(c)).astype(BF16)
    o_ref[0] = _dot(s, w_ref[0].astype(BF16)) + b_ref[0]


def _mod_call(cvec, mod_w, mod_b):
    depth, d, n = mod_w.shape
    tn = min(1024, n)
    return pl.pallas_call(
        _mod_kernel,
        grid=(depth, n // tn),
        in_specs=[
            pl.BlockSpec((MOD_ROWS, d), lambda l, j: (0, 0)),
            pl.BlockSpec((1, d, tn), lambda l, j: (l, 0, j)),
            pl.BlockSpec((1, 1, tn), lambda l, j: (l, 0, j)),
        ],
        out_specs=pl.BlockSpec((1, MOD_ROWS, tn), lambda l, j: (l, 0, j)),
        out_shape=jax.ShapeDtypeStruct((depth, MOD_ROWS, n), F32),
        compiler_params=_cparams("parallel", "parallel"),
        name="adaln_mod",
    )(cvec, mod_w, mod_b.reshape(depth, 1, n))


def _swiglu_act(xn, wg_ref, wu_ref):
    hg = _dot(xn, wg_ref[...])
    hu = _dot(xn, wu_ref[...])
    return (hg * jax.nn.sigmoid(hg) * hu).astype(BF16)


def _ffn_kernel(x_ref, mod_ref, g_ref, wg_hbm, wu_hbm, wd_hbm, *rest, layer, m0, final, with_ctx):
    rest = list(rest)
    fn_ref = rest.pop(0) if final else None
    ctx_ref = rest.pop(0) if with_ctx else None
    o_ref, xn_ref, wg_buf, wu_buf, wd_buf, sem = rest
    is_ctx = pl.program_id(0) == pl.num_programs(0) - 1
    n_ff = wg_hbm.shape[1]
    tm = xn_ref.shape[0]
    step = pl.program_id(0) * pl.num_programs(1) + pl.program_id(1)
    backward = step % 2 == 1
    half_gate = 0.5 * mod_ref[0, m0 + 2:m0 + 3, :]

    def copies(j, slot):
        blk = jnp.where(backward, n_ff - 1 - j, j)
        pairs = ((wg_hbm, wg_buf), (wu_hbm, wu_buf), (wd_hbm, wd_buf))
        return [pltpu.make_async_copy(hbm.at[layer, blk], buf.at[slot], sem.at[k, slot])
                for k, (hbm, buf) in enumerate(pairs)]

    @pl.when(step == 0)
    def _():
        for c in copies(0, 0):
            c.start()
        for c in copies(0, 0):
            c.wait()

    for j in range(n_ff):
        slot = j % 2
        if j + 1 < n_ff:
            for c in copies(j + 1, 1 - slot):
                c.start()
        if j > 0:
            for c in copies(j, slot):
                c.wait()
        wg, wu, wd = wg_buf.at[slot], wu_buf.at[slot], wd_buf.at[slot]
        if j == 0:
            rs = tm // FFN_ROW_CHUNKS
            for r in range(FFN_ROW_CHUNKS):
                rows = slice(r * rs, (r + 1) * rs)
                x = x_ref[0, rows, :]
                if with_ctx:
                    x = jnp.where(is_ctx, ctx_ref[0, rows, :], x)
                xn = _modnorm(x, g_ref[...], mod_ref[0, m0:m0 + 1, :],
                              mod_ref[0, m0 + 1:m0 + 2, :]).astype(BF16)
                xn_ref[rows, :] = xn
                o_ref[0, rows, :] = x + half_gate * _dot(_swiglu_act(xn, wg, wu), wd[...])
        else:
            o_ref[0] += half_gate * _dot(_swiglu_act(xn_ref[...], wg, wu), wd[...])

    if final:
        o_ref[0] = _rms(o_ref[0], fn_ref[...])


def _ffn_call(x, nb, mod, g, wg, wu, wd, layer, m0, final_g=None, ctx=None):
    _, t, d = x.shape
    _, n_ff, _, tf = wg.shape
    assert n_ff % 2 == 1
    final = final_g is not None
    tm = min(512, t)
    nt = t // tm
    if ctx is None:
        x_spec = pl.BlockSpec((1, tm, d), lambda b, i: (b, i, 0))
    else:
        x_spec = pl.BlockSpec((1, tm, d), lambda b, i: (jnp.minimum(b, nb - 2),
                                                         jnp.where(b < nb - 1, i, nt - 1), 0))
    in_specs = [
        x_spec,
        pl.BlockSpec((1, N_MOD, d), lambda b, i: (b, 0, 0)),
        pl.BlockSpec((1, d), lambda b, i: (0, 0)),
        pl.BlockSpec(memory_space=pl.ANY),
        pl.BlockSpec(memory_space=pl.ANY),
        pl.BlockSpec(memory_space=pl.ANY),
    ]
    args = [x, mod, g.reshape(1, d), wg, wu, wd]
    if final:
        in_specs.append(pl.BlockSpec((1, d), lambda b, i: (0, 0)))
        args.append(final_g.reshape(1, d))
    if ctx is not None:
        in_specs.append(pl.BlockSpec((1, tm, d), lambda b, i: (0, jnp.where(b < nb - 1, 0, i), 0)))
        args.append(ctx)
    return pl.pallas_call(
        functools.partial(_ffn_kernel, layer=layer, m0=m0, final=final, with_ctx=ctx is not None),
        grid=(nb, nt),
        in_specs=in_specs,
        out_specs=pl.BlockSpec((1, tm, d), lambda b, i: (b, i, 0)),
        out_shape=jax.ShapeDtypeStruct((nb, t, d), F32),
        scratch_shapes=[
            pltpu.VMEM((tm, d), BF16),
            pltpu.VMEM((2, d, tf), BF16), pltpu.VMEM((2, d, tf), BF16), pltpu.VMEM((2, tf, d), BF16),
            pltpu.SemaphoreType.DMA((3, 2)),
        ],
        compiler_params=_cparams("arbitrary", "arbitrary"),
        name="swiglu_half_step",
    )(*args)


def _cast_kernel(w_ref, o_ref):
    cb = o_ref.shape[3]
    for n in range(o_ref.shape[1]):
        o_ref[0, n] = w_ref[0, :, n * cb:(n + 1) * cb].astype(BF16)


def _cast_call(w, col_block):
    depth, r, c = w.shape
    nblk = c // col_block
    tr = max(n for n in range(16, r + 1, 16) if r % n == 0 and 4 * c * n <= CAST_BLOCK_BYTES)
    return pl.pallas_call(
        _cast_kernel,
        grid=(depth, r // tr),
        in_specs=[pl.BlockSpec((1, tr, c), lambda l, i: (l, i, 0))],
        out_specs=pl.BlockSpec((1, nblk, tr, col_block), lambda l, i: (l, 0, i, 0)),
        out_shape=jax.ShapeDtypeStruct((depth, nblk, r, col_block), BF16),
        compiler_params=_cparams("parallel", "parallel"),
        name="weight_cast",
    )(w)


def _ffn_weights(w_gate, w_up, w_down):
    depth, d, dff = w_gate.shape
    tf = min(FFN_TILE, dff)
    return (_cast_call(w_gate, tf), _cast_call(w_up, tf),
            _cast_call(w_down, d).reshape(depth, dff // tf, tf, d))


def _hyb_in_kernel(x_ref, mod_ref, g_ref, wgb_ref, wgc_ref, wu_ref, wq_ref, wk_ref, wv_ref,
                   qg_ref, kg_ref, cos_ref, sin_ref,
                   gb_ref, z_ref, q_ref, k_ref, v_ref):
    xn = _modnorm(x_ref[0], g_ref[...], mod_ref[0, 3:4, :], mod_ref[0, 4:5, :]).astype(BF16)
    gb_ref[0] = _dot(xn, wgb_ref[...]).astype(BF16)
    z_ref[0] = (_dot(xn, wgc_ref[...]) * _dot(xn, wu_ref[...])).astype(BF16)
    v_ref[0] = _dot(xn, wv_ref[...]).astype(BF16)
    cos = cos_ref[0]
    sin = sin_ref[0]
    q = _dot(xn, wq_ref[...])
    q_scale = HEAD_DIM ** -0.5 * LOG2E
    for h in range(q.shape[1] // HEAD_DIM):
        sl = slice(h * HEAD_DIM, (h + 1) * HEAD_DIM)
        q_ref[0, :, sl] = (_rope(_rms(q[:, sl], qg_ref[...]), cos, sin) * q_scale).astype(BF16)
    k = _dot(xn, wk_ref[...])
    for h in range(k.shape[1] // HEAD_DIM):
        sl = slice(h * HEAD_DIM, (h + 1) * HEAD_DIM)
        k_ref[0, :, sl] = _rope(_rms(k[:, sl], kg_ref[...]), cos, sin).astype(BF16)


def _hyb_in_call(x, mod, g, wgb, wgc, wu, wq, wk, wv, qg, kg, cos2, sin2):
    nb, t, d = x.shape
    tm = min(512, t)
    cc, nq, nkv = wgb.shape[1], wq.shape[1], wk.shape[1]
    row = lambda n: pl.BlockSpec((1, tm, n), lambda b, i: (b, i, 0))
    table = pl.BlockSpec((1, tm, LANES), lambda b, i: (jnp.where(b == nb - 1, 1, 0), i, 0))
    return pl.pallas_call(
        _hyb_in_kernel,
        grid=(nb, t // tm),
        in_specs=[
            row(d),
            pl.BlockSpec((1, N_MOD, d), lambda b, i: (b, 0, 0)),
            _resident((1, d)),
            _resident(wgb.shape), _resident(wgc.shape), _resident(wu.shape),
            _resident(wq.shape), _resident(wk.shape), _resident(wv.shape),
            _resident((1, HEAD_DIM)), _resident((1, HEAD_DIM)),
            table, table,
        ],
        out_specs=[row(cc), row(cc), row(nq), row(nkv), row(nkv)],
        out_shape=[jax.ShapeDtypeStruct((nb, t, n), BF16) for n in (cc, cc, nq, nkv, nkv)],
        compiler_params=_cparams("parallel", "parallel"),
        name="hyb_in_proj",
    )(x, mod, g.reshape(1, d), wgb, wgc, wu, wq, wk, wv,
      qg.reshape(1, HEAD_DIM), kg.reshape(1, HEAD_DIM), cos2, sin2)


def _softmax_pv(s_list, kv_list):
    m = s_list[0].max(axis=-1, keepdims=True)
    for s in s_list[1:]:
        m = jnp.maximum(m, s.max(axis=-1, keepdims=True))
    l = 0.0
    o = 0.0
    for s, (_, v) in zip(s_list, kv_list):
        p = jnp.exp2(s - m)
        l = l + p.sum(axis=-1, keepdims=True)
        o = o + _dot(p.astype(BF16), v)
    return o / l


def _attend_chains(chains, kv_list):
    scores = lambda load_q: [_dot_t(load_q(), k) for k, _ in kv_list]
    ahead = [scores(load_q) for load_q, _ in chains[:ATTN_SCORE_LOOKAHEAD]]
    for n, (_, store_o) in enumerate(chains):
        s_list = ahead.pop(0)
        if n + ATTN_SCORE_LOOKAHEAD < len(chains):
            ahead.append(scores(chains[n + ATTN_SCORE_LOOKAHEAD][0]))
        store_o(_softmax_pv(s_list, kv_list).astype(BF16))


def _row_chunks(n, c):
    c = min(c, n)
    return [slice(r * c, (r + 1) * c) for r in range(n // c)]


def _gqa_kernel(*refs, has_lat):
    if has_lat:
        q_ref, kc_ref, vc_ref, kl_ref, vl_ref, o_ref = refs
    else:
        q_ref, kc_ref, vc_ref, o_ref = refs
    kv_list = [(kc_ref[0], vc_ref[0])]
    if has_lat:
        kv_list.append((kl_ref[0], vl_ref[0]))

    def chain(rows, sl):
        def store(o):
            o_ref[0, rows, sl] = o
        return (lambda: q_ref[0, rows, sl]), store

    _attend_chains([chain(rows, slice(h * HEAD_DIM, (h + 1) * HEAD_DIM))
                    for h in range(q_ref.shape[2] // HEAD_DIM)
                    for rows in _row_chunks(q_ref.shape[1], GQA_ROW_CHUNK)], kv_list)


def _gqa_latent_call(q, k, v, b_lat, tc):
    nb, t, nq = q.shape
    gw = nq // GQA_KV_HEADS
    tq = min(512, t)
    ctx = pl.BlockSpec((1, tc, HEAD_DIM), lambda b, g, i: (b_lat, b, g))
    lat = pl.BlockSpec((1, t, HEAD_DIM), lambda b, g, i: (b, 0, g))
    qo = pl.BlockSpec((1, tq, gw), lambda b, g, i: (b, i, g))
    return pl.pallas_call(
        functools.partial(_gqa_kernel, has_lat=True),
        grid=(b_lat, GQA_KV_HEADS, t // tq),
        in_specs=[qo, ctx, ctx, lat, lat],
        out_specs=qo,
        out_shape=jax.ShapeDtypeStruct((b_lat, t, nq), BF16),
        compiler_params=_cparams("parallel", "parallel", "parallel"),
        name="gqa_latent",
    )(q, k, v, k, v)


def _gqa_ctx_call(q, k, v, b_lat, tc):
    nb, t, nq = q.shape
    gw = nq // GQA_KV_HEADS
    ctx = pl.BlockSpec((1, tc, HEAD_DIM), lambda b, g: (b_lat, b, g))
    return pl.pallas_call(
        functools.partial(_gqa_kernel, has_lat=False),
        grid=(t // tc, GQA_KV_HEADS),
        in_specs=[pl.BlockSpec((1, tc, gw), lambda b, g: (b_lat, b, g)), ctx, ctx],
        out_specs=pl.BlockSpec((1, tc, gw), lambda b, g: (0, b, g)),
        out_shape=jax.ShapeDtypeStruct((1, t, nq), BF16),
        compiler_params=_cparams("parallel", "parallel"),
        name="gqa_context",
    )(q, k, v)


def _hyb_out_kernel(x_ref, mod_ref, gb_ref, z_ref, zp_ref, zn_ref, att_ref, attc_ref, cw_ref, w_ref,
                    o_ref, *, seq_lat, seq_ctx, b_lat):
    b, i = pl.program_id(0), pl.program_id(1)
    tm, cc = z_ref.shape[1], z_ref.shape[2]
    acc = _dot(jnp.where(b == b_lat, attc_ref[0], att_ref[0]), w_ref[cc:, :])
    z = z_ref[0].astype(F32)
    rows = lax.broadcasted_iota(jnp.int32, (tm, cc), 0)
    seq = jnp.where(b == b_lat, seq_ctx, seq_lat)
    pos = (i * tm + rows) & (seq - 1)
    z_prev = jnp.where(rows == 0, zp_ref[0, 7:8, :].astype(F32), pltpu.roll(z, 1, axis=0))
    z_next = jnp.where(rows == tm - 1, zn_ref[0, 0:1, :].astype(F32),
                       pltpu.roll(z, tm - 1, axis=0))
    z_prev = jnp.where(pos == 0, 0.0, z_prev)
    z_next = jnp.where(pos == seq - 1, 0.0, z_next)
    conv = z_prev * cw_ref[0:1, :] + z * cw_ref[1:2, :] + z_next * cw_ref[2:3, :]
    acc = acc + _dot((gb_ref[0].astype(F32) * conv).astype(BF16), w_ref[:cc, :])
    o_ref[0] = x_ref[0] + mod_ref[0, 5:6, :] * acc


def _hyb_out_call(x, mod, gb, z, att, att_c, conv_w, w_out, b_lat, tc):
    nb, t, d = x.shape
    cc = gb.shape[2]
    tm = min(512, t)
    nt = t // tm
    hb = tm // 8
    row = lambda n: pl.BlockSpec((1, tm, n), lambda b, i: (b, i, 0))
    nq = att.shape[2]
    att_spec = pl.BlockSpec((1, tm, nq), lambda b, i: (jnp.minimum(b, b_lat - 1),
                                                       jnp.where(b < b_lat, i, nt - 1), 0))
    attc_spec = pl.BlockSpec((1, tm, nq), lambda b, i: (0, jnp.where(b < b_lat, 0, i), 0))
    return pl.pallas_call(
        functools.partial(_hyb_out_kernel, seq_lat=t, seq_ctx=tc, b_lat=b_lat),
        grid=(nb, t // tm),
        in_specs=[
            row(d),
            pl.BlockSpec((1, N_MOD, d), lambda b, i: (b, 0, 0)),
            row(cc), row(cc),
            pl.BlockSpec((1, 8, cc), lambda b, i: (b, jnp.maximum(i * hb - 1, 0), 0)),
            pl.BlockSpec((1, 8, cc), lambda b, i: (b, jnp.minimum((i + 1) * hb, t // 8 - 1), 0)),
            att_spec, attc_spec,
            _resident(conv_w.shape),
            _resident(w_out.shape),
        ],
        out_specs=row(d),
        out_shape=jax.ShapeDtypeStruct((nb, t, d), F32),
        compiler_params=_cparams("parallel", "parallel"),
        name="hyb_out_proj",
    )(x, mod, gb, z, z, z, att, att_c, conv_w, w_out)


def _mla_in_kernel(x_ref, mod_ref, g_ref, wd_ref, qg_ref, kvg_ref, wuq_ref, wukv_ref,
                   cos_ref, sin_ref, qn_ref, qr_ref, kn_ref, v_ref, kr_ref, *, q_rank, kv_rank):
    xn = _modnorm(x_ref[0], g_ref[...], mod_ref[0, 3:4, :], mod_ref[0, 4:5, :]).astype(BF16)
    d = _dot(xn, wd_ref[...])
    cos = cos_ref[0]
    sin = sin_ref[0]
    n_nope = qn_ref.shape[2]
    q_scale = (MLA_NOPE + MLA_ROPE) ** -0.5 * LOG2E
    q = _dot(_rms(d[:, :q_rank], qg_ref[...]).astype(BF16), wuq_ref[...])
    qn_ref[0] = (q[:, :n_nope] * q_scale).astype(BF16)
    for j in range(qr_ref.shape[2] // LANES):
        qr_ref[0, :, j * LANES:(j + 1) * LANES] = (
            _rope(q[:, n_nope + j * LANES:n_nope + (j + 1) * LANES], cos, sin) * q_scale).astype(BF16)
    kv = _dot(_rms(d[:, q_rank:q_rank + kv_rank], kvg_ref[...]).astype(BF16), wukv_ref[...])
    kn_ref[0] = kv[:, :n_nope].astype(BF16)
    v_ref[0] = kv[:, n_nope:].astype(BF16)
    kr = _rope(d[:, q_rank + kv_rank:], cos, sin)
    first = (lax.broadcasted_iota(jnp.int32, kr.shape, 1) & (MLA_ROPE // 2)) == 0
    kr_ref[0, :, :LANES] = jnp.where(first, kr, 0.0).astype(BF16)
    kr_ref[0, :, LANES:] = jnp.where(first, 0.0, kr).astype(BF16)


def _mla_in_call(x, mod, g, wd, qg, kvg, wuq, wukv, cos2, sin2):
    nb, t, d = x.shape
    tm = min(256, t)
    q_rank, kv_rank = qg.shape[0], kvg.shape[0]
    n_nope = wukv.shape[1] // 2
    n_rope = wuq.shape[1] - n_nope
    row = lambda n: pl.BlockSpec((1, tm, n), lambda b, i: (b, i, 0))
    table = pl.BlockSpec((1, tm, LANES), lambda b, i: (jnp.where(b == nb - 1, 1, 0), i, 0))
    widths = (n_nope, n_rope, n_nope, n_nope, 2 * LANES)
    return pl.pallas_call(
        functools.partial(_mla_in_kernel, q_rank=q_rank, kv_rank=kv_rank),
        grid=(nb, t // tm),
        in_specs=[
            row(d),
            pl.BlockSpec((1, N_MOD, d), lambda b, i: (b, 0, 0)),
            _resident((1, d)),
            _resident(wd.shape), _resident((1, q_rank)), _resident((1, kv_rank)),
            _resident(wuq.shape), _resident(wukv.shape),
            table, table,
        ],
        out_specs=[row(n) for n in widths],
        out_shape=[jax.ShapeDtypeStruct((nb, t, n), BF16) for n in widths],
        compiler_params=_cparams("parallel", "parallel"),
        name="mla_in_proj",
    )(x, mod, g.reshape(1, d), wd, qg.reshape(1, q_rank), kvg.reshape(1, kv_rank), wuq, wukv,
      cos2, sin2)


def _mla_attn_kernel(qn_ref, qr_ref, knc_ref, krc_ref, vc_ref, knl_ref, krl_ref, vl_ref,
                     o_ref, k_ref, v_ref):
    tc = knc_ref.shape[1]

    @pl.when(pl.program_id(2) == 0)
    def _():
        k_ref[:tc, :LANES] = knc_ref[0]
        k_ref[:tc, LANES:] = krc_ref[0]
        k_ref[tc:, :LANES] = knl_ref[0]
        k_ref[tc:, LANES:] = krl_ref[0]
        v_ref[:tc, :] = vc_ref[0]
        v_ref[tc:, :] = vl_ref[0]

    def chain(rows):
        def store(o):
            o_ref[0, rows, :] = o
        return (lambda: jnp.concatenate([qn_ref[0, rows, :], qr_ref[0, rows, :]], axis=1)), store

    _attend_chains([chain(rows) for rows in _row_chunks(qn_ref.shape[1], MLA_ROW_CHUNK)],
                   [(k_ref[...], v_ref[...])])


def _mla_attn_call(qn, qr, kn, v, kr, b_lat, tc):
    nb, t, n_nope = qn.shape
    heads = n_nope // MLA_NOPE
    tq = min(2048, t)
    qspec = pl.BlockSpec((1, tq, LANES), lambda b, h, i: (b, i, h))
    qrspec = pl.BlockSpec((1, tq, LANES), lambda b, h, i: (b, i, h // 2))
    ctx = lambda f: pl.BlockSpec((1, tc, LANES), lambda b, h, i: (b_lat, b, f(h)))
    lat = lambda f: pl.BlockSpec((1, t, LANES), lambda b, h, i: (b, 0, f(h)))
    same = lambda h: h
    parity = lambda h: h % 2
    return pl.pallas_call(
        _mla_attn_kernel,
        grid=(b_lat, heads, t // tq),
        in_specs=[qspec, qrspec, ctx(same), ctx(parity), ctx(same), lat(same), lat(parity), lat(same)],
        out_specs=qspec,
        out_shape=jax.ShapeDtypeStruct((b_lat, t, n_nope), BF16),
        scratch_shapes=[pltpu.VMEM((tc + t, 2 * LANES), BF16), pltpu.VMEM((tc + t, LANES), BF16)],
        compiler_params=_cparams("parallel", "parallel", "arbitrary"),
        name="mla_attention",
    )(qn, qr, kn, kr, v, kn, kr, v)


def _mla_out_kernel(x_ref, mod_ref, att_ref, w_ref, o_ref):
    o_ref[0] = x_ref[0] + mod_ref[0, 5:6, :] * _dot(att_ref[0], w_ref[...])


def _mla_out_call(x, mod, att, w_o):
    nb, t, k = att.shape
    d = w_o.shape[1]
    tm = min(512, t)
    row = lambda n: pl.BlockSpec((1, tm, n), lambda b, i: (b, i, 0))
    return pl.pallas_call(
        _mla_out_kernel,
        grid=(nb, t // tm),
        in_specs=[row(d), pl.BlockSpec((1, N_MOD, d), lambda b, i: (b, 0, 0)), row(k),
                  _resident(w_o.shape)],
        out_specs=row(d),
        out_shape=jax.ShapeDtypeStruct((nb, t, d), F32),
        compiler_params=_cparams("parallel", "parallel"),
        name="mla_out_proj",
    )(x, mod, att, w_o)


def _rope_angles(n_tok, dim):
    n_rows = n_tok // GRID_W
    row = jnp.repeat(jnp.arange(n_rows), GRID_W).astype(F32)
    col = jnp.tile(jnp.arange(GRID_W), n_rows).astype(F32)
    half = dim // 2
    inv = 1.0 / (ROPE_THETA ** (jnp.arange(0, half, 2, dtype=F32) / half))
    return jnp.concatenate([row[:, None] * inv, col[:, None] * inv], axis=-1)


def _rope_tables(n_tok, dim):
    ang = _rope_angles(n_tok, dim)
    reps = LANES // dim
    cos = jnp.concatenate([jnp.cos(ang)] * (2 * reps), axis=-1)
    sin = jnp.concatenate([-jnp.sin(ang)] * reps + [jnp.sin(ang)] * reps, axis=-1)
    return (jnp.stack([cos, jnp.ones_like(cos)]), jnp.stack([sin, jnp.zeros_like(sin)]))


def _deinterleave(n):
    return np.concatenate([np.arange(0, n, 2), np.arange(1, n, 2)])


def kernel(x, c, ctx, c_ctx, mod_w, mod_b, norm_ffn1, norm_mix, norm_ffn2,
           ffn1_w_gate, ffn1_w_up, ffn1_w_down, ffn2_w_gate, ffn2_w_up, ffn2_w_down,
           hyb_w_in, hyb_conv_w, hyb_q_norm, hyb_k_norm, hyb_w_out,
           mla_w_down, mla_q_norm, mla_kv_norm, mla_w_uq, mla_w_ukv, mla_w_o, final_norm):
    b_lat, t, d = x.shape
    tc = ctx.shape[1]
    depth = mod_w.shape[0]
    assert depth == 2 and ctx.shape[0] * tc == t and b_lat + 1 <= MOD_ROWS
    assert t & (t - 1) == 0 and tc & (tc - 1) == 0 and t % GRID_W == 0
    nb = b_lat + 1

    cvec = jnp.concatenate([c, c_ctx[None], jnp.zeros((MOD_ROWS - nb, d), F32)], axis=0)
    mods = _mod_call(cvec, mod_w, mod_b).reshape(depth, MOD_ROWS, N_MOD, d)
    cast = lambda w: w.astype(BF16)
    f1 = _ffn_weights(ffn1_w_gate, ffn1_w_up, ffn1_w_down)
    f2 = _ffn_weights(ffn2_w_gate, ffn2_w_up, ffn2_w_down)

    mod = mods[0]
    xs = _ffn_call(x, nb, mod, norm_ffn1[0], *f1, 0, 0, ctx=ctx.reshape(1, t, d))
    cc = d // 2
    n_q = cc
    n_kv = GQA_KV_HEADS * HEAD_DIM
    w_in = hyb_w_in[0]
    perm = _deinterleave(HEAD_DIM)
    head_perm = lambda n: (np.arange(n).reshape(-1, HEAD_DIM)[:, perm]).reshape(-1)
    wq = w_in[:, 3 * cc:3 * cc + n_q][:, head_perm(n_q)]
    wk = w_in[:, 3 * cc + n_q:3 * cc + n_q + n_kv][:, head_perm(n_kv)]
    wv = w_in[:, 3 * cc + n_q + n_kv:]
    cos_a, sin_a = _rope_tables(t, HEAD_DIM)
    gb, z, q, k, v = _hyb_in_call(
        xs, mod, norm_mix[0], cast(w_in[:, :cc]), cast(w_in[:, cc:2 * cc]), cast(w_in[:, 2 * cc:3 * cc]),
        cast(wq), cast(wk), cast(wv), hyb_q_norm[0][perm], hyb_k_norm[0][perm], cos_a, sin_a)
    att = _gqa_latent_call(q, k, v, b_lat, tc)
    att_c = _gqa_ctx_call(q, k, v, b_lat, tc)
    xs = _hyb_out_call(xs, mod, gb, z, att, att_c, hyb_conv_w[0], cast(hyb_w_out[0]), b_lat, tc)
    xs = _ffn_call(xs, nb, mod, norm_ffn2[0], *f2, 0, 6)

    mod = mods[1]
    xs = _ffn_call(xs, nb, mod, norm_ffn1[1], *f1, 1, 0)
    heads = d // HEAD_DIM
    q_rank, kv_rank = mla_q_norm.shape[1], mla_kv_norm.shape[1]
    qk = MLA_NOPE + MLA_ROPE
    ev, od = np.arange(0, MLA_ROPE, 2), np.arange(1, MLA_ROPE, 2)
    nope_cols = (np.arange(heads)[:, None] * qk + np.arange(MLA_NOPE)[None]).reshape(-1)
    rope_cols = []
    for j in range(heads // 2):
        a0, b0 = 2 * j * qk + MLA_NOPE, (2 * j + 1) * qk + MLA_NOPE
        rope_cols += [a0 + ev, b0 + ev, a0 + od, b0 + od]
    wuq = mla_w_uq[0][:, np.concatenate([nope_cols] + rope_cols)]
    kvw = MLA_NOPE + MLA_V
    kn_cols = (np.arange(heads)[:, None] * kvw + np.arange(MLA_NOPE)[None]).reshape(-1)
    wukv = mla_w_ukv[0][:, np.concatenate([kn_cols, kn_cols + MLA_NOPE])]
    kr0 = q_rank + kv_rank
    wdn = mla_w_down[0][:, np.concatenate([np.arange(kr0), kr0 + ev, kr0 + ev, kr0 + od, kr0 + od])]
    cos_m, sin_m = _rope_tables(t, MLA_ROPE)
    qn, qr, kn, vv, kr = _mla_in_call(xs, mod, norm_mix[1], cast(wdn), mla_q_norm[0], mla_kv_norm[0],
                                      cast(wuq), cast(wukv), cos_m, sin_m)
    att = _mla_attn_call(qn, qr, kn, vv, kr, b_lat, tc)
    xs = _mla_out_call(xs, mod, att, cast(mla_w_o[0]))
    return _ffn_call(xs, b_lat, mod, norm_ffn2[1], *f2, 1, 6, final_g=final_norm)
```

```python
import functools

import numpy as np
import jax
import jax.numpy as jnp
from jax import lax
from jax.experimental import pallas as pl
from jax.experimental.pallas import tpu as pltpu

HEAD_DIM = 128
GRID_W = 64
ROPE_THETA = 10000.0
EPS = 1e-6
N_MOD = 9
GQA_KV_HEADS = 2
MLA_NOPE = 128
MLA_ROPE = 64
MLA_V = 128
LANES = 128
VMEM_LIMIT = 56 * 1024 * 1024
MOD_ROWS = 16
FFN_ROW_CHUNKS = 4
FFN_TILE = 512
FFN_SLOTS = 4
PROJ_ROW_CHUNK = 256
GQA_ROW_CHUNK = 512
MLA_ROW_CHUNK = 256
ATTN_SCORE_LOOKAHEAD = 1
LOG2E = 1.4426950408889634
CAST_BLOCK_BYTES = 4 * 1024 * 1024

F32 = jnp.float32
BF16 = jnp.bfloat16


def _cparams(*sem):
    return pltpu.CompilerParams(dimension_semantics=sem, vmem_limit_bytes=VMEM_LIMIT)


def _resident(shape):
    nd = len(shape)
    return pl.BlockSpec(shape, lambda *_: (0,) * nd, pipeline_mode=pl.Buffered(1))


def _rms(x, g):
    return x * lax.rsqrt(jnp.mean(x * x, axis=-1, keepdims=True) + EPS) * g


def _modnorm(x, g, shift, scale):
    return _rms(x, g) * (1.0 + scale) + shift


def _rope(x, cos, sin):
    return x * cos + pltpu.roll(x, LANES // 2, axis=1) * sin


def _dot(a, b):
    return jnp.dot(a, b, preferred_element_type=F32)


def _dot_t(a, b):
    return lax.dot_general(a, b, (((1,), (1,)), ((), ())), preferred_element_type=F32)


def _mod_kernel(c_ref, w_ref, b_ref, o_ref):
    c = c_ref[...]
    s = (c * jax.nn.sigmoid(c)).astype(BF16)
    o_ref[0] = _dot(s, w_ref[0].astype(BF16)) + b_ref[0]


def _mod_call(cvec, mod_w, mod_b):
    depth, d, n = mod_w.shape
    tn = min(1024, n)
    return pl.pallas_call(
        _mod_kernel,
        grid=(depth, n // tn),
        in_specs=[
            pl.BlockSpec((MOD_ROWS, d), lambda l, j: (0, 0)),
            pl.BlockSpec((1, d, tn), lambda l, j: (l, 0, j)),
            pl.BlockSpec((1, 1, tn), lambda l, j: (l, 0, j)),
        ],
        out_specs=pl.BlockSpec((1, MOD_ROWS, tn), lambda l, j: (l, 0, j)),
        out_shape=jax.ShapeDtypeStruct((depth, MOD_ROWS, n), F32),
        compiler_params=_cparams("parallel", "parallel"),
        name="adaln_mod",
    )(cvec, mod_w, mod_b.reshape(depth, 1, n))


def _swiglu_act(xn, wg_ref, wu_ref):
    hg = _dot(xn, wg_ref[...])
    hu = _dot(xn, wu_ref[...])
    return (hg * jax.nn.sigmoid(hg) * hu).astype(BF16)


def _ffn_kernel(x_ref, mod_ref, g_ref, wg_hbm, wu_hbm, wd_hbm, *rest, layer, m0, final, with_ctx):
    rest = list(rest)
    fn_ref = rest.pop(0) if final else None
    ctx_ref = rest.pop(0) if with_ctx else None
    o_ref, xn_ref, h_ref, wg_buf, wu_buf, wd_buf, sem = rest
    is_ctx = pl.program_id(0) == pl.num_programs(0) - 1
    n_ff = wg_hbm.shape[1]
    tm = xn_ref.shape[0]
    step = pl.program_id(0) * pl.num_programs(1) + pl.program_id(1)
    not_last_step = step < pl.num_programs(0) * pl.num_programs(1) - 1
    half_gate = 0.5 * mod_ref[0, m0 + 2:m0 + 3, :]

    def copies(blk):
        slot = blk % FFN_SLOTS
        pairs = ((wg_hbm, wg_buf), (wu_hbm, wu_buf), (wd_hbm, wd_buf))
        return [pltpu.make_async_copy(hbm.at[layer, blk], buf.at[slot], sem.at[k, slot])
                for k, (hbm, buf) in enumerate(pairs)]

    def start(blk):
        for c in copies(blk):
            c.start()

    def wait(blk):
        for c in copies(blk):
            c.wait()

    def x_rows(rows):
        x = x_ref[0, rows, :]
        return jnp.where(is_ctx, ctx_ref[0, rows, :], x) if with_ctx else x

    @pl.when(step == 0)
    def _():
        start(0)
        start(1)

    wait(0)
    rs = tm // FFN_ROW_CHUNKS
    for r in range(FFN_ROW_CHUNKS):
        rows = slice(r * rs, (r + 1) * rs)
        xn = _modnorm(x_rows(rows), g_ref[...], mod_ref[0, m0:m0 + 1, :],
                      mod_ref[0, m0 + 1:m0 + 2, :]).astype(BF16)
        xn_ref[rows, :] = xn
        h_ref[0, rows, :] = _swiglu_act(xn, wg_buf.at[0], wu_buf.at[0])

    for j in range(n_ff):
        if j + 2 < n_ff:
            start(j + 2)
        else:
            pl.when(not_last_step)(functools.partial(start, j + 2 - n_ff))
        if j + 1 < n_ff:
            wait(j + 1)
            nslot = (j + 1) % FFN_SLOTS
            h_ref[(j + 1) % 2] = _swiglu_act(xn_ref[...], wg_buf.at[nslot], wu_buf.at[nslot])
        contrib = half_gate * _dot(h_ref[j % 2], wd_buf[j % FFN_SLOTS])
        if j == 0:
            o_ref[0] = x_rows(slice(None)) + contrib
        else:
            o_ref[0] += contrib

    if final:
        o_ref[0] = _rms(o_ref[0], fn_ref[...])


def _ffn_call(x, nb, mod, g, wg, wu, wd, layer, m0, final_g=None, ctx=None):
    _, t, d = x.shape
    _, n_ff, _, tf = wg.shape
    assert n_ff >= 2 and all(
        max(b for b in range(n_ff) if b % FFN_SLOTS == k) <= n_ff - 3 + k for k in (0, 1))
    final = final_g is not None
    tm = min(512, t)
    nt = t // tm
    if ctx is None:
        x_spec = pl.BlockSpec((1, tm, d), lambda b, i: (b, i, 0))
    else:
        x_spec = pl.BlockSpec((1, tm, d), lambda b, i: (jnp.minimum(b, nb - 2),
                                                         jnp.where(b < nb - 1, i, nt - 1), 0))
    in_specs = [
        x_spec,
        pl.BlockSpec((1, N_MOD, d), lambda b, i: (b, 0, 0)),
        pl.BlockSpec((1, d), lambda b, i: (0, 0)),
        pl.BlockSpec(memory_space=pl.ANY),
        pl.BlockSpec(memory_space=pl.ANY),
        pl.BlockSpec(memory_space=pl.ANY),
    ]
    args = [x, mod, g.reshape(1, d), wg, wu, wd]
    if final:
        in_specs.append(pl.BlockSpec((1, d), lambda b, i: (0, 0)))
        args.append(final_g.reshape(1, d))
    if ctx is not None:
        in_specs.append(pl.BlockSpec((1, tm, d), lambda b, i: (0, jnp.where(b < nb - 1, 0, i), 0)))
        args.append(ctx)
    return pl.pallas_call(
        functools.partial(_ffn_kernel, layer=layer, m0=m0, final=final, with_ctx=ctx is not None),
        grid=(nb, nt),
        in_specs=in_specs,
        out_specs=pl.BlockSpec((1, tm, d), lambda b, i: (b, i, 0)),
        out_shape=jax.ShapeDtypeStruct((nb, t, d), F32),
        scratch_shapes=[
            pltpu.VMEM((tm, d), BF16),
            pltpu.VMEM((2, tm, tf), BF16),
            pltpu.VMEM((FFN_SLOTS, d, tf), BF16), pltpu.VMEM((FFN_SLOTS, d, tf), BF16),
            pltpu.VMEM((FFN_SLOTS, tf, d), BF16),
            pltpu.SemaphoreType.DMA((3, FFN_SLOTS)),
        ],
        compiler_params=_cparams("arbitrary", "arbitrary"),
        name="swiglu_half_step",
    )(*args)


def _cast_kernel(w_ref, o_ref):
    cb = o_ref.shape[3]
    for n in range(o_ref.shape[1]):
        o_ref[0, n] = w_ref[0, :, n * cb:(n + 1) * cb].astype(BF16)


def _cast_call(w, col_block):
    depth, r, c = w.shape
    nblk = c // col_block
    tr = max(n for n in range(16, r + 1, 16) if r % n == 0 and 4 * c * n <= CAST_BLOCK_BYTES)
    return pl.pallas_call(
        _cast_kernel,
        grid=(depth, r // tr),
        in_specs=[pl.BlockSpec((1, tr, c), lambda l, i: (l, i, 0))],
        out_specs=pl.BlockSpec((1, nblk, tr, col_block), lambda l, i: (l, 0, i, 0)),
        out_shape=jax.ShapeDtypeStruct((depth, nblk, r, col_block), BF16),
        compiler_params=_cparams("parallel", "parallel"),
        name="weight_cast",
    )(w)


def _ffn_weights(w_gate, w_up, w_down):
    depth, d, dff = w_gate.shape
    tf = min(FFN_TILE, dff)
    return (_cast_call(w_gate, tf), _cast_call(w_up, tf),
            _cast_call(w_down, d).reshape(depth, dff // tf, tf, d))


def _hyb_in_kernel(x_ref, mod_ref, g_ref, wgb_ref, wgc_ref, wu_ref, wq_ref, wk_ref, wv_ref,
                   qg_ref, kg_ref, cos_ref, sin_ref,
                   gb_ref, z_ref, q_ref, k_ref, v_ref):
    q_scale = HEAD_DIM ** -0.5 * LOG2E
    norm = lambda rows: _modnorm(x_ref[0, rows, :], g_ref[...], mod_ref[0, 3:4, :],
                                 mod_ref[0, 4:5, :]).astype(BF16)
    chunks = _row_chunks(x_ref.shape[1], PROJ_ROW_CHUNK)
    xn_next = norm(chunks[0])
    for n, rows in enumerate(chunks):
        xn = xn_next
        if n + 1 < len(chunks):
            xn_next = norm(chunks[n + 1])
        gb_ref[0, rows, :] = _dot(xn, wgb_ref[...]).astype(BF16)
        z_ref[0, rows, :] = (_dot(xn, wgc_ref[...]) * _dot(xn, wu_ref[...])).astype(BF16)
        v_ref[0, rows, :] = _dot(xn, wv_ref[...]).astype(BF16)
        cos = cos_ref[0, rows, :]
        sin = sin_ref[0, rows, :]
        q = _dot(xn, wq_ref[...])
        for h in range(q.shape[1] // HEAD_DIM):
            sl = slice(h * HEAD_DIM, (h + 1) * HEAD_DIM)
            q_ref[0, rows, sl] = (_rope(_rms(q[:, sl], qg_ref[...]), cos, sin) * q_scale).astype(BF16)
        k = _dot(xn, wk_ref[...])
        for h in range(k.shape[1] // HEAD_DIM):
            sl = slice(h * HEAD_DIM, (h + 1) * HEAD_DIM)
            k_ref[0, rows, sl] = _rope(_rms(k[:, sl], kg_ref[...]), cos, sin).astype(BF16)


def _hyb_in_call(x, mod, g, wgb, wgc, wu, wq, wk, wv, qg, kg, cos2, sin2):
    nb, t, d = x.shape
    tm = min(512, t)
    cc, nq, nkv = wgb.shape[1], wq.shape[1], wk.shape[1]
    row = lambda n: pl.BlockSpec((1, tm, n), lambda b, i: (b, i, 0))
    table = pl.BlockSpec((1, tm, LANES), lambda b, i: (jnp.where(b == nb - 1, 1, 0), i, 0))
    return pl.pallas_call(
        _hyb_in_kernel,
        grid=(nb, t // tm),
        in_specs=[
            row(d),
            pl.BlockSpec((1, N_MOD, d), lambda b, i: (b, 0, 0)),
            _resident((1, d)),
            _resident(wgb.shape), _resident(wgc.shape), _resident(wu.shape),
            _resident(wq.shape), _resident(wk.shape), _resident(wv.shape),
            _resident((1, HEAD_DIM)), _resident((1, HEAD_DIM)),
            table, table,
        ],
        out_specs=[row(cc), row(cc), row(nq), row(nkv), row(nkv)],
        out_shape=[jax.ShapeDtypeStruct((nb, t, n), BF16) for n in (cc, cc, nq, nkv, nkv)],
        compiler_params=_cparams("parallel", "parallel"),
        name="hyb_in_proj",
    )(x, mod, g.reshape(1, d), wgb, wgc, wu, wq, wk, wv,
      qg.reshape(1, HEAD_DIM), kg.reshape(1, HEAD_DIM), cos2, sin2)


def _softmax_pv(s_list, kv_list):
    m = s_list[0].max(axis=-1, keepdims=True)
    for s in s_list[1:]:
        m = jnp.maximum(m, s.max(axis=-1, keepdims=True))
    l = 0.0
    o = 0.0
    for s, (_, v) in zip(s_list, kv_list):
        p = jnp.exp2(s - m)
        l = l + p.sum(axis=-1, keepdims=True)
        o = o + _dot(p.astype(BF16), v)
    return o / l


def _attend_chains(chains, kv_list):
    scores = lambda load_q: [_dot_t(load_q(), k) for k, _ in kv_list]
    ahead = [scores(load_q) for load_q, _ in chains[:ATTN_SCORE_LOOKAHEAD]]
    for n, (_, store_o) in enumerate(chains):
        s_list = ahead.pop(0)
        if n + ATTN_SCORE_LOOKAHEAD < len(chains):
            ahead.append(scores(chains[n + ATTN_SCORE_LOOKAHEAD][0]))
        store_o(_softmax_pv(s_list, kv_list).astype(BF16))


def _row_chunks(n, c):
    c = min(c, n)
    return [slice(r * c, (r + 1) * c) for r in range(n // c)]


def _gqa_kernel(*refs, has_lat):
    if has_lat:
        q_ref, kc_ref, vc_ref, kl_ref, vl_ref, o_ref = refs
    else:
        q_ref, kc_ref, vc_ref, o_ref = refs
    kv_list = [(kc_ref[0], vc_ref[0])]
    if has_lat:
        kv_list.append((kl_ref[0], vl_ref[0]))

    def chain(rows, sl):
        def store(o):
            o_ref[0, rows, sl] = o
        return (lambda: q_ref[0, rows, sl]), store

    _attend_chains([chain(rows, slice(h * HEAD_DIM, (h + 1) * HEAD_DIM))
                    for h in range(q_ref.shape[2] // HEAD_DIM)
                    for rows in _row_chunks(q_ref.shape[1], GQA_ROW_CHUNK)], kv_list)


def _gqa_latent_call(q, k, v, b_lat, tc):
    nb, t, nq = q.shape
    gw = nq // GQA_KV_HEADS
    tq = min(512, t)
    ctx = pl.BlockSpec((1, tc, HEAD_DIM), lambda b, g, i: (b_lat, b, g))
    lat = pl.BlockSpec((1, t, HEAD_DIM), lambda b, g, i: (b, 0, g))
    qo = pl.BlockSpec((1, tq, gw), lambda b, g, i: (b, i, g))
    return pl.pallas_call(
        functools.partial(_gqa_kernel, has_lat=True),
        grid=(b_lat, GQA_KV_HEADS, t // tq),
        in_specs=[qo, ctx, ctx, lat, lat],
        out_specs=qo,
        out_shape=jax.ShapeDtypeStruct((b_lat, t, nq), BF16),
        compiler_params=_cparams("parallel", "parallel", "parallel"),
        name="gqa_latent",
    )(q, k, v, k, v)


def _gqa_ctx_call(q, k, v, b_lat, tc):
    nb, t, nq = q.shape
    gw = nq // GQA_KV_HEADS
    ctx = pl.BlockSpec((1, tc, HEAD_DIM), lambda b, g: (b_lat, b, g))
    return pl.pallas_call(
        functools.partial(_gqa_kernel, has_lat=False),
        grid=(t // tc, GQA_KV_HEADS),
        in_specs=[pl.BlockSpec((1, tc, gw), lambda b, g: (b_lat, b, g)), ctx, ctx],
        out_specs=pl.BlockSpec((1, tc, gw), lambda b, g: (0, b, g)),
        out_shape=jax.ShapeDtypeStruct((1, t, nq), BF16),
        compiler_params=_cparams("parallel", "parallel"),
        name="gqa_context",
    )(q, k, v)


def _hyb_out_kernel(x_ref, mod_ref, gb_ref, z_ref, zp_ref, zn_ref, att_ref, attc_ref, cw_ref, w_ref,
                    o_ref, *, seq_lat, seq_ctx, b_lat):
    b, i = pl.program_id(0), pl.program_id(1)
    tm, cc = z_ref.shape[1], z_ref.shape[2]
    acc = _dot(jnp.where(b == b_lat, attc_ref[0], att_ref[0]), w_ref[cc:, :])
    z = z_ref[0].astype(F32)
    rows = lax.broadcasted_iota(jnp.int32, (tm, cc), 0)
    seq = jnp.where(b == b_lat, seq_ctx, seq_lat)
    pos = (i * tm + rows) & (seq - 1)
    z_prev = jnp.where(rows == 0, zp_ref[0, 7:8, :].astype(F32), pltpu.roll(z, 1, axis=0))
    z_next = jnp.where(rows == tm - 1, zn_ref[0, 0:1, :].astype(F32),
                       pltpu.roll(z, tm - 1, axis=0))
    z_prev = jnp.where(pos == 0, 0.0, z_prev)
    z_next = jnp.where(pos == seq - 1, 0.0, z_next)
    conv = z_prev * cw_ref[0:1, :] + z * cw_ref[1:2, :] + z_next * cw_ref[2:3, :]
    acc = acc + _dot((gb_ref[0].astype(F32) * conv).astype(BF16), w_ref[:cc, :])
    o_ref[0] = x_ref[0] + mod_ref[0, 5:6, :] * acc


def _hyb_out_call(x, mod, gb, z, att, att_c, conv_w, w_out, b_lat, tc):
    nb, t, d = x.shape
    cc = gb.shape[2]
    tm = min(512, t)
    nt = t // tm
    hb = tm // 8
    row = lambda n: pl.BlockSpec((1, tm, n), lambda b, i: (b, i, 0))
    nq = att.shape[2]
    att_spec = pl.BlockSpec((1, tm, nq), lambda b, i: (jnp.minimum(b, b_lat - 1),
                                                       jnp.where(b < b_lat, i, nt - 1), 0))
    attc_spec = pl.BlockSpec((1, tm, nq), lambda b, i: (0, jnp.where(b < b_lat, 0, i), 0))
    return pl.pallas_call(
        functools.partial(_hyb_out_kernel, seq_lat=t, seq_ctx=tc, b_lat=b_lat),
        grid=(nb, t // tm),
        in_specs=[
            row(d),
            pl.BlockSpec((1, N_MOD, d), lambda b, i: (b, 0, 0)),
            row(cc), row(cc),
            pl.BlockSpec((1, 8, cc), lambda b, i: (b, jnp.maximum(i * hb - 1, 0), 0)),
            pl.BlockSpec((1, 8, cc), lambda b, i: (b, jnp.minimum((i + 1) * hb, t // 8 - 1), 0)),
            att_spec, attc_spec,
            _resident(conv_w.shape),
            _resident(w_out.shape),
        ],
        out_specs=row(d),
        out_shape=jax.ShapeDtypeStruct((nb, t, d), F32),
        compiler_params=_cparams("parallel", "parallel"),
        name="hyb_out_proj",
    )(x, mod, gb, z, z, z, att, att_c, conv_w, w_out)


def _mla_in_kernel(x_ref, mod_ref, g_ref, wd_ref, qg_ref, kvg_ref, wuq_ref, wukv_ref,
                   cos_ref, sin_ref, qn_ref, qr_ref, kn_ref, v_ref, kr_ref, *, q_rank, kv_rank):
    n_nope = qn_ref.shape[2]
    q_scale = (MLA_NOPE + MLA_ROPE) ** -0.5 * LOG2E
    down = lambda rows: _dot(_modnorm(x_ref[0, rows, :], g_ref[...], mod_ref[0, 3:4, :],
                                      mod_ref[0, 4:5, :]).astype(BF16), wd_ref[...])
    chunks = _row_chunks(x_ref.shape[1], PROJ_ROW_CHUNK)
    d_next = down(chunks[0])
    for n, rows in enumerate(chunks):
        d = d_next
        if n + 1 < len(chunks):
            d_next = down(chunks[n + 1])
        cos = cos_ref[0, rows, :]
        sin = sin_ref[0, rows, :]
        q = _dot(_rms(d[:, :q_rank], qg_ref[...]).astype(BF16), wuq_ref[...])
        qn_ref[0, rows, :] = (q[:, :n_nope] * q_scale).astype(BF16)
        for j in range(qr_ref.shape[2] // LANES):
            qr_ref[0, rows, j * LANES:(j + 1) * LANES] = (
                _rope(q[:, n_nope + j * LANES:n_nope + (j + 1) * LANES], cos, sin) * q_scale
            ).astype(BF16)
        kv = _dot(_rms(d[:, q_rank:q_rank + kv_rank], kvg_ref[...]).astype(BF16), wukv_ref[...])
        kn_ref[0, rows, :] = kv[:, :n_nope].astype(BF16)
        v_ref[0, rows, :] = kv[:, n_nope:].astype(BF16)
        kr = _rope(d[:, q_rank + kv_rank:], cos, sin)
        first = (lax.broadcasted_iota(jnp.int32, kr.shape, 1) & (MLA_ROPE // 2)) == 0
        kr_ref[0, rows, :LANES] = jnp.where(first, kr, 0.0).astype(BF16)
        kr_ref[0, rows, LANES:] = jnp.where(first, 0.0, kr).astype(BF16)


def _mla_in_call(x, mod, g, wd, qg, kvg, wuq, wukv, cos2, sin2):
    nb, t, d = x.shape
    tm = min(512, t)
    q_rank, kv_rank = qg.shape[0], kvg.shape[0]
    n_nope = wukv.shape[1] // 2
    n_rope = wuq.shape[1] - n_nope
    row = lambda n: pl.BlockSpec((1, tm, n), lambda b, i: (b, i, 0))
    table = pl.BlockSpec((1, tm, LANES), lambda b, i: (jnp.where(b == nb - 1, 1, 0), i, 0))
    widths = (n_nope, n_rope, n_nope, n_nope, 2 * LANES)
    return pl.pallas_call(
        functools.partial(_mla_in_kernel, q_rank=q_rank, kv_rank=kv_rank),
        grid=(nb, t // tm),
        in_specs=[
            row(d),
            pl.BlockSpec((1, N_MOD, d), lambda b, i: (b, 0, 0)),
            _resident((1, d)),
            _resident(wd.shape), _resident((1, q_rank)), _resident((1, kv_rank)),
            _resident(wuq.shape), _resident(wukv.shape),
            table, table,
        ],
        out_specs=[row(n) for n in widths],
        out_shape=[jax.ShapeDtypeStruct((nb, t, n), BF16) for n in widths],
        compiler_params=_cparams("parallel", "parallel"),
        name="mla_in_proj",
    )(x, mod, g.reshape(1, d), wd, qg.reshape(1, q_rank), kvg.reshape(1, kv_rank), wuq, wukv,
      cos2, sin2)


def _mla_attn_kernel(qn_ref, qr_ref, knc_ref, krc_ref, vc_ref, knl_ref, krl_ref, vl_ref,
                     o_ref, k_ref, v_ref):
    tc = knc_ref.shape[1]

    @pl.when(pl.program_id(2) == 0)
    def _():
        k_ref[:tc, :LANES] = knc_ref[0]
        k_ref[:tc, LANES:] = krc_ref[0]
        k_ref[tc:, :LANES] = knl_ref[0]
        k_ref[tc:, LANES:] = krl_ref[0]
        v_ref[:tc, :] = vc_ref[0]
        v_ref[tc:, :] = vl_ref[0]

    def chain(rows):
        def store(o):
            o_ref[0, rows, :] = o
        return (lambda: jnp.concatenate([qn_ref[0, rows, :], qr_ref[0, rows, :]], axis=1)), store

    _attend_chains([chain(rows) for rows in _row_chunks(qn_ref.shape[1], MLA_ROW_CHUNK)],
                   [(k_ref[...], v_ref[...])])


def _mla_attn_call(qn, qr, kn, v, kr, b_lat, tc):
    nb, t, n_nope = qn.shape
    heads = n_nope // MLA_NOPE
    tq = min(2048, t)
    qspec = pl.BlockSpec((1, tq, LANES), lambda b, h, i: (b, i, h))
    qrspec = pl.BlockSpec((1, tq, LANES), lambda b, h, i: (b, i, h // 2))
    ctx = lambda f: pl.BlockSpec((1, tc, LANES), lambda b, h, i: (b_lat, b, f(h)))
    lat = lambda f: pl.BlockSpec((1, t, LANES), lambda b, h, i: (b, 0, f(h)))
    same = lambda h: h
    parity = lambda h: h % 2
    return pl.pallas_call(
        _mla_attn_kernel,
        grid=(b_lat, heads, t // tq),
        in_specs=[qspec, qrspec, ctx(same), ctx(parity), ctx(same), lat(same), lat(parity), lat(same)],
        out_specs=qspec,
        out_shape=jax.ShapeDtypeStruct((b_lat, t, n_nope), BF16),
        scratch_shapes=[pltpu.VMEM((tc + t, 2 * LANES), BF16), pltpu.VMEM((tc + t, LANES), BF16)],
        compiler_params=_cparams("parallel", "parallel", "arbitrary"),
        name="mla_attention",
    )(qn, qr, kn, kr, v, kn, kr, v)


def _mla_out_kernel(x_ref, mod_ref, att_ref, w_ref, o_ref):
    o_ref[0] = x_ref[0] + mod_ref[0, 5:6, :] * _dot(att_ref[0], w_ref[...])


def _mla_out_call(x, mod, att, w_o):
    nb, t, k = att.shape
    d = w_o.shape[1]
    tm = min(512, t)
    row = lambda n: pl.BlockSpec((1, tm, n), lambda b, i: (b, i, 0))
    return pl.pallas_call(
        _mla_out_kernel,
        grid=(nb, t // tm),
        in_specs=[row(d), pl.BlockSpec((1, N_MOD, d), lambda b, i: (b, 0, 0)), row(k),
                  _resident(w_o.shape)],
        out_specs=row(d),
        out_shape=jax.ShapeDtypeStruct((nb, t, d), F32),
        compiler_params=_cparams("parallel", "parallel"),
        name="mla_out_proj",
    )(x, mod, att, w_o)


def _rope_angles(n_tok, dim):
    n_rows = n_tok // GRID_W
    row = jnp.repeat(jnp.arange(n_rows), GRID_W).astype(F32)
    col = jnp.tile(jnp.arange(GRID_W), n_rows).astype(F32)
    half = dim // 2
    inv = 1.0 / (ROPE_THETA ** (jnp.arange(0, half, 2, dtype=F32) / half))
    return jnp.concatenate([row[:, None] * inv, col[:, None] * inv], axis=-1)


def _rope_tables(n_tok, dim):
    ang = _rope_angles(n_tok, dim)
    reps = LANES // dim
    cos = jnp.concatenate([jnp.cos(ang)] * (2 * reps), axis=-1)
    sin = jnp.concatenate([-jnp.sin(ang)] * reps + [jnp.sin(ang)] * reps, axis=-1)
    return (jnp.stack([cos, jnp.ones_like(cos)]), jnp.stack([sin, jnp.zeros_like(sin)]))


def _deinterleave(n):
    return np.concatenate([np.arange(0, n, 2), np.arange(1, n, 2)])


def kernel(x, c, ctx, c_ctx, mod_w, mod_b, norm_ffn1, norm_mix, norm_ffn2,
           ffn1_w_gate, ffn1_w_up, ffn1_w_down, ffn2_w_gate, ffn2_w_up, ffn2_w_down,
           hyb_w_in, hyb_conv_w, hyb_q_norm, hyb_k_norm, hyb_w_out,
           mla_w_down, mla_q_norm, mla_kv_norm, mla_w_uq, mla_w_ukv, mla_w_o, final_norm):
    b_lat, t, d = x.shape
    tc = ctx.shape[1]
    depth = mod_w.shape[0]
    assert depth == 2 and ctx.shape[0] * tc == t and b_lat + 1 <= MOD_ROWS
    assert t & (t - 1) == 0 and tc & (tc - 1) == 0 and t % GRID_W == 0
    nb = b_lat + 1

    cvec = jnp.concatenate([c, c_ctx[None], jnp.zeros((MOD_ROWS - nb, d), F32)], axis=0)
    mods = _mod_call(cvec, mod_w, mod_b).reshape(depth, MOD_ROWS, N_MOD, d)
    cast = lambda w: w.astype(BF16)
    f1 = _ffn_weights(ffn1_w_gate, ffn1_w_up, ffn1_w_down)
    f2 = _ffn_weights(ffn2_w_gate, ffn2_w_up, ffn2_w_down)

    mod = mods[0]
    xs = _ffn_call(x, nb, mod, norm_ffn1[0], *f1, 0, 0, ctx=ctx.reshape(1, t, d))
    cc = d // 2
    n_q = cc
    n_kv = GQA_KV_HEADS * HEAD_DIM
    w_in = hyb_w_in[0]
    perm = _deinterleave(HEAD_DIM)
    head_perm = lambda n: (np.arange(n).reshape(-1, HEAD_DIM)[:, perm]).reshape(-1)
    wq = w_in[:, 3 * cc:3 * cc + n_q][:, head_perm(n_q)]
    wk = w_in[:, 3 * cc + n_q:3 * cc + n_q + n_kv][:, head_perm(n_kv)]
    wv = w_in[:, 3 * cc + n_q + n_kv:]
    cos_a, sin_a = _rope_tables(t, HEAD_DIM)
    gb, z, q, k, v = _hyb_in_call(
        xs, mod, norm_mix[0], cast(w_in[:, :cc]), cast(w_in[:, cc:2 * cc]), cast(w_in[:, 2 * cc:3 * cc]),
        cast(wq), cast(wk), cast(wv), hyb_q_norm[0][perm], hyb_k_norm[0][perm], cos_a, sin_a)
    att = _gqa_latent_call(q, k, v, b_lat, tc)
    att_c = _gqa_ctx_call(q, k, v, b_lat, tc)
    xs = _hyb_out_call(xs, mod, gb, z, att, att_c, hyb_conv_w[0], cast(hyb_w_out[0]), b_lat, tc)
    xs = _ffn_call(xs, nb, mod, norm_ffn2[0], *f2, 0, 6)

    mod = mods[1]
    xs = _ffn_call(xs, nb, mod, norm_ffn1[1], *f1, 1, 0)
    heads = d // HEAD_DIM
    q_rank, kv_rank = mla_q_norm.shape[1], mla_kv_norm.shape[1]
    qk = MLA_NOPE + MLA_ROPE
    ev, od = np.arange(0, MLA_ROPE, 2), np.arange(1, MLA_ROPE, 2)
    nope_cols = (np.arange(heads)[:, None] * qk + np.arange(MLA_NOPE)[None]).reshape(-1)
    rope_cols = []
    for j in range(heads // 2):
        a0, b0 = 2 * j * qk + MLA_NOPE, (2 * j + 1) * qk + MLA_NOPE
        rope_cols += [a0 + ev, b0 + ev, a0 + od, b0 + od]
    wuq = mla_w_uq[0][:, np.concatenate([nope_cols] + rope_cols)]
    kvw = MLA_NOPE + MLA_V
    kn_cols = (np.arange(heads)[:, None] * kvw + np.arange(MLA_NOPE)[None]).reshape(-1)
    wukv = mla_w_ukv[0][:, np.concatenate([kn_cols, kn_cols + MLA_NOPE])]
    kr0 = q_rank + kv_rank
    wdn = mla_w_down[0][:, np.concatenate([np.arange(kr0), kr0 + ev, kr0 + ev, kr0 + od, kr0 + od])]
    cos_m, sin_m = _rope_tables(t, MLA_ROPE)
    qn, qr, kn, vv, kr = _mla_in_call(xs, mod, norm_mix[1], cast(wdn), mla_q_norm[0], mla_kv_norm[0],
                                      cast(wuq), cast(wukv), cos_m, sin_m)
    att = _mla_attn_call(qn, qr, kn, vv, kr, b_lat, tc)
    xs = _mla_out_call(xs, mod, att, cast(mla_w_o[0]))
    return _ffn_call(xs, b_lat, mod, norm_ffn2[1], *f2, 1, 6, final_g=final_norm)
```

```python
import functools

import numpy as np
import jax
import jax.numpy as jnp
from jax import lax
from jax.experimental import pallas as pl
from jax.experimental.pallas import tpu as pltpu

HEAD_DIM = 128
GRID_W = 64
ROPE_THETA = 10000.0
EPS = 1e-6
N_MOD = 9
GQA_KV_HEADS = 2
MLA_NOPE = 128
MLA_ROPE = 64
MLA_V = 128
LANES = 128
VMEM_LIMIT = 56 * 1024 * 1024
MOD_ROWS = 16
FFN_ROW_CHUNKS = 4
FFN_TILE = 512
FFN_SLOTS = 4
PROJ_ROW_CHUNK = 256
GQA_ROW_CHUNK = 512
MLA_ROW_CHUNK = 512
ATTN_SCORE_LOOKAHEAD = 1
LOG2E = 1.4426950408889634
CAST_BLOCK_BYTES = 4 * 1024 * 1024

F32 = jnp.float32
BF16 = jnp.bfloat16


def _cparams(*sem):
    return pltpu.CompilerParams(dimension_semantics=sem, vmem_limit_bytes=VMEM_LIMIT)


def _resident(shape):
    nd = len(shape)
    return pl.BlockSpec(shape, lambda *_: (0,) * nd, pipeline_mode=pl.Buffered(1))


def _rms(x, g):
    return x * lax.rsqrt(jnp.mean(x * x, axis=-1, keepdims=True) + EPS) * g


def _modnorm(x, g, shift, scale):
    return _rms(x, g) * (1.0 + scale) + shift


def _rope(x, cos, sin):
    return x * cos + pltpu.roll(x, LANES // 2, axis=1) * sin


def _dot(a, b):
    return jnp.dot(a, b, preferred_element_type=F32)


def _dot_t(a, b):
    return lax.dot_general(a, b, (((1,), (1,)), ((), ())), preferred_element_type=F32)


def _mod_kernel(c_ref, w_ref, b_ref, o_ref):
    c = c_ref[...]
    s = (c * jax.nn.sigmoid(c)).astype(BF16)
    o_ref[0] = _dot(s, w_ref[0].astype(BF16)) + b_ref[0]


def _mod_call(cvec, mod_w, mod_b):
    depth, d, n = mod_w.shape
    tn = min(1024, n)
    return pl.pallas_call(
        _mod_kernel,
        grid=(depth, n // tn),
        in_specs=[
            pl.BlockSpec((MOD_ROWS, d), lambda l, j: (0, 0)),
            pl.BlockSpec((1, d, tn), lambda l, j: (l, 0, j)),
            pl.BlockSpec((1, 1, tn), lambda l, j: (l, 0, j)),
        ],
        out_specs=pl.BlockSpec((1, MOD_ROWS, tn), lambda l, j: (l, 0, j)),
        out_shape=jax.ShapeDtypeStruct((depth, MOD_ROWS, n), F32),
        compiler_params=_cparams("parallel", "parallel"),
        name="adaln_mod",
    )(cvec, mod_w, mod_b.reshape(depth, 1, n))


def _swiglu_act(xn, wg_ref, wu_ref):
    hg = _dot(xn, wg_ref[...])
    hu = _dot(xn, wu_ref[...])
    return (hg * jax.nn.sigmoid(hg) * hu).astype(BF16)


def _ffn_kernel(x_ref, mod_ref, g_ref, wg_hbm, wu_hbm, wd_hbm, *rest, layer, m0, final, with_ctx):
    rest = list(rest)
    fn_ref = rest.pop(0) if final else None
    ctx_ref = rest.pop(0) if with_ctx else None
    o_ref, xn_ref, h_ref, wg_buf, wu_buf, wd_buf, sem = rest
    is_ctx = pl.program_id(0) == pl.num_programs(0) - 1
    n_ff = wg_hbm.shape[1]
    tm = xn_ref.shape[0]
    step = pl.program_id(0) * pl.num_programs(1) + pl.program_id(1)
    not_last_step = step < pl.num_programs(0) * pl.num_programs(1) - 1
    half_gate = 0.5 * mod_ref[0, m0 + 2:m0 + 3, :]

    def copies(blk):
        slot = blk % FFN_SLOTS
        pairs = ((wg_hbm, wg_buf), (wu_hbm, wu_buf), (wd_hbm, wd_buf))
        return [pltpu.make_async_copy(hbm.at[layer, blk], buf.at[slot], sem.at[k, slot])
                for k, (hbm, buf) in enumerate(pairs)]

    def start(blk):
        for c in copies(blk):
            c.start()

    def wait(blk):
        for c in copies(blk):
            c.wait()

    def x_rows(rows):
        x = x_ref[0, rows, :]
        return jnp.where(is_ctx, ctx_ref[0, rows, :], x) if with_ctx else x

    @pl.when(step == 0)
    def _():
        start(0)
        start(1)

    wait(0)
    rs = tm // FFN_ROW_CHUNKS
    for r in range(FFN_ROW_CHUNKS):
        rows = slice(r * rs, (r + 1) * rs)
        xn = _modnorm(x_rows(rows), g_ref[...], mod_ref[0, m0:m0 + 1, :],
                      mod_ref[0, m0 + 1:m0 + 2, :]).astype(BF16)
        xn_ref[rows, :] = xn
        h_ref[0, rows, :] = _swiglu_act(xn, wg_buf.at[0], wu_buf.at[0])

    for j in range(n_ff):
        if j + 2 < n_ff:
            start(j + 2)
        else:
            pl.when(not_last_step)(functools.partial(start, j + 2 - n_ff))
        if j + 1 < n_ff:
            wait(j + 1)
            nslot = (j + 1) % FFN_SLOTS
            h_ref[(j + 1) % 2] = _swiglu_act(xn_ref[...], wg_buf.at[nslot], wu_buf.at[nslot])
        contrib = half_gate * _dot(h_ref[j % 2], wd_buf[j % FFN_SLOTS])
        if j == 0:
            o_ref[0] = x_rows(slice(None)) + contrib
        else:
            o_ref[0] += contrib

    if final:
        o_ref[0] = _rms(o_ref[0], fn_ref[...])


def _ffn_call(x, nb, mod, g, wg, wu, wd, layer, m0, final_g=None, ctx=None):
    _, t, d = x.shape
    _, n_ff, _, tf = wg.shape
    assert n_ff >= 2 and all(
        max(b for b in range(n_ff) if b % FFN_SLOTS == k) <= n_ff - 3 + k for k in (0, 1))
    final = final_g is not None
    tm = min(512, t)
    nt = t // tm
    if ctx is None:
        x_spec = pl.BlockSpec((1, tm, d), lambda b, i: (b, i, 0))
    else:
        x_spec = pl.BlockSpec((1, tm, d), lambda b, i: (jnp.minimum(b, nb - 2),
                                                         jnp.where(b < nb - 1, i, nt - 1), 0))
    in_specs = [
        x_spec,
        pl.BlockSpec((1, N_MOD, d), lambda b, i: (b, 0, 0)),
        pl.BlockSpec((1, d), lambda b, i: (0, 0)),
        pl.BlockSpec(memory_space=pl.ANY),
        pl.BlockSpec(memory_space=pl.ANY),
        pl.BlockSpec(memory_space=pl.ANY),
    ]
    args = [x, mod, g.reshape(1, d), wg, wu, wd]
    if final:
        in_specs.append(pl.BlockSpec((1, d), lambda b, i: (0, 0)))
        args.append(final_g.reshape(1, d))
    if ctx is not None:
        in_specs.append(pl.BlockSpec((1, tm, d), lambda b, i: (0, jnp.where(b < nb - 1, 0, i), 0)))
        args.append(ctx)
    return pl.pallas_call(
        functools.partial(_ffn_kernel, layer=layer, m0=m0, final=final, with_ctx=ctx is not None),
        grid=(nb, nt),
        in_specs=in_specs,
        out_specs=pl.BlockSpec((1, tm, d), lambda b, i: (b, i, 0)),
        out_shape=jax.ShapeDtypeStruct((nb, t, d), F32),
        scratch_shapes=[
            pltpu.VMEM((tm, d), BF16),
            pltpu.VMEM((2, tm, tf), BF16),
            pltpu.VMEM((FFN_SLOTS, d, tf), BF16), pltpu.VMEM((FFN_SLOTS, d, tf), BF16),
            pltpu.VMEM((FFN_SLOTS, tf, d), BF16),
            pltpu.SemaphoreType.DMA((3, FFN_SLOTS)),
        ],
        compiler_params=_cparams("arbitrary", "arbitrary"),
        name="swiglu_half_step",
    )(*args)


def _cast_kernel(w_ref, o_ref):
    cb = o_ref.shape[3]
    for n in range(o_ref.shape[1]):
        o_ref[0, n] = w_ref[0, :, n * cb:(n + 1) * cb].astype(BF16)


def _cast_call(w, col_block):
    depth, r, c = w.shape
    nblk = c // col_block
    tr = max(n for n in range(16, r + 1, 16) if r % n == 0 and 4 * c * n <= CAST_BLOCK_BYTES)
    return pl.pallas_call(
        _cast_kernel,
        grid=(depth, r // tr),
        in_specs=[pl.BlockSpec((1, tr, c), lambda l, i: (l, i, 0))],
        out_specs=pl.BlockSpec((1, nblk, tr, col_block), lambda l, i: (l, 0, i, 0)),
        out_shape=jax.ShapeDtypeStruct((depth, nblk, r, col_block), BF16),
        compiler_params=_cparams("parallel", "parallel"),
        name="weight_cast",
    )(w)


def _ffn_weights(w_gate, w_up, w_down):
    depth, d, dff = w_gate.shape
    tf = min(FFN_TILE, dff)
    return (_cast_call(w_gate, tf), _cast_call(w_up, tf),
            _cast_call(w_down, d).reshape(depth, dff // tf, tf, d))


def _hyb_in_kernel(x_ref, mod_ref, g_ref, wgb_ref, wgc_ref, wu_ref, wq_ref, wk_ref, wvt_ref,
                   qg_ref, kg_ref, cos_ref, sin_ref,
                   gb_ref, z_ref, q_ref, k_ref, vt_ref):
    q_scale = HEAD_DIM ** -0.5 * LOG2E
    norm = lambda rows: _modnorm(x_ref[0, rows, :], g_ref[...], mod_ref[0, 3:4, :],
                                 mod_ref[0, 4:5, :]).astype(BF16)
    chunks = _row_chunks(x_ref.shape[1], PROJ_ROW_CHUNK)
    xn_next = norm(chunks[0])
    for n, rows in enumerate(chunks):
        xn = xn_next
        if n + 1 < len(chunks):
            xn_next = norm(chunks[n + 1])
        gb_ref[0, rows, :] = _dot(xn, wgb_ref[...]).astype(BF16)
        z_ref[0, rows, :] = (_dot(xn, wgc_ref[...]) * _dot(xn, wu_ref[...])).astype(BF16)
        vt_ref[0, :, rows] = _dot_t(wvt_ref[...], xn).astype(BF16)
        cos = cos_ref[0, rows, :]
        sin = sin_ref[0, rows, :]
        q = _dot(xn, wq_ref[...])
        for h in range(q.shape[1] // HEAD_DIM):
            sl = slice(h * HEAD_DIM, (h + 1) * HEAD_DIM)
            q_ref[0, rows, sl] = (_rope(_rms(q[:, sl], qg_ref[...]), cos, sin) * q_scale).astype(BF16)
        k = _dot(xn, wk_ref[...])
        for h in range(k.shape[1] // HEAD_DIM):
            sl = slice(h * HEAD_DIM, (h + 1) * HEAD_DIM)
            k_ref[0, rows, sl] = _rope(_rms(k[:, sl], kg_ref[...]), cos, sin).astype(BF16)


def _hyb_in_call(x, mod, g, wgb, wgc, wu, wq, wk, wvt, qg, kg, cos2, sin2):
    nb, t, d = x.shape
    tm = min(512, t)
    cc, nq, nkv = wgb.shape[1], wq.shape[1], wk.shape[1]
    row = lambda n: pl.BlockSpec((1, tm, n), lambda b, i: (b, i, 0))
    table = pl.BlockSpec((1, tm, LANES), lambda b, i: (jnp.where(b == nb - 1, 1, 0), i, 0))
    return pl.pallas_call(
        _hyb_in_kernel,
        grid=(nb, t // tm),
        in_specs=[
            row(d),
            pl.BlockSpec((1, N_MOD, d), lambda b, i: (b, 0, 0)),
            _resident((1, d)),
            _resident(wgb.shape), _resident(wgc.shape), _resident(wu.shape),
            _resident(wq.shape), _resident(wk.shape), _resident(wvt.shape),
            _resident((1, HEAD_DIM)), _resident((1, HEAD_DIM)),
            table, table,
        ],
        out_specs=[row(cc), row(cc), row(nq), row(nkv),
                   pl.BlockSpec((1, nkv, tm), lambda b, i: (b, 0, i))],
        out_shape=[jax.ShapeDtypeStruct((nb, t, n), BF16) for n in (cc, cc, nq, nkv)]
        + [jax.ShapeDtypeStruct((nb, nkv, t), BF16)],
        compiler_params=_cparams("parallel", "parallel"),
        name="hyb_in_proj",
    )(x, mod, g.reshape(1, d), wgb, wgc, wu, wq, wk, wvt,
      qg.reshape(1, HEAD_DIM), kg.reshape(1, HEAD_DIM), cos2, sin2)


def _softmax_pv(s_list, kv_list):
    m = s_list[0].max(axis=-1, keepdims=True)
    for s in s_list[1:]:
        m = jnp.maximum(m, s.max(axis=-1, keepdims=True))
    l = 0.0
    o = 0.0
    for s, (_, v) in zip(s_list, kv_list):
        p = jnp.exp2(s - m)
        l = l + p.sum(axis=-1, keepdims=True)
        o = o + _dot(p.astype(BF16), v)
    return o / l


def _attend_chains(chains, kv_list):
    scores = lambda load_q: [_dot_t(load_q(), k) for k, _ in kv_list]
    ahead = [scores(load_q) for load_q, _ in chains[:ATTN_SCORE_LOOKAHEAD]]
    for n, (_, store_o) in enumerate(chains):
        s_list = ahead.pop(0)
        if n + ATTN_SCORE_LOOKAHEAD < len(chains):
            ahead.append(scores(chains[n + ATTN_SCORE_LOOKAHEAD][0]))
        store_o(_softmax_pv(s_list, kv_list).astype(BF16))


def _softmax_pv_t(st_list, kvt_list):
    m = st_list[0].max(axis=0, keepdims=True)
    for s in st_list[1:]:
        m = jnp.maximum(m, s.max(axis=0, keepdims=True))
    l = 0.0
    o = 0.0
    for s, (_, vt) in zip(st_list, kvt_list):
        p = jnp.exp2(s - m)
        l = l + p.sum(axis=0, keepdims=True)
        o = o + _dot(vt, p.astype(BF16))
    return (o / l).T


def _attend_chains_t(chains, kvt_list):
    scores = lambda load_q: [_dot_t(k, load_q()) for k, _ in kvt_list]
    ahead = [scores(load_q) for load_q, _ in chains[:ATTN_SCORE_LOOKAHEAD]]
    for n, (_, store_o) in enumerate(chains):
        st_list = ahead.pop(0)
        if n + ATTN_SCORE_LOOKAHEAD < len(chains):
            ahead.append(scores(chains[n + ATTN_SCORE_LOOKAHEAD][0]))
        store_o(_softmax_pv_t(st_list, kvt_list).astype(BF16))


def _row_chunks(n, c):
    c = min(c, n)
    return [slice(r * c, (r + 1) * c) for r in range(n // c)]


def _gqa_kernel(*refs, has_lat):
    if has_lat:
        q_ref, kc_ref, vtc_ref, kl_ref, vtl_ref, o_ref = refs
    else:
        q_ref, kc_ref, vtc_ref, o_ref = refs
    kvt_list = [(kc_ref[0], vtc_ref[0])]
    if has_lat:
        kvt_list.append((kl_ref[0], vtl_ref[0]))

    def chain(rows, sl):
        def store(o):
            o_ref[0, rows, sl] = o
        return (lambda: q_ref[0, rows, sl]), store

    _attend_chains_t([chain(rows, slice(h * HEAD_DIM, (h + 1) * HEAD_DIM))
                      for h in range(q_ref.shape[2] // HEAD_DIM)
                      for rows in _row_chunks(q_ref.shape[1], GQA_ROW_CHUNK)], kvt_list)


def _gqa_latent_call(q, k, vt, b_lat, tc):
    nb, t, nq = q.shape
    gw = nq // GQA_KV_HEADS
    tq = min(512, t)
    ctx = pl.BlockSpec((1, tc, HEAD_DIM), lambda b, g, i: (b_lat, b, g))
    lat = pl.BlockSpec((1, t, HEAD_DIM), lambda b, g, i: (b, 0, g))
    vt_ctx = pl.BlockSpec((1, HEAD_DIM, tc), lambda b, g, i: (b_lat, g, b))
    vt_lat = pl.BlockSpec((1, HEAD_DIM, t), lambda b, g, i: (b, g, 0))
    qo = pl.BlockSpec((1, tq, gw), lambda b, g, i: (b, i, g))
    return pl.pallas_call(
        functools.partial(_gqa_kernel, has_lat=True),
        grid=(b_lat, GQA_KV_HEADS, t // tq),
        in_specs=[qo, ctx, vt_ctx, lat, vt_lat],
        out_specs=qo,
        out_shape=jax.ShapeDtypeStruct((b_lat, t, nq), BF16),
        compiler_params=_cparams("parallel", "parallel", "parallel"),
        name="gqa_latent",
    )(q, k, vt, k, vt)


def _gqa_ctx_call(q, k, vt, b_lat, tc):
    nb, t, nq = q.shape
    gw = nq // GQA_KV_HEADS
    ctx = pl.BlockSpec((1, tc, HEAD_DIM), lambda b, g: (b_lat, b, g))
    vt_ctx = pl.BlockSpec((1, HEAD_DIM, tc), lambda b, g: (b_lat, g, b))
    return pl.pallas_call(
        functools.partial(_gqa_kernel, has_lat=False),
        grid=(t // tc, GQA_KV_HEADS),
        in_specs=[pl.BlockSpec((1, tc, gw), lambda b, g: (b_lat, b, g)), ctx, vt_ctx],
        out_specs=pl.BlockSpec((1, tc, gw), lambda b, g: (0, b, g)),
        out_shape=jax.ShapeDtypeStruct((1, t, nq), BF16),
        compiler_params=_cparams("parallel", "parallel"),
        name="gqa_context",
    )(q, k, vt)


def _hyb_out_kernel(x_ref, mod_ref, gb_ref, z_ref, zp_ref, zn_ref, att_ref, attc_ref, cw_ref, w_ref,
                    o_ref, *, seq_lat, seq_ctx, b_lat):
    b, i = pl.program_id(0), pl.program_id(1)
    tm, cc = z_ref.shape[1], z_ref.shape[2]
    acc = _dot(jnp.where(b == b_lat, attc_ref[0], att_ref[0]), w_ref[cc:, :])
    z = z_ref[0].astype(F32)
    rows = lax.broadcasted_iota(jnp.int32, (tm, cc), 0)
    seq = jnp.where(b == b_lat, seq_ctx, seq_lat)
    pos = (i * tm + rows) & (seq - 1)
    z_prev = jnp.where(rows == 0, zp_ref[0, 7:8, :].astype(F32), pltpu.roll(z, 1, axis=0))
    z_next = jnp.where(rows == tm - 1, zn_ref[0, 0:1, :].astype(F32),
                       pltpu.roll(z, tm - 1, axis=0))
    z_prev = jnp.where(pos == 0, 0.0, z_prev)
    z_next = jnp.where(pos == seq - 1, 0.0, z_next)
    conv = z_prev * cw_ref[0:1, :] + z * cw_ref[1:2, :] + z_next * cw_ref[2:3, :]
    acc = acc + _dot((gb_ref[0].astype(F32) * conv).astype(BF16), w_ref[:cc, :])
    o_ref[0] = x_ref[0] + mod_ref[0, 5:6, :] * acc


def _hyb_out_call(x, mod, gb, z, att, att_c, conv_w, w_out, b_lat, tc):
    nb, t, d = x.shape
    cc = gb.shape[2]
    tm = min(512, t)
    nt = t // tm
    hb = tm // 8
    row = lambda n: pl.BlockSpec((1, tm, n), lambda b, i: (b, i, 0))
    nq = att.shape[2]
    att_spec = pl.BlockSpec((1, tm, nq), lambda b, i: (jnp.minimum(b, b_lat - 1),
                                                       jnp.where(b < b_lat, i, nt - 1), 0))
    attc_spec = pl.BlockSpec((1, tm, nq), lambda b, i: (0, jnp.where(b < b_lat, 0, i), 0))
    return pl.pallas_call(
        functools.partial(_hyb_out_kernel, seq_lat=t, seq_ctx=tc, b_lat=b_lat),
        grid=(nb, t // tm),
        in_specs=[
            row(d),
            pl.BlockSpec((1, N_MOD, d), lambda b, i: (b, 0, 0)),
            row(cc), row(cc),
            pl.BlockSpec((1, 8, cc), lambda b, i: (b, jnp.maximum(i * hb - 1, 0), 0)),
            pl.BlockSpec((1, 8, cc), lambda b, i: (b, jnp.minimum((i + 1) * hb, t // 8 - 1), 0)),
            att_spec, attc_spec,
            _resident(conv_w.shape),
            _resident(w_out.shape),
        ],
        out_specs=row(d),
        out_shape=jax.ShapeDtypeStruct((nb, t, d), F32),
        compiler_params=_cparams("parallel", "parallel"),
        name="hyb_out_proj",
    )(x, mod, gb, z, z, z, att, att_c, conv_w, w_out)


def _mla_in_kernel(x_ref, mod_ref, g_ref, wd_ref, qg_ref, kvg_ref, wuq_ref, wuk_ref, wuvt_ref,
                   cos_ref, sin_ref, qn_ref, qr_ref, kn_ref, vt_ref, kr_ref, *, q_rank, kv_rank):
    n_nope = qn_ref.shape[2]
    q_scale = (MLA_NOPE + MLA_ROPE) ** -0.5 * LOG2E
    down = lambda rows: _dot(_modnorm(x_ref[0, rows, :], g_ref[...], mod_ref[0, 3:4, :],
                                      mod_ref[0, 4:5, :]).astype(BF16), wd_ref[...])
    chunks = _row_chunks(x_ref.shape[1], PROJ_ROW_CHUNK)
    d_next = down(chunks[0])
    for n, rows in enumerate(chunks):
        d = d_next
        if n + 1 < len(chunks):
            d_next = down(chunks[n + 1])
        cos = cos_ref[0, rows, :]
        sin = sin_ref[0, rows, :]
        q = _dot(_rms(d[:, :q_rank], qg_ref[...]).astype(BF16), wuq_ref[...])
        qn_ref[0, rows, :] = (q[:, :n_nope] * q_scale).astype(BF16)
        for j in range(qr_ref.shape[2] // LANES):
            qr_ref[0, rows, j * LANES:(j + 1) * LANES] = (
                _rope(q[:, n_nope + j * LANES:n_nope + (j + 1) * LANES], cos, sin) * q_scale
            ).astype(BF16)
        ckv = _rms(d[:, q_rank:q_rank + kv_rank], kvg_ref[...]).astype(BF16)
        kn_ref[0, rows, :] = _dot(ckv, wuk_ref[...]).astype(BF16)
        vt_ref[0, :, rows] = _dot_t(wuvt_ref[...], ckv).astype(BF16)
        kr = _rope(d[:, q_rank + kv_rank:], cos, sin)
        first = (lax.broadcasted_iota(jnp.int32, kr.shape, 1) & (MLA_ROPE // 2)) == 0
        kr_ref[0, rows, :LANES] = jnp.where(first, kr, 0.0).astype(BF16)
        kr_ref[0, rows, LANES:] = jnp.where(first, 0.0, kr).astype(BF16)


def _mla_in_call(x, mod, g, wd, qg, kvg, wuq, wuk, wuvt, cos2, sin2):
    nb, t, d = x.shape
    tm = min(512, t)
    q_rank, kv_rank = qg.shape[0], kvg.shape[0]
    n_nope = wuk.shape[1]
    n_rope = wuq.shape[1] - n_nope
    n_v = wuvt.shape[0]
    row = lambda n: pl.BlockSpec((1, tm, n), lambda b, i: (b, i, 0))
    table = pl.BlockSpec((1, tm, LANES), lambda b, i: (jnp.where(b == nb - 1, 1, 0), i, 0))
    out_specs = [row(n_nope), row(n_rope), row(n_nope),
                 pl.BlockSpec((1, n_v, tm), lambda b, i: (b, 0, i)), row(2 * LANES)]
    out_shapes = [(nb, t, n_nope), (nb, t, n_rope), (nb, t, n_nope), (nb, n_v, t), (nb, t, 2 * LANES)]
    return pl.pallas_call(
        functools.partial(_mla_in_kernel, q_rank=q_rank, kv_rank=kv_rank),
        grid=(nb, t // tm),
        in_specs=[
            row(d),
            pl.BlockSpec((1, N_MOD, d), lambda b, i: (b, 0, 0)),
            _resident((1, d)),
            _resident(wd.shape), _resident((1, q_rank)), _resident((1, kv_rank)),
            _resident(wuq.shape), _resident(wuk.shape), _resident(wuvt.shape),
            table, table,
        ],
        out_specs=out_specs,
        out_shape=[jax.ShapeDtypeStruct(s, BF16) for s in out_shapes],
        compiler_params=_cparams("parallel", "parallel"),
        name="mla_in_proj",
    )(x, mod, g.reshape(1, d), wd, qg.reshape(1, q_rank), kvg.reshape(1, kv_rank), wuq, wuk, wuvt,
      cos2, sin2)


def _mla_attn_kernel(qn_ref, qr_ref, knc_ref, krc_ref, vtc_ref, knl_ref, krl_ref, vtl_ref,
                     o_ref, k_ref, vt_ref):
    tc = knc_ref.shape[1]

    @pl.when(pl.program_id(2) == 0)
    def _():
        k_ref[:tc, :LANES] = knc_ref[0]
        k_ref[:tc, LANES:] = krc_ref[0]
        k_ref[tc:, :LANES] = knl_ref[0]
        k_ref[tc:, LANES:] = krl_ref[0]
        vt_ref[:, :tc] = vtc_ref[0]
        vt_ref[:, tc:] = vtl_ref[0]

    def chain(rows):
        def store(o):
            o_ref[0, rows, :] = o
        return (lambda: jnp.concatenate([qn_ref[0, rows, :], qr_ref[0, rows, :]], axis=1)), store

    _attend_chains_t([chain(rows) for rows in _row_chunks(qn_ref.shape[1], MLA_ROW_CHUNK)],
                     [(k_ref[...], vt_ref[...])])


def _mla_attn_call(qn, qr, kn, vt, kr, b_lat, tc):
    nb, t, n_nope = qn.shape
    heads = n_nope // MLA_NOPE
    tq = min(2048, t)
    qspec = pl.BlockSpec((1, tq, LANES), lambda b, h, i: (b, i, h))
    qrspec = pl.BlockSpec((1, tq, LANES), lambda b, h, i: (b, i, h // 2))
    ctx = lambda f: pl.BlockSpec((1, tc, LANES), lambda b, h, i: (b_lat, b, f(h)))
    lat = lambda f: pl.BlockSpec((1, t, LANES), lambda b, h, i: (b, 0, f(h)))
    vt_ctx = pl.BlockSpec((1, MLA_V, tc), lambda b, h, i: (b_lat, h, b))
    vt_lat = pl.BlockSpec((1, MLA_V, t), lambda b, h, i: (b, h, 0))
    same = lambda h: h
    parity = lambda h: h % 2
    return pl.pallas_call(
        _mla_attn_kernel,
        grid=(b_lat, heads, t // tq),
        in_specs=[qspec, qrspec, ctx(same), ctx(parity), vt_ctx, lat(same), lat(parity), vt_lat],
        out_specs=qspec,
        out_shape=jax.ShapeDtypeStruct((b_lat, t, n_nope), BF16),
        scratch_shapes=[pltpu.VMEM((tc + t, 2 * LANES), BF16), pltpu.VMEM((MLA_V, tc + t), BF16)],
        compiler_params=_cparams("parallel", "parallel", "arbitrary"),
        name="mla_attention",
    )(qn, qr, kn, kr, vt, kn, kr, vt)


def _mla_out_kernel(x_ref, mod_ref, att_ref, w_ref, o_ref):
    o_ref[0] = x_ref[0] + mod_ref[0, 5:6, :] * _dot(att_ref[0], w_ref[...])


def _mla_out_call(x, mod, att, w_o):
    nb, t, k = att.shape
    d = w_o.shape[1]
    tm = min(512, t)
    row = lambda n: pl.BlockSpec((1, tm, n), lambda b, i: (b, i, 0))
    return pl.pallas_call(
        _mla_out_kernel,
        grid=(nb, t // tm),
        in_specs=[row(d), pl.BlockSpec((1, N_MOD, d), lambda b, i: (b, 0, 0)), row(k),
                  _resident(w_o.shape)],
        out_specs=row(d),
        out_shape=jax.ShapeDtypeStruct((nb, t, d), F32),
        compiler_params=_cparams("parallel", "parallel"),
        name="mla_out_proj",
    )(x, mod, att, w_o)


def _rope_angles(n_tok, dim):
    n_rows = n_tok // GRID_W
    row = jnp.repeat(jnp.arange(n_rows), GRID_W).astype(F32)
    col = jnp.tile(jnp.arange(GRID_W), n_rows).astype(F32)
    half = dim // 2
    inv = 1.0 / (ROPE_THETA ** (jnp.arange(0, half, 2, dtype=F32) / half))
    return jnp.concatenate([row[:, None] * inv, col[:, None] * inv], axis=-1)


def _rope_tables(n_tok, dim):
    ang = _rope_angles(n_tok, dim)
    reps = LANES // dim
    cos = jnp.concatenate([jnp.cos(ang)] * (2 * reps), axis=-1)
    sin = jnp.concatenate([-jnp.sin(ang)] * reps + [jnp.sin(ang)] * reps, axis=-1)
    return (jnp.stack([cos, jnp.ones_like(cos)]), jnp.stack([sin, jnp.zeros_like(sin)]))


def _deinterleave(n):
    return np.concatenate([np.arange(0, n, 2), np.arange(1, n, 2)])


def kernel(x, c, ctx, c_ctx, mod_w, mod_b, norm_ffn1, norm_mix, norm_ffn2,
           ffn1_w_gate, ffn1_w_up, ffn1_w_down, ffn2_w_gate, ffn2_w_up, ffn2_w_down,
           hyb_w_in, hyb_conv_w, hyb_q_norm, hyb_k_norm, hyb_w_out,
           mla_w_down, mla_q_norm, mla_kv_norm, mla_w_uq, mla_w_ukv, mla_w_o, final_norm):
    b_lat, t, d = x.shape
    tc = ctx.shape[1]
    depth = mod_w.shape[0]
    assert depth == 2 and ctx.shape[0] * tc == t and b_lat + 1 <= MOD_ROWS
    assert t & (t - 1) == 0 and tc & (tc - 1) == 0 and t % GRID_W == 0
    nb = b_lat + 1

    cvec = jnp.concatenate([c, c_ctx[None], jnp.zeros((MOD_ROWS - nb, d), F32)], axis=0)
    mods = _mod_call(cvec, mod_w, mod_b).reshape(depth, MOD_ROWS, N_MOD, d)
    cast = lambda w: w.astype(BF16)
    f1 = _ffn_weights(ffn1_w_gate, ffn1_w_up, ffn1_w_down)
    f2 = _ffn_weights(ffn2_w_gate, ffn2_w_up, ffn2_w_down)

    mod = mods[0]
    xs = _ffn_call(x, nb, mod, norm_ffn1[0], *f1, 0, 0, ctx=ctx.reshape(1, t, d))
    cc = d // 2
    n_q = cc
    n_kv = GQA_KV_HEADS * HEAD_DIM
    w_in = hyb_w_in[0]
    perm = _deinterleave(HEAD_DIM)
    head_perm = lambda n: (np.arange(n).reshape(-1, HEAD_DIM)[:, perm]).reshape(-1)
    wq = w_in[:, 3 * cc:3 * cc + n_q][:, head_perm(n_q)]
    wk = w_in[:, 3 * cc + n_q:3 * cc + n_q + n_kv][:, head_perm(n_kv)]
    wvt = w_in[:, 3 * cc + n_q + n_kv:].T
    cos_a, sin_a = _rope_tables(t, HEAD_DIM)
    gb, z, q, k, v = _hyb_in_call(
        xs, mod, norm_mix[0], cast(w_in[:, :cc]), cast(w_in[:, cc:2 * cc]), cast(w_in[:, 2 * cc:3 * cc]),
        cast(wq), cast(wk), cast(wvt), hyb_q_norm[0][perm], hyb_k_norm[0][perm], cos_a, sin_a)
    att = _gqa_latent_call(q, k, v, b_lat, tc)
    att_c = _gqa_ctx_call(q, k, v, b_lat, tc)
    xs = _hyb_out_call(xs, mod, gb, z, att, att_c, hyb_conv_w[0], cast(hyb_w_out[0]), b_lat, tc)
    xs = _ffn_call(xs, nb, mod, norm_ffn2[0], *f2, 0, 6)

    mod = mods[1]
    xs = _ffn_call(xs, nb, mod, norm_ffn1[1], *f1, 1, 0)
    heads = d // HEAD_DIM
    q_rank, kv_rank = mla_q_norm.shape[1], mla_kv_norm.shape[1]
    qk = MLA_NOPE + MLA_ROPE
    ev, od = np.arange(0, MLA_ROPE, 2), np.arange(1, MLA_ROPE, 2)
    nope_cols = (np.arange(heads)[:, None] * qk + np.arange(MLA_NOPE)[None]).reshape(-1)
    rope_cols = []
    for j in range(heads // 2):
        a0, b0 = 2 * j * qk + MLA_NOPE, (2 * j + 1) * qk + MLA_NOPE
        rope_cols += [a0 + ev, b0 + ev, a0 + od, b0 + od]
    wuq = mla_w_uq[0][:, np.concatenate([nope_cols] + rope_cols)]
    kvw = MLA_NOPE + MLA_V
    kn_cols = (np.arange(heads)[:, None] * kvw + np.arange(MLA_NOPE)[None]).reshape(-1)
    wuk = mla_w_ukv[0][:, kn_cols]
    wuvt = mla_w_ukv[0][:, kn_cols + MLA_NOPE].T
    kr0 = q_rank + kv_rank
    wdn = mla_w_down[0][:, np.concatenate([np.arange(kr0), kr0 + ev, kr0 + ev, kr0 + od, kr0 + od])]
    cos_m, sin_m = _rope_tables(t, MLA_ROPE)
    qn, qr, kn, vt, kr = _mla_in_call(xs, mod, norm_mix[1], cast(wdn), mla_q_norm[0], mla_kv_norm[0],
                                      cast(wuq), cast(wuk), cast(wuvt), cos_m, sin_m)
    att = _mla_attn_call(qn, qr, kn, vt, kr, b_lat, tc)
    xs = _mla_out_call(xs, mod, att, cast(mla_w_o[0]))
    return _ffn_call(xs, b_lat, mod, norm_ffn2[1], *f2, 1, 6, final_g=final_norm)
```

```python
import functools

import numpy as np
import jax
import jax.numpy as jnp
from jax import lax
from jax.experimental import pallas as pl
from jax.experimental.pallas import tpu as pltpu

HEAD_DIM = 128
GRID_W = 64
ROPE_THETA = 10000.0
EPS = 1e-6
N_MOD = 9
GQA_KV_HEADS = 2
MLA_NOPE = 128
MLA_ROPE = 64
MLA_V = 128
LANES = 128
VMEM_LIMIT = 56 * 1024 * 1024
MOD_ROWS = 16
FFN_ROW_CHUNKS = 4
FFN_TILE = 512
FFN_SLOTS = 4
PROJ_ROW_CHUNK = 256
GQA_ROW_CHUNK = 512
MLA_ROW_CHUNK = 512
ATTN_SCORE_LOOKAHEAD = 2
LOG2E = 1.4426950408889634
CAST_BLOCK_BYTES = 4 * 1024 * 1024

F32 = jnp.float32
BF16 = jnp.bfloat16


def _cparams(*sem):
    return pltpu.CompilerParams(dimension_semantics=sem, vmem_limit_bytes=VMEM_LIMIT)


def _resident(shape):
    nd = len(shape)
    return pl.BlockSpec(shape, lambda *_: (0,) * nd, pipeline_mode=pl.Buffered(1))


def _rms(x, g):
    return x * lax.rsqrt(jnp.mean(x * x, axis=-1, keepdims=True) + EPS) * g


def _modnorm(x, g, shift, scale):
    return _rms(x, g) * (1.0 + scale) + shift


def _rope(x, cos, sin):
    return x * cos + pltpu.roll(x, LANES // 2, axis=1) * sin


def _dot(a, b):
    return jnp.dot(a, b, preferred_element_type=F32)


def _dot_t(a, b):
    return lax.dot_general(a, b, (((1,), (1,)), ((), ())), preferred_element_type=F32)


def _mod_kernel(c_ref, w_ref, b_ref, o_ref):
    c = c_ref[...]
    s = (c * jax.nn.sigmoid(c)).astype(BF16)
    o_ref[0] = _dot(s, w_ref[0].astype(BF16)) + b_ref[0]


def _mod_call(cvec, mod_w, mod_b):
    depth, d, n = mod_w.shape
    tn = min(1024, n)
    return pl.pallas_call(
        _mod_kernel,
        grid=(depth, n // tn),
        in_specs=[
            pl.BlockSpec((MOD_ROWS, d), lambda l, j: (0, 0)),
            pl.BlockSpec((1, d, tn), lambda l, j: (l, 0, j)),
            pl.BlockSpec((1, 1, tn), lambda l, j: (l, 0, j)),
        ],
        out_specs=pl.BlockSpec((1, MOD_ROWS, tn), lambda l, j: (l, 0, j)),
        out_shape=jax.ShapeDtypeStruct((depth, MOD_ROWS, n), F32),
        compiler_params=_cparams("parallel", "parallel"),
        name="adaln_mod",
    )(cvec, mod_w, mod_b.reshape(depth, 1, n))


def _swiglu_act(xn, wg_ref, wu_ref, between=None):
    hg = _dot(xn, wg_ref[...])
    if between is not None:
        between()
    hu = _dot(xn, wu_ref[...])
    return (hg * jax.nn.sigmoid(hg) * hu).astype(BF16)


def _ffn_kernel(x_ref, mod_ref, g_ref, wg_hbm, wu_hbm, wd_hbm, *rest, layer, m0, final, with_ctx):
    rest = list(rest)
    fn_ref = rest.pop(0) if final else None
    ctx_ref = rest.pop(0) if with_ctx else None
    o_ref, xn_ref, h_ref, wg_buf, wu_buf, wd_buf, sem = rest
    is_ctx = pl.program_id(0) == pl.num_programs(0) - 1
    n_ff = wg_hbm.shape[1]
    tm = xn_ref.shape[0]
    step = pl.program_id(0) * pl.num_programs(1) + pl.program_id(1)
    not_last_step = step < pl.num_programs(0) * pl.num_programs(1) - 1
    half_gate = 0.5 * mod_ref[0, m0 + 2:m0 + 3, :]

    def copies(blk):
        slot = blk % FFN_SLOTS
        pairs = ((wg_hbm, wg_buf), (wu_hbm, wu_buf), (wd_hbm, wd_buf))
        return [pltpu.make_async_copy(hbm.at[layer, blk], buf.at[slot], sem.at[k, slot])
                for k, (hbm, buf) in enumerate(pairs)]

    def start(blk):
        for c in copies(blk):
            c.start()

    def wait(blk):
        for c in copies(blk):
            c.wait()

    def x_rows(rows):
        x = x_ref[0, rows, :]
        return jnp.where(is_ctx, ctx_ref[0, rows, :], x) if with_ctx else x

    @pl.when(step == 0)
    def _():
        start(0)
        start(1)

    def ring_ops(j):
        def run():
            if j + 2 < n_ff:
                wait(j + 2)
            if j + 3 < n_ff:
                start(j + 3)
            if j >= n_ff - 2:
                pl.when(not_last_step)(functools.partial(start, j - (n_ff - 2)))
        return run

    wait(0)
    rs = tm // FFN_ROW_CHUNKS
    for r in range(FFN_ROW_CHUNKS):
        rows = slice(r * rs, (r + 1) * rs)
        xn = _modnorm(x_rows(rows), g_ref[...], mod_ref[0, m0:m0 + 1, :],
                      mod_ref[0, m0 + 1:m0 + 2, :]).astype(BF16)
        xn_ref[rows, :] = xn
        h_ref[0, rows, :] = _swiglu_act(xn, wg_buf.at[0], wu_buf.at[0],
                                        ring_ops(-1) if r == 0 else None)

    for j in range(n_ff):
        if j + 1 < n_ff:
            nslot = (j + 1) % FFN_SLOTS
            h_ref[(j + 1) % 2] = _swiglu_act(xn_ref[...], wg_buf.at[nslot], wu_buf.at[nslot],
                                             ring_ops(j))
        else:
            ring_ops(j)()
        contrib = half_gate * _dot(h_ref[j % 2], wd_buf[j % FFN_SLOTS])
        if j == 0:
            o_ref[0] = x_rows(slice(None)) + contrib
        else:
            o_ref[0] += contrib

    if final:
        o_ref[0] = _rms(o_ref[0], fn_ref[...])


def _ffn_call(x, nb, mod, g, wg, wu, wd, layer, m0, final_g=None, ctx=None):
    _, t, d = x.shape
    _, n_ff, _, tf = wg.shape
    assert n_ff >= 2 and all(
        max(b for b in range(n_ff) if b % FFN_SLOTS == k) <= n_ff - 3 + k for k in (0, 1))
    final = final_g is not None
    tm = min(512, t)
    nt = t // tm
    if ctx is None:
        x_spec = pl.BlockSpec((1, tm, d), lambda b, i: (b, i, 0))
    else:
        x_spec = pl.BlockSpec((1, tm, d), lambda b, i: (jnp.minimum(b, nb - 2),
                                                         jnp.where(b < nb - 1, i, nt - 1), 0))
    in_specs = [
        x_spec,
        pl.BlockSpec((1, N_MOD, d), lambda b, i: (b, 0, 0)),
        pl.BlockSpec((1, d), lambda b, i: (0, 0)),
        pl.BlockSpec(memory_space=pl.ANY),
        pl.BlockSpec(memory_space=pl.ANY),
        pl.BlockSpec(memory_space=pl.ANY),
    ]
    args = [x, mod, g.reshape(1, d), wg, wu, wd]
    if final:
        in_specs.append(pl.BlockSpec((1, d), lambda b, i: (0, 0)))
        args.append(final_g.reshape(1, d))
    if ctx is not None:
        in_specs.append(pl.BlockSpec((1, tm, d), lambda b, i: (0, jnp.where(b < nb - 1, 0, i), 0)))
        args.append(ctx)
    return pl.pallas_call(
        functools.partial(_ffn_kernel, layer=layer, m0=m0, final=final, with_ctx=ctx is not None),
        grid=(nb, nt),
        in_specs=in_specs,
        out_specs=pl.BlockSpec((1, tm, d), lambda b, i: (b, i, 0)),
        out_shape=jax.ShapeDtypeStruct((nb, t, d), F32),
        scratch_shapes=[
            pltpu.VMEM((tm, d), BF16),
            pltpu.VMEM((2, tm, tf), BF16),
            pltpu.VMEM((FFN_SLOTS, d, tf), BF16), pltpu.VMEM((FFN_SLOTS, d, tf), BF16),
            pltpu.VMEM((FFN_SLOTS, tf, d), BF16),
            pltpu.SemaphoreType.DMA((3, FFN_SLOTS)),
        ],
        compiler_params=_cparams("arbitrary", "arbitrary"),
        name="swiglu_half_step",
    )(*args)


def _cast_kernel(w_ref, o_ref):
    cb = o_ref.shape[3]
    for n in range(o_ref.shape[1]):
        o_ref[0, n] = w_ref[0, :, n * cb:(n + 1) * cb].astype(BF16)


def _cast_call(w, col_block):
    depth, r, c = w.shape
    nblk = c // col_block
    tr = max(n for n in range(16, r + 1, 16) if r % n == 0 and 4 * c * n <= CAST_BLOCK_BYTES)
    return pl.pallas_call(
        _cast_kernel,
        grid=(depth, r // tr),
        in_specs=[pl.BlockSpec((1, tr, c), lambda l, i: (l, i, 0))],
        out_specs=pl.BlockSpec((1, nblk, tr, col_block), lambda l, i: (l, 0, i, 0)),
        out_shape=jax.ShapeDtypeStruct((depth, nblk, r, col_block), BF16),
        compiler_params=_cparams("parallel", "parallel"),
        name="weight_cast",
    )(w)


def _ffn_weights(w_gate, w_up, w_down):
    depth, d, dff = w_gate.shape
    tf = min(FFN_TILE, dff)
    return (_cast_call(w_gate, tf), _cast_call(w_up, tf),
            _cast_call(w_down, d).reshape(depth, dff // tf, tf, d))


def _hyb_in_kernel(x_ref, mod_ref, g_ref, wgb_ref, wgc_ref, wu_ref, wq_ref, wk_ref, wvt_ref,
                   qg_ref, kg_ref, cos_ref, sin_ref,
                   gb_ref, z_ref, q_ref, k_ref, vt_ref):
    q_scale = HEAD_DIM ** -0.5 * LOG2E
    norm = lambda rows: _modnorm(x_ref[0, rows, :], g_ref[...], mod_ref[0, 3:4, :],
                                 mod_ref[0, 4:5, :]).astype(BF16)
    chunks = _row_chunks(x_ref.shape[1], PROJ_ROW_CHUNK)
    xn_next = norm(chunks[0])
    for n, rows in enumerate(chunks):
        xn = xn_next
        if n + 1 < len(chunks):
            xn_next = norm(chunks[n + 1])
        gb_ref[0, rows, :] = _dot(xn, wgb_ref[...]).astype(BF16)
        z_ref[0, rows, :] = (_dot(xn, wgc_ref[...]) * _dot(xn, wu_ref[...])).astype(BF16)
        vt_ref[0, :, rows] = _dot_t(wvt_ref[...], xn).astype(BF16)
        cos = cos_ref[0, rows, :]
        sin = sin_ref[0, rows, :]
        q = _dot(xn, wq_ref[...])
        for h in range(q.shape[1] // HEAD_DIM):
            sl = slice(h * HEAD_DIM, (h + 1) * HEAD_DIM)
            q_ref[0, rows, sl] = (_rope(_rms(q[:, sl], qg_ref[...]), cos, sin) * q_scale).astype(BF16)
        k = _dot(xn, wk_ref[...])
        for h in range(k.shape[1] // HEAD_DIM):
            sl = slice(h * HEAD_DIM, (h + 1) * HEAD_DIM)
            k_ref[0, rows, sl] = _rope(_rms(k[:, sl], kg_ref[...]), cos, sin).astype(BF16)


def _hyb_in_call(x, mod, g, wgb, wgc, wu, wq, wk, wvt, qg, kg, cos2, sin2):
    nb, t, d = x.shape
    tm = min(512, t)
    cc, nq, nkv = wgb.shape[1], wq.shape[1], wk.shape[1]
    row = lambda n: pl.BlockSpec((1, tm, n), lambda b, i: (b, i, 0))
    table = pl.BlockSpec((1, tm, LANES), lambda b, i: (jnp.where(b == nb - 1, 1, 0), i, 0))
    return pl.pallas_call(
        _hyb_in_kernel,
        grid=(nb, t // tm),
        in_specs=[
            row(d),
            pl.BlockSpec((1, N_MOD, d), lambda b, i: (b, 0, 0)),
            _resident((1, d)),
            _resident(wgb.shape), _resident(wgc.shape), _resident(wu.shape),
            _resident(wq.shape), _resident(wk.shape), _resident(wvt.shape),
            _resident((1, HEAD_DIM)), _resident((1, HEAD_DIM)),
            table, table,
        ],
        out_specs=[row(cc), row(cc), row(nq), row(nkv),
                   pl.BlockSpec((1, nkv, tm), lambda b, i: (b, 0, i))],
        out_shape=[jax.ShapeDtypeStruct((nb, t, n), BF16) for n in (cc, cc, nq, nkv)]
        + [jax.ShapeDtypeStruct((nb, nkv, t), BF16)],
        compiler_params=_cparams("parallel", "parallel"),
        name="hyb_in_proj",
    )(x, mod, g.reshape(1, d), wgb, wgc, wu, wq, wk, wvt,
      qg.reshape(1, HEAD_DIM), kg.reshape(1, HEAD_DIM), cos2, sin2)


def _softmax_pv(s_list, kv_list):
    m = s_list[0].max(axis=-1, keepdims=True)
    for s in s_list[1:]:
        m = jnp.maximum(m, s.max(axis=-1, keepdims=True))
    l = 0.0
    o = 0.0
    for s, (_, v) in zip(s_list, kv_list):
        p = jnp.exp2(s - m)
        l = l + p.sum(axis=-1, keepdims=True)
        o = o + _dot(p.astype(BF16), v)
    return o / l


def _attend_chains(chains, kv_list):
    scores = lambda load_q: [_dot_t(load_q(), k) for k, _ in kv_list]
    ahead = [scores(load_q) for load_q, _ in chains[:ATTN_SCORE_LOOKAHEAD]]
    for n, (_, store_o) in enumerate(chains):
        s_list = ahead.pop(0)
        if n + ATTN_SCORE_LOOKAHEAD < len(chains):
            ahead.append(scores(chains[n + ATTN_SCORE_LOOKAHEAD][0]))
        store_o(_softmax_pv(s_list, kv_list).astype(BF16))


def _softmax_pv_t(st_list, kvt_list):
    m = st_list[0].max(axis=0, keepdims=True)
    for s in st_list[1:]:
        m = jnp.maximum(m, s.max(axis=0, keepdims=True))
    l = 0.0
    o = 0.0
    for s, (_, vt) in zip(st_list, kvt_list):
        p = jnp.exp2(s - m)
        l = l + p.sum(axis=0, keepdims=True)
        o = o + _dot(vt, p.astype(BF16))
    return (o / l).T


def _attend_chains_t(chains, kvt_list):
    scores = lambda load_q: [_dot_t(k, load_q()) for k, _ in kvt_list]
    ahead = [scores(load_q) for load_q, _ in chains[:ATTN_SCORE_LOOKAHEAD]]
    for n, (_, store_o) in enumerate(chains):
        st_list = ahead.pop(0)
        if n + ATTN_SCORE_LOOKAHEAD < len(chains):
            ahead.append(scores(chains[n + ATTN_SCORE_LOOKAHEAD][0]))
        store_o(_softmax_pv_t(st_list, kvt_list).astype(BF16))


def _row_chunks(n, c):
    c = min(c, n)
    return [slice(r * c, (r + 1) * c) for r in range(n // c)]


def _gqa_kernel(*refs, has_lat):
    if has_lat:
        q_ref, kc_ref, vtc_ref, kl_ref, vtl_ref, o_ref = refs
    else:
        q_ref, kc_ref, vtc_ref, o_ref = refs
    kvt_list = [(kc_ref[0], vtc_ref[0])]
    if has_lat:
        kvt_list.append((kl_ref[0], vtl_ref[0]))

    def chain(rows, sl):
        def store(o):
            o_ref[0, rows, sl] = o
        return (lambda: q_ref[0, rows, sl]), store

    _attend_chains_t([chain(rows, slice(h * HEAD_DIM, (h + 1) * HEAD_DIM))
                      for h in range(q_ref.shape[2] // HEAD_DIM)
                      for rows in _row_chunks(q_ref.shape[1], GQA_ROW_CHUNK)], kvt_list)


def _gqa_latent_call(q, k, vt, b_lat, tc):
    nb, t, nq = q.shape
    gw = nq // GQA_KV_HEADS
    tq = min(1024, t)
    ctx = pl.BlockSpec((1, tc, HEAD_DIM), lambda b, g, i: (b_lat, b, g))
    lat = pl.BlockSpec((1, t, HEAD_DIM), lambda b, g, i: (b, 0, g))
    vt_ctx = pl.BlockSpec((1, HEAD_DIM, tc), lambda b, g, i: (b_lat, g, b))
    vt_lat = pl.BlockSpec((1, HEAD_DIM, t), lambda b, g, i: (b, g, 0))
    qo = pl.BlockSpec((1, tq, gw), lambda b, g, i: (b, i, g))
    return pl.pallas_call(
        functools.partial(_gqa_kernel, has_lat=True),
        grid=(b_lat, GQA_KV_HEADS, t // tq),
        in_specs=[qo, ctx, vt_ctx, lat, vt_lat],
        out_specs=qo,
        out_shape=jax.ShapeDtypeStruct((b_lat, t, nq), BF16),
        compiler_params=_cparams("parallel", "parallel", "parallel"),
        name="gqa_latent",
    )(q, k, vt, k, vt)


def _gqa_ctx_call(q, k, vt, b_lat, tc):
    nb, t, nq = q.shape
    gw = nq // GQA_KV_HEADS
    ctx = pl.BlockSpec((1, tc, HEAD_DIM), lambda b, g: (b_lat, b, g))
    vt_ctx = pl.BlockSpec((1, HEAD_DIM, tc), lambda b, g: (b_lat, g, b))
    return pl.pallas_call(
        functools.partial(_gqa_kernel, has_lat=False),
        grid=(t // tc, GQA_KV_HEADS),
        in_specs=[pl.BlockSpec((1, tc, gw), lambda b, g: (b_lat, b, g)), ctx, vt_ctx],
        out_specs=pl.BlockSpec((1, tc, gw), lambda b, g: (0, b, g)),
        out_shape=jax.ShapeDtypeStruct((1, t, nq), BF16),
        compiler_params=_cparams("parallel", "parallel"),
        name="gqa_context",
    )(q, k, vt)


def _hyb_out_kernel(x_ref, mod_ref, gb_ref, z_ref, zp_ref, zn_ref, att_ref, attc_ref, cw_ref, w_ref,
                    o_ref, *, seq_lat, seq_ctx, b_lat):
    b, i = pl.program_id(0), pl.program_id(1)
    tm, cc = z_ref.shape[1], z_ref.shape[2]
    acc = _dot(jnp.where(b == b_lat, attc_ref[0], att_ref[0]), w_ref[cc:, :])
    z = z_ref[0].astype(F32)
    rows = lax.broadcasted_iota(jnp.int32, (tm, cc), 0)
    seq = jnp.where(b == b_lat, seq_ctx, seq_lat)
    pos = (i * tm + rows) & (seq - 1)
    z_prev = jnp.where(rows == 0, zp_ref[0, 7:8, :].astype(F32), pltpu.roll(z, 1, axis=0))
    z_next = jnp.where(rows == tm - 1, zn_ref[0, 0:1, :].astype(F32),
                       pltpu.roll(z, tm - 1, axis=0))
    z_prev = jnp.where(pos == 0, 0.0, z_prev)
    z_next = jnp.where(pos == seq - 1, 0.0, z_next)
    conv = z_prev * cw_ref[0:1, :] + z * cw_ref[1:2, :] + z_next * cw_ref[2:3, :]
    acc = acc + _dot((gb_ref[0].astype(F32) * conv).astype(BF16), w_ref[:cc, :])
    o_ref[0] = x_ref[0] + mod_ref[0, 5:6, :] * acc


def _hyb_out_call(x, mod, gb, z, att, att_c, conv_w, w_out, b_lat, tc):
    nb, t, d = x.shape
    cc = gb.shape[2]
    tm = min(512, t)
    nt = t // tm
    hb = tm // 8
    row = lambda n: pl.BlockSpec((1, tm, n), lambda b, i: (b, i, 0))
    nq = att.shape[2]
    att_spec = pl.BlockSpec((1, tm, nq), lambda b, i: (jnp.minimum(b, b_lat - 1),
                                                       jnp.where(b < b_lat, i, nt - 1), 0))
    attc_spec = pl.BlockSpec((1, tm, nq), lambda b, i: (0, jnp.where(b < b_lat, 0, i), 0))
    return pl.pallas_call(
        functools.partial(_hyb_out_kernel, seq_lat=t, seq_ctx=tc, b_lat=b_lat),
        grid=(nb, t // tm),
        in_specs=[
            row(d),
            pl.BlockSpec((1, N_MOD, d), lambda b, i: (b, 0, 0)),
            row(cc), row(cc),
            pl.BlockSpec((1, 8, cc), lambda b, i: (b, jnp.maximum(i * hb - 1, 0), 0)),
            pl.BlockSpec((1, 8, cc), lambda b, i: (b, jnp.minimum((i + 1) * hb, t // 8 - 1), 0)),
            att_spec, attc_spec,
            _resident(conv_w.shape),
            _resident(w_out.shape),
        ],
        out_specs=row(d),
        out_shape=jax.ShapeDtypeStruct((nb, t, d), F32),
        compiler_params=_cparams("parallel", "parallel"),
        name="hyb_out_proj",
    )(x, mod, gb, z, z, z, att, att_c, conv_w, w_out)


def _mla_in_kernel(x_ref, mod_ref, g_ref, wd_ref, qg_ref, kvg_ref, wuq_ref, wuk_ref, wuvt_ref,
                   cos_ref, sin_ref, qn_ref, qr_ref, kn_ref, vt_ref, kr_ref, *, q_rank, kv_rank):
    n_nope = qn_ref.shape[2]
    q_scale = (MLA_NOPE + MLA_ROPE) ** -0.5 * LOG2E
    down = lambda rows: _dot(_modnorm(x_ref[0, rows, :], g_ref[...], mod_ref[0, 3:4, :],
                                      mod_ref[0, 4:5, :]).astype(BF16), wd_ref[...])
    chunks = _row_chunks(x_ref.shape[1], PROJ_ROW_CHUNK)
    d_next = down(chunks[0])
    for n, rows in enumerate(chunks):
        d = d_next
        if n + 1 < len(chunks):
            d_next = down(chunks[n + 1])
        cos = cos_ref[0, rows, :]
        sin = sin_ref[0, rows, :]
        q = _dot(_rms(d[:, :q_rank], qg_ref[...]).astype(BF16), wuq_ref[...])
        qn_ref[0, rows, :] = (q[:, :n_nope] * q_scale).astype(BF16)
        for j in range(qr_ref.shape[2] // LANES):
            qr_ref[0, rows, j * LANES:(j + 1) * LANES] = (
                _rope(q[:, n_nope + j * LANES:n_nope + (j + 1) * LANES], cos, sin) * q_scale
            ).astype(BF16)
        ckv = _rms(d[:, q_rank:q_rank + kv_rank], kvg_ref[...]).astype(BF16)
        kn_ref[0, rows, :] = _dot(ckv, wuk_ref[...]).astype(BF16)
        vt_ref[0, :, rows] = _dot_t(wuvt_ref[...], ckv).astype(BF16)
        kr = _rope(d[:, q_rank + kv_rank:], cos, sin)
        first = (lax.broadcasted_iota(jnp.int32, kr.shape, 1) & (MLA_ROPE // 2)) == 0
        kr_ref[0, rows, :LANES] = jnp.where(first, kr, 0.0).astype(BF16)
        kr_ref[0, rows, LANES:] = jnp.where(first, 0.0, kr).astype(BF16)


def _mla_in_call(x, mod, g, wd, qg, kvg, wuq, wuk, wuvt, cos2, sin2):
    nb, t, d = x.shape
    tm = min(512, t)
    q_rank, kv_rank = qg.shape[0], kvg.shape[0]
    n_nope = wuk.shape[1]
    n_rope = wuq.shape[1] - n_nope
    n_v = wuvt.shape[0]
    row = lambda n: pl.BlockSpec((1, tm, n), lambda b, i: (b, i, 0))
    table = pl.BlockSpec((1, tm, LANES), lambda b, i: (jnp.where(b == nb - 1, 1, 0), i, 0))
    out_specs = [row(n_nope), row(n_rope), row(n_nope),
                 pl.BlockSpec((1, n_v, tm), lambda b, i: (b, 0, i)), row(2 * LANES)]
    out_shapes = [(nb, t, n_nope), (nb, t, n_rope), (nb, t, n_nope), (nb, n_v, t), (nb, t, 2 * LANES)]
    return pl.pallas_call(
        functools.partial(_mla_in_kernel, q_rank=q_rank, kv_rank=kv_rank),
        grid=(nb, t // tm),
        in_specs=[
            row(d),
            pl.BlockSpec((1, N_MOD, d), lambda b, i: (b, 0, 0)),
            _resident((1, d)),
            _resident(wd.shape), _resident((1, q_rank)), _resident((1, kv_rank)),
            _resident(wuq.shape), _resident(wuk.shape), _resident(wuvt.shape),
            table, table,
        ],
        out_specs=out_specs,
        out_shape=[jax.ShapeDtypeStruct(s, BF16) for s in out_shapes],
        compiler_params=_cparams("parallel", "parallel"),
        name="mla_in_proj",
    )(x, mod, g.reshape(1, d), wd, qg.reshape(1, q_rank), kvg.reshape(1, kv_rank), wuq, wuk, wuvt,
      cos2, sin2)


def _mla_attn_kernel(qn_ref, qr_ref, knc_ref, krc_ref, vtc_ref, knl_ref, krl_ref, vtl_ref,
                     o_ref, k_ref, vt_ref):
    tc = knc_ref.shape[1]

    @pl.when(pl.program_id(2) == 0)
    def _():
        k_ref[:tc, :LANES] = knc_ref[0]
        k_ref[:tc, LANES:] = krc_ref[0]
        k_ref[tc:, :LANES] = knl_ref[0]
        k_ref[tc:, LANES:] = krl_ref[0]
        vt_ref[:, :tc] = vtc_ref[0]
        vt_ref[:, tc:] = vtl_ref[0]

    def chain(rows):
        def store(o):
            o_ref[0, rows, :] = o
        return (lambda: jnp.concatenate([qn_ref[0, rows, :], qr_ref[0, rows, :]], axis=1)), store

    _attend_chains_t([chain(rows) for rows in _row_chunks(qn_ref.shape[1], MLA_ROW_CHUNK)],
                     [(k_ref[...], vt_ref[...])])


def _mla_attn_call(qn, qr, kn, vt, kr, b_lat, tc):
    nb, t, n_nope = qn.shape
    heads = n_nope // MLA_NOPE
    tq = min(2048, t)
    qspec = pl.BlockSpec((1, tq, LANES), lambda b, h, i: (b, i, h))
    qrspec = pl.BlockSpec((1, tq, LANES), lambda b, h, i: (b, i, h // 2))
    ctx = lambda f: pl.BlockSpec((1, tc, LANES), lambda b, h, i: (b_lat, b, f(h)))
    lat = lambda f: pl.BlockSpec((1, t, LANES), lambda b, h, i: (b, 0, f(h)))
    vt_ctx = pl.BlockSpec((1, MLA_V, tc), lambda b, h, i: (b_lat, h, b))
    vt_lat = pl.BlockSpec((1, MLA_V, t), lambda b, h, i: (b, h, 0))
    same = lambda h: h
    parity = lambda h: h % 2
    return pl.pallas_call(
        _mla_attn_kernel,
        grid=(b_lat, heads, t // tq),
        in_specs=[qspec, qrspec, ctx(same), ctx(parity), vt_ctx, lat(same), lat(parity), vt_lat],
        out_specs=qspec,
        out_shape=jax.ShapeDtypeStruct((b_lat, t, n_nope), BF16),
        scratch_shapes=[pltpu.VMEM((tc + t, 2 * LANES), BF16), pltpu.VMEM((MLA_V, tc + t), BF16)],
        compiler_params=_cparams("parallel", "parallel", "arbitrary"),
        name="mla_attention",
    )(qn, qr, kn, kr, vt, kn, kr, vt)


def _mla_out_kernel(x_ref, mod_ref, att_ref, w_ref, o_ref):
    o_ref[0] = x_ref[0] + mod_ref[0, 5:6, :] * _dot(att_ref[0], w_ref[...])


def _mla_out_call(x, mod, att, w_o):
    nb, t, k = att.shape
    d = w_o.shape[1]
    tm = min(512, t)
    row = lambda n: pl.BlockSpec((1, tm, n), lambda b, i: (b, i, 0))
    return pl.pallas_call(
        _mla_out_kernel,
        grid=(nb, t // tm),
        in_specs=[row(d), pl.BlockSpec((1, N_MOD, d), lambda b, i: (b, 0, 0)), row(k),
                  _resident(w_o.shape)],
        out_specs=row(d),
        out_shape=jax.ShapeDtypeStruct((nb, t, d), F32),
        compiler_params=_cparams("parallel", "parallel"),
        name="mla_out_proj",
    )(x, mod, att, w_o)


def _rope_angles(n_tok, dim):
    n_rows = n_tok // GRID_W
    row = jnp.repeat(jnp.arange(n_rows), GRID_W).astype(F32)
    col = jnp.tile(jnp.arange(GRID_W), n_rows).astype(F32)
    half = dim // 2
    inv = 1.0 / (ROPE_THETA ** (jnp.arange(0, half, 2, dtype=F32) / half))
    return jnp.concatenate([row[:, None] * inv, col[:, None] * inv], axis=-1)


def _rope_tables(n_tok, dim):
    ang = _rope_angles(n_tok, dim)
    reps = LANES // dim
    cos = jnp.concatenate([jnp.cos(ang)] * (2 * reps), axis=-1)
    sin = jnp.concatenate([-jnp.sin(ang)] * reps + [jnp.sin(ang)] * reps, axis=-1)
    return (jnp.stack([cos, jnp.ones_like(cos)]), jnp.stack([sin, jnp.zeros_like(sin)]))


def _deinterleave(n):
    return np.concatenate([np.arange(0, n, 2), np.arange(1, n, 2)])


def kernel(x, c, ctx, c_ctx, mod_w, mod_b, norm_ffn1, norm_mix, norm_ffn2,
           ffn1_w_gate, ffn1_w_up, ffn1_w_down, ffn2_w_gate, ffn2_w_up, ffn2_w_down,
           hyb_w_in, hyb_conv_w, hyb_q_norm, hyb_k_norm, hyb_w_out,
           mla_w_down, mla_q_norm, mla_kv_norm, mla_w_uq, mla_w_ukv, mla_w_o, final_norm):
    b_lat, t, d = x.shape
    tc = ctx.shape[1]
    depth = mod_w.shape[0]
    assert depth == 2 and ctx.shape[0] * tc == t and b_lat + 1 <= MOD_ROWS
    assert t & (t - 1) == 0 and tc & (tc - 1) == 0 and t % GRID_W == 0
    nb = b_lat + 1

    cvec = jnp.concatenate([c, c_ctx[None], jnp.zeros((MOD_ROWS - nb, d), F32)], axis=0)
    mods = _mod_call(cvec, mod_w, mod_b).reshape(depth, MOD_ROWS, N_MOD, d)
    cast = lambda w: w.astype(BF16)
    f1 = _ffn_weights(ffn1_w_gate, ffn1_w_up, ffn1_w_down)
    f2 = _ffn_weights(ffn2_w_gate, ffn2_w_up, ffn2_w_down)

    mod = mods[0]
    xs = _ffn_call(x, nb, mod, norm_ffn1[0], *f1, 0, 0, ctx=ctx.reshape(1, t, d))
    cc = d // 2
    n_q = cc
    n_kv = GQA_KV_HEADS * HEAD_DIM
    w_in = hyb_w_in[0]
    perm = _deinterleave(HEAD_DIM)
    head_perm = lambda n: (np.arange(n).reshape(-1, HEAD_DIM)[:, perm]).reshape(-1)
    wq = w_in[:, 3 * cc:3 * cc + n_q][:, head_perm(n_q)]
    wk = w_in[:, 3 * cc + n_q:3 * cc + n_q + n_kv][:, head_perm(n_kv)]
    wvt = w_in[:, 3 * cc + n_q + n_kv:].T
    cos_a, sin_a = _rope_tables(t, HEAD_DIM)
    gb, z, q, k, v = _hyb_in_call(
        xs, mod, norm_mix[0], cast(w_in[:, :cc]), cast(w_in[:, cc:2 * cc]), cast(w_in[:, 2 * cc:3 * cc]),
        cast(wq), cast(wk), cast(wvt), hyb_q_norm[0][perm], hyb_k_norm[0][perm], cos_a, sin_a)
    att = _gqa_latent_call(q, k, v, b_lat, tc)
    att_c = _gqa_ctx_call(q, k, v, b_lat, tc)
    xs = _hyb_out_call(xs, mod, gb, z, att, att_c, hyb_conv_w[0], cast(hyb_w_out[0]), b_lat, tc)
    xs = _ffn_call(xs, nb, mod, norm_ffn2[0], *f2, 0, 6)

    mod = mods[1]
    xs = _ffn_call(xs, nb, mod, norm_ffn1[1], *f1, 1, 0)
    heads = d // HEAD_DIM
    q_rank, kv_rank = mla_q_norm.shape[1], mla_kv_norm.shape[1]
    qk = MLA_NOPE + MLA_ROPE
    ev, od = np.arange(0, MLA_ROPE, 2), np.arange(1, MLA_ROPE, 2)
    nope_cols = (np.arange(heads)[:, None] * qk + np.arange(MLA_NOPE)[None]).reshape(-1)
    rope_cols = []
    for j in range(heads // 2):
        a0, b0 = 2 * j * qk + MLA_NOPE, (2 * j + 1) * qk + MLA_NOPE
        rope_cols += [a0 + ev, b0 + ev, a0 + od, b0 + od]
    wuq = mla_w_uq[0][:, np.concatenate([nope_cols] + rope_cols)]
    kvw = MLA_NOPE + MLA_V
    kn_cols = (np.arange(heads)[:, None] * kvw + np.arange(MLA_NOPE)[None]).reshape(-1)
    wuk = mla_w_ukv[0][:, kn_cols]
    wuvt = mla_w_ukv[0][:, kn_cols + MLA_NOPE].T
    kr0 = q_rank + kv_rank
    wdn = mla_w_down[0][:, np.concatenate([np.arange(kr0), kr0 + ev, kr0 + ev, kr0 + od, kr0 + od])]
    cos_m, sin_m = _rope_tables(t, MLA_ROPE)
    qn, qr, kn, vt, kr = _mla_in_call(xs, mod, norm_mix[1], cast(wdn), mla_q_norm[0], mla_kv_norm[0],
                                      cast(wuq), cast(wuk), cast(wuvt), cos_m, sin_m)
    att = _mla_attn_call(qn, qr, kn, vt, kr, b_lat, tc)
    xs = _mla_out_call(xs, mod, att, cast(mla_w_o[0]))
    return _ffn_call(xs, b_lat, mod, norm_ffn2[1], *f2, 1, 6, final_g=final_norm)
```

```python
import functools

import numpy as np
import jax
import jax.numpy as jnp
from jax import lax
from jax.experimental import pallas as pl
from jax.experimental.pallas import tpu as pltpu

HEAD_DIM = 128
GRID_W = 64
ROPE_THETA = 10000.0
EPS = 1e-6
N_MOD = 9
GQA_KV_HEADS = 2
MLA_NOPE = 128
MLA_ROPE = 64
MLA_V = 128
LANES = 128
VMEM_LIMIT = 56 * 1024 * 1024
MOD_ROWS = 16
FFN_ROW_CHUNKS = 4
FFN_TILE = 512
FFN_SLOTS = 4
PROJ_ROW_CHUNK = 256
GQA_ROW_CHUNK = 512
MLA_ROW_CHUNK = 512
ATTN_SCORE_LOOKAHEAD = 2
LOG2E = 1.4426950408889634
CAST_BLOCK_BYTES = 4 * 1024 * 1024

F32 = jnp.float32
BF16 = jnp.bfloat16


def _cparams(*sem):
    return pltpu.CompilerParams(dimension_semantics=sem, vmem_limit_bytes=VMEM_LIMIT)


def _resident(shape):
    nd = len(shape)
    return pl.BlockSpec(shape, lambda *_: (0,) * nd, pipeline_mode=pl.Buffered(1))


def _rms(x, g):
    return x * lax.rsqrt(jnp.mean(x * x, axis=-1, keepdims=True) + EPS) * g


def _modnorm(x, g, shift, scale):
    return _rms(x, g) * (1.0 + scale) + shift


def _rope(x, cos, sin):
    return x * cos + pltpu.roll(x, LANES // 2, axis=1) * sin


def _dot(a, b):
    return jnp.dot(a, b, preferred_element_type=F32)


def _dot_t(a, b):
    return lax.dot_general(a, b, (((1,), (1,)), ((), ())), preferred_element_type=F32)


def _mod_kernel(c_ref, w_ref, b_ref, o_ref):
    c = c_ref[...]
    s = (c * jax.nn.sigmoid(c)).astype(BF16)
    o_ref[0] = _dot(s, w_ref[0].astype(BF16)) + b_ref[0]


def _mod_call(cvec, mod_w, mod_b):
    depth, d, n = mod_w.shape
    tn = min(1024, n)
    return pl.pallas_call(
        _mod_kernel,
        grid=(depth, n // tn),
        in_specs=[
            pl.BlockSpec((MOD_ROWS, d), lambda l, j: (0, 0)),
            pl.BlockSpec((1, d, tn), lambda l, j: (l, 0, j)),
            pl.BlockSpec((1, 1, tn), lambda l, j: (l, 0, j)),
        ],
        out_specs=pl.BlockSpec((1, MOD_ROWS, tn), lambda l, j: (l, 0, j)),
        out_shape=jax.ShapeDtypeStruct((depth, MOD_ROWS, n), F32),
        compiler_params=_cparams("parallel", "parallel"),
        name="adaln_mod",
    )(cvec, mod_w, mod_b.reshape(depth, 1, n))


def _swiglu_act(xn, wg_ref, wu_ref):
    hg = _dot(xn, wg_ref[...])
    hu = _dot(xn, wu_ref[...])
    return (hg * jax.nn.sigmoid(hg) * hu).astype(BF16)


def _ffn_kernel(x_ref, mod_ref, g_ref, wg_hbm, wu_hbm, wd_hbm, *rest, layer, m0, final, with_ctx):
    rest = list(rest)
    fn_ref = rest.pop(0) if final else None
    ctx_ref = rest.pop(0) if with_ctx else None
    o_ref, xn_ref, h_ref, wg_buf, wu_buf, wd_buf, sem = rest
    is_ctx = pl.program_id(0) == pl.num_programs(0) - 1
    n_ff = wg_hbm.shape[1]
    tm = xn_ref.shape[0]
    step = pl.program_id(0) * pl.num_programs(1) + pl.program_id(1)
    not_last_step = step < pl.num_programs(0) * pl.num_programs(1) - 1
    half_gate = 0.5 * mod_ref[0, m0 + 2:m0 + 3, :]

    def copies(blk):
        slot = blk % FFN_SLOTS
        pairs = ((wg_hbm, wg_buf), (wu_hbm, wu_buf), (wd_hbm, wd_buf))
        return [pltpu.make_async_copy(hbm.at[layer, blk], buf.at[slot], sem.at[k, slot])
                for k, (hbm, buf) in enumerate(pairs)]

    def start(blk):
        for c in copies(blk):
            c.start()

    def wait(blk):
        for c in copies(blk):
            c.wait()

    def x_rows(rows):
        x = x_ref[0, rows, :]
        return jnp.where(is_ctx, ctx_ref[0, rows, :], x) if with_ctx else x

    @pl.when(step == 0)
    def _():
        start(0)
        start(1)

    wait(0)
    rs = tm // FFN_ROW_CHUNKS
    for r in range(FFN_ROW_CHUNKS):
        rows = slice(r * rs, (r + 1) * rs)
        xn = _modnorm(x_rows(rows), g_ref[...], mod_ref[0, m0:m0 + 1, :],
                      mod_ref[0, m0 + 1:m0 + 2, :]).astype(BF16)
        xn_ref[rows, :] = xn
        h_ref[0, rows, :] = _swiglu_act(xn, wg_buf.at[0], wu_buf.at[0])

    for j in range(n_ff):
        if j + 2 < n_ff:
            start(j + 2)
        else:
            pl.when(not_last_step)(functools.partial(start, j + 2 - n_ff))
        if j + 1 < n_ff:
            wait(j + 1)
            nslot = (j + 1) % FFN_SLOTS
            h_ref[(j + 1) % 2] = _swiglu_act(xn_ref[...], wg_buf.at[nslot], wu_buf.at[nslot])
        contrib = half_gate * _dot(h_ref[j % 2], wd_buf[j % FFN_SLOTS])
        if j == 0:
            o_ref[0] = x_rows(slice(None)) + contrib
        else:
            o_ref[0] += contrib

    if final:
        o_ref[0] = _rms(o_ref[0], fn_ref[...])


def _ffn_call(x, nb, mod, g, wg, wu, wd, layer, m0, final_g=None, ctx=None):
    _, t, d = x.shape
    _, n_ff, _, tf = wg.shape
    assert n_ff >= 2 and all(
        max(b for b in range(n_ff) if b % FFN_SLOTS == k) <= n_ff - 3 + k for k in (0, 1))
    final = final_g is not None
    tm = min(512, t)
    nt = t // tm
    if ctx is None:
        x_spec = pl.BlockSpec((1, tm, d), lambda b, i: (b, i, 0))
    else:
        x_spec = pl.BlockSpec((1, tm, d), lambda b, i: (jnp.minimum(b, nb - 2),
                                                         jnp.where(b < nb - 1, i, nt - 1), 0))
    in_specs = [
        x_spec,
        pl.BlockSpec((1, N_MOD, d), lambda b, i: (b, 0, 0)),
        pl.BlockSpec((1, d), lambda b, i: (0, 0)),
        pl.BlockSpec(memory_space=pl.ANY),
        pl.BlockSpec(memory_space=pl.ANY),
        pl.BlockSpec(memory_space=pl.ANY),
    ]
    args = [x, mod, g.reshape(1, d), wg, wu, wd]
    if final:
        in_specs.append(pl.BlockSpec((1, d), lambda b, i: (0, 0)))
        args.append(final_g.reshape(1, d))
    if ctx is not None:
        in_specs.append(pl.BlockSpec((1, tm, d), lambda b, i: (0, jnp.where(b < nb - 1, 0, i), 0)))
        args.append(ctx)
    return pl.pallas_call(
        functools.partial(_ffn_kernel, layer=layer, m0=m0, final=final, with_ctx=ctx is not None),
        grid=(nb, nt),
        in_specs=in_specs,
        out_specs=pl.BlockSpec((1, tm, d), lambda b, i: (b, i, 0)),
        out_shape=jax.ShapeDtypeStruct((nb, t, d), F32),
        scratch_shapes=[
            pltpu.VMEM((tm, d), BF16),
            pltpu.VMEM((2, tm, tf), BF16),
            pltpu.VMEM((FFN_SLOTS, d, tf), BF16), pltpu.VMEM((FFN_SLOTS, d, tf), BF16),
            pltpu.VMEM((FFN_SLOTS, tf, d), BF16),
            pltpu.SemaphoreType.DMA((3, FFN_SLOTS)),
        ],
        compiler_params=_cparams("arbitrary", "arbitrary"),
        name="swiglu_half_step",
    )(*args)


def _cast_kernel(w_ref, o_ref):
    cb = o_ref.shape[3]
    for n in range(o_ref.shape[1]):
        o_ref[0, n] = w_ref[0, :, n * cb:(n + 1) * cb].astype(BF16)


def _cast_call(w, col_block):
    depth, r, c = w.shape
    nblk = c // col_block
    tr = max(n for n in range(16, r + 1, 16) if r % n == 0 and 4 * c * n <= CAST_BLOCK_BYTES)
    return pl.pallas_call(
        _cast_kernel,
        grid=(depth, r // tr),
        in_specs=[pl.BlockSpec((1, tr, c), lambda l, i: (l, i, 0))],
        out_specs=pl.BlockSpec((1, nblk, tr, col_block), lambda l, i: (l, 0, i, 0)),
        out_shape=jax.ShapeDtypeStruct((depth, nblk, r, col_block), BF16),
        compiler_params=_cparams("parallel", "parallel"),
        name="weight_cast",
    )(w)


def _ffn_weights(w_gate, w_up, w_down):
    depth, d, dff = w_gate.shape
    tf = min(FFN_TILE, dff)
    return (_cast_call(w_gate, tf), _cast_call(w_up, tf),
            _cast_call(w_down, d).reshape(depth, dff // tf, tf, d))


def _hyb_in_kernel(x_ref, mod_ref, g_ref, wgb_ref, wgc_ref, wu_ref, wq_ref, wk_ref, wvt_ref,
                   qg_ref, kg_ref, cos_ref, sin_ref,
                   gb_ref, z_ref, q_ref, k_ref, vt_ref):
    q_scale = HEAD_DIM ** -0.5 * LOG2E
    norm = lambda rows: _modnorm(x_ref[0, rows, :], g_ref[...], mod_ref[0, 3:4, :],
                                 mod_ref[0, 4:5, :]).astype(BF16)
    chunks = _row_chunks(x_ref.shape[1], PROJ_ROW_CHUNK)
    xn_next = norm(chunks[0])
    for n, rows in enumerate(chunks):
        xn = xn_next
        if n + 1 < len(chunks):
            xn_next = norm(chunks[n + 1])
        gb_ref[0, rows, :] = _dot(xn, wgb_ref[...]).astype(BF16)
        z_ref[0, rows, :] = (_dot(xn, wgc_ref[...]) * _dot(xn, wu_ref[...])).astype(BF16)
        vt_ref[0, :, rows] = _dot_t(wvt_ref[...], xn).astype(BF16)
        cos = cos_ref[0, rows, :]
        sin = sin_ref[0, rows, :]
        q = _dot(xn, wq_ref[...])
        for h in range(q.shape[1] // HEAD_DIM):
            sl = slice(h * HEAD_DIM, (h + 1) * HEAD_DIM)
            q_ref[0, rows, sl] = (_rope(_rms(q[:, sl], qg_ref[...]), cos, sin) * q_scale).astype(BF16)
        k = _dot(xn, wk_ref[...])
        for h in range(k.shape[1] // HEAD_DIM):
            sl = slice(h * HEAD_DIM, (h + 1) * HEAD_DIM)
            k_ref[0, rows, sl] = _rope(_rms(k[:, sl], kg_ref[...]), cos, sin).astype(BF16)


def _hyb_in_call(x, mod, g, wgb, wgc, wu, wq, wk, wvt, qg, kg, cos2, sin2):
    nb, t, d = x.shape
    tm = min(512, t)
    cc, nq, nkv = wgb.shape[1], wq.shape[1], wk.shape[1]
    row = lambda n: pl.BlockSpec((1, tm, n), lambda b, i: (b, i, 0))
    table = pl.BlockSpec((1, tm, LANES), lambda b, i: (jnp.where(b == nb - 1, 1, 0), i, 0))
    return pl.pallas_call(
        _hyb_in_kernel,
        grid=(nb, t // tm),
        in_specs=[
            row(d),
            pl.BlockSpec((1, N_MOD, d), lambda b, i: (b, 0, 0)),
            _resident((1, d)),
            _resident(wgb.shape), _resident(wgc.shape), _resident(wu.shape),
            _resident(wq.shape), _resident(wk.shape), _resident(wvt.shape),
            _resident((1, HEAD_DIM)), _resident((1, HEAD_DIM)),
            table, table,
        ],
        out_specs=[row(cc), row(cc), row(nq), row(nkv),
                   pl.BlockSpec((1, nkv, tm), lambda b, i: (b, 0, i))],
        out_shape=[jax.ShapeDtypeStruct((nb, t, n), BF16) for n in (cc, cc, nq, nkv)]
        + [jax.ShapeDtypeStruct((nb, nkv, t), BF16)],
        compiler_params=_cparams("parallel", "parallel"),
        name="hyb_in_proj",
    )(x, mod, g.reshape(1, d), wgb, wgc, wu, wq, wk, wvt,
      qg.reshape(1, HEAD_DIM), kg.reshape(1, HEAD_DIM), cos2, sin2)


def _softmax_pv(s_list, kv_list):
    m = s_list[0].max(axis=-1, keepdims=True)
    for s in s_list[1:]:
        m = jnp.maximum(m, s.max(axis=-1, keepdims=True))
    l = 0.0
    o = 0.0
    for s, (_, v) in zip(s_list, kv_list):
        p = jnp.exp2(s - m)
        l = l + p.sum(axis=-1, keepdims=True)
        o = o + _dot(p.astype(BF16), v)
    return o / l


def _attend_chains(chains, kv_list):
    scores = lambda load_q: [_dot_t(load_q(), k) for k, _ in kv_list]
    ahead = [scores(load_q) for load_q, _ in chains[:ATTN_SCORE_LOOKAHEAD]]
    for n, (_, store_o) in enumerate(chains):
        s_list = ahead.pop(0)
        if n + ATTN_SCORE_LOOKAHEAD < len(chains):
            ahead.append(scores(chains[n + ATTN_SCORE_LOOKAHEAD][0]))
        store_o(_softmax_pv(s_list, kv_list).astype(BF16))


def _softmax_pv_t(st_list, kvt_list):
    m = st_list[0].max(axis=0, keepdims=True)
    for s in st_list[1:]:
        m = jnp.maximum(m, s.max(axis=0, keepdims=True))
    l = 0.0
    o = 0.0
    for s, (_, vt) in zip(st_list, kvt_list):
        p = jnp.exp2(s - m)
        l = l + p.sum(axis=0, keepdims=True)
        o = o + _dot(vt, p.astype(BF16))
    return (o / l).T


def _attend_chains_t(chains, kvt_list):
    scores = lambda load_q: [_dot_t(k, load_q()) for k, _ in kvt_list]
    ahead = [scores(load_q) for load_q, _ in chains[:ATTN_SCORE_LOOKAHEAD]]
    for n, (_, store_o) in enumerate(chains):
        st_list = ahead.pop(0)
        if n + ATTN_SCORE_LOOKAHEAD < len(chains):
            ahead.append(scores(chains[n + ATTN_SCORE_LOOKAHEAD][0]))
        store_o(_softmax_pv_t(st_list, kvt_list).astype(BF16))


def _row_chunks(n, c):
    c = min(c, n)
    return [slice(r * c, (r + 1) * c) for r in range(n // c)]


def _gqa_kernel(*refs, has_lat):
    if has_lat:
        q_ref, kc_ref, vtc_ref, kl_ref, vtl_ref, o_ref = refs
    else:
        q_ref, kc_ref, vtc_ref, o_ref = refs
    kvt_list = [(kc_ref[0], vtc_ref[0])]
    if has_lat:
        kvt_list.append((kl_ref[0], vtl_ref[0]))

    def chain(rows, sl):
        def store(o):
            o_ref[0, rows, sl] = o
        return (lambda: q_ref[0, rows, sl]), store

    _attend_chains_t([chain(rows, slice(h * HEAD_DIM, (h + 1) * HEAD_DIM))
                      for h in range(q_ref.shape[2] // HEAD_DIM)
                      for rows in _row_chunks(q_ref.shape[1], GQA_ROW_CHUNK)], kvt_list)


def _gqa_latent_call(q, k, vt, b_lat, tc):
    nb, t, nq = q.shape
    gw = nq // GQA_KV_HEADS
    tq = min(1024, t)
    ctx = pl.BlockSpec((1, tc, HEAD_DIM), lambda b, g, i: (b_lat, b, g))
    lat = pl.BlockSpec((1, t, HEAD_DIM), lambda b, g, i: (b, 0, g))
    vt_ctx = pl.BlockSpec((1, HEAD_DIM, tc), lambda b, g, i: (b_lat, g, b))
    vt_lat = pl.BlockSpec((1, HEAD_DIM, t), lambda b, g, i: (b, g, 0))
    qo = pl.BlockSpec((1, tq, gw), lambda b, g, i: (b, i, g))
    return pl.pallas_call(
        functools.partial(_gqa_kernel, has_lat=True),
        grid=(b_lat, GQA_KV_HEADS, t // tq),
        in_specs=[qo, ctx, vt_ctx, lat, vt_lat],
        out_specs=qo,
        out_shape=jax.ShapeDtypeStruct((b_lat, t, nq), BF16),
        compiler_params=_cparams("parallel", "parallel", "parallel"),
        name="gqa_latent",
    )(q, k, vt, k, vt)


def _gqa_ctx_call(q, k, vt, b_lat, tc):
    nb, t, nq = q.shape
    gw = nq // GQA_KV_HEADS
    ctx = pl.BlockSpec((1, tc, HEAD_DIM), lambda b, g: (b_lat, b, g))
    vt_ctx = pl.BlockSpec((1, HEAD_DIM, tc), lambda b, g: (b_lat, g, b))
    return pl.pallas_call(
        functools.partial(_gqa_kernel, has_lat=False),
        grid=(t // tc, GQA_KV_HEADS),
        in_specs=[pl.BlockSpec((1, tc, gw), lambda b, g: (b_lat, b, g)), ctx, vt_ctx],
        out_specs=pl.BlockSpec((1, tc, gw), lambda b, g: (0, b, g)),
        out_shape=jax.ShapeDtypeStruct((1, t, nq), BF16),
        compiler_params=_cparams("parallel", "parallel"),
        name="gqa_context",
    )(q, k, vt)


def _hyb_out_kernel(x_ref, mod_ref, gb_ref, z_ref, zp_ref, zn_ref, att_ref, attc_ref, cw_ref, w_ref,
                    o_ref, *, seq_lat, seq_ctx, b_lat):
    b, i = pl.program_id(0), pl.program_id(1)
    tm, cc = z_ref.shape[1], z_ref.shape[2]
    acc = _dot(jnp.where(b == b_lat, attc_ref[0], att_ref[0]), w_ref[cc:, :])
    z = z_ref[0].astype(F32)
    rows = lax.broadcasted_iota(jnp.int32, (tm, cc), 0)
    seq = jnp.where(b == b_lat, seq_ctx, seq_lat)
    pos = (i * tm + rows) & (seq - 1)
    z_prev = jnp.where(rows == 0, zp_ref[0, 7:8, :].astype(F32), pltpu.roll(z, 1, axis=0))
    z_next = jnp.where(rows == tm - 1, zn_ref[0, 0:1, :].astype(F32),
                       pltpu.roll(z, tm - 1, axis=0))
    z_prev = jnp.where(pos == 0, 0.0, z_prev)
    z_next = jnp.where(pos == seq - 1, 0.0, z_next)
    conv = z_prev * cw_ref[0:1, :] + z * cw_ref[1:2, :] + z_next * cw_ref[2:3, :]
    acc = acc + _dot((gb_ref[0].astype(F32) * conv).astype(BF16), w_ref[:cc, :])
    o_ref[0] = x_ref[0] + mod_ref[0, 5:6, :] * acc


def _hyb_out_call(x, mod, gb, z, att, att_c, conv_w, w_out, b_lat, tc):
    nb, t, d = x.shape
    cc = gb.shape[2]
    tm = min(512, t)
    nt = t // tm
    hb = tm // 8
    row = lambda n: pl.BlockSpec((1, tm, n), lambda b, i: (b, i, 0))
    nq = att.shape[2]
    att_spec = pl.BlockSpec((1, tm, nq), lambda b, i: (jnp.minimum(b, b_lat - 1),
                                                       jnp.where(b < b_lat, i, nt - 1), 0))
    attc_spec = pl.BlockSpec((1, tm, nq), lambda b, i: (0, jnp.where(b < b_lat, 0, i), 0))
    return pl.pallas_call(
        functools.partial(_hyb_out_kernel, seq_lat=t, seq_ctx=tc, b_lat=b_lat),
        grid=(nb, t // tm),
        in_specs=[
            row(d),
            pl.BlockSpec((1, N_MOD, d), lambda b, i: (b, 0, 0)),
            row(cc), row(cc),
            pl.BlockSpec((1, 8, cc), lambda b, i: (b, jnp.maximum(i * hb - 1, 0), 0)),
            pl.BlockSpec((1, 8, cc), lambda b, i: (b, jnp.minimum((i + 1) * hb, t // 8 - 1), 0)),
            att_spec, attc_spec,
            _resident(conv_w.shape),
            _resident(w_out.shape),
        ],
        out_specs=row(d),
        out_shape=jax.ShapeDtypeStruct((nb, t, d), F32),
        compiler_params=_cparams("parallel", "parallel"),
        name="hyb_out_proj",
    )(x, mod, gb, z, z, z, att, att_c, conv_w, w_out)


def _mla_in_kernel(x_ref, mod_ref, g_ref, wd_ref, qg_ref, kvg_ref, wuq_ref, wuk_ref, wuvt_ref,
                   cos_ref, sin_ref, qn_ref, qr_ref, kn_ref, vt_ref, kr_ref, *, q_rank, kv_rank):
    n_nope = qn_ref.shape[2]
    q_scale = (MLA_NOPE + MLA_ROPE) ** -0.5 * LOG2E
    down = lambda rows: _dot(_modnorm(x_ref[0, rows, :], g_ref[...], mod_ref[0, 3:4, :],
                                      mod_ref[0, 4:5, :]).astype(BF16), wd_ref[...])
    chunks = _row_chunks(x_ref.shape[1], PROJ_ROW_CHUNK)
    d_next = down(chunks[0])
    for n, rows in enumerate(chunks):
        d = d_next
        if n + 1 < len(chunks):
            d_next = down(chunks[n + 1])
        cos = cos_ref[0, rows, :]
        sin = sin_ref[0, rows, :]
        q = _dot(_rms(d[:, :q_rank], qg_ref[...]).astype(BF16), wuq_ref[...])
        qn_ref[0, rows, :] = (q[:, :n_nope] * q_scale).astype(BF16)
        for j in range(qr_ref.shape[2] // LANES):
            qr_ref[0, rows, j * LANES:(j + 1) * LANES] = (
                _rope(q[:, n_nope + j * LANES:n_nope + (j + 1) * LANES], cos, sin) * q_scale
            ).astype(BF16)
        ckv = _rms(d[:, q_rank:q_rank + kv_rank], kvg_ref[...]).astype(BF16)
        kn_ref[0, rows, :] = _dot(ckv, wuk_ref[...]).astype(BF16)
        vt_ref[0, :, rows] = _dot_t(wuvt_ref[...], ckv).astype(BF16)
        kr = _rope(d[:, q_rank + kv_rank:], cos, sin)
        first = (lax.broadcasted_iota(jnp.int32, kr.shape, 1) & (MLA_ROPE // 2)) == 0
        kr_ref[0, rows, :LANES] = jnp.where(first, kr, 0.0).astype(BF16)
        kr_ref[0, rows, LANES:] = jnp.where(first, 0.0, kr).astype(BF16)


def _mla_in_call(x, mod, g, wd, qg, kvg, wuq, wuk, wuvt, cos2, sin2):
    nb, t, d = x.shape
    tm = min(512, t)
    q_rank, kv_rank = qg.shape[0], kvg.shape[0]
    n_nope = wuk.shape[1]
    n_rope = wuq.shape[1] - n_nope
    n_v = wuvt.shape[0]
    row = lambda n: pl.BlockSpec((1, tm, n), lambda b, i: (b, i, 0))
    table = pl.BlockSpec((1, tm, LANES), lambda b, i: (jnp.where(b == nb - 1, 1, 0), i, 0))
    out_specs = [row(n_nope), row(n_rope), row(n_nope),
                 pl.BlockSpec((1, n_v, tm), lambda b, i: (b, 0, i)), row(2 * LANES)]
    out_shapes = [(nb, t, n_nope), (nb, t, n_rope), (nb, t, n_nope), (nb, n_v, t), (nb, t, 2 * LANES)]
    return pl.pallas_call(
        functools.partial(_mla_in_kernel, q_rank=q_rank, kv_rank=kv_rank),
        grid=(nb, t // tm),
        in_specs=[
            row(d),
            pl.BlockSpec((1, N_MOD, d), lambda b, i: (b, 0, 0)),
            _resident((1, d)),
            _resident(wd.shape), _resident((1, q_rank)), _resident((1, kv_rank)),
            _resident(wuq.shape), _resident(wuk.shape), _resident(wuvt.shape),
            table, table,
        ],
        out_specs=out_specs,
        out_shape=[jax.ShapeDtypeStruct(s, BF16) for s in out_shapes],
        compiler_params=_cparams("parallel", "parallel"),
        name="mla_in_proj",
    )(x, mod, g.reshape(1, d), wd, qg.reshape(1, q_rank), kvg.reshape(1, kv_rank), wuq, wuk, wuvt,
      cos2, sin2)


def _mla_attn_kernel(qn_ref, qr_ref, knc_ref, krc_ref, vtc_ref, knl_ref, krl_ref, vtl_ref,
                     o_ref, k_ref, vt_ref):
    tc = knc_ref.shape[1]

    @pl.when(pl.program_id(2) == 0)
    def _():
        k_ref[:tc, :LANES] = knc_ref[0]
        k_ref[:tc, LANES:] = krc_ref[0]
        k_ref[tc:, :LANES] = knl_ref[0]
        k_ref[tc:, LANES:] = krl_ref[0]
        vt_ref[:, :tc] = vtc_ref[0]
        vt_ref[:, tc:] = vtl_ref[0]

    def chain(rows):
        def store(o):
            o_ref[0, rows, :] = o
        return (lambda: jnp.concatenate([qn_ref[0, rows, :], qr_ref[0, rows, :]], axis=1)), store

    _attend_chains_t([chain(rows) for rows in _row_chunks(qn_ref.shape[1], MLA_ROW_CHUNK)],
                     [(k_ref[...], vt_ref[...])])


def _mla_attn_call(qn, qr, kn, vt, kr, b_lat, tc):
    nb, t, n_nope = qn.shape
    heads = n_nope // MLA_NOPE
    tq = min(2048, t)
    qspec = pl.BlockSpec((1, tq, LANES), lambda b, h, i: (b, i, h))
    qrspec = pl.BlockSpec((1, tq, LANES), lambda b, h, i: (b, i, h // 2))
    ctx = lambda f: pl.BlockSpec((1, tc, LANES), lambda b, h, i: (b_lat, b, f(h)))
    lat = lambda f: pl.BlockSpec((1, t, LANES), lambda b, h, i: (b, 0, f(h)))
    vt_ctx = pl.BlockSpec((1, MLA_V, tc), lambda b, h, i: (b_lat, h, b))
    vt_lat = pl.BlockSpec((1, MLA_V, t), lambda b, h, i: (b, h, 0))
    same = lambda h: h
    parity = lambda h: h % 2
    return pl.pallas_call(
        _mla_attn_kernel,
        grid=(b_lat, heads, t // tq),
        in_specs=[qspec, qrspec, ctx(same), ctx(parity), vt_ctx, lat(same), lat(parity), vt_lat],
        out_specs=qspec,
        out_shape=jax.ShapeDtypeStruct((b_lat, t, n_nope), BF16),
        scratch_shapes=[pltpu.VMEM((tc + t, 2 * LANES), BF16), pltpu.VMEM((MLA_V, tc + t), BF16)],
        compiler_params=_cparams("parallel", "parallel", "arbitrary"),
        name="mla_attention",
    )(qn, qr, kn, kr, vt, kn, kr, vt)


def _mla_out_kernel(x_ref, mod_ref, att_ref, w_ref, o_ref):
    o_ref[0] = x_ref[0] + mod_ref[0, 5:6, :] * _dot(att_ref[0], w_ref[...])


def _mla_out_call(x, mod, att, w_o):
    nb, t, k = att.shape
    d = w_o.shape[1]
    tm = min(512, t)
    row = lambda n: pl.BlockSpec((1, tm, n), lambda b, i: (b, i, 0))
    return pl.pallas_call(
        _mla_out_kernel,
        grid=(nb, t // tm),
        in_specs=[row(d), pl.BlockSpec((1, N_MOD, d), lambda b, i: (b, 0, 0)), row(k),
                  _resident(w_o.shape)],
        out_specs=row(d),
        out_shape=jax.ShapeDtypeStruct((nb, t, d), F32),
        compiler_params=_cparams("parallel", "parallel"),
        name="mla_out_proj",
    )(x, mod, att, w_o)


def _rope_angles(n_tok, dim):
    n_rows = n_tok // GRID_W
    row = jnp.repeat(jnp.arange(n_rows), GRID_W).astype(F32)
    col = jnp.tile(jnp.arange(GRID_W), n_rows).astype(F32)
    half = dim // 2
    inv = 1.0 / (ROPE_THETA ** (jnp.arange(0, half, 2, dtype=F32) / half))
    return jnp.concatenate([row[:, None] * inv, col[:, None] * inv], axis=-1)


def _rope_tables(n_tok, dim):
    ang = _rope_angles(n_tok, dim)
    reps = LANES // dim
    cos = jnp.concatenate([jnp.cos(ang)] * (2 * reps), axis=-1)
    sin = jnp.concatenate([-jnp.sin(ang)] * reps + [jnp.sin(ang)] * reps, axis=-1)
    return (jnp.stack([cos, jnp.ones_like(cos)]), jnp.stack([sin, jnp.zeros_like(sin)]))


def _deinterleave(n):
    return np.concatenate([np.arange(0, n, 2), np.arange(1, n, 2)])


def kernel(x, c, ctx, c_ctx, mod_w, mod_b, norm_ffn1, norm_mix, norm_ffn2,
           ffn1_w_gate, ffn1_w_up, ffn1_w_down, ffn2_w_gate, ffn2_w_up, ffn2_w_down,
           hyb_w_in, hyb_conv_w, hyb_q_norm, hyb_k_norm, hyb_w_out,
           mla_w_down, mla_q_norm, mla_kv_norm, mla_w_uq, mla_w_ukv, mla_w_o, final_norm):
    b_lat, t, d = x.shape
    tc = ctx.shape[1]
    depth = mod_w.shape[0]
    assert depth == 2 and ctx.shape[0] * tc == t and b_lat + 1 <= MOD_ROWS
    assert t & (t - 1) == 0 and tc & (tc - 1) == 0 and t % GRID_W == 0
    nb = b_lat + 1

    cvec = jnp.concatenate([c, c_ctx[None], jnp.zeros((MOD_ROWS - nb, d), F32)], axis=0)
    mods = _mod_call(cvec, mod_w, mod_b).reshape(depth, MOD_ROWS, N_MOD, d)
    cast = lambda w: w.astype(BF16)
    f1 = _ffn_weights(ffn1_w_gate, ffn1_w_up, ffn1_w_down)
    f2 = _ffn_weights(ffn2_w_gate, ffn2_w_up, ffn2_w_down)

    mod = mods[0]
    xs = _ffn_call(x, nb, mod, norm_ffn1[0], *f1, 0, 0, ctx=ctx.reshape(1, t, d))
    cc = d // 2
    n_q = cc
    n_kv = GQA_KV_HEADS * HEAD_DIM
    w_in = hyb_w_in[0]
    perm = _deinterleave(HEAD_DIM)
    pairs_last = lambda w: w.reshape(d, -1, HEAD_DIM // 2, 2).swapaxes(2, 3).reshape(d, -1)
    wq = pairs_last(w_in[:, 3 * cc:3 * cc + n_q])
    wk = pairs_last(w_in[:, 3 * cc + n_q:3 * cc + n_q + n_kv])
    wvt = w_in[:, 3 * cc + n_q + n_kv:].T
    cos_a, sin_a = _rope_tables(t, HEAD_DIM)
    gb, z, q, k, v = _hyb_in_call(
        xs, mod, norm_mix[0], cast(w_in[:, :cc]), cast(w_in[:, cc:2 * cc]), cast(w_in[:, 2 * cc:3 * cc]),
        cast(wq), cast(wk), cast(wvt), hyb_q_norm[0][perm], hyb_k_norm[0][perm], cos_a, sin_a)
    att = _gqa_latent_call(q, k, v, b_lat, tc)
    att_c = _gqa_ctx_call(q, k, v, b_lat, tc)
    xs = _hyb_out_call(xs, mod, gb, z, att, att_c, hyb_conv_w[0], cast(hyb_w_out[0]), b_lat, tc)
    xs = _ffn_call(xs, nb, mod, norm_ffn2[0], *f2, 0, 6)

    mod = mods[1]
    xs = _ffn_call(xs, nb, mod, norm_ffn1[1], *f1, 1, 0)
    heads = d // HEAD_DIM
    q_rank, kv_rank = mla_q_norm.shape[1], mla_kv_norm.shape[1]
    uq = mla_w_uq[0].reshape(q_rank, heads, MLA_NOPE + MLA_ROPE)
    uq_rope = uq[:, :, MLA_NOPE:].reshape(q_rank, heads // 2, 2, MLA_ROPE // 2, 2)
    wuq = jnp.concatenate([uq[:, :, :MLA_NOPE].reshape(q_rank, -1),
                           uq_rope.transpose(0, 1, 4, 2, 3).reshape(q_rank, -1)], axis=1)
    ukv = mla_w_ukv[0].reshape(kv_rank, heads, MLA_NOPE + MLA_V)
    wuk = ukv[:, :, :MLA_NOPE].reshape(kv_rank, -1)
    wuvt = ukv[:, :, MLA_NOPE:].reshape(kv_rank, -1).T
    kr0 = q_rank + kv_rank
    w_kr = mla_w_down[0][:, kr0:].reshape(d, MLA_ROPE // 2, 2)
    wdn = jnp.concatenate([mla_w_down[0][:, :kr0], w_kr[:, :, 0], w_kr[:, :, 0],
                           w_kr[:, :, 1], w_kr[:, :, 1]], axis=1)
    cos_m, sin_m = _rope_tables(t, MLA_ROPE)
    qn, qr, kn, vt, kr = _mla_in_call(xs, mod, norm_mix[1], cast(wdn), mla_q_norm[0], mla_kv_norm[0],
                                      cast(wuq), cast(wuk), cast(wuvt), cos_m, sin_m)
    att = _mla_attn_call(qn, qr, kn, vt, kr, b_lat, tc)
    xs = _mla_out_call(xs, mod, att, cast(mla_w_o[0]))
    return _ffn_call(xs, b_lat, mod, norm_ffn2[1], *f2, 1, 6, final_g=final_norm)
```

```python
import functools

import numpy as np
import jax
import jax.numpy as jnp
from jax import lax
from jax.experimental import pallas as pl
from jax.experimental.pallas import tpu as pltpu

HEAD_DIM = 128
GRID_W = 64
ROPE_THETA = 10000.0
EPS = 1e-6
N_MOD = 9
GQA_KV_HEADS = 2
MLA_NOPE = 128
MLA_ROPE = 64
MLA_V = 128
LANES = 128
VMEM_LIMIT = 56 * 1024 * 1024
MOD_ROWS = 16
FFN_ROW_CHUNKS = 4
FFN_TOKEN_TILE = 1024
FFN_TILE = 512
FFN_SLOTS = 4
PROJ_ROW_CHUNK = 256
GQA_ROW_CHUNK = 512
MLA_ROW_CHUNK = 512
ATTN_SCORE_LOOKAHEAD = 2
LOG2E = 1.4426950408889634
CAST_BLOCK_BYTES = 4 * 1024 * 1024

F32 = jnp.float32
BF16 = jnp.bfloat16


def _cparams(*sem):
    return pltpu.CompilerParams(dimension_semantics=sem, vmem_limit_bytes=VMEM_LIMIT)


def _resident(shape):
    nd = len(shape)
    return pl.BlockSpec(shape, lambda *_: (0,) * nd, pipeline_mode=pl.Buffered(1))


def _rms(x, g):
    return x * lax.rsqrt(jnp.mean(x * x, axis=-1, keepdims=True) + EPS) * g


def _modnorm(x, g, shift, scale):
    return _rms(x, g) * (1.0 + scale) + shift


def _rope(x, cos, sin):
    return x * cos + pltpu.roll(x, LANES // 2, axis=1) * sin


def _dot(a, b):
    return jnp.dot(a, b, preferred_element_type=F32)


def _dot_t(a, b):
    return lax.dot_general(a, b, (((1,), (1,)), ((), ())), preferred_element_type=F32)


def _mod_kernel(c_ref, w_ref, b_ref, o_ref):
    c = c_ref[...]
    s = (c * jax.nn.sigmoid(c)).astype(BF16)
    o_ref[0] = _dot(s, w_ref[0].astype(BF16)) + b_ref[0]


def _mod_call(cvec, mod_w, mod_b):
    depth, d, n = mod_w.shape
    tn = min(1024, n)
    return pl.pallas_call(
        _mod_kernel,
        grid=(depth, n // tn),
        in_specs=[
            pl.BlockSpec((MOD_ROWS, d), lambda l, j: (0, 0)),
            pl.BlockSpec((1, d, tn), lambda l, j: (l, 0, j)),
            pl.BlockSpec((1, 1, tn), lambda l, j: (l, 0, j)),
        ],
        out_specs=pl.BlockSpec((1, MOD_ROWS, tn), lambda l, j: (l, 0, j)),
        out_shape=jax.ShapeDtypeStruct((depth, MOD_ROWS, n), F32),
        compiler_params=_cparams("parallel", "parallel"),
        name="adaln_mod",
    )(cvec, mod_w, mod_b.reshape(depth, 1, n))


def _swiglu_act(xn, wg_ref, wu_ref):
    hg = _dot(xn, wg_ref[...])
    hu = _dot(xn, wu_ref[...])
    return (hg * jax.nn.sigmoid(hg) * hu).astype(BF16)


def _ffn_kernel(x_hbm, mod_ref, g_ref, wg_hbm, wu_hbm, wd_hbm, *rest, layer, m0, final, with_ctx):
    rest = list(rest)
    fn_ref = rest.pop(0) if final else None
    ctx_hbm = rest.pop(0) if with_ctx else None
    o_hbm, x_buf, o_buf, xn_ref, h_ref, wg_buf, wu_buf, wd_buf, sem, x_sem, o_sem = rest
    n_ff = wg_hbm.shape[1]
    tm = xn_ref.shape[0]
    nb, nt = pl.num_programs(0), pl.num_programs(1)
    step = pl.program_id(0) * nt + pl.program_id(1)
    not_first_step = step > 0
    not_last_step = step < nb * nt - 1
    half_gate = 0.5 * mod_ref[0, m0 + 2:m0 + 3, :]

    def copies(blk):
        slot = blk % FFN_SLOTS
        pairs = ((wg_hbm, wg_buf), (wu_hbm, wu_buf), (wd_hbm, wd_buf))
        return [pltpu.make_async_copy(hbm.at[layer, blk], buf.at[slot], sem.at[k, slot])
                for k, (hbm, buf) in enumerate(pairs)]

    def start(blk):
        for c in copies(blk):
            c.start()

    def wait(blk):
        for c in copies(blk):
            c.wait()

    def x_copy(s, from_ctx):
        b, i = s // nt, s % nt
        rows = pl.ds(pl.multiple_of(i * tm, tm), tm)
        src = ctx_hbm.at[0, rows] if from_ctx else x_hbm.at[b, rows]
        return pltpu.make_async_copy(src, x_buf, x_sem)

    def x_op(s, op):
        if with_ctx:
            is_ctx = s // nt == nb - 1
            pl.when(is_ctx)(lambda: op(x_copy(s, True)))
            pl.when(jnp.logical_not(is_ctx))(lambda: op(x_copy(s, False)))
        else:
            op(x_copy(s, False))

    def o_copy(s):
        b, i = s // nt, s % nt
        return pltpu.make_async_copy(o_buf, o_hbm.at[b, pl.ds(pl.multiple_of(i * tm, tm), tm)], o_sem)

    @pl.when(step == 0)
    def _():
        x_op(step, lambda c: c.start())
        start(0)
        start(1)

    x_op(step, lambda c: c.wait())
    wait(0)
    rs = tm // FFN_ROW_CHUNKS
    for r in range(FFN_ROW_CHUNKS):
        rows = slice(r * rs, (r + 1) * rs)
        xn = _modnorm(x_buf[rows, :], g_ref[...], mod_ref[0, m0:m0 + 1, :],
                      mod_ref[0, m0 + 1:m0 + 2, :]).astype(BF16)
        xn_ref[rows, :] = xn
        h_ref[0, rows, :] = _swiglu_act(xn, wg_buf.at[0], wu_buf.at[0])

    for j in range(n_ff):
        if j + 2 < n_ff:
            start(j + 2)
        else:
            pl.when(not_last_step)(functools.partial(start, j + 2 - n_ff))
        if j == 0:
            pl.when(not_first_step)(lambda: o_copy(step - 1).wait())
        if j == 1:
            pl.when(not_last_step)(lambda: x_op(step + 1, lambda c: c.start()))
        if j + 1 < n_ff:
            wait(j + 1)
            nslot = (j + 1) % FFN_SLOTS
            h_ref[(j + 1) % 2] = _swiglu_act(xn_ref[...], wg_buf.at[nslot], wu_buf.at[nslot])
        contrib = half_gate * _dot(h_ref[j % 2], wd_buf[j % FFN_SLOTS])
        if j == 0:
            o_buf[...] = x_buf[...] + contrib
        else:
            o_buf[...] += contrib

    if final:
        o_buf[...] = _rms(o_buf[...], fn_ref[...])
    o_copy(step).start()
    pl.when(jnp.logical_not(not_last_step))(lambda: o_copy(step).wait())


def _ffn_call(x, nb, mod, g, wg, wu, wd, layer, m0, final_g=None, ctx=None):
    _, t, d = x.shape
    _, n_ff, _, tf = wg.shape
    assert n_ff >= 2 and all(
        max(b for b in range(n_ff) if b % FFN_SLOTS == k) <= n_ff - 3 + k for k in (0, 1))
    final = final_g is not None
    tm = min(FFN_TOKEN_TILE, t)
    nt = t // tm
    hbm = pl.BlockSpec(memory_space=pl.ANY)
    in_specs = [
        hbm,
        pl.BlockSpec((1, N_MOD, d), lambda b, i: (b, 0, 0)),
        pl.BlockSpec((1, d), lambda b, i: (0, 0)),
        hbm, hbm, hbm,
    ]
    args = [x, mod, g.reshape(1, d), wg, wu, wd]
    if final:
        in_specs.append(pl.BlockSpec((1, d), lambda b, i: (0, 0)))
        args.append(final_g.reshape(1, d))
    if ctx is not None:
        in_specs.append(hbm)
        args.append(ctx)
    return pl.pallas_call(
        functools.partial(_ffn_kernel, layer=layer, m0=m0, final=final, with_ctx=ctx is not None),
        grid=(nb, nt),
        in_specs=in_specs,
        out_specs=hbm,
        out_shape=jax.ShapeDtypeStruct((nb, t, d), F32),
        scratch_shapes=[
            pltpu.VMEM((tm, d), F32), pltpu.VMEM((tm, d), F32),
            pltpu.VMEM((tm, d), BF16),
            pltpu.VMEM((2, tm, tf), BF16),
            pltpu.VMEM((FFN_SLOTS, d, tf), BF16), pltpu.VMEM((FFN_SLOTS, d, tf), BF16),
            pltpu.VMEM((FFN_SLOTS, tf, d), BF16),
            pltpu.SemaphoreType.DMA((3, FFN_SLOTS)),
            pltpu.SemaphoreType.DMA(()), pltpu.SemaphoreType.DMA(()),
        ],
        compiler_params=_cparams("arbitrary", "arbitrary"),
        name="swiglu_half_step",
    )(*args)


def _cast_kernel(w_ref, o_ref):
    cb = o_ref.shape[3]
    for n in range(o_ref.shape[1]):
        o_ref[0, n] = w_ref[0, :, n * cb:(n + 1) * cb].astype(BF16)


def _cast_call(w, col_block):
    depth, r, c = w.shape
    nblk = c // col_block
    tr = max(n for n in range(16, r + 1, 16) if r % n == 0 and 4 * c * n <= CAST_BLOCK_BYTES)
    return pl.pallas_call(
        _cast_kernel,
        grid=(depth, r // tr),
        in_specs=[pl.BlockSpec((1, tr, c), lambda l, i: (l, i, 0))],
        out_specs=pl.BlockSpec((1, nblk, tr, col_block), lambda l, i: (l, 0, i, 0)),
        out_shape=jax.ShapeDtypeStruct((depth, nblk, r, col_block), BF16),
        compiler_params=_cparams("parallel", "parallel"),
        name="weight_cast",
    )(w)


def _ffn_weights(w_gate, w_up, w_down):
    depth, d, dff = w_gate.shape
    tf = min(FFN_TILE, dff)
    return (_cast_call(w_gate, tf), _cast_call(w_up, tf),
            _cast_call(w_down, d).reshape(depth, dff // tf, tf, d))


def _hyb_in_kernel(x_ref, mod_ref, g_ref, wgb_ref, wgc_ref, wu_ref, wq_ref, wk_ref, wvt_ref,
                   qg_ref, kg_ref, cos_ref, sin_ref,
                   gb_ref, z_ref, q_ref, k_ref, vt_ref):
    q_scale = HEAD_DIM ** -0.5 * LOG2E
    norm = lambda rows: _modnorm(x_ref[0, rows, :], g_ref[...], mod_ref[0, 3:4, :],
                                 mod_ref[0, 4:5, :]).astype(BF16)
    chunks = _row_chunks(x_ref.shape[1], PROJ_ROW_CHUNK)
    xn_next = norm(chunks[0])
    for n, rows in enumerate(chunks):
        xn = xn_next
        if n + 1 < len(chunks):
            xn_next = norm(chunks[n + 1])
        gb_ref[0, rows, :] = _dot(xn, wgb_ref[...]).astype(BF16)
        z_ref[0, rows, :] = (_dot(xn, wgc_ref[...]) * _dot(xn, wu_ref[...])).astype(BF16)
        vt_ref[0, :, rows] = _dot_t(wvt_ref[...], xn).astype(BF16)
        cos = cos_ref[0, rows, :]
        sin = sin_ref[0, rows, :]
        q = _dot(xn, wq_ref[...])
        for h in range(q.shape[1] // HEAD_DIM):
            sl = slice(h * HEAD_DIM, (h + 1) * HEAD_DIM)
            q_ref[0, rows, sl] = (_rope(_rms(q[:, sl], qg_ref[...]), cos, sin) * q_scale).astype(BF16)
        k = _dot(xn, wk_ref[...])
        for h in range(k.shape[1] // HEAD_DIM):
            sl = slice(h * HEAD_DIM, (h + 1) * HEAD_DIM)
            k_ref[0, rows, sl] = _rope(_rms(k[:, sl], kg_ref[...]), cos, sin).astype(BF16)


def _hyb_in_call(x, mod, g, wgb, wgc, wu, wq, wk, wvt, qg, kg, cos2, sin2):
    nb, t, d = x.shape
    tm = min(512, t)
    cc, nq, nkv = wgb.shape[1], wq.shape[1], wk.shape[1]
    row = lambda n: pl.BlockSpec((1, tm, n), lambda b, i: (b, i, 0))
    table = pl.BlockSpec((1, tm, LANES), lambda b, i: (jnp.where(b == nb - 1, 1, 0), i, 0))
    return pl.pallas_call(
        _hyb_in_kernel,
        grid=(nb, t // tm),
        in_specs=[
            row(d),
            pl.BlockSpec((1, N_MOD, d), lambda b, i: (b, 0, 0)),
            _resident((1, d)),
            _resident(wgb.shape), _resident(wgc.shape), _resident(wu.shape),
            _resident(wq.shape), _resident(wk.shape), _resident(wvt.shape),
            _resident((1, HEAD_DIM)), _resident((1, HEAD_DIM)),
            table, table,
        ],
        out_specs=[row(cc), row(cc), row(nq), row(nkv),
                   pl.BlockSpec((1, nkv, tm), lambda b, i: (b, 0, i))],
        out_shape=[jax.ShapeDtypeStruct((nb, t, n), BF16) for n in (cc, cc, nq, nkv)]
        + [jax.ShapeDtypeStruct((nb, nkv, t), BF16)],
        compiler_params=_cparams("parallel", "parallel"),
        name="hyb_in_proj",
    )(x, mod, g.reshape(1, d), wgb, wgc, wu, wq, wk, wvt,
      qg.reshape(1, HEAD_DIM), kg.reshape(1, HEAD_DIM), cos2, sin2)


def _softmax_pv(s_list, kv_list):
    m = s_list[0].max(axis=-1, keepdims=True)
    for s in s_list[1:]:
        m = jnp.maximum(m, s.max(axis=-1, keepdims=True))
    l = 0.0
    o = 0.0
    for s, (_, v) in zip(s_list, kv_list):
        p = jnp.exp2(s - m)
        l = l + p.sum(axis=-1, keepdims=True)
        o = o + _dot(p.astype(BF16), v)
    return o / l


def _attend_chains(chains, kv_list):
    scores = lambda load_q: [_dot_t(load_q(), k) for k, _ in kv_list]
    ahead = [scores(load_q) for load_q, _ in chains[:ATTN_SCORE_LOOKAHEAD]]
    for n, (_, store_o) in enumerate(chains):
        s_list = ahead.pop(0)
        if n + ATTN_SCORE_LOOKAHEAD < len(chains):
            ahead.append(scores(chains[n + ATTN_SCORE_LOOKAHEAD][0]))
        store_o(_softmax_pv(s_list, kv_list).astype(BF16))


def _softmax_pv_t(st_list, kvt_list):
    m = st_list[0].max(axis=0, keepdims=True)
    for s in st_list[1:]:
        m = jnp.maximum(m, s.max(axis=0, keepdims=True))
    l = 0.0
    o = 0.0
    for s, (_, vt) in zip(st_list, kvt_list):
        p = jnp.exp2(s - m)
        l = l + p.sum(axis=0, keepdims=True)
        o = o + _dot(vt, p.astype(BF16))
    return (o / l).T


def _attend_chains_t(chains, kvt_list):
    scores = lambda load_q: [_dot_t(k, load_q()) for k, _ in kvt_list]
    ahead = [scores(load_q) for load_q, _ in chains[:ATTN_SCORE_LOOKAHEAD]]
    for n, (_, store_o) in enumerate(chains):
        st_list = ahead.pop(0)
        if n + ATTN_SCORE_LOOKAHEAD < len(chains):
            ahead.append(scores(chains[n + ATTN_SCORE_LOOKAHEAD][0]))
        store_o(_softmax_pv_t(st_list, kvt_list).astype(BF16))


def _row_chunks(n, c):
    c = min(c, n)
    return [slice(r * c, (r + 1) * c) for r in range(n // c)]


def _gqa_kernel(*refs, has_lat):
    if has_lat:
        q_ref, kc_ref, vtc_ref, kl_ref, vtl_ref, o_ref = refs
    else:
        q_ref, kc_ref, vtc_ref, o_ref = refs
    kvt_list = [(kc_ref[0], vtc_ref[0])]
    if has_lat:
        kvt_list.append((kl_ref[0], vtl_ref[0]))

    def chain(rows, sl):
        def store(o):
            o_ref[0, rows, sl] = o
        return (lambda: q_ref[0, rows, sl]), store

    _attend_chains_t([chain(rows, slice(h * HEAD_DIM, (h + 1) * HEAD_DIM))
                      for h in range(q_ref.shape[2] // HEAD_DIM)
                      for rows in _row_chunks(q_ref.shape[1], GQA_ROW_CHUNK)], kvt_list)


def _gqa_latent_call(q, k, vt, b_lat, tc):
    nb, t, nq = q.shape
    gw = nq // GQA_KV_HEADS
    tq = min(1024, t)
    ctx = pl.BlockSpec((1, tc, HEAD_DIM), lambda b, g, i: (b_lat, b, g))
    lat = pl.BlockSpec((1, t, HEAD_DIM), lambda b, g, i: (b, 0, g))
    vt_ctx = pl.BlockSpec((1, HEAD_DIM, tc), lambda b, g, i: (b_lat, g, b))
    vt_lat = pl.BlockSpec((1, HEAD_DIM, t), lambda b, g, i: (b, g, 0))
    qo = pl.BlockSpec((1, tq, gw), lambda b, g, i: (b, i, g))
    return pl.pallas_call(
        functools.partial(_gqa_kernel, has_lat=True),
        grid=(b_lat, GQA_KV_HEADS, t // tq),
        in_specs=[qo, ctx, vt_ctx, lat, vt_lat],
        out_specs=qo,
        out_shape=jax.ShapeDtypeStruct((b_lat, t, nq), BF16),
        compiler_params=_cparams("parallel", "parallel", "parallel"),
        name="gqa_latent",
    )(q, k, vt, k, vt)


def _gqa_ctx_call(q, k, vt, b_lat, tc):
    nb, t, nq = q.shape
    gw = nq // GQA_KV_HEADS
    ctx = pl.BlockSpec((1, tc, HEAD_DIM), lambda b, g: (b_lat, b, g))
    vt_ctx = pl.BlockSpec((1, HEAD_DIM, tc), lambda b, g: (b_lat, g, b))
    return pl.pallas_call(
        functools.partial(_gqa_kernel, has_lat=False),
        grid=(t // tc, GQA_KV_HEADS),
        in_specs=[pl.BlockSpec((1, tc, gw), lambda b, g: (b_lat, b, g)), ctx, vt_ctx],
        out_specs=pl.BlockSpec((1, tc, gw), lambda b, g: (0, b, g)),
        out_shape=jax.ShapeDtypeStruct((1, t, nq), BF16),
        compiler_params=_cparams("parallel", "parallel"),
        name="gqa_context",
    )(q, k, vt)


def _hyb_out_kernel(x_ref, mod_ref, gb_ref, z_ref, zp_ref, zn_ref, att_ref, attc_ref, cw_ref, w_ref,
                    o_ref, *, seq_lat, seq_ctx, b_lat):
    b, i = pl.program_id(0), pl.program_id(1)
    tm, cc = z_ref.shape[1], z_ref.shape[2]
    acc = _dot(jnp.where(b == b_lat, attc_ref[0], att_ref[0]), w_ref[cc:, :])
    z = z_ref[0].astype(F32)
    rows = lax.broadcasted_iota(jnp.int32, (tm, cc), 0)
    seq = jnp.where(b == b_lat, seq_ctx, seq_lat)
    pos = (i * tm + rows) & (seq - 1)
    z_prev = jnp.where(rows == 0, zp_ref[0, 7:8, :].astype(F32), pltpu.roll(z, 1, axis=0))
    z_next = jnp.where(rows == tm - 1, zn_ref[0, 0:1, :].astype(F32),
                       pltpu.roll(z, tm - 1, axis=0))
    z_prev = jnp.where(pos == 0, 0.0, z_prev)
    z_next = jnp.where(pos == seq - 1, 0.0, z_next)
    conv = z_prev * cw_ref[0:1, :] + z * cw_ref[1:2, :] + z_next * cw_ref[2:3, :]
    acc = acc + _dot((gb_ref[0].astype(F32) * conv).astype(BF16), w_ref[:cc, :])
    o_ref[0] = x_ref[0] + mod_ref[0, 5:6, :] * acc


def _hyb_out_call(x, mod, gb, z, att, att_c, conv_w, w_out, b_lat, tc):
    nb, t, d = x.shape
    cc = gb.shape[2]
    tm = min(512, t)
    nt = t // tm
    hb = tm // 8
    row = lambda n: pl.BlockSpec((1, tm, n), lambda b, i: (b, i, 0))
    nq = att.shape[2]
    att_spec = pl.BlockSpec((1, tm, nq), lambda b, i: (jnp.minimum(b, b_lat - 1),
                                                       jnp.where(b < b_lat, i, nt - 1), 0))
    attc_spec = pl.BlockSpec((1, tm, nq), lambda b, i: (0, jnp.where(b < b_lat, 0, i), 0))
    return pl.pallas_call(
        functools.partial(_hyb_out_kernel, seq_lat=t, seq_ctx=tc, b_lat=b_lat),
        grid=(nb, t // tm),
        in_specs=[
            row(d),
            pl.BlockSpec((1, N_MOD, d), lambda b, i: (b, 0, 0)),
            row(cc), row(cc),
            pl.BlockSpec((1, 8, cc), lambda b, i: (b, jnp.maximum(i * hb - 1, 0), 0)),
            pl.BlockSpec((1, 8, cc), lambda b, i: (b, jnp.minimum((i + 1) * hb, t // 8 - 1), 0)),
            att_spec, attc_spec,
            _resident(conv_w.shape),
            _resident(w_out.shape),
        ],
        out_specs=row(d),
        out_shape=jax.ShapeDtypeStruct((nb, t, d), F32),
        compiler_params=_cparams("parallel", "parallel"),
        name="hyb_out_proj",
    )(x, mod, gb, z, z, z, att, att_c, conv_w, w_out)


def _mla_in_kernel(x_ref, mod_ref, g_ref, wd_ref, qg_ref, kvg_ref, wuq_ref, wuk_ref, wuvt_ref,
                   cos_ref, sin_ref, qn_ref, qr_ref, kn_ref, vt_ref, kr_ref, *, q_rank, kv_rank):
    n_nope = qn_ref.shape[2]
    q_scale = (MLA_NOPE + MLA_ROPE) ** -0.5 * LOG2E
    down = lambda rows: _dot(_modnorm(x_ref[0, rows, :], g_ref[...], mod_ref[0, 3:4, :],
                                      mod_ref[0, 4:5, :]).astype(BF16), wd_ref[...])
    chunks = _row_chunks(x_ref.shape[1], PROJ_ROW_CHUNK)
    d_next = down(chunks[0])
    for n, rows in enumerate(chunks):
        d = d_next
        if n + 1 < len(chunks):
            d_next = down(chunks[n + 1])
        cos = cos_ref[0, rows, :]
        sin = sin_ref[0, rows, :]
        q = _dot(_rms(d[:, :q_rank], qg_ref[...]).astype(BF16), wuq_ref[...])
        qn_ref[0, rows, :] = (q[:, :n_nope] * q_scale).astype(BF16)
        for j in range(qr_ref.shape[2] // LANES):
            qr_ref[0, rows, j * LANES:(j + 1) * LANES] = (
                _rope(q[:, n_nope + j * LANES:n_nope + (j + 1) * LANES], cos, sin) * q_scale
            ).astype(BF16)
        ckv = _rms(d[:, q_rank:q_rank + kv_rank], kvg_ref[...]).astype(BF16)
        kn_ref[0, rows, :] = _dot(ckv, wuk_ref[...]).astype(BF16)
        vt_ref[0, :, rows] = _dot_t(wuvt_ref[...], ckv).astype(BF16)
        kr = _rope(d[:, q_rank + kv_rank:], cos, sin)
        first = (lax.broadcasted_iota(jnp.int32, kr.shape, 1) & (MLA_ROPE // 2)) == 0
        kr_ref[0, rows, :LANES] = jnp.where(first, kr, 0.0).astype(BF16)
        kr_ref[0, rows, LANES:] = jnp.where(first, 0.0, kr).astype(BF16)


def _mla_in_call(x, mod, g, wd, qg, kvg, wuq, wuk, wuvt, cos2, sin2):
    nb, t, d = x.shape
    tm = min(512, t)
    q_rank, kv_rank = qg.shape[0], kvg.shape[0]
    n_nope = wuk.shape[1]
    n_rope = wuq.shape[1] - n_nope
    n_v = wuvt.shape[0]
    row = lambda n: pl.BlockSpec((1, tm, n), lambda b, i: (b, i, 0))
    table = pl.BlockSpec((1, tm, LANES), lambda b, i: (jnp.where(b == nb - 1, 1, 0), i, 0))
    out_specs = [row(n_nope), row(n_rope), row(n_nope),
                 pl.BlockSpec((1, n_v, tm), lambda b, i: (b, 0, i)), row(2 * LANES)]
    out_shapes = [(nb, t, n_nope), (nb, t, n_rope), (nb, t, n_nope), (nb, n_v, t), (nb, t, 2 * LANES)]
    return pl.pallas_call(
        functools.partial(_mla_in_kernel, q_rank=q_rank, kv_rank=kv_rank),
        grid=(nb, t // tm),
        in_specs=[
            row(d),
            pl.BlockSpec((1, N_MOD, d), lambda b, i: (b, 0, 0)),
            _resident((1, d)),
            _resident(wd.shape), _resident((1, q_rank)), _resident((1, kv_rank)),
            _resident(wuq.shape), _resident(wuk.shape), _resident(wuvt.shape),
            table, table,
        ],
        out_specs=out_specs,
        out_shape=[jax.ShapeDtypeStruct(s, BF16) for s in out_shapes],
        compiler_params=_cparams("parallel", "parallel"),
        name="mla_in_proj",
    )(x, mod, g.reshape(1, d), wd, qg.reshape(1, q_rank), kvg.reshape(1, kv_rank), wuq, wuk, wuvt,
      cos2, sin2)


def _mla_attn_kernel(qn_ref, qr_ref, knc_ref, krc_ref, vtc_ref, knl_ref, krl_ref, vtl_ref,
                     o_ref, k_ref, vt_ref):
    tc = knc_ref.shape[1]

    @pl.when(pl.program_id(2) == 0)
    def _():
        k_ref[:tc, :LANES] = knc_ref[0]
        k_ref[:tc, LANES:] = krc_ref[0]
        k_ref[tc:, :LANES] = knl_ref[0]
        k_ref[tc:, LANES:] = krl_ref[0]
        vt_ref[:, :tc] = vtc_ref[0]
        vt_ref[:, tc:] = vtl_ref[0]

    def chain(rows):
        def store(o):
            o_ref[0, rows, :] = o
        return (lambda: jnp.concatenate([qn_ref[0, rows, :], qr_ref[0, rows, :]], axis=1)), store

    _attend_chains_t([chain(rows) for rows in _row_chunks(qn_ref.shape[1], MLA_ROW_CHUNK)],
                     [(k_ref[...], vt_ref[...])])


def _mla_attn_call(qn, qr, kn, vt, kr, b_lat, tc):
    nb, t, n_nope = qn.shape
    heads = n_nope // MLA_NOPE
    tq = min(2048, t)
    qspec = pl.BlockSpec((1, tq, LANES), lambda b, h, i: (b, i, h))
    qrspec = pl.BlockSpec((1, tq, LANES), lambda b, h, i: (b, i, h // 2))
    ctx = lambda f: pl.BlockSpec((1, tc, LANES), lambda b, h, i: (b_lat, b, f(h)))
    lat = lambda f: pl.BlockSpec((1, t, LANES), lambda b, h, i: (b, 0, f(h)))
    vt_ctx = pl.BlockSpec((1, MLA_V, tc), lambda b, h, i: (b_lat, h, b))
    vt_lat = pl.BlockSpec((1, MLA_V, t), lambda b, h, i: (b, h, 0))
    same = lambda h: h
    parity = lambda h: h % 2
    return pl.pallas_call(
        _mla_attn_kernel,
        grid=(b_lat, heads, t // tq),
        in_specs=[qspec, qrspec, ctx(same), ctx(parity), vt_ctx, lat(same), lat(parity), vt_lat],
        out_specs=qspec,
        out_shape=jax.ShapeDtypeStruct((b_lat, t, n_nope), BF16),
        scratch_shapes=[pltpu.VMEM((tc + t, 2 * LANES), BF16), pltpu.VMEM((MLA_V, tc + t), BF16)],
        compiler_params=_cparams("parallel", "parallel", "arbitrary"),
        name="mla_attention",
    )(qn, qr, kn, kr, vt, kn, kr, vt)


def _mla_out_kernel(x_ref, mod_ref, att_ref, w_ref, o_ref):
    o_ref[0] = x_ref[0] + mod_ref[0, 5:6, :] * _dot(att_ref[0], w_ref[...])


def _mla_out_call(x, mod, att, w_o):
    nb, t, k = att.shape
    d = w_o.shape[1]
    tm = min(512, t)
    row = lambda n: pl.BlockSpec((1, tm, n), lambda b, i: (b, i, 0))
    return pl.pallas_call(
        _mla_out_kernel,
        grid=(nb, t // tm),
        in_specs=[row(d), pl.BlockSpec((1, N_MOD, d), lambda b, i: (b, 0, 0)), row(k),
                  _resident(w_o.shape)],
        out_specs=row(d),
        out_shape=jax.ShapeDtypeStruct((nb, t, d), F32),
        compiler_params=_cparams("parallel", "parallel"),
        name="mla_out_proj",
    )(x, mod, att, w_o)


def _rope_angles(n_tok, dim):
    n_rows = n_tok // GRID_W
    row = jnp.repeat(jnp.arange(n_rows), GRID_W).astype(F32)
    col = jnp.tile(jnp.arange(GRID_W), n_rows).astype(F32)
    half = dim // 2
    inv = 1.0 / (ROPE_THETA ** (jnp.arange(0, half, 2, dtype=F32) / half))
    return jnp.concatenate([row[:, None] * inv, col[:, None] * inv], axis=-1)


def _rope_tables(n_tok, dim):
    ang = _rope_angles(n_tok, dim)
    reps = LANES // dim
    cos = jnp.concatenate([jnp.cos(ang)] * (2 * reps), axis=-1)
    sin = jnp.concatenate([-jnp.sin(ang)] * reps + [jnp.sin(ang)] * reps, axis=-1)
    return (jnp.stack([cos, jnp.ones_like(cos)]), jnp.stack([sin, jnp.zeros_like(sin)]))


def _deinterleave(n):
    return np.concatenate([np.arange(0, n, 2), np.arange(1, n, 2)])


def kernel(x, c, ctx, c_ctx, mod_w, mod_b, norm_ffn1, norm_mix, norm_ffn2,
           ffn1_w_gate, ffn1_w_up, ffn1_w_down, ffn2_w_gate, ffn2_w_up, ffn2_w_down,
           hyb_w_in, hyb_conv_w, hyb_q_norm, hyb_k_norm, hyb_w_out,
           mla_w_down, mla_q_norm, mla_kv_norm, mla_w_uq, mla_w_ukv, mla_w_o, final_norm):
    b_lat, t, d = x.shape
    tc = ctx.shape[1]
    depth = mod_w.shape[0]
    assert depth == 2 and ctx.shape[0] * tc == t and b_lat + 1 <= MOD_ROWS
    assert t & (t - 1) == 0 and tc & (tc - 1) == 0 and t % GRID_W == 0
    nb = b_lat + 1

    cvec = jnp.concatenate([c, c_ctx[None], jnp.zeros((MOD_ROWS - nb, d), F32)], axis=0)
    mods = _mod_call(cvec, mod_w, mod_b).reshape(depth, MOD_ROWS, N_MOD, d)
    cast = lambda w: w.astype(BF16)
    f1 = _ffn_weights(ffn1_w_gate, ffn1_w_up, ffn1_w_down)
    f2 = _ffn_weights(ffn2_w_gate, ffn2_w_up, ffn2_w_down)

    mod = mods[0]
    xs = _ffn_call(x, nb, mod, norm_ffn1[0], *f1, 0, 0, ctx=ctx.reshape(1, t, d))
    cc = d // 2
    n_q = cc
    n_kv = GQA_KV_HEADS * HEAD_DIM
    w_in = hyb_w_in[0]
    perm = _deinterleave(HEAD_DIM)
    pairs_last = lambda w: w.reshape(d, -1, HEAD_DIM // 2, 2).swapaxes(2, 3).reshape(d, -1)
    wq = pairs_last(w_in[:, 3 * cc:3 * cc + n_q])
    wk = pairs_last(w_in[:, 3 * cc + n_q:3 * cc + n_q + n_kv])
    wvt = w_in[:, 3 * cc + n_q + n_kv:].T
    cos_a, sin_a = _rope_tables(t, HEAD_DIM)
    gb, z, q, k, v = _hyb_in_call(
        xs, mod, norm_mix[0], cast(w_in[:, :cc]), cast(w_in[:, cc:2 * cc]), cast(w_in[:, 2 * cc:3 * cc]),
        cast(wq), cast(wk), cast(wvt), hyb_q_norm[0][perm], hyb_k_norm[0][perm], cos_a, sin_a)
    att = _gqa_latent_call(q, k, v, b_lat, tc)
    att_c = _gqa_ctx_call(q, k, v, b_lat, tc)
    xs = _hyb_out_call(xs, mod, gb, z, att, att_c, hyb_conv_w[0], cast(hyb_w_out[0]), b_lat, tc)
    xs = _ffn_call(xs, nb, mod, norm_ffn2[0], *f2, 0, 6)

    mod = mods[1]
    xs = _ffn_call(xs, nb, mod, norm_ffn1[1], *f1, 1, 0)
    heads = d // HEAD_DIM
    q_rank, kv_rank = mla_q_norm.shape[1], mla_kv_norm.shape[1]
    uq = mla_w_uq[0].reshape(q_rank, heads, MLA_NOPE + MLA_ROPE)
    uq_rope = uq[:, :, MLA_NOPE:].reshape(q_rank, heads // 2, 2, MLA_ROPE // 2, 2)
    wuq = jnp.concatenate([uq[:, :, :MLA_NOPE].reshape(q_rank, -1),
                           uq_rope.transpose(0, 1, 4, 2, 3).reshape(q_rank, -1)], axis=1)
    ukv = mla_w_ukv[0].reshape(kv_rank, heads, MLA_NOPE + MLA_V)
    wuk = ukv[:, :, :MLA_NOPE].reshape(kv_rank, -1)
    wuvt = ukv[:, :, MLA_NOPE:].reshape(kv_rank, -1).T
    kr0 = q_rank + kv_rank
    w_kr = mla_w_down[0][:, kr0:].reshape(d, MLA_ROPE // 2, 2)
    wdn = jnp.concatenate([mla_w_down[0][:, :kr0], w_kr[:, :, 0], w_kr[:, :, 0],
                           w_kr[:, :, 1], w_kr[:, :, 1]], axis=1)
    cos_m, sin_m = _rope_tables(t, MLA_ROPE)
    qn, qr, kn, vt, kr = _mla_in_call(xs, mod, norm_mix[1], cast(wdn), mla_q_norm[0], mla_kv_norm[0],
                                      cast(wuq), cast(wuk), cast(wuvt), cos_m, sin_m)
    att = _mla_attn_call(qn, qr, kn, vt, kr, b_lat, tc)
    xs = _mla_out_call(xs, mod, att, cast(mla_w_o[0]))
    return _ffn_call(xs, b_lat, mod, norm_ffn2[1], *f2, 1, 6, final_g=final_norm)
```

```python
import functools

import numpy as np
import jax
import jax.numpy as jnp
from jax import lax
from jax.experimental import pallas as pl
from jax.experimental.pallas import tpu as pltpu

HEAD_DIM = 128
GRID_W = 64
ROPE_THETA = 10000.0
EPS = 1e-6
N_MOD = 9
GQA_KV_HEADS = 2
MLA_NOPE = 128
MLA_ROPE = 64
MLA_V = 128
LANES = 128
VMEM_LIMIT = 60 * 1024 * 1024
MOD_ROWS = 16
FFN_ROW_CHUNKS = 4
FFN_TILE = 512
FFN_SLOTS = 4
PROJ_ROW_CHUNK = 256
GQA_ROW_CHUNK = 512
MLA_ROW_CHUNK = 512
ATTN_SCORE_LOOKAHEAD = 2
LOG2E = 1.4426950408889634
CAST_BLOCK_BYTES = 4 * 1024 * 1024

F32 = jnp.float32
BF16 = jnp.bfloat16


def _cparams(*sem):
    return pltpu.CompilerParams(dimension_semantics=sem, vmem_limit_bytes=VMEM_LIMIT)


def _resident(shape):
    nd = len(shape)
    return pl.BlockSpec(shape, lambda *_: (0,) * nd, pipeline_mode=pl.Buffered(1))


def _rms(x, g):
    return x * lax.rsqrt(jnp.mean(x * x, axis=-1, keepdims=True) + EPS) * g


def _modnorm(x, g, shift, scale):
    return _rms(x, g) * (1.0 + scale) + shift


def _rope(x, cos, sin):
    return x * cos + pltpu.roll(x, LANES // 2, axis=1) * sin


def _dot(a, b):
    return jnp.dot(a, b, preferred_element_type=F32)


def _dot_t(a, b):
    return lax.dot_general(a, b, (((1,), (1,)), ((), ())), preferred_element_type=F32)


def _mod_kernel(c_ref, w_ref, b_ref, o_ref):
    c = c_ref[...]
    s = (c * jax.nn.sigmoid(c)).astype(BF16)
    o_ref[0] = _dot(s, w_ref[0].astype(BF16)) + b_ref[0]


def _mod_call(cvec, mod_w, mod_b):
    depth, d, n = mod_w.shape
    tn = min(1024, n)
    return pl.pallas_call(
        _mod_kernel,
        grid=(depth, n // tn),
        in_specs=[
            pl.BlockSpec((MOD_ROWS, d), lambda l, j: (0, 0)),
            pl.BlockSpec((1, d, tn), lambda l, j: (l, 0, j)),
            pl.BlockSpec((1, 1, tn), lambda l, j: (l, 0, j)),
        ],
        out_specs=pl.BlockSpec((1, MOD_ROWS, tn), lambda l, j: (l, 0, j)),
        out_shape=jax.ShapeDtypeStruct((depth, MOD_ROWS, n), F32),
        compiler_params=_cparams("parallel", "parallel"),
        name="adaln_mod",
    )(cvec, mod_w, mod_b.reshape(depth, 1, n))


def _swiglu_act(xn, wg_ref, wu_ref):
    hg = _dot(xn, wg_ref[...])
    hu = _dot(xn, wu_ref[...])
    return (hg * jax.nn.sigmoid(hg) * hu).astype(BF16)


_GATE_UP, _DOWN = "gate_up", "down"


def _ring_plan(n_ff):
    assert FFN_SLOTS >= 4
    plan = {}
    for j in range(0, n_ff, 2):
        waits = ([(_GATE_UP, t) for t in (j + 1, j + 2) if t < n_ff]
                 + [(_DOWN, t) for t in (j, j + 1) if t < n_ff])
        starts = ([(_GATE_UP, t) for t in (j + 3, j + 4) if t < n_ff]
                  + [(_DOWN, t) for t in (j + 2, j + 3) if t < n_ff])
        plan[j] = (waits, starts, [])
    first = [(_GATE_UP, t) for t in range(min(3, n_ff))] + [(_DOWN, t) for t in range(min(2, n_ff))]
    for kind, tile in first:
        last_user = max(t for t in range(n_ff) if t % FFN_SLOTS == tile % FFN_SLOTS)
        free_from = last_user if kind == _GATE_UP else last_user + 1
        barrier = free_from + free_from % 2
        assert barrier in plan, "no barrier left to prefetch the next grid step's tile"
        plan[barrier][2].append((kind, tile))
    return plan, first


def _ffn_kernel(x_ref, mod_ref, g_ref, wg_hbm, wu_hbm, wd_hbm, *rest, layer, m0, final, with_ctx):
    rest = list(rest)
    fn_ref = rest.pop(0) if final else None
    ctx_ref = rest.pop(0) if with_ctx else None
    o_ref, xn_ref, h_ref, wg_buf, wu_buf, wd_buf, sem = rest
    is_ctx = pl.program_id(0) == pl.num_programs(0) - 1
    n_ff = wg_hbm.shape[1]
    tm = xn_ref.shape[0]
    step = pl.program_id(0) * pl.num_programs(1) + pl.program_id(1)
    not_last_step = step < pl.num_programs(0) * pl.num_programs(1) - 1
    half_gate = 0.5 * mod_ref[0, m0 + 2:m0 + 3, :]

    def copies(kind, blk):
        slot = blk % FFN_SLOTS
        pairs = (((0, wg_hbm, wg_buf), (1, wu_hbm, wu_buf)) if kind == _GATE_UP
                 else ((2, wd_hbm, wd_buf),))
        return [pltpu.make_async_copy(hbm.at[layer, blk], buf.at[slot], sem.at[k, slot])
                for k, hbm, buf in pairs]

    def start(kind, blk):
        for c in copies(kind, blk):
            c.start()

    def wait(kind, blk):
        for c in copies(kind, blk):
            c.wait()

    def x_rows(rows):
        x = x_ref[0, rows, :]
        return jnp.where(is_ctx, ctx_ref[0, rows, :], x) if with_ctx else x

    plan, first = _ring_plan(n_ff)

    @pl.when(step == 0)
    def _():
        for item in first:
            start(*item)

    wait(_GATE_UP, 0)
    rs = tm // FFN_ROW_CHUNKS
    for r in range(FFN_ROW_CHUNKS):
        rows = slice(r * rs, (r + 1) * rs)
        xn = _modnorm(x_rows(rows), g_ref[...], mod_ref[0, m0:m0 + 1, :],
                      mod_ref[0, m0 + 1:m0 + 2, :]).astype(BF16)
        xn_ref[rows, :] = xn
        h_ref[0, rows, :] = _swiglu_act(xn, wg_buf.at[0], wu_buf.at[0])

    for j in range(n_ff):
        if j in plan:
            waits, starts, next_starts = plan[j]
            for item in starts:
                start(*item)
            if next_starts:
                pl.when(not_last_step)(
                    lambda items=next_starts: [start(*item) for item in items] and None)
            for item in waits:
                wait(*item)
        if j + 1 < n_ff:
            nslot = (j + 1) % FFN_SLOTS
            h_ref[(j + 1) % 2] = _swiglu_act(xn_ref[...], wg_buf.at[nslot], wu_buf.at[nslot])
        contrib = half_gate * _dot(h_ref[j % 2], wd_buf[j % FFN_SLOTS])
        if j == 0:
            o_ref[0] = x_rows(slice(None)) + contrib
        else:
            o_ref[0] += contrib

    if final:
        o_ref[0] = _rms(o_ref[0], fn_ref[...])


def _ffn_call(x, nb, mod, g, wg, wu, wd, layer, m0, final_g=None, ctx=None):
    _, t, d = x.shape
    _, n_ff, _, tf = wg.shape
    _ring_plan(n_ff)
    final = final_g is not None
    tm = min(512, t)
    nt = t // tm
    if ctx is None:
        x_spec = pl.BlockSpec((1, tm, d), lambda b, i: (b, i, 0))
    else:
        x_spec = pl.BlockSpec((1, tm, d), lambda b, i: (jnp.minimum(b, nb - 2),
                                                         jnp.where(b < nb - 1, i, nt - 1), 0))
    in_specs = [
        x_spec,
        pl.BlockSpec((1, N_MOD, d), lambda b, i: (b, 0, 0)),
        pl.BlockSpec((1, d), lambda b, i: (0, 0)),
        pl.BlockSpec(memory_space=pl.ANY),
        pl.BlockSpec(memory_space=pl.ANY),
        pl.BlockSpec(memory_space=pl.ANY),
    ]
    args = [x, mod, g.reshape(1, d), wg, wu, wd]
    if final:
        in_specs.append(pl.BlockSpec((1, d), lambda b, i: (0, 0)))
        args.append(final_g.reshape(1, d))
    if ctx is not None:
        in_specs.append(pl.BlockSpec((1, tm, d), lambda b, i: (0, jnp.where(b < nb - 1, 0, i), 0)))
        args.append(ctx)
    return pl.pallas_call(
        functools.partial(_ffn_kernel, layer=layer, m0=m0, final=final, with_ctx=ctx is not None),
        grid=(nb, nt),
        in_specs=in_specs,
        out_specs=pl.BlockSpec((1, tm, d), lambda b, i: (b, i, 0)),
        out_shape=jax.ShapeDtypeStruct((nb, t, d), F32),
        scratch_shapes=[
            pltpu.VMEM((tm, d), BF16),
            pltpu.VMEM((2, tm, tf), BF16),
            pltpu.VMEM((FFN_SLOTS, d, tf), BF16), pltpu.VMEM((FFN_SLOTS, d, tf), BF16),
            pltpu.VMEM((FFN_SLOTS, tf, d), BF16),
            pltpu.SemaphoreType.DMA((3, FFN_SLOTS)),
        ],
        compiler_params=_cparams("arbitrary", "arbitrary"),
        name="swiglu_half_step",
    )(*args)


def _cast_kernel(w_ref, o_ref):
    cb = o_ref.shape[3]
    for n in range(o_ref.shape[1]):
        o_ref[0, n] = w_ref[0, :, n * cb:(n + 1) * cb].astype(BF16)


def _cast_call(w, col_block):
    depth, r, c = w.shape
    nblk = c // col_block
    tr = max(n for n in range(16, r + 1, 16) if r % n == 0 and 4 * c * n <= CAST_BLOCK_BYTES)
    return pl.pallas_call(
        _cast_kernel,
        grid=(depth, r // tr),
        in_specs=[pl.BlockSpec((1, tr, c), lambda l, i: (l, i, 0))],
        out_specs=pl.BlockSpec((1, nblk, tr, col_block), lambda l, i: (l, 0, i, 0)),
        out_shape=jax.ShapeDtypeStruct((depth, nblk, r, col_block), BF16),
        compiler_params=_cparams("parallel", "parallel"),
        name="weight_cast",
    )(w)


def _ffn_weights(w_gate, w_up, w_down):
    depth, d, dff = w_gate.shape
    tf = min(FFN_TILE, dff)
    return (_cast_call(w_gate, tf), _cast_call(w_up, tf),
            _cast_call(w_down, d).reshape(depth, dff // tf, tf, d))


def _hyb_in_kernel(x_ref, mod_ref, g_ref, wgb_ref, wgc_ref, wu_ref, wq_ref, wk_ref, wvt_ref,
                   qg_ref, kg_ref, cos_ref, sin_ref,
                   gb_ref, z_ref, q_ref, k_ref, vt_ref):
    q_scale = HEAD_DIM ** -0.5 * LOG2E
    norm = lambda rows: _modnorm(x_ref[0, rows, :], g_ref[...], mod_ref[0, 3:4, :],
                                 mod_ref[0, 4:5, :]).astype(BF16)
    chunks = _row_chunks(x_ref.shape[1], PROJ_ROW_CHUNK)
    xn_next = norm(chunks[0])
    for n, rows in enumerate(chunks):
        xn = xn_next
        if n + 1 < len(chunks):
            xn_next = norm(chunks[n + 1])
        gb_ref[0, rows, :] = _dot(xn, wgb_ref[...]).astype(BF16)
        z_ref[0, rows, :] = (_dot(xn, wgc_ref[...]) * _dot(xn, wu_ref[...])).astype(BF16)
        vt_ref[0, :, rows] = _dot_t(wvt_ref[...], xn).astype(BF16)
        cos = cos_ref[0, rows, :]
        sin = sin_ref[0, rows, :]
        q = _dot(xn, wq_ref[...])
        for h in range(q.shape[1] // HEAD_DIM):
            sl = slice(h * HEAD_DIM, (h + 1) * HEAD_DIM)
            q_ref[0, rows, sl] = (_rope(_rms(q[:, sl], qg_ref[...]), cos, sin) * q_scale).astype(BF16)
        k = _dot(xn, wk_ref[...])
        for h in range(k.shape[1] // HEAD_DIM):
            sl = slice(h * HEAD_DIM, (h + 1) * HEAD_DIM)
            k_ref[0, rows, sl] = _rope(_rms(k[:, sl], kg_ref[...]), cos, sin).astype(BF16)


def _hyb_in_call(x, mod, g, wgb, wgc, wu, wq, wk, wvt, qg, kg, cos2, sin2):
    nb, t, d = x.shape
    tm = min(512, t)
    cc, nq, nkv = wgb.shape[1], wq.shape[1], wk.shape[1]
    row = lambda n: pl.BlockSpec((1, tm, n), lambda b, i: (b, i, 0))
    table = pl.BlockSpec((1, tm, LANES), lambda b, i: (jnp.where(b == nb - 1, 1, 0), i, 0))
    return pl.pallas_call(
        _hyb_in_kernel,
        grid=(nb, t // tm),
        in_specs=[
            row(d),
            pl.BlockSpec((1, N_MOD, d), lambda b, i: (b, 0, 0)),
            _resident((1, d)),
            _resident(wgb.shape), _resident(wgc.shape), _resident(wu.shape),
            _resident(wq.shape), _resident(wk.shape), _resident(wvt.shape),
            _resident((1, HEAD_DIM)), _resident((1, HEAD_DIM)),
            table, table,
        ],
        out_specs=[row(cc), row(cc), row(nq), row(nkv),
                   pl.BlockSpec((1, nkv, tm), lambda b, i: (b, 0, i))],
        out_shape=[jax.ShapeDtypeStruct((nb, t, n), BF16) for n in (cc, cc, nq, nkv)]
        + [jax.ShapeDtypeStruct((nb, nkv, t), BF16)],
        compiler_params=_cparams("parallel", "parallel"),
        name="hyb_in_proj",
    )(x, mod, g.reshape(1, d), wgb, wgc, wu, wq, wk, wvt,
      qg.reshape(1, HEAD_DIM), kg.reshape(1, HEAD_DIM), cos2, sin2)


def _softmax_pv(s_list, kv_list):
    m = s_list[0].max(axis=-1, keepdims=True)
    for s in s_list[1:]:
        m = jnp.maximum(m, s.max(axis=-1, keepdims=True))
    l = 0.0
    o = 0.0
    for s, (_, v) in zip(s_list, kv_list):
        p = jnp.exp2(s - m)
        l = l + p.sum(axis=-1, keepdims=True)
        o = o + _dot(p.astype(BF16), v)
    return o / l


def _attend_chains(chains, kv_list):
    scores = lambda load_q: [_dot_t(load_q(), k) for k, _ in kv_list]
    ahead = [scores(load_q) for load_q, _ in chains[:ATTN_SCORE_LOOKAHEAD]]
    for n, (_, store_o) in enumerate(chains):
        s_list = ahead.pop(0)
        if n + ATTN_SCORE_LOOKAHEAD < len(chains):
            ahead.append(scores(chains[n + ATTN_SCORE_LOOKAHEAD][0]))
        store_o(_softmax_pv(s_list, kv_list).astype(BF16))


def _softmax_pv_t(st_list, kvt_list):
    m = st_list[0].max(axis=0, keepdims=True)
    for s in st_list[1:]:
        m = jnp.maximum(m, s.max(axis=0, keepdims=True))
    l = 0.0
    o = 0.0
    for s, (_, vt) in zip(st_list, kvt_list):
        p = jnp.exp2(s - m)
        l = l + p.sum(axis=0, keepdims=True)
        o = o + _dot(vt, p.astype(BF16))
    return (o / l).T


def _attend_chains_t(chains, kvt_list):
    scores = lambda load_q: [_dot_t(k, load_q()) for k, _ in kvt_list]
    ahead = [scores(load_q) for load_q, _ in chains[:ATTN_SCORE_LOOKAHEAD]]
    for n, (_, store_o) in enumerate(chains):
        st_list = ahead.pop(0)
        if n + ATTN_SCORE_LOOKAHEAD < len(chains):
            ahead.append(scores(chains[n + ATTN_SCORE_LOOKAHEAD][0]))
        store_o(_softmax_pv_t(st_list, kvt_list).astype(BF16))


def _row_chunks(n, c):
    c = min(c, n)
    return [slice(r * c, (r + 1) * c) for r in range(n // c)]


def _gqa_kernel(*refs, has_lat):
    if has_lat:
        q_ref, kc_ref, vtc_ref, kl_ref, vtl_ref, o_ref = refs
    else:
        q_ref, kc_ref, vtc_ref, o_ref = refs
    kvt_list = [(kc_ref[0], vtc_ref[0])]
    if has_lat:
        kvt_list.append((kl_ref[0], vtl_ref[0]))

    def chain(rows, sl):
        def store(o):
            o_ref[0, rows, sl] = o
        return (lambda: q_ref[0, rows, sl]), store

    _attend_chains_t([chain(rows, slice(h * HEAD_DIM, (h + 1) * HEAD_DIM))
                      for h in range(q_ref.shape[2] // HEAD_DIM)
                      for rows in _row_chunks(q_ref.shape[1], GQA_ROW_CHUNK)], kvt_list)


def _gqa_latent_call(q, k, vt, b_lat, tc):
    nb, t, nq = q.shape
    gw = nq // GQA_KV_HEADS
    tq = min(1024, t)
    ctx = pl.BlockSpec((1, tc, HEAD_DIM), lambda b, g, i: (b_lat, b, g))
    lat = pl.BlockSpec((1, t, HEAD_DIM), lambda b, g, i: (b, 0, g))
    vt_ctx = pl.BlockSpec((1, HEAD_DIM, tc), lambda b, g, i: (b_lat, g, b))
    vt_lat = pl.BlockSpec((1, HEAD_DIM, t), lambda b, g, i: (b, g, 0))
    qo = pl.BlockSpec((1, tq, gw), lambda b, g, i: (b, i, g))
    return pl.pallas_call(
        functools.partial(_gqa_kernel, has_lat=True),
        grid=(b_lat, GQA_KV_HEADS, t // tq),
        in_specs=[qo, ctx, vt_ctx, lat, vt_lat],
        out_specs=qo,
        out_shape=jax.ShapeDtypeStruct((b_lat, t, nq), BF16),
        compiler_params=_cparams("parallel", "parallel", "parallel"),
        name="gqa_latent",
    )(q, k, vt, k, vt)


def _gqa_ctx_call(q, k, vt, b_lat, tc):
    nb, t, nq = q.shape
    gw = nq // GQA_KV_HEADS
    ctx = pl.BlockSpec((1, tc, HEAD_DIM), lambda b, g: (b_lat, b, g))
    vt_ctx = pl.BlockSpec((1, HEAD_DIM, tc), lambda b, g: (b_lat, g, b))
    return pl.pallas_call(
        functools.partial(_gqa_kernel, has_lat=False),
        grid=(t // tc, GQA_KV_HEADS),
        in_specs=[pl.BlockSpec((1, tc, gw), lambda b, g: (b_lat, b, g)), ctx, vt_ctx],
        out_specs=pl.BlockSpec((1, tc, gw), lambda b, g: (0, b, g)),
        out_shape=jax.ShapeDtypeStruct((1, t, nq), BF16),
        compiler_params=_cparams("parallel", "parallel"),
        name="gqa_context",
    )(q, k, vt)


def _hyb_out_kernel(x_ref, mod_ref, gb_ref, z_ref, zp_ref, zn_ref, att_ref, attc_ref, cw_ref, w_ref,
                    o_ref, *, seq_lat, seq_ctx, b_lat):
    b, i = pl.program_id(0), pl.program_id(1)
    tm, cc = z_ref.shape[1], z_ref.shape[2]
    acc = _dot(jnp.where(b == b_lat, attc_ref[0], att_ref[0]), w_ref[cc:, :])
    z = z_ref[0].astype(F32)
    rows = lax.broadcasted_iota(jnp.int32, (tm, cc), 0)
    seq = jnp.where(b == b_lat, seq_ctx, seq_lat)
    pos = (i * tm + rows) & (seq - 1)
    z_prev = jnp.where(rows == 0, zp_ref[0, 7:8, :].astype(F32), pltpu.roll(z, 1, axis=0))
    z_next = jnp.where(rows == tm - 1, zn_ref[0, 0:1, :].astype(F32),
                       pltpu.roll(z, tm - 1, axis=0))
    z_prev = jnp.where(pos == 0, 0.0, z_prev)
    z_next = jnp.where(pos == seq - 1, 0.0, z_next)
    conv = z_prev * cw_ref[0:1, :] + z * cw_ref[1:2, :] + z_next * cw_ref[2:3, :]
    acc = acc + _dot((gb_ref[0].astype(F32) * conv).astype(BF16), w_ref[:cc, :])
    o_ref[0] = x_ref[0] + mod_ref[0, 5:6, :] * acc


def _hyb_out_call(x, mod, gb, z, att, att_c, conv_w, w_out, b_lat, tc):
    nb, t, d = x.shape
    cc = gb.shape[2]
    tm = min(512, t)
    nt = t // tm
    hb = tm // 8
    row = lambda n: pl.BlockSpec((1, tm, n), lambda b, i: (b, i, 0))
    nq = att.shape[2]
    att_spec = pl.BlockSpec((1, tm, nq), lambda b, i: (jnp.minimum(b, b_lat - 1),
                                                       jnp.where(b < b_lat, i, nt - 1), 0))
    attc_spec = pl.BlockSpec((1, tm, nq), lambda b, i: (0, jnp.where(b < b_lat, 0, i), 0))
    return pl.pallas_call(
        functools.partial(_hyb_out_kernel, seq_lat=t, seq_ctx=tc, b_lat=b_lat),
        grid=(nb, t // tm),
        in_specs=[
            row(d),
            pl.BlockSpec((1, N_MOD, d), lambda b, i: (b, 0, 0)),
            row(cc), row(cc),
            pl.BlockSpec((1, 8, cc), lambda b, i: (b, jnp.maximum(i * hb - 1, 0), 0)),
            pl.BlockSpec((1, 8, cc), lambda b, i: (b, jnp.minimum((i + 1) * hb, t // 8 - 1), 0)),
            att_spec, attc_spec,
            _resident(conv_w.shape),
            _resident(w_out.shape),
        ],
        out_specs=row(d),
        out_shape=jax.ShapeDtypeStruct((nb, t, d), F32),
        compiler_params=_cparams("parallel", "parallel"),
        name="hyb_out_proj",
    )(x, mod, gb, z, z, z, att, att_c, conv_w, w_out)


def _mla_in_kernel(x_ref, mod_ref, g_ref, wd_ref, qg_ref, kvg_ref, wuq_ref, wuk_ref, wuvt_ref,
                   cos_ref, sin_ref, qn_ref, qr_ref, kn_ref, vt_ref, kr_ref, *, q_rank, kv_rank):
    n_nope = qn_ref.shape[2]
    q_scale = (MLA_NOPE + MLA_ROPE) ** -0.5 * LOG2E
    down = lambda rows: _dot(_modnorm(x_ref[0, rows, :], g_ref[...], mod_ref[0, 3:4, :],
                                      mod_ref[0, 4:5, :]).astype(BF16), wd_ref[...])
    chunks = _row_chunks(x_ref.shape[1], PROJ_ROW_CHUNK)
    d_next = down(chunks[0])
    for n, rows in enumerate(chunks):
        d = d_next
        if n + 1 < len(chunks):
            d_next = down(chunks[n + 1])
        cos = cos_ref[0, rows, :]
        sin = sin_ref[0, rows, :]
        q = _dot(_rms(d[:, :q_rank], qg_ref[...]).astype(BF16), wuq_ref[...])
        qn_ref[0, rows, :] = (q[:, :n_nope] * q_scale).astype(BF16)
        for j in range(qr_ref.shape[2] // LANES):
            qr_ref[0, rows, j * LANES:(j + 1) * LANES] = (
                _rope(q[:, n_nope + j * LANES:n_nope + (j + 1) * LANES], cos, sin) * q_scale
            ).astype(BF16)
        ckv = _rms(d[:, q_rank:q_rank + kv_rank], kvg_ref[...]).astype(BF16)
        kn_ref[0, rows, :] = _dot(ckv, wuk_ref[...]).astype(BF16)
        vt_ref[0, :, rows] = _dot_t(wuvt_ref[...], ckv).astype(BF16)
        kr = _rope(d[:, q_rank + kv_rank:], cos, sin)
        first = (lax.broadcasted_iota(jnp.int32, kr.shape, 1) & (MLA_ROPE // 2)) == 0
        kr_ref[0, rows, :LANES] = jnp.where(first, kr, 0.0).astype(BF16)
        kr_ref[0, rows, LANES:] = jnp.where(first, 0.0, kr).astype(BF16)


def _mla_in_call(x, mod, g, wd, qg, kvg, wuq, wuk, wuvt, cos2, sin2):
    nb, t, d = x.shape
    tm = min(512, t)
    q_rank, kv_rank = qg.shape[0], kvg.shape[0]
    n_nope = wuk.shape[1]
    n_rope = wuq.shape[1] - n_nope
    n_v = wuvt.shape[0]
    row = lambda n: pl.BlockSpec((1, tm, n), lambda b, i: (b, i, 0))
    table = pl.BlockSpec((1, tm, LANES), lambda b, i: (jnp.where(b == nb - 1, 1, 0), i, 0))
    out_specs = [row(n_nope), row(n_rope), row(n_nope),
                 pl.BlockSpec((1, n_v, tm), lambda b, i: (b, 0, i)), row(2 * LANES)]
    out_shapes = [(nb, t, n_nope), (nb, t, n_rope), (nb, t, n_nope), (nb, n_v, t), (nb, t, 2 * LANES)]
    return pl.pallas_call(
        functools.partial(_mla_in_kernel, q_rank=q_rank, kv_rank=kv_rank),
        grid=(nb, t // tm),
        in_specs=[
            row(d),
            pl.BlockSpec((1, N_MOD, d), lambda b, i: (b, 0, 0)),
            _resident((1, d)),
            _resident(wd.shape), _resident((1, q_rank)), _resident((1, kv_rank)),
            _resident(wuq.shape), _resident(wuk.shape), _resident(wuvt.shape),
            table, table,
        ],
        out_specs=out_specs,
        out_shape=[jax.ShapeDtypeStruct(s, BF16) for s in out_shapes],
        compiler_params=_cparams("parallel", "parallel"),
        name="mla_in_proj",
    )(x, mod, g.reshape(1, d), wd, qg.reshape(1, q_rank), kvg.reshape(1, kv_rank), wuq, wuk, wuvt,
      cos2, sin2)


def _mla_attn_kernel(qn_ref, qr_ref, knc_ref, krc_ref, vtc_ref, knl_ref, krl_ref, vtl_ref,
                     o_ref, k_ref, vt_ref):
    tc = knc_ref.shape[1]

    @pl.when(pl.program_id(2) == 0)
    def _():
        k_ref[:tc, :LANES] = knc_ref[0]
        k_ref[:tc, LANES:] = krc_ref[0]
        k_ref[tc:, :LANES] = knl_ref[0]
        k_ref[tc:, LANES:] = krl_ref[0]
        vt_ref[:, :tc] = vtc_ref[0]
        vt_ref[:, tc:] = vtl_ref[0]

    def chain(rows):
        def store(o):
            o_ref[0, rows, :] = o
        return (lambda: jnp.concatenate([qn_ref[0, rows, :], qr_ref[0, rows, :]], axis=1)), store

    _attend_chains_t([chain(rows) for rows in _row_chunks(qn_ref.shape[1], MLA_ROW_CHUNK)],
                     [(k_ref[...], vt_ref[...])])


def _mla_attn_call(qn, qr, kn, vt, kr, b_lat, tc):
    nb, t, n_nope = qn.shape
    heads = n_nope // MLA_NOPE
    tq = min(2048, t)
    qspec = pl.BlockSpec((1, tq, LANES), lambda b, h, i: (b, i, h))
    qrspec = pl.BlockSpec((1, tq, LANES), lambda b, h, i: (b, i, h // 2))
    ctx = lambda f: pl.BlockSpec((1, tc, LANES), lambda b, h, i: (b_lat, b, f(h)))
    lat = lambda f: pl.BlockSpec((1, t, LANES), lambda b, h, i: (b, 0, f(h)))
    vt_ctx = pl.BlockSpec((1, MLA_V, tc), lambda b, h, i: (b_lat, h, b))
    vt_lat = pl.BlockSpec((1, MLA_V, t), lambda b, h, i: (b, h, 0))
    same = lambda h: h
    parity = lambda h: h % 2
    return pl.pallas_call(
        _mla_attn_kernel,
        grid=(b_lat, heads, t // tq),
        in_specs=[qspec, qrspec, ctx(same), ctx(parity), vt_ctx, lat(same), lat(parity), vt_lat],
        out_specs=qspec,
        out_shape=jax.ShapeDtypeStruct((b_lat, t, n_nope), BF16),
        scratch_shapes=[pltpu.VMEM((tc + t, 2 * LANES), BF16), pltpu.VMEM((MLA_V, tc + t), BF16)],
        compiler_params=_cparams("parallel", "parallel", "arbitrary"),
        name="mla_attention",
    )(qn, qr, kn, kr, vt, kn, kr, vt)


def _mla_out_kernel(x_ref, mod_ref, att_ref, w_ref, o_ref):
    o_ref[0] = x_ref[0] + mod_ref[0, 5:6, :] * _dot(att_ref[0], w_ref[...])


def _mla_out_call(x, mod, att, w_o):
    nb, t, k = att.shape
    d = w_o.shape[1]
    tm = min(512, t)
    row = lambda n: pl.BlockSpec((1, tm, n), lambda b, i: (b, i, 0))
    return pl.pallas_call(
        _mla_out_kernel,
        grid=(nb, t // tm),
        in_specs=[row(d), pl.BlockSpec((1, N_MOD, d), lambda b, i: (b, 0, 0)), row(k),
                  _resident(w_o.shape)],
        out_specs=row(d),
        out_shape=jax.ShapeDtypeStruct((nb, t, d), F32),
        compiler_params=_cparams("parallel", "parallel"),
        name="mla_out_proj",
    )(x, mod, att, w_o)


def _rope_angles(n_tok, dim):
    n_rows = n_tok // GRID_W
    row = jnp.repeat(jnp.arange(n_rows), GRID_W).astype(F32)
    col = jnp.tile(jnp.arange(GRID_W), n_rows).astype(F32)
    half = dim // 2
    inv = 1.0 / (ROPE_THETA ** (jnp.arange(0, half, 2, dtype=F32) / half))
    return jnp.concatenate([row[:, None] * inv, col[:, None] * inv], axis=-1)


def _rope_tables(n_tok, dim):
    ang = _rope_angles(n_tok, dim)
    reps = LANES // dim
    cos = jnp.concatenate([jnp.cos(ang)] * (2 * reps), axis=-1)
    sin = jnp.concatenate([-jnp.sin(ang)] * reps + [jnp.sin(ang)] * reps, axis=-1)
    return (jnp.stack([cos, jnp.ones_like(cos)]), jnp.stack([sin, jnp.zeros_like(sin)]))


def _deinterleave(n):
    return np.concatenate([np.arange(0, n, 2), np.arange(1, n, 2)])


def kernel(x, c, ctx, c_ctx, mod_w, mod_b, norm_ffn1, norm_mix, norm_ffn2,
           ffn1_w_gate, ffn1_w_up, ffn1_w_down, ffn2_w_gate, ffn2_w_up, ffn2_w_down,
           hyb_w_in, hyb_conv_w, hyb_q_norm, hyb_k_norm, hyb_w_out,
           mla_w_down, mla_q_norm, mla_kv_norm, mla_w_uq, mla_w_ukv, mla_w_o, final_norm):
    b_lat, t, d = x.shape
    tc = ctx.shape[1]
    depth = mod_w.shape[0]
    assert depth == 2 and ctx.shape[0] * tc == t and b_lat + 1 <= MOD_ROWS
    assert t & (t - 1) == 0 and tc & (tc - 1) == 0 and t % GRID_W == 0
    nb = b_lat + 1

    cvec = jnp.concatenate([c, c_ctx[None], jnp.zeros((MOD_ROWS - nb, d), F32)], axis=0)
    mods = _mod_call(cvec, mod_w, mod_b).reshape(depth, MOD_ROWS, N_MOD, d)
    cast = lambda w: w.astype(BF16)
    f1 = _ffn_weights(ffn1_w_gate, ffn1_w_up, ffn1_w_down)
    f2 = _ffn_weights(ffn2_w_gate, ffn2_w_up, ffn2_w_down)

    mod = mods[0]
    xs = _ffn_call(x, nb, mod, norm_ffn1[0], *f1, 0, 0, ctx=ctx.reshape(1, t, d))
    cc = d // 2
    n_q = cc
    n_kv = GQA_KV_HEADS * HEAD_DIM
    w_in = hyb_w_in[0]
    perm = _deinterleave(HEAD_DIM)
    pairs_last = lambda w: w.reshape(d, -1, HEAD_DIM // 2, 2).swapaxes(2, 3).reshape(d, -1)
    wq = pairs_last(w_in[:, 3 * cc:3 * cc + n_q])
    wk = pairs_last(w_in[:, 3 * cc + n_q:3 * cc + n_q + n_kv])
    wvt = w_in[:, 3 * cc + n_q + n_kv:].T
    cos_a, sin_a = _rope_tables(t, HEAD_DIM)
    gb, z, q, k, v = _hyb_in_call(
        xs, mod, norm_mix[0], cast(w_in[:, :cc]), cast(w_in[:, cc:2 * cc]), cast(w_in[:, 2 * cc:3 * cc]),
        cast(wq), cast(wk), cast(wvt), hyb_q_norm[0][perm], hyb_k_norm[0][perm], cos_a, sin_a)
    att = _gqa_latent_call(q, k, v, b_lat, tc)
    att_c = _gqa_ctx_call(q, k, v, b_lat, tc)
    xs = _hyb_out_call(xs, mod, gb, z, att, att_c, hyb_conv_w[0], cast(hyb_w_out[0]), b_lat, tc)
    xs = _ffn_call(xs, nb, mod, norm_ffn2[0], *f2, 0, 6)

    mod = mods[1]
    xs = _ffn_call(xs, nb, mod, norm_ffn1[1], *f1, 1, 0)
    heads = d // HEAD_DIM
    q_rank, kv_rank = mla_q_norm.shape[1], mla_kv_norm.shape[1]
    uq = mla_w_uq[0].reshape(q_rank, heads, MLA_NOPE + MLA_ROPE)
    uq_rope = uq[:, :, MLA_NOPE:].reshape(q_rank, heads // 2, 2, MLA_ROPE // 2, 2)
    wuq = jnp.concatenate([uq[:, :, :MLA_NOPE].reshape(q_rank, -1),
                           uq_rope.transpose(0, 1, 4, 2, 3).reshape(q_rank, -1)], axis=1)
    ukv = mla_w_ukv[0].reshape(kv_rank, heads, MLA_NOPE + MLA_V)
    wuk = ukv[:, :, :MLA_NOPE].reshape(kv_rank, -1)
    wuvt = ukv[:, :, MLA_NOPE:].reshape(kv_rank, -1).T
    kr0 = q_rank + kv_rank
    w_kr = mla_w_down[0][:, kr0:].reshape(d, MLA_ROPE // 2, 2)
    wdn = jnp.concatenate([mla_w_down[0][:, :kr0], w_kr[:, :, 0], w_kr[:, :, 0],
                           w_kr[:, :, 1], w_kr[:, :, 1]], axis=1)
    cos_m, sin_m = _rope_tables(t, MLA_ROPE)
    qn, qr, kn, vt, kr = _mla_in_call(xs, mod, norm_mix[1], cast(wdn), mla_q_norm[0], mla_kv_norm[0],
                                      cast(wuq), cast(wuk), cast(wuvt), cos_m, sin_m)
    att = _mla_attn_call(qn, qr, kn, vt, kr, b_lat, tc)
    xs = _mla_out_call(xs, mod, att, cast(mla_w_o[0]))
    return _ffn_call(xs, b_lat, mod, norm_ffn2[1], *f2, 1, 6, final_g=final_norm)
```

```python
import functools

import numpy as np
import jax
import jax.numpy as jnp
from jax import lax
from jax.experimental import pallas as pl
from jax.experimental.pallas import tpu as pltpu

HEAD_DIM = 128
GRID_W = 64
ROPE_THETA = 10000.0
EPS = 1e-6
N_MOD = 9
GQA_KV_HEADS = 2
MLA_NOPE = 128
MLA_ROPE = 64
MLA_V = 128
LANES = 128
VMEM_LIMIT = 60 * 1024 * 1024
MOD_ROWS = 16
FFN_ROW_CHUNKS = 4
FFN_TILE = 512
FFN_SLOTS = 4
PROJ_ROW_CHUNK = 256
GQA_ROW_CHUNK = 512
MLA_ROW_CHUNK = 512
ATTN_SCORE_LOOKAHEAD = 2
LOG2E = 1.4426950408889634
CAST_BLOCK_BYTES = 4 * 1024 * 1024

F32 = jnp.float32
BF16 = jnp.bfloat16


def _cparams(*sem):
    return pltpu.CompilerParams(dimension_semantics=sem, vmem_limit_bytes=VMEM_LIMIT)


def _resident(shape):
    nd = len(shape)
    return pl.BlockSpec(shape, lambda *_: (0,) * nd, pipeline_mode=pl.Buffered(1))


def _rms(x, g):
    return x * lax.rsqrt(jnp.mean(x * x, axis=-1, keepdims=True) + EPS) * g


def _modnorm(x, g, shift, scale):
    return _rms(x, g) * (1.0 + scale) + shift


def _rope(x, cos, sin):
    return x * cos + pltpu.roll(x, LANES // 2, axis=1) * sin


def _dot(a, b):
    return jnp.dot(a, b, preferred_element_type=F32)


def _dot_t(a, b):
    return lax.dot_general(a, b, (((1,), (1,)), ((), ())), preferred_element_type=F32)


def _mod_kernel(c_ref, w_ref, b_ref, o_ref):
    c = c_ref[...]
    s = (c * jax.nn.sigmoid(c)).astype(BF16)
    o_ref[0] = _dot(s, w_ref[0].astype(BF16)) + b_ref[0]


def _mod_call(cvec, mod_w, mod_b):
    depth, d, n = mod_w.shape
    tn = min(1024, n)
    return pl.pallas_call(
        _mod_kernel,
        grid=(depth, n // tn),
        in_specs=[
            pl.BlockSpec((MOD_ROWS, d), lambda l, j: (0, 0)),
            pl.BlockSpec((1, d, tn), lambda l, j: (l, 0, j)),
            pl.BlockSpec((1, 1, tn), lambda l, j: (l, 0, j)),
        ],
        out_specs=pl.BlockSpec((1, MOD_ROWS, tn), lambda l, j: (l, 0, j)),
        out_shape=jax.ShapeDtypeStruct((depth, MOD_ROWS, n), F32),
        compiler_params=_cparams("parallel", "parallel"),
        name="adaln_mod",
    )(cvec, mod_w, mod_b.reshape(depth, 1, n))


def _swiglu_act(xn, wg_ref, wu_ref):
    hg = _dot(xn, wg_ref[...])
    hu = _dot(xn, wu_ref[...])
    return (hg * jax.nn.sigmoid(hg) * hu).astype(BF16)


_GATE_UP, _DOWN = "gate_up", "down"


def _ring_plan(n_ff):
    assert FFN_SLOTS >= 4
    plan = {}
    for j in range(0, n_ff, 2):
        waits = ([(_GATE_UP, t) for t in (j + 1, j + 2) if t < n_ff]
                 + [(_DOWN, t) for t in (j, j + 1) if t < n_ff])
        starts = ([(_GATE_UP, t) for t in (j + 3, j + 4) if t < n_ff]
                  + [(_DOWN, t) for t in (j + 2, j + 3) if t < n_ff])
        plan[j] = (waits, starts, [])
    first = [(_GATE_UP, t) for t in range(min(3, n_ff))] + [(_DOWN, t) for t in range(min(2, n_ff))]
    for kind, tile in first:
        last_user = max(t for t in range(n_ff) if t % FFN_SLOTS == tile % FFN_SLOTS)
        free_from = last_user if kind == _GATE_UP else last_user + 1
        barrier = free_from + free_from % 2
        assert barrier in plan, "no barrier left to prefetch the next grid step's tile"
        plan[barrier][2].append((kind, tile))
    return plan, first


def _ffn_kernel(x_ref, mod_ref, g_ref, wg_hbm, wu_hbm, wd_hbm, *rest, layer, m0, final, with_ctx):
    rest = list(rest)
    fn_ref = rest.pop(0) if final else None
    ctx_ref = rest.pop(0) if with_ctx else None
    o_ref, xn_ref, h_ref, wg_buf, wu_buf, wd_buf, sem = rest
    is_ctx = pl.program_id(0) == pl.num_programs(0) - 1
    n_ff = wg_hbm.shape[1]
    tm = xn_ref.shape[0]
    step = pl.program_id(0) * pl.num_programs(1) + pl.program_id(1)
    not_last_step = step < pl.num_programs(0) * pl.num_programs(1) - 1
    half_gate = 0.5 * mod_ref[0, m0 + 2:m0 + 3, :]

    def copies(kind, blk):
        slot = blk % FFN_SLOTS
        pairs = (((0, wg_hbm, wg_buf), (1, wu_hbm, wu_buf)) if kind == _GATE_UP
                 else ((2, wd_hbm, wd_buf),))
        return [pltpu.make_async_copy(hbm.at[layer, blk], buf.at[slot], sem.at[k, slot])
                for k, hbm, buf in pairs]

    def start(kind, blk):
        for c in copies(kind, blk):
            c.start()

    def wait(kind, blk):
        for c in copies(kind, blk):
            c.wait()

    def x_rows(rows):
        x = x_ref[0, rows, :]
        return jnp.where(is_ctx, ctx_ref[0, rows, :], x) if with_ctx else x

    plan, first = _ring_plan(n_ff)

    @pl.when(step == 0)
    def _():
        for item in first:
            start(*item)

    wait(_GATE_UP, 0)
    rs = tm // FFN_ROW_CHUNKS
    for r in range(FFN_ROW_CHUNKS):
        rows = slice(r * rs, (r + 1) * rs)
        xn = _modnorm(x_rows(rows), g_ref[...], mod_ref[0, m0:m0 + 1, :],
                      mod_ref[0, m0 + 1:m0 + 2, :]).astype(BF16)
        xn_ref[rows, :] = xn
        h_ref[0, rows, :] = _swiglu_act(xn, wg_buf.at[0], wu_buf.at[0])

    for j in range(n_ff):
        if j in plan:
            waits, starts, next_starts = plan[j]
            for item in starts:
                start(*item)
            if next_starts:
                pl.when(not_last_step)(
                    lambda items=next_starts: [start(*item) for item in items] and None)
            for item in waits:
                wait(*item)
        if j + 1 < n_ff:
            nslot = (j + 1) % FFN_SLOTS
            h_ref[(j + 1) % 2] = _swiglu_act(xn_ref[...], wg_buf.at[nslot], wu_buf.at[nslot])
        contrib = half_gate * _dot(h_ref[j % 2], wd_buf[j % FFN_SLOTS])
        if j == 0:
            o_ref[0] = x_rows(slice(None)) + contrib
        else:
            o_ref[0] += contrib

    if final:
        o_ref[0] = _rms(o_ref[0], fn_ref[...])


def _ffn_call(x, nb, mod, g, wg, wu, wd, layer, m0, final_g=None, ctx=None):
    _, t, d = x.shape
    _, n_ff, _, tf = wg.shape
    _ring_plan(n_ff)
    final = final_g is not None
    tm = min(512, t)
    nt = t // tm
    if ctx is None:
        x_spec = pl.BlockSpec((1, tm, d), lambda b, i: (b, i, 0))
    else:
        x_spec = pl.BlockSpec((1, tm, d), lambda b, i: (jnp.minimum(b, nb - 2),
                                                         jnp.where(b < nb - 1, i, nt - 1), 0))
    in_specs = [
        x_spec,
        pl.BlockSpec((1, N_MOD, d), lambda b, i: (b, 0, 0)),
        pl.BlockSpec((1, d), lambda b, i: (0, 0)),
        pl.BlockSpec(memory_space=pl.ANY),
        pl.BlockSpec(memory_space=pl.ANY),
        pl.BlockSpec(memory_space=pl.ANY),
    ]
    args = [x, mod, g.reshape(1, d), wg, wu, wd]
    if final:
        in_specs.append(pl.BlockSpec((1, d), lambda b, i: (0, 0)))
        args.append(final_g.reshape(1, d))
    if ctx is not None:
        in_specs.append(pl.BlockSpec((1, tm, d), lambda b, i: (0, jnp.where(b < nb - 1, 0, i), 0)))
        args.append(ctx)
    return pl.pallas_call(
        functools.partial(_ffn_kernel, layer=layer, m0=m0, final=final, with_ctx=ctx is not None),
        grid=(nb, nt),
        in_specs=in_specs,
        out_specs=pl.BlockSpec((1, tm, d), lambda b, i: (b, i, 0)),
        out_shape=jax.ShapeDtypeStruct((nb, t, d), F32),
        scratch_shapes=[
            pltpu.VMEM((tm, d), BF16),
            pltpu.VMEM((2, tm, tf), BF16),
            pltpu.VMEM((FFN_SLOTS, d, tf), BF16), pltpu.VMEM((FFN_SLOTS, d, tf), BF16),
            pltpu.VMEM((FFN_SLOTS, tf, d), BF16),
            pltpu.SemaphoreType.DMA((3, FFN_SLOTS)),
        ],
        compiler_params=_cparams("arbitrary", "arbitrary"),
        name="swiglu_half_step",
    )(*args)


def _cast_kernel(w_ref, o_ref):
    cb = o_ref.shape[3]
    for n in range(o_ref.shape[1]):
        o_ref[0, n] = w_ref[0, :, n * cb:(n + 1) * cb].astype(BF16)


def _cast_call(w, layer, col_block):
    _, r, c = w.shape
    nblk = c // col_block
    tr = max(n for n in range(16, r + 1, 16) if r % n == 0 and 4 * c * n <= CAST_BLOCK_BYTES)
    return pl.pallas_call(
        _cast_kernel,
        grid=(r // tr,),
        in_specs=[pl.BlockSpec((1, tr, c), lambda i: (layer, i, 0))],
        out_specs=pl.BlockSpec((1, nblk, tr, col_block), lambda i: (0, 0, i, 0)),
        out_shape=jax.ShapeDtypeStruct((1, nblk, r, col_block), BF16),
        compiler_params=_cparams("parallel"),
        name="weight_cast",
    )(w)


def _call_with_casts(kernel_fn, riders, *, grid, in_specs, out_specs, out_shape, **kw):
    n_in, n_out, n_r = len(in_specs), len(out_specs), len(riders)
    steps = int(np.prod(grid))
    r_in, r_out, r_shape = [], [], []
    for w, layer, cb in riders:
        _, r, c = w.shape
        tr = min(n for n in range(16, r + 1, 16) if r % n == 0 and r // n <= steps)

        def row_block(*g, n_rows=r // tr):
            lin = g[0]
            for gi, n in zip(g[1:], grid[1:]):
                lin = lin * n + gi
            return jnp.minimum(lin, n_rows - 1)

        r_in.append(pl.BlockSpec((1, tr, c), lambda *g, f=row_block, l=layer: (l, f(*g), 0)))
        r_out.append(pl.BlockSpec((1, c // cb, tr, cb), lambda *g, f=row_block: (0, 0, f(*g), 0)))
        r_shape.append(jax.ShapeDtypeStruct((1, c // cb, r, cb), BF16))

    def body(*refs):
        outs_at = n_in + n_r
        for w_ref, o_ref in zip(refs[n_in:outs_at], refs[outs_at + n_out:outs_at + n_out + n_r]):
            _cast_kernel(w_ref, o_ref)
        kernel_fn(*refs[:n_in], *refs[outs_at:outs_at + n_out], *refs[outs_at + n_out + n_r:])

    if riders:
        kw["compiler_params"] = _cparams(*["arbitrary"] * len(grid))
    call = pl.pallas_call(body, grid=grid, in_specs=list(in_specs) + r_in,
                          out_specs=list(out_specs) + r_out, out_shape=list(out_shape) + r_shape, **kw)

    def run(*args):
        res = call(*args, *[w for w, _, _ in riders])
        return list(res[:n_out]), list(res[n_out:])
    return run


def _as_ffn_weights(gate, up, down):
    _, _, dff, d = down.shape
    tf = gate.shape[3]
    return gate, up, down.reshape(1, dff // tf, tf, d)


def _hyb_in_kernel(x_ref, mod_ref, g_ref, wgb_ref, wgc_ref, wu_ref, wq_ref, wk_ref, wvt_ref,
                   qg_ref, kg_ref, cos_ref, sin_ref,
                   gb_ref, z_ref, q_ref, k_ref, vt_ref):
    q_scale = HEAD_DIM ** -0.5 * LOG2E
    norm = lambda rows: _modnorm(x_ref[0, rows, :], g_ref[...], mod_ref[0, 3:4, :],
                                 mod_ref[0, 4:5, :]).astype(BF16)
    chunks = _row_chunks(x_ref.shape[1], PROJ_ROW_CHUNK)
    xn_next = norm(chunks[0])
    for n, rows in enumerate(chunks):
        xn = xn_next
        if n + 1 < len(chunks):
            xn_next = norm(chunks[n + 1])
        gb_ref[0, rows, :] = _dot(xn, wgb_ref[...]).astype(BF16)
        z_ref[0, rows, :] = (_dot(xn, wgc_ref[...]) * _dot(xn, wu_ref[...])).astype(BF16)
        vt_ref[0, :, rows] = _dot_t(wvt_ref[...], xn).astype(BF16)
        cos = cos_ref[0, rows, :]
        sin = sin_ref[0, rows, :]
        q = _dot(xn, wq_ref[...])
        for h in range(q.shape[1] // HEAD_DIM):
            sl = slice(h * HEAD_DIM, (h + 1) * HEAD_DIM)
            q_ref[0, rows, sl] = (_rope(_rms(q[:, sl], qg_ref[...]), cos, sin) * q_scale).astype(BF16)
        k = _dot(xn, wk_ref[...])
        for h in range(k.shape[1] // HEAD_DIM):
            sl = slice(h * HEAD_DIM, (h + 1) * HEAD_DIM)
            k_ref[0, rows, sl] = _rope(_rms(k[:, sl], kg_ref[...]), cos, sin).astype(BF16)


def _hyb_in_call(x, mod, g, wgb, wgc, wu, wq, wk, wvt, qg, kg, cos2, sin2, riders=()):
    nb, t, d = x.shape
    tm = min(512, t)
    cc, nq, nkv = wgb.shape[1], wq.shape[1], wk.shape[1]
    row = lambda n: pl.BlockSpec((1, tm, n), lambda b, i: (b, i, 0))
    table = pl.BlockSpec((1, tm, LANES), lambda b, i: (jnp.where(b == nb - 1, 1, 0), i, 0))
    return _call_with_casts(
        _hyb_in_kernel, riders,
        grid=(nb, t // tm),
        in_specs=[
            row(d),
            pl.BlockSpec((1, N_MOD, d), lambda b, i: (b, 0, 0)),
            _resident((1, d)),
            _resident(wgb.shape), _resident(wgc.shape), _resident(wu.shape),
            _resident(wq.shape), _resident(wk.shape), _resident(wvt.shape),
            _resident((1, HEAD_DIM)), _resident((1, HEAD_DIM)),
            table, table,
        ],
        out_specs=[row(cc), row(cc), row(nq), row(nkv),
                   pl.BlockSpec((1, nkv, tm), lambda b, i: (b, 0, i))],
        out_shape=[jax.ShapeDtypeStruct((nb, t, n), BF16) for n in (cc, cc, nq, nkv)]
        + [jax.ShapeDtypeStruct((nb, nkv, t), BF16)],
        compiler_params=_cparams("parallel", "parallel"),
        name="hyb_in_proj",
    )(x, mod, g.reshape(1, d), wgb, wgc, wu, wq, wk, wvt,
      qg.reshape(1, HEAD_DIM), kg.reshape(1, HEAD_DIM), cos2, sin2)


def _softmax_pv(s_list, kv_list):
    m = s_list[0].max(axis=-1, keepdims=True)
    for s in s_list[1:]:
        m = jnp.maximum(m, s.max(axis=-1, keepdims=True))
    l = 0.0
    o = 0.0
    for s, (_, v) in zip(s_list, kv_list):
        p = jnp.exp2(s - m)
        l = l + p.sum(axis=-1, keepdims=True)
        o = o + _dot(p.astype(BF16), v)
    return o / l


def _attend_chains(chains, kv_list):
    scores = lambda load_q: [_dot_t(load_q(), k) for k, _ in kv_list]
    ahead = [scores(load_q) for load_q, _ in chains[:ATTN_SCORE_LOOKAHEAD]]
    for n, (_, store_o) in enumerate(chains):
        s_list = ahead.pop(0)
        if n + ATTN_SCORE_LOOKAHEAD < len(chains):
            ahead.append(scores(chains[n + ATTN_SCORE_LOOKAHEAD][0]))
        store_o(_softmax_pv(s_list, kv_list).astype(BF16))


def _softmax_pv_t(st_list, kvt_list):
    m = st_list[0].max(axis=0, keepdims=True)
    for s in st_list[1:]:
        m = jnp.maximum(m, s.max(axis=0, keepdims=True))
    l = 0.0
    o = 0.0
    for s, (_, vt) in zip(st_list, kvt_list):
        p = jnp.exp2(s - m)
        l = l + p.sum(axis=0, keepdims=True)
        o = o + _dot(vt, p.astype(BF16))
    return (o / l).T


def _attend_chains_t(chains, kvt_list):
    scores = lambda load_q: [_dot_t(k, load_q()) for k, _ in kvt_list]
    ahead = [scores(load_q) for load_q, _ in chains[:ATTN_SCORE_LOOKAHEAD]]
    for n, (_, store_o) in enumerate(chains):
        st_list = ahead.pop(0)
        if n + ATTN_SCORE_LOOKAHEAD < len(chains):
            ahead.append(scores(chains[n + ATTN_SCORE_LOOKAHEAD][0]))
        store_o(_softmax_pv_t(st_list, kvt_list).astype(BF16))


def _row_chunks(n, c):
    c = min(c, n)
    return [slice(r * c, (r + 1) * c) for r in range(n // c)]


def _gqa_kernel(*refs, has_lat):
    if has_lat:
        q_ref, kc_ref, vtc_ref, kl_ref, vtl_ref, o_ref = refs
    else:
        q_ref, kc_ref, vtc_ref, o_ref = refs
    kvt_list = [(kc_ref[0], vtc_ref[0])]
    if has_lat:
        kvt_list.append((kl_ref[0], vtl_ref[0]))

    def chain(rows, sl):
        def store(o):
            o_ref[0, rows, sl] = o
        return (lambda: q_ref[0, rows, sl]), store

    _attend_chains_t([chain(rows, slice(h * HEAD_DIM, (h + 1) * HEAD_DIM))
                      for h in range(q_ref.shape[2] // HEAD_DIM)
                      for rows in _row_chunks(q_ref.shape[1], GQA_ROW_CHUNK)], kvt_list)


def _gqa_latent_call(q, k, vt, b_lat, tc, riders=()):
    nb, t, nq = q.shape
    gw = nq // GQA_KV_HEADS
    tq = min(1024, t)
    ctx = pl.BlockSpec((1, tc, HEAD_DIM), lambda b, g, i: (b_lat, b, g))
    lat = pl.BlockSpec((1, t, HEAD_DIM), lambda b, g, i: (b, 0, g))
    vt_ctx = pl.BlockSpec((1, HEAD_DIM, tc), lambda b, g, i: (b_lat, g, b))
    vt_lat = pl.BlockSpec((1, HEAD_DIM, t), lambda b, g, i: (b, g, 0))
    qo = pl.BlockSpec((1, tq, gw), lambda b, g, i: (b, i, g))
    return _call_with_casts(
        functools.partial(_gqa_kernel, has_lat=True), riders,
        grid=(b_lat, GQA_KV_HEADS, t // tq),
        in_specs=[qo, ctx, vt_ctx, lat, vt_lat],
        out_specs=[qo],
        out_shape=[jax.ShapeDtypeStruct((b_lat, t, nq), BF16)],
        compiler_params=_cparams("parallel", "parallel", "parallel"),
        name="gqa_latent",
    )(q, k, vt, k, vt)


def _gqa_ctx_call(q, k, vt, b_lat, tc):
    nb, t, nq = q.shape
    gw = nq // GQA_KV_HEADS
    ctx = pl.BlockSpec((1, tc, HEAD_DIM), lambda b, g: (b_lat, b, g))
    vt_ctx = pl.BlockSpec((1, HEAD_DIM, tc), lambda b, g: (b_lat, g, b))
    return pl.pallas_call(
        functools.partial(_gqa_kernel, has_lat=False),
        grid=(t // tc, GQA_KV_HEADS),
        in_specs=[pl.BlockSpec((1, tc, gw), lambda b, g: (b_lat, b, g)), ctx, vt_ctx],
        out_specs=pl.BlockSpec((1, tc, gw), lambda b, g: (0, b, g)),
        out_shape=jax.ShapeDtypeStruct((1, t, nq), BF16),
        compiler_params=_cparams("parallel", "parallel"),
        name="gqa_context",
    )(q, k, vt)


def _hyb_out_kernel(x_ref, mod_ref, gb_ref, z_ref, zp_ref, zn_ref, att_ref, attc_ref, cw_ref, w_ref,
                    o_ref, *, seq_lat, seq_ctx, b_lat):
    b, i = pl.program_id(0), pl.program_id(1)
    tm, cc = z_ref.shape[1], z_ref.shape[2]
    acc = _dot(jnp.where(b == b_lat, attc_ref[0], att_ref[0]), w_ref[cc:, :])
    z = z_ref[0].astype(F32)
    rows = lax.broadcasted_iota(jnp.int32, (tm, cc), 0)
    seq = jnp.where(b == b_lat, seq_ctx, seq_lat)
    pos = (i * tm + rows) & (seq - 1)
    z_prev = jnp.where(rows == 0, zp_ref[0, 7:8, :].astype(F32), pltpu.roll(z, 1, axis=0))
    z_next = jnp.where(rows == tm - 1, zn_ref[0, 0:1, :].astype(F32),
                       pltpu.roll(z, tm - 1, axis=0))
    z_prev = jnp.where(pos == 0, 0.0, z_prev)
    z_next = jnp.where(pos == seq - 1, 0.0, z_next)
    conv = z_prev * cw_ref[0:1, :] + z * cw_ref[1:2, :] + z_next * cw_ref[2:3, :]
    acc = acc + _dot((gb_ref[0].astype(F32) * conv).astype(BF16), w_ref[:cc, :])
    o_ref[0] = x_ref[0] + mod_ref[0, 5:6, :] * acc


def _hyb_out_call(x, mod, gb, z, att, att_c, conv_w, w_out, b_lat, tc, riders=()):
    nb, t, d = x.shape
    cc = gb.shape[2]
    tm = min(512, t)
    nt = t // tm
    hb = tm // 8
    row = lambda n: pl.BlockSpec((1, tm, n), lambda b, i: (b, i, 0))
    nq = att.shape[2]
    att_spec = pl.BlockSpec((1, tm, nq), lambda b, i: (jnp.minimum(b, b_lat - 1),
                                                       jnp.where(b < b_lat, i, nt - 1), 0))
    attc_spec = pl.BlockSpec((1, tm, nq), lambda b, i: (0, jnp.where(b < b_lat, 0, i), 0))
    return _call_with_casts(
        functools.partial(_hyb_out_kernel, seq_lat=t, seq_ctx=tc, b_lat=b_lat), riders,
        grid=(nb, t // tm),
        in_specs=[
            row(d),
            pl.BlockSpec((1, N_MOD, d), lambda b, i: (b, 0, 0)),
            row(cc), row(cc),
            pl.BlockSpec((1, 8, cc), lambda b, i: (b, jnp.maximum(i * hb - 1, 0), 0)),
            pl.BlockSpec((1, 8, cc), lambda b, i: (b, jnp.minimum((i + 1) * hb, t // 8 - 1), 0)),
            att_spec, attc_spec,
            _resident(conv_w.shape),
            _resident(w_out.shape),
        ],
        out_specs=[row(d)],
        out_shape=[jax.ShapeDtypeStruct((nb, t, d), F32)],
        compiler_params=_cparams("parallel", "parallel"),
        name="hyb_out_proj",
    )(x, mod, gb, z, z, z, att, att_c, conv_w, w_out)


def _mla_in_kernel(x_ref, mod_ref, g_ref, wd_ref, qg_ref, kvg_ref, wuq_ref, wuk_ref, wuvt_ref,
                   cos_ref, sin_ref, qn_ref, qr_ref, kn_ref, vt_ref, kr_ref, *, q_rank, kv_rank):
    n_nope = qn_ref.shape[2]
    q_scale = (MLA_NOPE + MLA_ROPE) ** -0.5 * LOG2E
    down = lambda rows: _dot(_modnorm(x_ref[0, rows, :], g_ref[...], mod_ref[0, 3:4, :],
                                      mod_ref[0, 4:5, :]).astype(BF16), wd_ref[...])
    chunks = _row_chunks(x_ref.shape[1], PROJ_ROW_CHUNK)
    d_next = down(chunks[0])
    for n, rows in enumerate(chunks):
        d = d_next
        if n + 1 < len(chunks):
            d_next = down(chunks[n + 1])
        cos = cos_ref[0, rows, :]
        sin = sin_ref[0, rows, :]
        q = _dot(_rms(d[:, :q_rank], qg_ref[...]).astype(BF16), wuq_ref[...])
        qn_ref[0, rows, :] = (q[:, :n_nope] * q_scale).astype(BF16)
        for j in range(qr_ref.shape[2] // LANES):
            qr_ref[0, rows, j * LANES:(j + 1) * LANES] = (
                _rope(q[:, n_nope + j * LANES:n_nope + (j + 1) * LANES], cos, sin) * q_scale
            ).astype(BF16)
        ckv = _rms(d[:, q_rank:q_rank + kv_rank], kvg_ref[...]).astype(BF16)
        kn_ref[0, rows, :] = _dot(ckv, wuk_ref[...]).astype(BF16)
        vt_ref[0, :, rows] = _dot_t(wuvt_ref[...], ckv).astype(BF16)
        kr = _rope(d[:, q_rank + kv_rank:], cos, sin)
        first = (lax.broadcasted_iota(jnp.int32, kr.shape, 1) & (MLA_ROPE // 2)) == 0
        kr_ref[0, rows, :LANES] = jnp.where(first, kr, 0.0).astype(BF16)
        kr_ref[0, rows, LANES:] = jnp.where(first, 0.0, kr).astype(BF16)


def _mla_in_call(x, mod, g, wd, qg, kvg, wuq, wuk, wuvt, cos2, sin2, riders=()):
    nb, t, d = x.shape
    tm = min(512, t)
    q_rank, kv_rank = qg.shape[0], kvg.shape[0]
    n_nope = wuk.shape[1]
    n_rope = wuq.shape[1] - n_nope
    n_v = wuvt.shape[0]
    row = lambda n: pl.BlockSpec((1, tm, n), lambda b, i: (b, i, 0))
    table = pl.BlockSpec((1, tm, LANES), lambda b, i: (jnp.where(b == nb - 1, 1, 0), i, 0))
    out_specs = [row(n_nope), row(n_rope), row(n_nope),
                 pl.BlockSpec((1, n_v, tm), lambda b, i: (b, 0, i)), row(2 * LANES)]
    out_shapes = [(nb, t, n_nope), (nb, t, n_rope), (nb, t, n_nope), (nb, n_v, t), (nb, t, 2 * LANES)]
    return _call_with_casts(
        functools.partial(_mla_in_kernel, q_rank=q_rank, kv_rank=kv_rank), riders,
        grid=(nb, t // tm),
        in_specs=[
            row(d),
            pl.BlockSpec((1, N_MOD, d), lambda b, i: (b, 0, 0)),
            _resident((1, d)),
            _resident(wd.shape), _resident((1, q_rank)), _resident((1, kv_rank)),
            _resident(wuq.shape), _resident(wuk.shape), _resident(wuvt.shape),
            table, table,
        ],
        out_specs=out_specs,
        out_shape=[jax.ShapeDtypeStruct(s, BF16) for s in out_shapes],
        compiler_params=_cparams("parallel", "parallel"),
        name="mla_in_proj",
    )(x, mod, g.reshape(1, d), wd, qg.reshape(1, q_rank), kvg.reshape(1, kv_rank), wuq, wuk, wuvt,
      cos2, sin2)


def _mla_attn_kernel(qn_ref, qr_ref, knc_ref, krc_ref, vtc_ref, knl_ref, krl_ref, vtl_ref,
                     o_ref, k_ref, vt_ref):
    tc = knc_ref.shape[1]

    @pl.when(pl.program_id(2) == 0)
    def _():
        k_ref[:tc, :LANES] = knc_ref[0]
        k_ref[:tc, LANES:] = krc_ref[0]
        k_ref[tc:, :LANES] = knl_ref[0]
        k_ref[tc:, LANES:] = krl_ref[0]
        vt_ref[:, :tc] = vtc_ref[0]
        vt_ref[:, tc:] = vtl_ref[0]

    def chain(rows):
        def store(o):
            o_ref[0, rows, :] = o
        return (lambda: jnp.concatenate([qn_ref[0, rows, :], qr_ref[0, rows, :]], axis=1)), store

    _attend_chains_t([chain(rows) for rows in _row_chunks(qn_ref.shape[1], MLA_ROW_CHUNK)],
                     [(k_ref[...], vt_ref[...])])


def _mla_attn_call(qn, qr, kn, vt, kr, b_lat, tc, riders=()):
    nb, t, n_nope = qn.shape
    heads = n_nope // MLA_NOPE
    tq = min(2048, t)
    qspec = pl.BlockSpec((1, tq, LANES), lambda b, h, i: (b, i, h))
    qrspec = pl.BlockSpec((1, tq, LANES), lambda b, h, i: (b, i, h // 2))
    ctx = lambda f: pl.BlockSpec((1, tc, LANES), lambda b, h, i: (b_lat, b, f(h)))
    lat = lambda f: pl.BlockSpec((1, t, LANES), lambda b, h, i: (b, 0, f(h)))
    vt_ctx = pl.BlockSpec((1, MLA_V, tc), lambda b, h, i: (b_lat, h, b))
    vt_lat = pl.BlockSpec((1, MLA_V, t), lambda b, h, i: (b, h, 0))
    same = lambda h: h
    parity = lambda h: h % 2
    return _call_with_casts(
        _mla_attn_kernel, riders,
        grid=(b_lat, heads, t // tq),
        in_specs=[qspec, qrspec, ctx(same), ctx(parity), vt_ctx, lat(same), lat(parity), vt_lat],
        out_specs=[qspec],
        out_shape=[jax.ShapeDtypeStruct((b_lat, t, n_nope), BF16)],
        scratch_shapes=[pltpu.VMEM((tc + t, 2 * LANES), BF16), pltpu.VMEM((MLA_V, tc + t), BF16)],
        compiler_params=_cparams("parallel", "parallel", "arbitrary"),
        name="mla_attention",
    )(qn, qr, kn, kr, vt, kn, kr, vt)


def _mla_out_kernel(x_ref, mod_ref, att_ref, w_ref, o_ref):
    o_ref[0] = x_ref[0] + mod_ref[0, 5:6, :] * _dot(att_ref[0], w_ref[...])


def _mla_out_call(x, mod, att, w_o, riders=()):
    nb, t, k = att.shape
    d = w_o.shape[1]
    tm = min(512, t)
    row = lambda n: pl.BlockSpec((1, tm, n), lambda b, i: (b, i, 0))
    return _call_with_casts(
        _mla_out_kernel, riders,
        grid=(nb, t // tm),
        in_specs=[row(d), pl.BlockSpec((1, N_MOD, d), lambda b, i: (b, 0, 0)), row(k),
                  _resident(w_o.shape)],
        out_specs=[row(d)],
        out_shape=[jax.ShapeDtypeStruct((nb, t, d), F32)],
        compiler_params=_cparams("parallel", "parallel"),
        name="mla_out_proj",
    )(x, mod, att, w_o)


def _rope_angles(n_tok, dim):
    n_rows = n_tok // GRID_W
    row = jnp.repeat(jnp.arange(n_rows), GRID_W).astype(F32)
    col = jnp.tile(jnp.arange(GRID_W), n_rows).astype(F32)
    half = dim // 2
    inv = 1.0 / (ROPE_THETA ** (jnp.arange(0, half, 2, dtype=F32) / half))
    return jnp.concatenate([row[:, None] * inv, col[:, None] * inv], axis=-1)


def _rope_tables(n_tok, dim):
    ang = _rope_angles(n_tok, dim)
    reps = LANES // dim
    cos = jnp.concatenate([jnp.cos(ang)] * (2 * reps), axis=-1)
    sin = jnp.concatenate([-jnp.sin(ang)] * reps + [jnp.sin(ang)] * reps, axis=-1)
    return (jnp.stack([cos, jnp.ones_like(cos)]), jnp.stack([sin, jnp.zeros_like(sin)]))


def _deinterleave(n):
    return np.concatenate([np.arange(0, n, 2), np.arange(1, n, 2)])


def kernel(x, c, ctx, c_ctx, mod_w, mod_b, norm_ffn1, norm_mix, norm_ffn2,
           ffn1_w_gate, ffn1_w_up, ffn1_w_down, ffn2_w_gate, ffn2_w_up, ffn2_w_down,
           hyb_w_in, hyb_conv_w, hyb_q_norm, hyb_k_norm, hyb_w_out,
           mla_w_down, mla_q_norm, mla_kv_norm, mla_w_uq, mla_w_ukv, mla_w_o, final_norm):
    b_lat, t, d = x.shape
    tc = ctx.shape[1]
    depth = mod_w.shape[0]
    assert depth == 2 and ctx.shape[0] * tc == t and b_lat + 1 <= MOD_ROWS
    assert t & (t - 1) == 0 and tc & (tc - 1) == 0 and t % GRID_W == 0
    nb = b_lat + 1

    cvec = jnp.concatenate([c, c_ctx[None], jnp.zeros((MOD_ROWS - nb, d), F32)], axis=0)
    mods = _mod_call(cvec, mod_w, mod_b).reshape(depth, MOD_ROWS, N_MOD, d)
    cast = lambda w: w.astype(BF16)
    tf = min(FFN_TILE, ffn1_w_gate.shape[2])
    f1 = _as_ffn_weights(_cast_call(ffn1_w_gate, 0, tf), _cast_call(ffn1_w_up, 0, tf),
                         _cast_call(ffn1_w_down, 0, d))

    mod = mods[0]
    xs = _ffn_call(x, nb, mod, norm_ffn1[0], *f1, 0, 0, ctx=ctx.reshape(1, t, d))
    cc = d // 2
    n_q = cc
    n_kv = GQA_KV_HEADS * HEAD_DIM
    w_in = hyb_w_in[0]
    perm = _deinterleave(HEAD_DIM)
    pairs_last = lambda w: w.reshape(d, -1, HEAD_DIM // 2, 2).swapaxes(2, 3).reshape(d, -1)
    wq = pairs_last(w_in[:, 3 * cc:3 * cc + n_q])
    wk = pairs_last(w_in[:, 3 * cc + n_q:3 * cc + n_q + n_kv])
    wvt = w_in[:, 3 * cc + n_q + n_kv:].T
    cos_a, sin_a = _rope_tables(t, HEAD_DIM)
    (gb, z, q, k, v), gates = _hyb_in_call(
        xs, mod, norm_mix[0], cast(w_in[:, :cc]), cast(w_in[:, cc:2 * cc]), cast(w_in[:, 2 * cc:3 * cc]),
        cast(wq), cast(wk), cast(wvt), hyb_q_norm[0][perm], hyb_k_norm[0][perm], cos_a, sin_a,
        riders=[(ffn2_w_gate, 0, tf), (ffn1_w_gate, 1, tf)])
    (att,), ups = _gqa_latent_call(q, k, v, b_lat, tc,
                                   riders=[(ffn2_w_up, 0, tf), (ffn1_w_up, 1, tf)])
    att_c = _gqa_ctx_call(q, k, v, b_lat, tc)
    (xs,), downs = _hyb_out_call(xs, mod, gb, z, att, att_c, hyb_conv_w[0], cast(hyb_w_out[0]),
                                 b_lat, tc, riders=[(ffn2_w_down, 0, d), (ffn1_w_down, 1, d)])
    xs = _ffn_call(xs, nb, mod, norm_ffn2[0], *_as_ffn_weights(gates[0], ups[0], downs[0]), 0, 6)

    mod = mods[1]
    xs = _ffn_call(xs, nb, mod, norm_ffn1[1], *_as_ffn_weights(gates[1], ups[1], downs[1]), 0, 0)
    heads = d // HEAD_DIM
    q_rank, kv_rank = mla_q_norm.shape[1], mla_kv_norm.shape[1]
    uq = mla_w_uq[0].reshape(q_rank, heads, MLA_NOPE + MLA_ROPE)
    uq_rope = uq[:, :, MLA_NOPE:].reshape(q_rank, heads // 2, 2, MLA_ROPE // 2, 2)
    wuq = jnp.concatenate([uq[:, :, :MLA_NOPE].reshape(q_rank, -1),
                           uq_rope.transpose(0, 1, 4, 2, 3).reshape(q_rank, -1)], axis=1)
    ukv = mla_w_ukv[0].reshape(kv_rank, heads, MLA_NOPE + MLA_V)
    wuk = ukv[:, :, :MLA_NOPE].reshape(kv_rank, -1)
    wuvt = ukv[:, :, MLA_NOPE:].reshape(kv_rank, -1).T
    kr0 = q_rank + kv_rank
    w_kr = mla_w_down[0][:, kr0:].reshape(d, MLA_ROPE // 2, 2)
    wdn = jnp.concatenate([mla_w_down[0][:, :kr0], w_kr[:, :, 0], w_kr[:, :, 0],
                           w_kr[:, :, 1], w_kr[:, :, 1]], axis=1)
    cos_m, sin_m = _rope_tables(t, MLA_ROPE)
    (qn, qr, kn, vt, kr), gates = _mla_in_call(
        xs, mod, norm_mix[1], cast(wdn), mla_q_norm[0], mla_kv_norm[0],
        cast(wuq), cast(wuk), cast(wuvt), cos_m, sin_m, riders=[(ffn2_w_gate, 1, tf)])
    (att,), ups = _mla_attn_call(qn, qr, kn, vt, kr, b_lat, tc, riders=[(ffn2_w_up, 1, tf)])
    (xs,), downs = _mla_out_call(xs, mod, att, cast(mla_w_o[0]), riders=[(ffn2_w_down, 1, d)])
    return _ffn_call(xs, b_lat, mod, norm_ffn2[1], *_as_ffn_weights(gates[0], ups[0], downs[0]),
                     0, 6, final_g=final_norm)
```

```python
import functools

import numpy as np
import jax
import jax.numpy as jnp
from jax import lax
from jax.experimental import pallas as pl
from jax.experimental.pallas import tpu as pltpu

HEAD_DIM = 128
GRID_W = 64
ROPE_THETA = 10000.0
EPS = 1e-6
N_MOD = 9
GQA_KV_HEADS = 2
MLA_NOPE = 128
MLA_ROPE = 64
MLA_V = 128
LANES = 128
VMEM_LIMIT = 60 * 1024 * 1024
MOD_ROWS = 16
FFN_ROW_CHUNKS = 4
FFN_TILE = 512
FFN_SLOTS = 4
PROJ_ROW_CHUNK = 256
GQA_ROW_CHUNK = 512
MLA_ROW_CHUNK = 512
ATTN_SCORE_LOOKAHEAD = 2
LOG2E = 1.4426950408889634
CAST_BLOCK_BYTES = 4 * 1024 * 1024

F32 = jnp.float32
BF16 = jnp.bfloat16


def _cparams(*sem):
    return pltpu.CompilerParams(dimension_semantics=sem, vmem_limit_bytes=VMEM_LIMIT)


def _resident(shape):
    nd = len(shape)
    return pl.BlockSpec(shape, lambda *_: (0,) * nd, pipeline_mode=pl.Buffered(1))


def _rms(x, g):
    return x * lax.rsqrt(jnp.mean(x * x, axis=-1, keepdims=True) + EPS) * g


def _modnorm(x, g, shift, scale):
    return _rms(x, g) * (1.0 + scale) + shift


def _rope(x, cos, sin):
    return x * cos + pltpu.roll(x, LANES // 2, axis=1) * sin


def _dot(a, b):
    return jnp.dot(a, b, preferred_element_type=F32)


def _dot_t(a, b):
    return lax.dot_general(a, b, (((1,), (1,)), ((), ())), preferred_element_type=F32)


def _mod_kernel(c_ref, w_ref, b_ref, o_ref):
    c = c_ref[...]
    s = (c * jax.nn.sigmoid(c)).astype(BF16)
    o_ref[0] = _dot(s, w_ref[0].astype(BF16)) + b_ref[0]


def _mod_call(cvec, mod_w, mod_b):
    depth, d, n = mod_w.shape
    tn = min(1024, n)
    return pl.pallas_call(
        _mod_kernel,
        grid=(depth, n // tn),
        in_specs=[
            pl.BlockSpec((MOD_ROWS, d), lambda l, j: (0, 0)),
            pl.BlockSpec((1, d, tn), lambda l, j: (l, 0, j)),
            pl.BlockSpec((1, 1, tn), lambda l, j: (l, 0, j)),
        ],
        out_specs=pl.BlockSpec((1, MOD_ROWS, tn), lambda l, j: (l, 0, j)),
        out_shape=jax.ShapeDtypeStruct((depth, MOD_ROWS, n), F32),
        compiler_params=_cparams("parallel", "parallel"),
        name="adaln_mod",
    )(cvec, mod_w, mod_b.reshape(depth, 1, n))


def _swiglu_act(xn, wg_ref, wu_ref):
    hg = _dot(xn, wg_ref[...])
    hu = _dot(xn, wu_ref[...])
    return (hg * jax.nn.sigmoid(hg) * hu).astype(BF16)


_GATE_UP, _DOWN = "gate_up", "down"


def _ring_plan(n_ff):
    assert FFN_SLOTS >= 4
    plan = {}
    for j in range(0, n_ff, 2):
        waits = ([(_GATE_UP, t) for t in (j + 1, j + 2) if t < n_ff]
                 + [(_DOWN, t) for t in (j, j + 1) if t < n_ff])
        starts = ([(_GATE_UP, t) for t in (j + 3, j + 4) if t < n_ff]
                  + [(_DOWN, t) for t in (j + 2, j + 3) if t < n_ff])
        plan[j] = (waits, starts, [])
    first = [(_GATE_UP, t) for t in range(min(3, n_ff))] + [(_DOWN, t) for t in range(min(2, n_ff))]
    for kind, tile in first:
        last_user = max(t for t in range(n_ff) if t % FFN_SLOTS == tile % FFN_SLOTS)
        free_from = last_user if kind == _GATE_UP else last_user + 1
        barrier = free_from + free_from % 2
        assert barrier in plan, "no barrier left to prefetch the next grid step's tile"
        plan[barrier][2].append((kind, tile))
    return plan, first


def _ffn_kernel(x_ref, mod_ref, g_ref, wg_hbm, wu_hbm, wd_hbm, *rest, layer, m0, final, with_ctx):
    rest = list(rest)
    fn_ref = rest.pop(0) if final else None
    ctx_ref = rest.pop(0) if with_ctx else None
    o_ref, xn_ref, h_ref, wg_buf, wu_buf, wd_buf, sem = rest
    is_ctx = pl.program_id(0) == pl.num_programs(0) - 1
    n_ff = wg_hbm.shape[1]
    tm = xn_ref.shape[0]
    step = pl.program_id(0) * pl.num_programs(1) + pl.program_id(1)
    not_last_step = step < pl.num_programs(0) * pl.num_programs(1) - 1
    half_gate = 0.5 * mod_ref[0, m0 + 2:m0 + 3, :]

    def copies(kind, blk):
        slot = blk % FFN_SLOTS
        if kind == _GATE_UP:
            pairs = ((0, wg_hbm.at[layer, blk], wg_buf), (1, wu_hbm.at[layer, blk], wu_buf))
        else:
            tf = wd_buf.shape[1]
            pairs = ((2, wd_hbm.at[layer, 0, pl.ds(blk * tf, tf)], wd_buf),)
        return [pltpu.make_async_copy(src, buf.at[slot], sem.at[k, slot]) for k, src, buf in pairs]

    def start(kind, blk):
        for c in copies(kind, blk):
            c.start()

    def wait(kind, blk):
        for c in copies(kind, blk):
            c.wait()

    def x_rows(rows):
        x = x_ref[0, rows, :]
        return jnp.where(is_ctx, ctx_ref[0, rows, :], x) if with_ctx else x

    plan, first = _ring_plan(n_ff)

    @pl.when(step == 0)
    def _():
        for item in first:
            start(*item)

    wait(_GATE_UP, 0)
    rs = tm // FFN_ROW_CHUNKS
    for r in range(FFN_ROW_CHUNKS):
        rows = slice(r * rs, (r + 1) * rs)
        xn = _modnorm(x_rows(rows), g_ref[...], mod_ref[0, m0:m0 + 1, :],
                      mod_ref[0, m0 + 1:m0 + 2, :]).astype(BF16)
        xn_ref[rows, :] = xn
        h_ref[0, rows, :] = _swiglu_act(xn, wg_buf.at[0], wu_buf.at[0])

    for j in range(n_ff):
        if j in plan:
            waits, starts, next_starts = plan[j]
            for item in starts:
                start(*item)
            if next_starts:
                pl.when(not_last_step)(
                    lambda items=next_starts: [start(*item) for item in items] and None)
            for item in waits:
                wait(*item)
        if j + 1 < n_ff:
            nslot = (j + 1) % FFN_SLOTS
            h_ref[(j + 1) % 2] = _swiglu_act(xn_ref[...], wg_buf.at[nslot], wu_buf.at[nslot])
        contrib = half_gate * _dot(h_ref[j % 2], wd_buf[j % FFN_SLOTS])
        if j == 0:
            o_ref[0] = x_rows(slice(None)) + contrib
        else:
            o_ref[0] += contrib

    if final:
        o_ref[0] = _rms(o_ref[0], fn_ref[...])


def _ffn_call(x, nb, mod, g, wg, wu, wd, layer, m0, final_g=None, ctx=None):
    _, t, d = x.shape
    _, n_ff, _, tf = wg.shape
    _ring_plan(n_ff)
    final = final_g is not None
    tm = min(512, t)
    nt = t // tm
    if ctx is None:
        x_spec = pl.BlockSpec((1, tm, d), lambda b, i: (b, i, 0))
    else:
        x_spec = pl.BlockSpec((1, tm, d), lambda b, i: (jnp.minimum(b, nb - 2),
                                                         jnp.where(b < nb - 1, i, nt - 1), 0))
    in_specs = [
        x_spec,
        pl.BlockSpec((1, N_MOD, d), lambda b, i: (b, 0, 0)),
        pl.BlockSpec((1, d), lambda b, i: (0, 0)),
        pl.BlockSpec(memory_space=pl.ANY),
        pl.BlockSpec(memory_space=pl.ANY),
        pl.BlockSpec(memory_space=pl.ANY),
    ]
    args = [x, mod, g.reshape(1, d), wg, wu, wd]
    if final:
        in_specs.append(pl.BlockSpec((1, d), lambda b, i: (0, 0)))
        args.append(final_g.reshape(1, d))
    if ctx is not None:
        in_specs.append(pl.BlockSpec((1, tm, d), lambda b, i: (0, jnp.where(b < nb - 1, 0, i), 0)))
        args.append(ctx)
    return pl.pallas_call(
        functools.partial(_ffn_kernel, layer=layer, m0=m0, final=final, with_ctx=ctx is not None),
        grid=(nb, nt),
        in_specs=in_specs,
        out_specs=pl.BlockSpec((1, tm, d), lambda b, i: (b, i, 0)),
        out_shape=jax.ShapeDtypeStruct((nb, t, d), F32),
        scratch_shapes=[
            pltpu.VMEM((tm, d), BF16),
            pltpu.VMEM((2, tm, tf), BF16),
            pltpu.VMEM((FFN_SLOTS, d, tf), BF16), pltpu.VMEM((FFN_SLOTS, d, tf), BF16),
            pltpu.VMEM((FFN_SLOTS, tf, d), BF16),
            pltpu.SemaphoreType.DMA((3, FFN_SLOTS)),
        ],
        compiler_params=_cparams("arbitrary", "arbitrary"),
        name="swiglu_half_step",
    )(*args)


def _cast_kernel(w_ref, o_ref):
    cb = o_ref.shape[3]
    for n in range(o_ref.shape[1]):
        o_ref[0, n] = w_ref[0, :, n * cb:(n + 1) * cb].astype(BF16)


def _cast_call(w, layer, col_block):
    _, r, c = w.shape
    nblk = c // col_block
    tr = max(n for n in range(16, r + 1, 16) if r % n == 0 and 4 * c * n <= CAST_BLOCK_BYTES)
    return pl.pallas_call(
        _cast_kernel,
        grid=(r // tr,),
        in_specs=[pl.BlockSpec((1, tr, c), lambda i: (layer, i, 0))],
        out_specs=pl.BlockSpec((1, nblk, tr, col_block), lambda i: (0, 0, i, 0)),
        out_shape=jax.ShapeDtypeStruct((1, nblk, r, col_block), BF16),
        compiler_params=_cparams("parallel"),
        name="weight_cast",
    )(w)


def _call_with_casts(kernel_fn, riders, *, grid, in_specs, out_specs, out_shape, **kw):
    n_in, n_out, n_r = len(in_specs), len(out_specs), len(riders)
    steps = int(np.prod(grid))
    r_in, r_out, r_shape = [], [], []
    for w, layer, cb in riders:
        _, r, c = w.shape
        tr = min(n for n in range(16, r + 1, 16) if r % n == 0 and r // n <= steps)

        def row_block(*g, n_rows=r // tr):
            lin = g[0]
            for gi, n in zip(g[1:], grid[1:]):
                lin = lin * n + gi
            return jnp.minimum(lin, n_rows - 1)

        r_in.append(pl.BlockSpec((1, tr, c), lambda *g, f=row_block, l=layer: (l, f(*g), 0)))
        r_out.append(pl.BlockSpec((1, c // cb, tr, cb), lambda *g, f=row_block: (0, 0, f(*g), 0)))
        r_shape.append(jax.ShapeDtypeStruct((1, c // cb, r, cb), BF16))

    def body(*refs):
        outs_at = n_in + n_r
        for w_ref, o_ref in zip(refs[n_in:outs_at], refs[outs_at + n_out:outs_at + n_out + n_r]):
            _cast_kernel(w_ref, o_ref)
        kernel_fn(*refs[:n_in], *refs[outs_at:outs_at + n_out], *refs[outs_at + n_out + n_r:])

    if riders:
        kw["compiler_params"] = _cparams(*["arbitrary"] * len(grid))
    call = pl.pallas_call(body, grid=grid, in_specs=list(in_specs) + r_in,
                          out_specs=list(out_specs) + r_out, out_shape=list(out_shape) + r_shape, **kw)

    def run(*args):
        res = call(*args, *[w for w, _, _ in riders])
        return list(res[:n_out]), list(res[n_out:])
    return run


def _hyb_in_kernel(x_ref, mod_ref, g_ref, wgb_ref, wgc_ref, wu_ref, wq_ref, wk_ref, wvt_ref,
                   qg_ref, kg_ref, cos_ref, sin_ref,
                   gb_ref, z_ref, q_ref, k_ref, vt_ref):
    q_scale = HEAD_DIM ** -0.5 * LOG2E
    norm = lambda rows: _modnorm(x_ref[0, rows, :], g_ref[...], mod_ref[0, 3:4, :],
                                 mod_ref[0, 4:5, :]).astype(BF16)
    chunks = _row_chunks(x_ref.shape[1], PROJ_ROW_CHUNK)
    xn_next = norm(chunks[0])
    for n, rows in enumerate(chunks):
        xn = xn_next
        if n + 1 < len(chunks):
            xn_next = norm(chunks[n + 1])
        gb_ref[0, rows, :] = _dot(xn, wgb_ref[...]).astype(BF16)
        z_ref[0, rows, :] = (_dot(xn, wgc_ref[...]) * _dot(xn, wu_ref[...])).astype(BF16)
        vt_ref[0, :, rows] = _dot_t(wvt_ref[...], xn).astype(BF16)
        cos = cos_ref[0, rows, :]
        sin = sin_ref[0, rows, :]
        q = _dot(xn, wq_ref[...])
        for h in range(q.shape[1] // HEAD_DIM):
            sl = slice(h * HEAD_DIM, (h + 1) * HEAD_DIM)
            q_ref[0, rows, sl] = (_rope(_rms(q[:, sl], qg_ref[...]), cos, sin) * q_scale).astype(BF16)
        k = _dot(xn, wk_ref[...])
        for h in range(k.shape[1] // HEAD_DIM):
            sl = slice(h * HEAD_DIM, (h + 1) * HEAD_DIM)
            k_ref[0, rows, sl] = _rope(_rms(k[:, sl], kg_ref[...]), cos, sin).astype(BF16)


def _hyb_in_call(x, mod, g, wgb, wgc, wu, wq, wk, wvt, qg, kg, cos2, sin2, riders=()):
    nb, t, d = x.shape
    tm = min(512, t)
    cc, nq, nkv = wgb.shape[1], wq.shape[1], wk.shape[1]
    row = lambda n: pl.BlockSpec((1, tm, n), lambda b, i: (b, i, 0))
    table = pl.BlockSpec((1, tm, LANES), lambda b, i: (jnp.where(b == nb - 1, 1, 0), i, 0))
    return _call_with_casts(
        _hyb_in_kernel, riders,
        grid=(nb, t // tm),
        in_specs=[
            row(d),
            pl.BlockSpec((1, N_MOD, d), lambda b, i: (b, 0, 0)),
            _resident((1, d)),
            _resident(wgb.shape), _resident(wgc.shape), _resident(wu.shape),
            _resident(wq.shape), _resident(wk.shape), _resident(wvt.shape),
            _resident((1, HEAD_DIM)), _resident((1, HEAD_DIM)),
            table, table,
        ],
        out_specs=[row(cc), row(cc), row(nq), row(nkv),
                   pl.BlockSpec((1, nkv, tm), lambda b, i: (b, 0, i))],
        out_shape=[jax.ShapeDtypeStruct((nb, t, n), BF16) for n in (cc, cc, nq, nkv)]
        + [jax.ShapeDtypeStruct((nb, nkv, t), BF16)],
        compiler_params=_cparams("parallel", "parallel"),
        name="hyb_in_proj",
    )(x, mod, g.reshape(1, d), wgb, wgc, wu, wq, wk, wvt,
      qg.reshape(1, HEAD_DIM), kg.reshape(1, HEAD_DIM), cos2, sin2)


def _softmax_pv(s_list, kv_list):
    m = s_list[0].max(axis=-1, keepdims=True)
    for s in s_list[1:]:
        m = jnp.maximum(m, s.max(axis=-1, keepdims=True))
    l = 0.0
    o = 0.0
    for s, (_, v) in zip(s_list, kv_list):
        p = jnp.exp2(s - m)
        l = l + p.sum(axis=-1, keepdims=True)
        o = o + _dot(p.astype(BF16), v)
    return o / l


def _attend_chains(chains, kv_list):
    scores = lambda load_q: [_dot_t(load_q(), k) for k, _ in kv_list]
    ahead = [scores(load_q) for load_q, _ in chains[:ATTN_SCORE_LOOKAHEAD]]
    for n, (_, store_o) in enumerate(chains):
        s_list = ahead.pop(0)
        if n + ATTN_SCORE_LOOKAHEAD < len(chains):
            ahead.append(scores(chains[n + ATTN_SCORE_LOOKAHEAD][0]))
        store_o(_softmax_pv(s_list, kv_list).astype(BF16))


def _softmax_pv_t(st_list, kvt_list):
    m = st_list[0].max(axis=0, keepdims=True)
    for s in st_list[1:]:
        m = jnp.maximum(m, s.max(axis=0, keepdims=True))
    l = 0.0
    o = 0.0
    for s, (_, vt) in zip(st_list, kvt_list):
        p = jnp.exp2(s - m)
        l = l + p.sum(axis=0, keepdims=True)
        o = o + _dot(vt, p.astype(BF16))
    return (o / l).T


def _attend_chains_t(chains, kvt_list):
    scores = lambda load_q: [_dot_t(k, load_q()) for k, _ in kvt_list]
    ahead = [scores(load_q) for load_q, _ in chains[:ATTN_SCORE_LOOKAHEAD]]
    for n, (_, store_o) in enumerate(chains):
        st_list = ahead.pop(0)
        if n + ATTN_SCORE_LOOKAHEAD < len(chains):
            ahead.append(scores(chains[n + ATTN_SCORE_LOOKAHEAD][0]))
        store_o(_softmax_pv_t(st_list, kvt_list).astype(BF16))


def _row_chunks(n, c):
    c = min(c, n)
    return [slice(r * c, (r + 1) * c) for r in range(n // c)]


def _gqa_kernel(*refs, has_lat):
    if has_lat:
        q_ref, kc_ref, vtc_ref, kl_ref, vtl_ref, o_ref = refs
    else:
        q_ref, kc_ref, vtc_ref, o_ref = refs
    kvt_list = [(kc_ref[0], vtc_ref[0])]
    if has_lat:
        kvt_list.append((kl_ref[0], vtl_ref[0]))

    def chain(rows, sl):
        def store(o):
            o_ref[0, rows, sl] = o
        return (lambda: q_ref[0, rows, sl]), store

    _attend_chains_t([chain(rows, slice(h * HEAD_DIM, (h + 1) * HEAD_DIM))
                      for h in range(q_ref.shape[2] // HEAD_DIM)
                      for rows in _row_chunks(q_ref.shape[1], GQA_ROW_CHUNK)], kvt_list)


def _gqa_latent_call(q, k, vt, b_lat, tc, riders=()):
    nb, t, nq = q.shape
    gw = nq // GQA_KV_HEADS
    tq = min(1024, t)
    ctx = pl.BlockSpec((1, tc, HEAD_DIM), lambda b, g, i: (b_lat, b, g))
    lat = pl.BlockSpec((1, t, HEAD_DIM), lambda b, g, i: (b, 0, g))
    vt_ctx = pl.BlockSpec((1, HEAD_DIM, tc), lambda b, g, i: (b_lat, g, b))
    vt_lat = pl.BlockSpec((1, HEAD_DIM, t), lambda b, g, i: (b, g, 0))
    qo = pl.BlockSpec((1, tq, gw), lambda b, g, i: (b, i, g))
    return _call_with_casts(
        functools.partial(_gqa_kernel, has_lat=True), riders,
        grid=(b_lat, GQA_KV_HEADS, t // tq),
        in_specs=[qo, ctx, vt_ctx, lat, vt_lat],
        out_specs=[qo],
        out_shape=[jax.ShapeDtypeStruct((b_lat, t, nq), BF16)],
        compiler_params=_cparams("parallel", "parallel", "parallel"),
        name="gqa_latent",
    )(q, k, vt, k, vt)


def _gqa_ctx_call(q, k, vt, b_lat, tc):
    nb, t, nq = q.shape
    gw = nq // GQA_KV_HEADS
    ctx = pl.BlockSpec((1, tc, HEAD_DIM), lambda b, g: (b_lat, b, g))
    vt_ctx = pl.BlockSpec((1, HEAD_DIM, tc), lambda b, g: (b_lat, g, b))
    return pl.pallas_call(
        functools.partial(_gqa_kernel, has_lat=False),
        grid=(t // tc, GQA_KV_HEADS),
        in_specs=[pl.BlockSpec((1, tc, gw), lambda b, g: (b_lat, b, g)), ctx, vt_ctx],
        out_specs=pl.BlockSpec((1, tc, gw), lambda b, g: (0, b, g)),
        out_shape=jax.ShapeDtypeStruct((1, t, nq), BF16),
        compiler_params=_cparams("parallel", "parallel"),
        name="gqa_context",
    )(q, k, vt)


def _hyb_out_kernel(x_ref, mod_ref, gb_ref, z_ref, zp_ref, zn_ref, att_ref, attc_ref, cw_ref, w_ref,
                    o_ref, *, seq_lat, seq_ctx, b_lat):
    b, i = pl.program_id(0), pl.program_id(1)
    tm, cc = z_ref.shape[1], z_ref.shape[2]
    acc = _dot(jnp.where(b == b_lat, attc_ref[0], att_ref[0]), w_ref[cc:, :])
    z = z_ref[0].astype(F32)
    rows = lax.broadcasted_iota(jnp.int32, (tm, cc), 0)
    seq = jnp.where(b == b_lat, seq_ctx, seq_lat)
    pos = (i * tm + rows) & (seq - 1)
    z_prev = jnp.where(rows == 0, zp_ref[0, 7:8, :].astype(F32), pltpu.roll(z, 1, axis=0))
    z_next = jnp.where(rows == tm - 1, zn_ref[0, 0:1, :].astype(F32),
                       pltpu.roll(z, tm - 1, axis=0))
    z_prev = jnp.where(pos == 0, 0.0, z_prev)
    z_next = jnp.where(pos == seq - 1, 0.0, z_next)
    conv = z_prev * cw_ref[0:1, :] + z * cw_ref[1:2, :] + z_next * cw_ref[2:3, :]
    acc = acc + _dot((gb_ref[0].astype(F32) * conv).astype(BF16), w_ref[:cc, :])
    o_ref[0] = x_ref[0] + mod_ref[0, 5:6, :] * acc


def _hyb_out_call(x, mod, gb, z, att, att_c, conv_w, w_out, b_lat, tc, riders=()):
    nb, t, d = x.shape
    cc = gb.shape[2]
    tm = min(512, t)
    nt = t // tm
    hb = tm // 8
    row = lambda n: pl.BlockSpec((1, tm, n), lambda b, i: (b, i, 0))
    nq = att.shape[2]
    att_spec = pl.BlockSpec((1, tm, nq), lambda b, i: (jnp.minimum(b, b_lat - 1),
                                                       jnp.where(b < b_lat, i, nt - 1), 0))
    attc_spec = pl.BlockSpec((1, tm, nq), lambda b, i: (0, jnp.where(b < b_lat, 0, i), 0))
    return _call_with_casts(
        functools.partial(_hyb_out_kernel, seq_lat=t, seq_ctx=tc, b_lat=b_lat), riders,
        grid=(nb, t // tm),
        in_specs=[
            row(d),
            pl.BlockSpec((1, N_MOD, d), lambda b, i: (b, 0, 0)),
            row(cc), row(cc),
            pl.BlockSpec((1, 8, cc), lambda b, i: (b, jnp.maximum(i * hb - 1, 0), 0)),
            pl.BlockSpec((1, 8, cc), lambda b, i: (b, jnp.minimum((i + 1) * hb, t // 8 - 1), 0)),
            att_spec, attc_spec,
            _resident(conv_w.shape),
            _resident(w_out.shape),
        ],
        out_specs=[row(d)],
        out_shape=[jax.ShapeDtypeStruct((nb, t, d), F32)],
        compiler_params=_cparams("parallel", "parallel"),
        name="hyb_out_proj",
    )(x, mod, gb, z, z, z, att, att_c, conv_w, w_out)


def _mla_in_kernel(x_ref, mod_ref, g_ref, wd_ref, qg_ref, kvg_ref, wuq_ref, wuk_ref, wuvt_ref,
                   cos_ref, sin_ref, qn_ref, qr_ref, kn_ref, vt_ref, kr_ref, *, q_rank, kv_rank):
    n_nope = qn_ref.shape[2]
    q_scale = (MLA_NOPE + MLA_ROPE) ** -0.5 * LOG2E
    down = lambda rows: _dot(_modnorm(x_ref[0, rows, :], g_ref[...], mod_ref[0, 3:4, :],
                                      mod_ref[0, 4:5, :]).astype(BF16), wd_ref[...])
    chunks = _row_chunks(x_ref.shape[1], PROJ_ROW_CHUNK)
    d_next = down(chunks[0])
    for n, rows in enumerate(chunks):
        d = d_next
        if n + 1 < len(chunks):
            d_next = down(chunks[n + 1])
        cos = cos_ref[0, rows, :]
        sin = sin_ref[0, rows, :]
        q = _dot(_rms(d[:, :q_rank], qg_ref[...]).astype(BF16), wuq_ref[...])
        qn_ref[0, rows, :] = (q[:, :n_nope] * q_scale).astype(BF16)
        for j in range(qr_ref.shape[2] // LANES):
            qr_ref[0, rows, j * LANES:(j + 1) * LANES] = (
                _rope(q[:, n_nope + j * LANES:n_nope + (j + 1) * LANES], cos, sin) * q_scale
            ).astype(BF16)
        ckv = _rms(d[:, q_rank:q_rank + kv_rank], kvg_ref[...]).astype(BF16)
        kn_ref[0, rows, :] = _dot(ckv, wuk_ref[...]).astype(BF16)
        vt_ref[0, :, rows] = _dot_t(wuvt_ref[...], ckv).astype(BF16)
        kr = _rope(d[:, q_rank + kv_rank:], cos, sin)
        first = (lax.broadcasted_iota(jnp.int32, kr.shape, 1) & (MLA_ROPE // 2)) == 0
        kr_ref[0, rows, :LANES] = jnp.where(first, kr, 0.0).astype(BF16)
        kr_ref[0, rows, LANES:] = jnp.where(first, 0.0, kr).astype(BF16)


def _mla_in_call(x, mod, g, wd, qg, kvg, wuq, wuk, wuvt, cos2, sin2, riders=()):
    nb, t, d = x.shape
    tm = min(512, t)
    q_rank, kv_rank = qg.shape[0], kvg.shape[0]
    n_nope = wuk.shape[1]
    n_rope = wuq.shape[1] - n_nope
    n_v = wuvt.shape[0]
    row = lambda n: pl.BlockSpec((1, tm, n), lambda b, i: (b, i, 0))
    table = pl.BlockSpec((1, tm, LANES), lambda b, i: (jnp.where(b == nb - 1, 1, 0), i, 0))
    out_specs = [row(n_nope), row(n_rope), row(n_nope),
                 pl.BlockSpec((1, n_v, tm), lambda b, i: (b, 0, i)), row(2 * LANES)]
    out_shapes = [(nb, t, n_nope), (nb, t, n_rope), (nb, t, n_nope), (nb, n_v, t), (nb, t, 2 * LANES)]
    return _call_with_casts(
        functools.partial(_mla_in_kernel, q_rank=q_rank, kv_rank=kv_rank), riders,
        grid=(nb, t // tm),
        in_specs=[
            row(d),
            pl.BlockSpec((1, N_MOD, d), lambda b, i: (b, 0, 0)),
            _resident((1, d)),
            _resident(wd.shape), _resident((1, q_rank)), _resident((1, kv_rank)),
            _resident(wuq.shape), _resident(wuk.shape), _resident(wuvt.shape),
            table, table,
        ],
        out_specs=out_specs,
        out_shape=[jax.ShapeDtypeStruct(s, BF16) for s in out_shapes],
        compiler_params=_cparams("parallel", "parallel"),
        name="mla_in_proj",
    )(x, mod, g.reshape(1, d), wd, qg.reshape(1, q_rank), kvg.reshape(1, kv_rank), wuq, wuk, wuvt,
      cos2, sin2)


def _mla_attn_kernel(qn_ref, qr_ref, knc_ref, krc_ref, vtc_ref, knl_ref, krl_ref, vtl_ref,
                     o_ref, k_ref, vt_ref):
    tc = knc_ref.shape[1]

    @pl.when(pl.program_id(2) == 0)
    def _():
        k_ref[:tc, :LANES] = knc_ref[0]
        k_ref[:tc, LANES:] = krc_ref[0]
        k_ref[tc:, :LANES] = knl_ref[0]
        k_ref[tc:, LANES:] = krl_ref[0]
        vt_ref[:, :tc] = vtc_ref[0]
        vt_ref[:, tc:] = vtl_ref[0]

    def chain(rows):
        def store(o):
            o_ref[0, rows, :] = o
        return (lambda: jnp.concatenate([qn_ref[0, rows, :], qr_ref[0, rows, :]], axis=1)), store

    _attend_chains_t([chain(rows) for rows in _row_chunks(qn_ref.shape[1], MLA_ROW_CHUNK)],
                     [(k_ref[...], vt_ref[...])])


def _mla_attn_call(qn, qr, kn, vt, kr, b_lat, tc, riders=()):
    nb, t, n_nope = qn.shape
    heads = n_nope // MLA_NOPE
    tq = min(2048, t)
    qspec = pl.BlockSpec((1, tq, LANES), lambda b, h, i: (b, i, h))
    qrspec = pl.BlockSpec((1, tq, LANES), lambda b, h, i: (b, i, h // 2))
    ctx = lambda f: pl.BlockSpec((1, tc, LANES), lambda b, h, i: (b_lat, b, f(h)))
    lat = lambda f: pl.BlockSpec((1, t, LANES), lambda b, h, i: (b, 0, f(h)))
    vt_ctx = pl.BlockSpec((1, MLA_V, tc), lambda b, h, i: (b_lat, h, b))
    vt_lat = pl.BlockSpec((1, MLA_V, t), lambda b, h, i: (b, h, 0))
    same = lambda h: h
    parity = lambda h: h % 2
    return _call_with_casts(
        _mla_attn_kernel, riders,
        grid=(b_lat, heads, t // tq),
        in_specs=[qspec, qrspec, ctx(same), ctx(parity), vt_ctx, lat(same), lat(parity), vt_lat],
        out_specs=[qspec],
        out_shape=[jax.ShapeDtypeStruct((b_lat, t, n_nope), BF16)],
        scratch_shapes=[pltpu.VMEM((tc + t, 2 * LANES), BF16), pltpu.VMEM((MLA_V, tc + t), BF16)],
        compiler_params=_cparams("parallel", "parallel", "arbitrary"),
        name="mla_attention",
    )(qn, qr, kn, kr, vt, kn, kr, vt)


def _mla_out_kernel(x_ref, mod_ref, att_ref, w_ref, o_ref):
    o_ref[0] = x_ref[0] + mod_ref[0, 5:6, :] * _dot(att_ref[0], w_ref[...])


def _mla_out_call(x, mod, att, w_o, riders=()):
    nb, t, k = att.shape
    d = w_o.shape[1]
    tm = min(512, t)
    row = lambda n: pl.BlockSpec((1, tm, n), lambda b, i: (b, i, 0))
    return _call_with_casts(
        _mla_out_kernel, riders,
        grid=(nb, t // tm),
        in_specs=[row(d), pl.BlockSpec((1, N_MOD, d), lambda b, i: (b, 0, 0)), row(k),
                  _resident(w_o.shape)],
        out_specs=[row(d)],
        out_shape=[jax.ShapeDtypeStruct((nb, t, d), F32)],
        compiler_params=_cparams("parallel", "parallel"),
        name="mla_out_proj",
    )(x, mod, att, w_o)


def _rope_angles(n_tok, dim):
    n_rows = n_tok // GRID_W
    row = jnp.repeat(jnp.arange(n_rows), GRID_W).astype(F32)
    col = jnp.tile(jnp.arange(GRID_W), n_rows).astype(F32)
    half = dim // 2
    inv = 1.0 / (ROPE_THETA ** (jnp.arange(0, half, 2, dtype=F32) / half))
    return jnp.concatenate([row[:, None] * inv, col[:, None] * inv], axis=-1)


def _rope_tables(n_tok, dim):
    ang = _rope_angles(n_tok, dim)
    reps = LANES // dim
    cos = jnp.concatenate([jnp.cos(ang)] * (2 * reps), axis=-1)
    sin = jnp.concatenate([-jnp.sin(ang)] * reps + [jnp.sin(ang)] * reps, axis=-1)
    return (jnp.stack([cos, jnp.ones_like(cos)]), jnp.stack([sin, jnp.zeros_like(sin)]))


def _deinterleave(n):
    return np.concatenate([np.arange(0, n, 2), np.arange(1, n, 2)])


def kernel(x, c, ctx, c_ctx, mod_w, mod_b, norm_ffn1, norm_mix, norm_ffn2,
           ffn1_w_gate, ffn1_w_up, ffn1_w_down, ffn2_w_gate, ffn2_w_up, ffn2_w_down,
           hyb_w_in, hyb_conv_w, hyb_q_norm, hyb_k_norm, hyb_w_out,
           mla_w_down, mla_q_norm, mla_kv_norm, mla_w_uq, mla_w_ukv, mla_w_o, final_norm):
    b_lat, t, d = x.shape
    tc = ctx.shape[1]
    depth = mod_w.shape[0]
    assert depth == 2 and ctx.shape[0] * tc == t and b_lat + 1 <= MOD_ROWS
    assert t & (t - 1) == 0 and tc & (tc - 1) == 0 and t % GRID_W == 0
    nb = b_lat + 1

    cvec = jnp.concatenate([c, c_ctx[None], jnp.zeros((MOD_ROWS - nb, d), F32)], axis=0)
    mods = _mod_call(cvec, mod_w, mod_b).reshape(depth, MOD_ROWS, N_MOD, d)
    cast = lambda w: w.astype(BF16)
    tf = min(FFN_TILE, ffn1_w_gate.shape[2])
    f1 = (_cast_call(ffn1_w_gate, 0, tf), _cast_call(ffn1_w_up, 0, tf),
          _cast_call(ffn1_w_down, 0, d))

    mod = mods[0]
    xs = _ffn_call(x, nb, mod, norm_ffn1[0], *f1, 0, 0, ctx=ctx.reshape(1, t, d))
    cc = d // 2
    n_q = cc
    n_kv = GQA_KV_HEADS * HEAD_DIM
    w_in = hyb_w_in[0]
    perm = _deinterleave(HEAD_DIM)
    pairs_last = lambda w: w.reshape(d, -1, HEAD_DIM // 2, 2).swapaxes(2, 3).reshape(d, -1)
    wq = pairs_last(w_in[:, 3 * cc:3 * cc + n_q])
    wk = pairs_last(w_in[:, 3 * cc + n_q:3 * cc + n_q + n_kv])
    wvt = w_in[:, 3 * cc + n_q + n_kv:].T
    cos_a, sin_a = _rope_tables(t, HEAD_DIM)
    (gb, z, q, k, v), gates = _hyb_in_call(
        xs, mod, norm_mix[0], cast(w_in[:, :cc]), cast(w_in[:, cc:2 * cc]), cast(w_in[:, 2 * cc:3 * cc]),
        cast(wq), cast(wk), cast(wvt), hyb_q_norm[0][perm], hyb_k_norm[0][perm], cos_a, sin_a,
        riders=[(ffn2_w_gate, 0, tf), (ffn1_w_gate, 1, tf)])
    (att,), ups = _gqa_latent_call(q, k, v, b_lat, tc,
                                   riders=[(ffn2_w_up, 0, tf), (ffn1_w_up, 1, tf)])
    att_c = _gqa_ctx_call(q, k, v, b_lat, tc)
    (xs,), downs = _hyb_out_call(xs, mod, gb, z, att, att_c, hyb_conv_w[0], cast(hyb_w_out[0]),
                                 b_lat, tc, riders=[(ffn2_w_down, 0, d), (ffn1_w_down, 1, d)])
    xs = _ffn_call(xs, nb, mod, norm_ffn2[0], gates[0], ups[0], downs[0], 0, 6)

    mod = mods[1]
    xs = _ffn_call(xs, nb, mod, norm_ffn1[1], gates[1], ups[1], downs[1], 0, 0)
    heads = d // HEAD_DIM
    q_rank, kv_rank = mla_q_norm.shape[1], mla_kv_norm.shape[1]
    uq = mla_w_uq[0].reshape(q_rank, heads, MLA_NOPE + MLA_ROPE)
    uq_rope = uq[:, :, MLA_NOPE:].reshape(q_rank, heads // 2, 2, MLA_ROPE // 2, 2)
    wuq = jnp.concatenate([uq[:, :, :MLA_NOPE].reshape(q_rank, -1),
                           uq_rope.transpose(0, 1, 4, 2, 3).reshape(q_rank, -1)], axis=1)
    ukv = mla_w_ukv[0].reshape(kv_rank, heads, MLA_NOPE + MLA_V)
    wuk = ukv[:, :, :MLA_NOPE].reshape(kv_rank, -1)
    wuvt = ukv[:, :, MLA_NOPE:].reshape(kv_rank, -1).T
    kr0 = q_rank + kv_rank
    w_kr = mla_w_down[0][:, kr0:].reshape(d, MLA_ROPE // 2, 2)
    wdn = jnp.concatenate([mla_w_down[0][:, :kr0], w_kr[:, :, 0], w_kr[:, :, 0],
                           w_kr[:, :, 1], w_kr[:, :, 1]], axis=1)
    cos_m, sin_m = _rope_tables(t, MLA_ROPE)
    (qn, qr, kn, vt, kr), gates = _mla_in_call(
        xs, mod, norm_mix[1], cast(wdn), mla_q_norm[0], mla_kv_norm[0],
        cast(wuq), cast(wuk), cast(wuvt), cos_m, sin_m, riders=[(ffn2_w_gate, 1, tf)])
    (att,), ups = _mla_attn_call(qn, qr, kn, vt, kr, b_lat, tc, riders=[(ffn2_w_up, 1, tf)])
    (xs,), downs = _mla_out_call(xs, mod, att, cast(mla_w_o[0]), riders=[(ffn2_w_down, 1, d)])
    return _ffn_call(xs, b_lat, mod, norm_ffn2[1], gates[0], ups[0], downs[0], 0, 6,
                     final_g=final_norm)
```

```python
import functools

import numpy as np
import jax
import jax.numpy as jnp
from jax import lax
from jax.experimental import pallas as pl
from jax.experimental.pallas import tpu as pltpu

HEAD_DIM = 128
GRID_W = 64
ROPE_THETA = 10000.0
EPS = 1e-6
N_MOD = 9
GQA_KV_HEADS = 2
MLA_NOPE = 128
MLA_ROPE = 64
MLA_V = 128
LANES = 128
VMEM_LIMIT = 60 * 1024 * 1024
MOD_ROWS = 16
FFN_ROW_CHUNKS = 4
FFN_TILE = 512
FFN_SLOTS = 4
PROJ_ROW_CHUNK = 256
GQA_ROW_CHUNK = 512
MLA_ROW_CHUNK = 512
ATTN_SCORE_LOOKAHEAD = 2
LOG2E = 1.4426950408889634
CAST_BLOCK_BYTES = 4 * 1024 * 1024

F32 = jnp.float32
BF16 = jnp.bfloat16


def _cparams(*sem):
    return pltpu.CompilerParams(dimension_semantics=sem, vmem_limit_bytes=VMEM_LIMIT)


def _resident(shape):
    nd = len(shape)
    return pl.BlockSpec(shape, lambda *_: (0,) * nd, pipeline_mode=pl.Buffered(1))


def _rms(x, g):
    return x * lax.rsqrt(jnp.mean(x * x, axis=-1, keepdims=True) + EPS) * g


def _modnorm(x, g, shift, scale):
    return _rms(x, g) * (1.0 + scale) + shift


def _rope(x, cos, sin):
    return x * cos + pltpu.roll(x, LANES // 2, axis=1) * sin


def _dot(a, b):
    return jnp.dot(a, b, preferred_element_type=F32)


def _dot_t(a, b):
    return lax.dot_general(a, b, (((1,), (1,)), ((), ())), preferred_element_type=F32)


def _mod_kernel(c_ref, w_ref, b_ref, o_ref):
    c = c_ref[...]
    s = (c * jax.nn.sigmoid(c)).astype(BF16)
    o_ref[0] = _dot(s, w_ref[0].astype(BF16)) + b_ref[0]


def _mod_call(cvec, mod_w, mod_b):
    depth, d, n = mod_w.shape
    tn = min(1024, n)
    return pl.pallas_call(
        _mod_kernel,
        grid=(depth, n // tn),
        in_specs=[
            pl.BlockSpec((MOD_ROWS, d), lambda l, j: (0, 0)),
            pl.BlockSpec((1, d, tn), lambda l, j: (l, 0, j)),
            pl.BlockSpec((1, 1, tn), lambda l, j: (l, 0, j)),
        ],
        out_specs=pl.BlockSpec((1, MOD_ROWS, tn), lambda l, j: (l, 0, j)),
        out_shape=jax.ShapeDtypeStruct((depth, MOD_ROWS, n), F32),
        compiler_params=_cparams("parallel", "parallel"),
        name="adaln_mod",
    )(cvec, mod_w, mod_b.reshape(depth, 1, n))


def _swiglu_act(xn, wg_ref, wu_ref):
    hg = _dot(xn, wg_ref[...])
    hu = _dot(xn, wu_ref[...])
    return (hg * jax.nn.sigmoid(hg) * hu).astype(BF16)


_GATE_UP, _DOWN = "gate_up", "down"


def _ring_plan(n_ff):
    assert FFN_SLOTS >= 4
    plan = {}
    for j in range(0, n_ff, 2):
        waits = ([(_GATE_UP, t) for t in (j + 1, j + 2) if t < n_ff]
                 + [(_DOWN, t) for t in (j, j + 1) if t < n_ff])
        starts = ([(_GATE_UP, t) for t in (j + 3, j + 4) if t < n_ff]
                  + [(_DOWN, t) for t in (j + 2, j + 3) if t < n_ff])
        plan[j] = (waits, starts, [])
    first = [(_GATE_UP, t) for t in range(min(3, n_ff))] + [(_DOWN, t) for t in range(min(2, n_ff))]
    for kind, tile in first:
        last_user = max(t for t in range(n_ff) if t % FFN_SLOTS == tile % FFN_SLOTS)
        free_from = last_user if kind == _GATE_UP else last_user + 1
        barrier = free_from + free_from % 2
        assert barrier in plan, "no barrier left to prefetch the next grid step's tile"
        plan[barrier][2].append((kind, tile))
    return plan, first


def _ffn_kernel(x_ref, mod_ref, g_ref, wg_hbm, wu_hbm, wd_hbm, *rest, layer, m0, final, with_ctx):
    rest = list(rest)
    fn_ref = rest.pop(0) if final else None
    ctx_ref = rest.pop(0) if with_ctx else None
    o_ref, xn_ref, h_ref, wg_buf, wu_buf, wd_buf, sem = rest
    is_ctx = pl.program_id(0) == pl.num_programs(0) - 1
    n_ff = wg_hbm.shape[1]
    tm = xn_ref.shape[0]
    step = pl.program_id(0) * pl.num_programs(1) + pl.program_id(1)
    not_last_step = step < pl.num_programs(0) * pl.num_programs(1) - 1
    half_gate = 0.5 * mod_ref[0, m0 + 2:m0 + 3, :]

    def copies(kind, blk):
        slot = blk % FFN_SLOTS
        pairs = (((0, wg_hbm, wg_buf), (1, wu_hbm, wu_buf)) if kind == _GATE_UP
                 else ((2, wd_hbm, wd_buf),))
        return [pltpu.make_async_copy(hbm.at[layer, blk], buf.at[slot], sem.at[k, slot])
                for k, hbm, buf in pairs]

    def start(kind, blk):
        for c in copies(kind, blk):
            c.start()

    def wait(kind, blk):
        for c in copies(kind, blk):
            c.wait()

    def x_rows(rows):
        x = x_ref[0, rows, :]
        return jnp.where(is_ctx, ctx_ref[0, rows, :], x) if with_ctx else x

    plan, first = _ring_plan(n_ff)

    @pl.when(step == 0)
    def _():
        for item in first:
            start(*item)

    wait(_GATE_UP, 0)
    rs = tm // FFN_ROW_CHUNKS
    for r in range(FFN_ROW_CHUNKS):
        rows = slice(r * rs, (r + 1) * rs)
        xn = _modnorm(x_rows(rows), g_ref[...], mod_ref[0, m0:m0 + 1, :],
                      mod_ref[0, m0 + 1:m0 + 2, :]).astype(BF16)
        xn_ref[rows, :] = xn
        h_ref[0, rows, :] = _swiglu_act(xn, wg_buf.at[0], wu_buf.at[0])

    for j in range(n_ff):
        if j in plan:
            waits, starts, next_starts = plan[j]
            for item in starts:
                start(*item)
            if next_starts:
                pl.when(not_last_step)(
                    lambda items=next_starts: [start(*item) for item in items] and None)
            for item in waits:
                wait(*item)
        if j + 1 < n_ff:
            nslot = (j + 1) % FFN_SLOTS
            h_ref[(j + 1) % 2] = _swiglu_act(xn_ref[...], wg_buf.at[nslot], wu_buf.at[nslot])
        contrib = half_gate * _dot(h_ref[j % 2], wd_buf[j % FFN_SLOTS])
        if j == 0:
            o_ref[0] = x_rows(slice(None)) + contrib
        else:
            o_ref[0] += contrib

    if final:
        o_ref[0] = _rms(o_ref[0], fn_ref[...])


def _ffn_call(x, nb, mod, g, wg, wu, wd, layer, m0, final_g=None, ctx=None):
    _, t, d = x.shape
    _, n_ff, _, tf = wg.shape
    _ring_plan(n_ff)
    final = final_g is not None
    tm = min(512, t)
    nt = t // tm
    if ctx is None:
        x_spec = pl.BlockSpec((1, tm, d), lambda b, i: (b, i, 0))
    else:
        x_spec = pl.BlockSpec((1, tm, d), lambda b, i: (jnp.minimum(b, nb - 2),
                                                         jnp.where(b < nb - 1, i, nt - 1), 0))
    in_specs = [
        x_spec,
        pl.BlockSpec((1, N_MOD, d), lambda b, i: (b, 0, 0)),
        pl.BlockSpec((1, d), lambda b, i: (0, 0)),
        pl.BlockSpec(memory_space=pl.ANY),
        pl.BlockSpec(memory_space=pl.ANY),
        pl.BlockSpec(memory_space=pl.ANY),
    ]
    args = [x, mod, g.reshape(1, d), wg, wu, wd]
    if final:
        in_specs.append(pl.BlockSpec((1, d), lambda b, i: (0, 0)))
        args.append(final_g.reshape(1, d))
    if ctx is not None:
        in_specs.append(pl.BlockSpec((1, tm, d), lambda b, i: (0, jnp.where(b < nb - 1, 0, i), 0)))
        args.append(ctx)
    return pl.pallas_call(
        functools.partial(_ffn_kernel, layer=layer, m0=m0, final=final, with_ctx=ctx is not None),
        grid=(nb, nt),
        in_specs=in_specs,
        out_specs=pl.BlockSpec((1, tm, d), lambda b, i: (b, i, 0)),
        out_shape=jax.ShapeDtypeStruct((nb, t, d), F32),
        scratch_shapes=[
            pltpu.VMEM((tm, d), BF16),
            pltpu.VMEM((2, tm, tf), BF16),
            pltpu.VMEM((FFN_SLOTS, d, tf), BF16), pltpu.VMEM((FFN_SLOTS, d, tf), BF16),
            pltpu.VMEM((FFN_SLOTS, tf, d), BF16),
            pltpu.SemaphoreType.DMA((3, FFN_SLOTS)),
        ],
        compiler_params=_cparams("arbitrary", "arbitrary"),
        name="swiglu_half_step",
    )(*args)


def _cast_kernel(w_ref, o_ref):
    cb = o_ref.shape[3]
    for n in range(o_ref.shape[1]):
        o_ref[0, n] = w_ref[0, :, n * cb:(n + 1) * cb].astype(BF16)


def _cast_call(w, layer, col_block):
    _, r, c = w.shape
    nblk = c // col_block
    tr = max(n for n in range(16, r + 1, 16) if r % n == 0 and 4 * c * n <= CAST_BLOCK_BYTES)
    return pl.pallas_call(
        _cast_kernel,
        grid=(r // tr,),
        in_specs=[pl.BlockSpec((1, tr, c), lambda i: (layer, i, 0))],
        out_specs=pl.BlockSpec((1, nblk, tr, col_block), lambda i: (0, 0, i, 0)),
        out_shape=jax.ShapeDtypeStruct((1, nblk, r, col_block), BF16),
        compiler_params=_cparams("parallel"),
        name="weight_cast",
    )(w)


def _call_with_casts(kernel_fn, riders, *, grid, in_specs, out_specs, out_shape, **kw):
    n_in, n_out, n_r = len(in_specs), len(out_specs), len(riders)
    steps = int(np.prod(grid))
    r_in, r_out, r_shape = [], [], []
    for w, layer, cb in riders:
        _, r, c = w.shape
        tr = min(n for n in range(16, r + 1, 16) if r % n == 0 and r // n <= steps)

        def row_block(*g, n_rows=r // tr):
            lin = g[0]
            for gi, n in zip(g[1:], grid[1:]):
                lin = lin * n + gi
            return jnp.minimum(lin, n_rows - 1)

        r_in.append(pl.BlockSpec((1, tr, c), lambda *g, f=row_block, l=layer: (l, f(*g), 0)))
        r_out.append(pl.BlockSpec((1, c // cb, tr, cb), lambda *g, f=row_block: (0, 0, f(*g), 0)))
        r_shape.append(jax.ShapeDtypeStruct((1, c // cb, r, cb), BF16))

    def body(*refs):
        outs_at = n_in + n_r
        for w_ref, o_ref in zip(refs[n_in:outs_at], refs[outs_at + n_out:outs_at + n_out + n_r]):
            _cast_kernel(w_ref, o_ref)
        kernel_fn(*refs[:n_in], *refs[outs_at:outs_at + n_out], *refs[outs_at + n_out + n_r:])

    if riders:
        kw["compiler_params"] = _cparams(*["arbitrary"] * len(grid))
    call = pl.pallas_call(body, grid=grid, in_specs=list(in_specs) + r_in,
                          out_specs=list(out_specs) + r_out, out_shape=list(out_shape) + r_shape, **kw)

    def run(*args):
        res = call(*args, *[w for w, _, _ in riders])
        return list(res[:n_out]), list(res[n_out:])
    return run


def _as_ffn_weights(gate, up, down):
    _, _, dff, d = down.shape
    tf = gate.shape[3]
    return gate, up, down.reshape(1, dff // tf, tf, d)


def _hyb_in_kernel(x_ref, mod_ref, g_ref, wgb_ref, wgc_ref, wu_ref, wq_ref, wk_ref, wvt_ref,
                   qg_ref, kg_ref, cos_ref, sin_ref,
                   gb_ref, z_ref, q_ref, k_ref, vt_ref):
    q_scale = HEAD_DIM ** -0.5 * LOG2E
    norm = lambda rows: _modnorm(x_ref[0, rows, :], g_ref[...], mod_ref[0, 3:4, :],
                                 mod_ref[0, 4:5, :]).astype(BF16)
    chunks = _row_chunks(x_ref.shape[1], PROJ_ROW_CHUNK)
    xn_next = norm(chunks[0])
    for n, rows in enumerate(chunks):
        xn = xn_next
        if n + 1 < len(chunks):
            xn_next = norm(chunks[n + 1])
        gb_ref[0, rows, :] = _dot(xn, wgb_ref[...]).astype(BF16)
        z_ref[0, rows, :] = (_dot(xn, wgc_ref[...]) * _dot(xn, wu_ref[...])).astype(BF16)
        vt_ref[0, :, rows] = _dot_t(wvt_ref[...], xn).astype(BF16)
        cos = cos_ref[0, rows, :]
        sin = sin_ref[0, rows, :]
        q = _dot(xn, wq_ref[...])
        for h in range(q.shape[1] // HEAD_DIM):
            sl = slice(h * HEAD_DIM, (h + 1) * HEAD_DIM)
            q_ref[0, rows, sl] = (_rope(_rms(q[:, sl], qg_ref[...]), cos, sin) * q_scale).astype(BF16)
        k = _dot(xn, wk_ref[...])
        for h in range(k.shape[1] // HEAD_DIM):
            sl = slice(h * HEAD_DIM, (h + 1) * HEAD_DIM)
            k_ref[0, rows, sl] = _rope(_rms(k[:, sl], kg_ref[...]), cos, sin).astype(BF16)


def _hyb_in_call(x, mod, g, wgb, wgc, wu, wq, wk, wvt, qg, kg, cos2, sin2, riders=()):
    nb, t, d = x.shape
    tm = min(512, t)
    cc, nq, nkv = wgb.shape[1], wq.shape[1], wk.shape[1]
    row = lambda n: pl.BlockSpec((1, tm, n), lambda b, i: (b, i, 0))
    table = pl.BlockSpec((1, tm, LANES), lambda b, i: (jnp.where(b == nb - 1, 1, 0), i, 0))
    return _call_with_casts(
        _hyb_in_kernel, riders,
        grid=(nb, t // tm),
        in_specs=[
            row(d),
            pl.BlockSpec((1, N_MOD, d), lambda b, i: (b, 0, 0)),
            _resident((1, d)),
            _resident(wgb.shape), _resident(wgc.shape), _resident(wu.shape),
            _resident(wq.shape), _resident(wk.shape), _resident(wvt.shape),
            _resident((1, HEAD_DIM)), _resident((1, HEAD_DIM)),
            table, table,
        ],
        out_specs=[row(cc), row(cc), row(nq), row(nkv),
                   pl.BlockSpec((1, nkv, tm), lambda b, i: (b, 0, i))],
        out_shape=[jax.ShapeDtypeStruct((nb, t, n), BF16) for n in (cc, cc, nq, nkv)]
        + [jax.ShapeDtypeStruct((nb, nkv, t), BF16)],
        compiler_params=_cparams("parallel", "parallel"),
        name="hyb_in_proj",
    )(x, mod, g.reshape(1, d), wgb, wgc, wu, wq, wk, wvt,
      qg.reshape(1, HEAD_DIM), kg.reshape(1, HEAD_DIM), cos2, sin2)


def _softmax_pv(s_list, kv_list):
    m = s_list[0].max(axis=-1, keepdims=True)
    for s in s_list[1:]:
        m = jnp.maximum(m, s.max(axis=-1, keepdims=True))
    l = 0.0
    o = 0.0
    for s, (_, v) in zip(s_list, kv_list):
        p = jnp.exp2(s - m)
        l = l + p.sum(axis=-1, keepdims=True)
        o = o + _dot(p.astype(BF16), v)
    return o / l


def _attend_chains(chains, kv_list):
    scores = lambda load_q: [_dot_t(load_q(), k) for k, _ in kv_list]
    ahead = [scores(load_q) for load_q, _ in chains[:ATTN_SCORE_LOOKAHEAD]]
    for n, (_, store_o) in enumerate(chains):
        s_list = ahead.pop(0)
        if n + ATTN_SCORE_LOOKAHEAD < len(chains):
            ahead.append(scores(chains[n + ATTN_SCORE_LOOKAHEAD][0]))
        store_o(_softmax_pv(s_list, kv_list).astype(BF16))


def _softmax_pv_t(st_list, kvt_list):
    m = st_list[0].max(axis=0, keepdims=True)
    for s in st_list[1:]:
        m = jnp.maximum(m, s.max(axis=0, keepdims=True))
    l = 0.0
    o = 0.0
    for s, (_, vt) in zip(st_list, kvt_list):
        p = jnp.exp2(s - m)
        l = l + p.sum(axis=0, keepdims=True)
        o = o + _dot(vt, p.astype(BF16))
    return (o / l).T


def _attend_chains_t(chains, kvt_list):
    scores = lambda load_q: [_dot_t(k, load_q()) for k, _ in kvt_list]
    ahead = [scores(load_q) for load_q, _ in chains[:ATTN_SCORE_LOOKAHEAD]]
    for n, (_, store_o) in enumerate(chains):
        st_list = ahead.pop(0)
        if n + ATTN_SCORE_LOOKAHEAD < len(chains):
            ahead.append(scores(chains[n + ATTN_SCORE_LOOKAHEAD][0]))
        store_o(_softmax_pv_t(st_list, kvt_list).astype(BF16))


def _row_chunks(n, c):
    c = min(c, n)
    return [slice(r * c, (r + 1) * c) for r in range(n // c)]


def _gqa_kernel(*refs, has_lat):
    if has_lat:
        q_ref, kc_ref, vtc_ref, kl_ref, vtl_ref, o_ref = refs
    else:
        q_ref, kc_ref, vtc_ref, o_ref = refs
    kvt_list = [(kc_ref[0], vtc_ref[0])]
    if has_lat:
        kvt_list.append((kl_ref[0], vtl_ref[0]))

    def chain(rows, sl):
        def store(o):
            o_ref[0, rows, sl] = o
        return (lambda: q_ref[0, rows, sl]), store

    _attend_chains_t([chain(rows, slice(h * HEAD_DIM, (h + 1) * HEAD_DIM))
                      for h in range(q_ref.shape[2] // HEAD_DIM)
                      for rows in _row_chunks(q_ref.shape[1], GQA_ROW_CHUNK)], kvt_list)


def _gqa_latent_call(q, k, vt, b_lat, tc, riders=()):
    nb, t, nq = q.shape
    gw = nq // GQA_KV_HEADS
    tq = min(1024, t)
    ctx = pl.BlockSpec((1, tc, HEAD_DIM), lambda b, g, i: (b_lat, b, g))
    lat = pl.BlockSpec((1, t, HEAD_DIM), lambda b, g, i: (b, 0, g))
    vt_ctx = pl.BlockSpec((1, HEAD_DIM, tc), lambda b, g, i: (b_lat, g, b))
    vt_lat = pl.BlockSpec((1, HEAD_DIM, t), lambda b, g, i: (b, g, 0))
    qo = pl.BlockSpec((1, tq, gw), lambda b, g, i: (b, i, g))
    return _call_with_casts(
        functools.partial(_gqa_kernel, has_lat=True), riders,
        grid=(b_lat, GQA_KV_HEADS, t // tq),
        in_specs=[qo, ctx, vt_ctx, lat, vt_lat],
        out_specs=[qo],
        out_shape=[jax.ShapeDtypeStruct((b_lat, t, nq), BF16)],
        compiler_params=_cparams("parallel", "parallel", "parallel"),
        name="gqa_latent",
    )(q, k, vt, k, vt)


def _gqa_ctx_call(q, k, vt, b_lat, tc):
    nb, t, nq = q.shape
    gw = nq // GQA_KV_HEADS
    ctx = pl.BlockSpec((1, tc, HEAD_DIM), lambda b, g: (b_lat, b, g))
    vt_ctx = pl.BlockSpec((1, HEAD_DIM, tc), lambda b, g: (b_lat, g, b))
    return pl.pallas_call(
        functools.partial(_gqa_kernel, has_lat=False),
        grid=(t // tc, GQA_KV_HEADS),
        in_specs=[pl.BlockSpec((1, tc, gw), lambda b, g: (b_lat, b, g)), ctx, vt_ctx],
        out_specs=pl.BlockSpec((1, tc, gw), lambda b, g: (0, b, g)),
        out_shape=jax.ShapeDtypeStruct((1, t, nq), BF16),
        compiler_params=_cparams("parallel", "parallel"),
        name="gqa_context",
    )(q, k, vt)


def _hyb_out_kernel(x_ref, mod_ref, gb_ref, z_ref, zp_ref, zn_ref, att_ref, attc_ref, cw_ref, w_ref,
                    o_ref, *, seq_lat, seq_ctx, b_lat):
    b, i = pl.program_id(0), pl.program_id(1)
    tm, cc = z_ref.shape[1], z_ref.shape[2]
    acc = _dot(jnp.where(b == b_lat, attc_ref[0], att_ref[0]), w_ref[cc:, :])
    z = z_ref[0].astype(F32)
    rows = lax.broadcasted_iota(jnp.int32, (tm, cc), 0)
    seq = jnp.where(b == b_lat, seq_ctx, seq_lat)
    pos = (i * tm + rows) & (seq - 1)
    z_prev = jnp.where(rows == 0, zp_ref[0, 7:8, :].astype(F32), pltpu.roll(z, 1, axis=0))
    z_next = jnp.where(rows == tm - 1, zn_ref[0, 0:1, :].astype(F32),
                       pltpu.roll(z, tm - 1, axis=0))
    z_prev = jnp.where(pos == 0, 0.0, z_prev)
    z_next = jnp.where(pos == seq - 1, 0.0, z_next)
    conv = z_prev * cw_ref[0:1, :] + z * cw_ref[1:2, :] + z_next * cw_ref[2:3, :]
    acc = acc + _dot((gb_ref[0].astype(F32) * conv).astype(BF16), w_ref[:cc, :])
    o_ref[0] = x_ref[0] + mod_ref[0, 5:6, :] * acc


def _hyb_out_call(x, mod, gb, z, att, att_c, conv_w, w_out, b_lat, tc, riders=()):
    nb, t, d = x.shape
    cc = gb.shape[2]
    tm = min(512, t)
    nt = t // tm
    hb = tm // 8
    row = lambda n: pl.BlockSpec((1, tm, n), lambda b, i: (b, i, 0))
    nq = att.shape[2]
    att_spec = pl.BlockSpec((1, tm, nq), lambda b, i: (jnp.minimum(b, b_lat - 1),
                                                       jnp.where(b < b_lat, i, nt - 1), 0))
    attc_spec = pl.BlockSpec((1, tm, nq), lambda b, i: (0, jnp.where(b < b_lat, 0, i), 0))
    return _call_with_casts(
        functools.partial(_hyb_out_kernel, seq_lat=t, seq_ctx=tc, b_lat=b_lat), riders,
        grid=(nb, t // tm),
        in_specs=[
            row(d),
            pl.BlockSpec((1, N_MOD, d), lambda b, i: (b, 0, 0)),
            row(cc), row(cc),
            pl.BlockSpec((1, 8, cc), lambda b, i: (b, jnp.maximum(i * hb - 1, 0), 0)),
            pl.BlockSpec((1, 8, cc), lambda b, i: (b, jnp.minimum((i + 1) * hb, t // 8 - 1), 0)),
            att_spec, attc_spec,
            _resident(conv_w.shape),
            _resident(w_out.shape),
        ],
        out_specs=[row(d)],
        out_shape=[jax.ShapeDtypeStruct((nb, t, d), F32)],
        compiler_params=_cparams("parallel", "parallel"),
        name="hyb_out_proj",
    )(x, mod, gb, z, z, z, att, att_c, conv_w, w_out)


def _mla_in_kernel(x_ref, mod_ref, g_ref, wd_ref, qg_ref, kvg_ref, wuq_ref, wuk_ref, wuvt_ref,
                   cos_ref, sin_ref, qn_ref, qr_ref, kn_ref, vt_ref, kr_ref, *, q_rank, kv_rank):
    n_nope = qn_ref.shape[2]
    q_scale = (MLA_NOPE + MLA_ROPE) ** -0.5 * LOG2E
    down = lambda rows: _dot(_modnorm(x_ref[0, rows, :], g_ref[...], mod_ref[0, 3:4, :],
                                      mod_ref[0, 4:5, :]).astype(BF16), wd_ref[...])
    chunks = _row_chunks(x_ref.shape[1], PROJ_ROW_CHUNK)
    d_next = down(chunks[0])
    for n, rows in enumerate(chunks):
        d = d_next
        if n + 1 < len(chunks):
            d_next = down(chunks[n + 1])
        cos = cos_ref[0, rows, :]
        sin = sin_ref[0, rows, :]
        q = _dot(_rms(d[:, :q_rank], qg_ref[...]).astype(BF16), wuq_ref[...])
        qn_ref[0, rows, :] = (q[:, :n_nope] * q_scale).astype(BF16)
        for j in range(qr_ref.shape[2] // LANES):
            qr_ref[0, rows, j * LANES:(j + 1) * LANES] = (
                _rope(q[:, n_nope + j * LANES:n_nope + (j + 1) * LANES], cos, sin) * q_scale
            ).astype(BF16)
        ckv = _rms(d[:, q_rank:q_rank + kv_rank], kvg_ref[...]).astype(BF16)
        kn_ref[0, rows, :] = _dot(ckv, wuk_ref[...]).astype(BF16)
        vt_ref[0, :, rows] = _dot_t(wuvt_ref[...], ckv).astype(BF16)
        kr = _rope(d[:, q_rank + kv_rank:], cos, sin)
        first = (lax.broadcasted_iota(jnp.int32, kr.shape, 1) & (MLA_ROPE // 2)) == 0
        kr_ref[0, rows, :LANES] = jnp.where(first, kr, 0.0).astype(BF16)
        kr_ref[0, rows, LANES:] = jnp.where(first, 0.0, kr).astype(BF16)


def _mla_in_call(x, mod, g, wd, qg, kvg, wuq, wuk, wuvt, cos2, sin2, riders=()):
    nb, t, d = x.shape
    tm = min(512, t)
    q_rank, kv_rank = qg.shape[0], kvg.shape[0]
    n_nope = wuk.shape[1]
    n_rope = wuq.shape[1] - n_nope
    n_v = wuvt.shape[0]
    row = lambda n: pl.BlockSpec((1, tm, n), lambda b, i: (b, i, 0))
    table = pl.BlockSpec((1, tm, LANES), lambda b, i: (jnp.where(b == nb - 1, 1, 0), i, 0))
    out_specs = [row(n_nope), row(n_rope), row(n_nope),
                 pl.BlockSpec((1, n_v, tm), lambda b, i: (b, 0, i)), row(2 * LANES)]
    out_shapes = [(nb, t, n_nope), (nb, t, n_rope), (nb, t, n_nope), (nb, n_v, t), (nb, t, 2 * LANES)]
    return _call_with_casts(
        functools.partial(_mla_in_kernel, q_rank=q_rank, kv_rank=kv_rank), riders,
        grid=(nb, t // tm),
        in_specs=[
            row(d),
            pl.BlockSpec((1, N_MOD, d), lambda b, i: (b, 0, 0)),
            _resident((1, d)),
            _resident(wd.shape), _resident((1, q_rank)), _resident((1, kv_rank)),
            _resident(wuq.shape), _resident(wuk.shape), _resident(wuvt.shape),
            table, table,
        ],
        out_specs=out_specs,
        out_shape=[jax.ShapeDtypeStruct(s, BF16) for s in out_shapes],
        compiler_params=_cparams("parallel", "parallel"),
        name="mla_in_proj",
    )(x, mod, g.reshape(1, d), wd, qg.reshape(1, q_rank), kvg.reshape(1, kv_rank), wuq, wuk, wuvt,
      cos2, sin2)


def _mla_attn_kernel(qn_ref, qr_ref, knc_ref, krc_ref, vtc_ref, knl_ref, krl_ref, vtl_ref,
                     o_ref, k_ref, vt_ref):
    tc = knc_ref.shape[1]

    @pl.when(pl.program_id(2) == 0)
    def _():
        k_ref[:tc, :LANES] = knc_ref[0]
        k_ref[:tc, LANES:] = krc_ref[0]
        k_ref[tc:, :LANES] = knl_ref[0]
        k_ref[tc:, LANES:] = krl_ref[0]
        vt_ref[:, :tc] = vtc_ref[0]
        vt_ref[:, tc:] = vtl_ref[0]

    def chain(rows):
        def store(o):
            o_ref[0, rows, :] = o
        return (lambda: jnp.concatenate([qn_ref[0, rows, :], qr_ref[0, rows, :]], axis=1)), store

    _attend_chains_t([chain(rows) for rows in _row_chunks(qn_ref.shape[1], MLA_ROW_CHUNK)],
                     [(k_ref[...], vt_ref[...])])


def _mla_attn_call(qn, qr, kn, vt, kr, b_lat, tc, riders=()):
    nb, t, n_nope = qn.shape
    heads = n_nope // MLA_NOPE
    tq = min(2048, t)
    qspec = pl.BlockSpec((1, tq, LANES), lambda b, h, i: (b, i, h))
    qrspec = pl.BlockSpec((1, tq, LANES), lambda b, h, i: (b, i, h // 2))
    ctx = lambda f: pl.BlockSpec((1, tc, LANES), lambda b, h, i: (b_lat, b, f(h)))
    lat = lambda f: pl.BlockSpec((1, t, LANES), lambda b, h, i: (b, 0, f(h)))
    vt_ctx = pl.BlockSpec((1, MLA_V, tc), lambda b, h, i: (b_lat, h, b))
    vt_lat = pl.BlockSpec((1, MLA_V, t), lambda b, h, i: (b, h, 0))
    same = lambda h: h
    parity = lambda h: h % 2
    return _call_with_casts(
        _mla_attn_kernel, riders,
        grid=(b_lat, heads, t // tq),
        in_specs=[qspec, qrspec, ctx(same), ctx(parity), vt_ctx, lat(same), lat(parity), vt_lat],
        out_specs=[qspec],
        out_shape=[jax.ShapeDtypeStruct((b_lat, t, n_nope), BF16)],
        scratch_shapes=[pltpu.VMEM((tc + t, 2 * LANES), BF16), pltpu.VMEM((MLA_V, tc + t), BF16)],
        compiler_params=_cparams("parallel", "parallel", "arbitrary"),
        name="mla_attention",
    )(qn, qr, kn, kr, vt, kn, kr, vt)


def _mla_out_kernel(x_ref, mod_ref, att_ref, w_ref, o_ref):
    o_ref[0] = x_ref[0] + mod_ref[0, 5:6, :] * _dot(att_ref[0], w_ref[...])


def _mla_out_call(x, mod, att, w_o, riders=()):
    nb, t, k = att.shape
    d = w_o.shape[1]
    tm = min(512, t)
    row = lambda n: pl.BlockSpec((1, tm, n), lambda b, i: (b, i, 0))
    return _call_with_casts(
        _mla_out_kernel, riders,
        grid=(nb, t // tm),
        in_specs=[row(d), pl.BlockSpec((1, N_MOD, d), lambda b, i: (b, 0, 0)), row(k),
                  _resident(w_o.shape)],
        out_specs=[row(d)],
        out_shape=[jax.ShapeDtypeStruct((nb, t, d), F32)],
        compiler_params=_cparams("parallel", "parallel"),
        name="mla_out_proj",
    )(x, mod, att, w_o)


def _rope_angles(n_tok, dim):
    n_rows = n_tok // GRID_W
    row = jnp.repeat(jnp.arange(n_rows), GRID_W).astype(F32)
    col = jnp.tile(jnp.arange(GRID_W), n_rows).astype(F32)
    half = dim // 2
    inv = 1.0 / (ROPE_THETA ** (jnp.arange(0, half, 2, dtype=F32) / half))
    return jnp.concatenate([row[:, None] * inv, col[:, None] * inv], axis=-1)


def _rope_tables(n_tok, dim):
    ang = _rope_angles(n_tok, dim)
    reps = LANES // dim
    cos = jnp.concatenate([jnp.cos(ang)] * (2 * reps), axis=-1)
    sin = jnp.concatenate([-jnp.sin(ang)] * reps + [jnp.sin(ang)] * reps, axis=-1)
    return (jnp.stack([cos, jnp.ones_like(cos)]), jnp.stack([sin, jnp.zeros_like(sin)]))


def _deinterleave(n):
    return np.concatenate([np.arange(0, n, 2), np.arange(1, n, 2)])


def kernel(x, c, ctx, c_ctx, mod_w, mod_b, norm_ffn1, norm_mix, norm_ffn2,
           ffn1_w_gate, ffn1_w_up, ffn1_w_down, ffn2_w_gate, ffn2_w_up, ffn2_w_down,
           hyb_w_in, hyb_conv_w, hyb_q_norm, hyb_k_norm, hyb_w_out,
           mla_w_down, mla_q_norm, mla_kv_norm, mla_w_uq, mla_w_ukv, mla_w_o, final_norm):
    b_lat, t, d = x.shape
    tc = ctx.shape[1]
    depth = mod_w.shape[0]
    assert depth == 2 and ctx.shape[0] * tc == t and b_lat + 1 <= MOD_ROWS
    assert t & (t - 1) == 0 and tc & (tc - 1) == 0 and t % GRID_W == 0
    nb = b_lat + 1

    cvec = jnp.concatenate([c, c_ctx[None], jnp.zeros((MOD_ROWS - nb, d), F32)], axis=0)
    mods = _mod_call(cvec, mod_w, mod_b).reshape(depth, MOD_ROWS, N_MOD, d)
    cast = lambda w: w.astype(BF16)
    tf = min(FFN_TILE, ffn1_w_gate.shape[2])
    f1 = _as_ffn_weights(_cast_call(ffn1_w_gate, 0, tf), _cast_call(ffn1_w_up, 0, tf),
                         _cast_call(ffn1_w_down, 0, d))

    mod = mods[0]
    xs = _ffn_call(x, nb, mod, norm_ffn1[0], *f1, 0, 0, ctx=ctx.reshape(1, t, d))
    cc = d // 2
    n_q = cc
    n_kv = GQA_KV_HEADS * HEAD_DIM
    w_in = hyb_w_in[0]
    perm = _deinterleave(HEAD_DIM)
    pairs_last = lambda w: w.reshape(d, -1, HEAD_DIM // 2, 2).swapaxes(2, 3).reshape(d, -1)
    wq = pairs_last(w_in[:, 3 * cc:3 * cc + n_q])
    wk = pairs_last(w_in[:, 3 * cc + n_q:3 * cc + n_q + n_kv])
    wvt = w_in[:, 3 * cc + n_q + n_kv:].T
    cos_a, sin_a = _rope_tables(t, HEAD_DIM)
    (gb, z, q, k, v), _ = _hyb_in_call(
        xs, mod, norm_mix[0], cast(w_in[:, :cc]), cast(w_in[:, cc:2 * cc]), cast(w_in[:, 2 * cc:3 * cc]),
        cast(wq), cast(wk), cast(wvt), hyb_q_norm[0][perm], hyb_k_norm[0][perm], cos_a, sin_a)
    (att,), casts = _gqa_latent_call(
        q, k, v, b_lat, tc,
        riders=[(ffn2_w_gate, 0, tf), (ffn1_w_gate, 1, tf), (ffn2_w_up, 0, tf), (ffn1_w_up, 1, tf),
                (ffn2_w_down, 0, d), (ffn1_w_down, 1, d)])
    gates, ups, downs = casts[0:2], casts[2:4], casts[4:6]
    att_c = _gqa_ctx_call(q, k, v, b_lat, tc)
    (xs,), _ = _hyb_out_call(xs, mod, gb, z, att, att_c, hyb_conv_w[0], cast(hyb_w_out[0]), b_lat, tc)
    xs = _ffn_call(xs, nb, mod, norm_ffn2[0], *_as_ffn_weights(gates[0], ups[0], downs[0]), 0, 6)

    mod = mods[1]
    xs = _ffn_call(xs, nb, mod, norm_ffn1[1], *_as_ffn_weights(gates[1], ups[1], downs[1]), 0, 0)
    heads = d // HEAD_DIM
    q_rank, kv_rank = mla_q_norm.shape[1], mla_kv_norm.shape[1]
    uq = mla_w_uq[0].reshape(q_rank, heads, MLA_NOPE + MLA_ROPE)
    uq_rope = uq[:, :, MLA_NOPE:].reshape(q_rank, heads // 2, 2, MLA_ROPE // 2, 2)
    wuq = jnp.concatenate([uq[:, :, :MLA_NOPE].reshape(q_rank, -1),
                           uq_rope.transpose(0, 1, 4, 2, 3).reshape(q_rank, -1)], axis=1)
    ukv = mla_w_ukv[0].reshape(kv_rank, heads, MLA_NOPE + MLA_V)
    wuk = ukv[:, :, :MLA_NOPE].reshape(kv_rank, -1)
    wuvt = ukv[:, :, MLA_NOPE:].reshape(kv_rank, -1).T
    kr0 = q_rank + kv_rank
    w_kr = mla_w_down[0][:, kr0:].reshape(d, MLA_ROPE // 2, 2)
    wdn = jnp.concatenate([mla_w_down[0][:, :kr0], w_kr[:, :, 0], w_kr[:, :, 0],
                           w_kr[:, :, 1], w_kr[:, :, 1]], axis=1)
    cos_m, sin_m = _rope_tables(t, MLA_ROPE)
    (qn, qr, kn, vt, kr), _ = _mla_in_call(
        xs, mod, norm_mix[1], cast(wdn), mla_q_norm[0], mla_kv_norm[0],
        cast(wuq), cast(wuk), cast(wuvt), cos_m, sin_m)
    (att,), casts = _mla_attn_call(
        qn, qr, kn, vt, kr, b_lat, tc,
        riders=[(ffn2_w_gate, 1, tf), (ffn2_w_up, 1, tf), (ffn2_w_down, 1, d)])
    (xs,), _ = _mla_out_call(xs, mod, att, cast(mla_w_o[0]))
    return _ffn_call(xs, b_lat, mod, norm_ffn2[1], *_as_ffn_weights(*casts), 0, 6,
                     final_g=final_norm)
```

```python
import functools

import numpy as np
import jax
import jax.numpy as jnp
from jax import lax
from jax.experimental import pallas as pl
from jax.experimental.pallas import tpu as pltpu

HEAD_DIM = 128
GRID_W = 64
ROPE_THETA = 10000.0
EPS = 1e-6
N_MOD = 9
GQA_KV_HEADS = 2
MLA_NOPE = 128
MLA_ROPE = 64
MLA_V = 128
LANES = 128
VMEM_LIMIT = 60 * 1024 * 1024
MOD_ROWS = 16
FFN_ROW_CHUNKS = 4
FFN_TILE = 512
FFN_SLOTS = 4
PROJ_ROW_CHUNK = 256
GQA_ROW_CHUNK = 512
MLA_ROW_CHUNK = 512
ATTN_SCORE_LOOKAHEAD = 2
LOG2E = 1.4426950408889634
CAST_BLOCK_BYTES = 4 * 1024 * 1024

F32 = jnp.float32
BF16 = jnp.bfloat16


def _cparams(*sem):
    return pltpu.CompilerParams(dimension_semantics=sem, vmem_limit_bytes=VMEM_LIMIT)


def _resident(shape):
    nd = len(shape)
    return pl.BlockSpec(shape, lambda *_: (0,) * nd, pipeline_mode=pl.Buffered(1))


def _rms(x, g):
    return x * lax.rsqrt(jnp.mean(x * x, axis=-1, keepdims=True) + EPS) * g


def _modnorm(x, g, shift, scale):
    return _rms(x, g) * (1.0 + scale) + shift


def _rope(x, cos, sin):
    return x * cos + pltpu.roll(x, LANES // 2, axis=1) * sin


def _dot(a, b):
    return jnp.dot(a, b, preferred_element_type=F32)


def _dot_t(a, b):
    return lax.dot_general(a, b, (((1,), (1,)), ((), ())), preferred_element_type=F32)


def _mod_kernel(c_ref, w_ref, b_ref, o_ref):
    c = c_ref[...]
    s = (c * jax.nn.sigmoid(c)).astype(BF16)
    o_ref[0] = _dot(s, w_ref[0].astype(BF16)) + b_ref[0]


def _mod_call(cvec, mod_w, mod_b):
    depth, d, n = mod_w.shape
    tn = min(1024, n)
    return pl.pallas_call(
        _mod_kernel,
        grid=(depth, n // tn),
        in_specs=[
            pl.BlockSpec((MOD_ROWS, d), lambda l, j: (0, 0)),
            pl.BlockSpec((1, d, tn), lambda l, j: (l, 0, j)),
            pl.BlockSpec((1, 1, tn), lambda l, j: (l, 0, j)),
        ],
        out_specs=pl.BlockSpec((1, MOD_ROWS, tn), lambda l, j: (l, 0, j)),
        out_shape=jax.ShapeDtypeStruct((depth, MOD_ROWS, n), F32),
        compiler_params=_cparams("parallel", "parallel"),
        name="adaln_mod",
    )(cvec, mod_w, mod_b.reshape(depth, 1, n))


def _swiglu_act(xn, wg_ref, wu_ref):
    hg = _dot(xn, wg_ref[...])
    hu = _dot(xn, wu_ref[...])
    return (hg * jax.nn.sigmoid(hg) * hu).astype(BF16)


_GATE_UP, _DOWN = "gate_up", "down"


def _ring_plan(n_ff):
    assert FFN_SLOTS >= 4
    plan = {}
    for j in range(0, n_ff, 2):
        waits = ([(_GATE_UP, t) for t in (j + 1, j + 2) if t < n_ff]
                 + [(_DOWN, t) for t in (j, j + 1) if t < n_ff])
        starts = ([(_GATE_UP, t) for t in (j + 3, j + 4) if t < n_ff]
                  + [(_DOWN, t) for t in (j + 2, j + 3) if t < n_ff])
        plan[j] = (waits, starts, [])
    first = [(_GATE_UP, t) for t in range(min(3, n_ff))] + [(_DOWN, t) for t in range(min(2, n_ff))]
    for kind, tile in first:
        last_user = max(t for t in range(n_ff) if t % FFN_SLOTS == tile % FFN_SLOTS)
        free_from = last_user if kind == _GATE_UP else last_user + 1
        barrier = free_from + free_from % 2
        assert barrier in plan, "no barrier left to prefetch the next grid step's tile"
        plan[barrier][2].append((kind, tile))
    return plan, first


def _ffn_kernel(x_ref, mod_ref, g_ref, wg_hbm, wu_hbm, wd_hbm, *rest, layer, m0, final, with_ctx):
    rest = list(rest)
    fn_ref = rest.pop(0) if final else None
    ctx_ref = rest.pop(0) if with_ctx else None
    o_ref, xn_ref, h_ref, wg_buf, wu_buf, wd_buf, sem = rest
    is_ctx = pl.program_id(0) == pl.num_programs(0) - 1
    n_ff = wg_hbm.shape[1]
    tm = xn_ref.shape[0]
    step = pl.program_id(0) * pl.num_programs(1) + pl.program_id(1)
    not_last_step = step < pl.num_programs(0) * pl.num_programs(1) - 1
    half_gate = 0.5 * mod_ref[0, m0 + 2:m0 + 3, :]

    def copies(kind, blk):
        slot = blk % FFN_SLOTS
        pairs = (((0, wg_hbm, wg_buf), (1, wu_hbm, wu_buf)) if kind == _GATE_UP
                 else ((2, wd_hbm, wd_buf),))
        return [pltpu.make_async_copy(hbm.at[layer, blk], buf.at[slot], sem.at[k, slot])
                for k, hbm, buf in pairs]

    def start(kind, blk):
        for c in copies(kind, blk):
            c.start()

    def wait(kind, blk):
        for c in copies(kind, blk):
            c.wait()

    def x_rows(rows):
        x = x_ref[0, rows, :]
        return jnp.where(is_ctx, ctx_ref[0, rows, :], x) if with_ctx else x

    plan, first = _ring_plan(n_ff)

    @pl.when(step == 0)
    def _():
        for item in first:
            start(*item)

    wait(_GATE_UP, 0)
    rs = tm // FFN_ROW_CHUNKS
    for r in range(FFN_ROW_CHUNKS):
        rows = slice(r * rs, (r + 1) * rs)
        xn = _modnorm(x_rows(rows), g_ref[...], mod_ref[0, m0:m0 + 1, :],
                      mod_ref[0, m0 + 1:m0 + 2, :]).astype(BF16)
        xn_ref[rows, :] = xn
        h_ref[0, rows, :] = _swiglu_act(xn, wg_buf.at[0], wu_buf.at[0])

    for j in range(n_ff):
        if j in plan:
            waits, starts, next_starts = plan[j]
            for item in starts:
                start(*item)
            if next_starts:
                pl.when(not_last_step)(
                    lambda items=next_starts: [start(*item) for item in items] and None)
            for item in waits:
                wait(*item)
        if j + 1 < n_ff:
            nslot = (j + 1) % FFN_SLOTS
            h_ref[(j + 1) % 2] = _swiglu_act(xn_ref[...], wg_buf.at[nslot], wu_buf.at[nslot])
        contrib = half_gate * _dot(h_ref[j % 2], wd_buf[j % FFN_SLOTS])
        if j == 0:
            o_ref[0] = x_rows(slice(None)) + contrib
        else:
            o_ref[0] += contrib

    if final:
        o_ref[0] = _rms(o_ref[0], fn_ref[...])


def _ffn_call(x, nb, mod, g, wg, wu, wd, layer, m0, final_g=None, ctx=None):
    _, t, d = x.shape
    _, n_ff, _, tf = wg.shape
    _ring_plan(n_ff)
    final = final_g is not None
    tm = min(512, t)
    nt = t // tm
    if ctx is None:
        x_spec = pl.BlockSpec((1, tm, d), lambda b, i: (b, i, 0))
    else:
        x_spec = pl.BlockSpec((1, tm, d), lambda b, i: (jnp.minimum(b, nb - 2),
                                                         jnp.where(b < nb - 1, i, nt - 1), 0))
    in_specs = [
        x_spec,
        pl.BlockSpec((1, N_MOD, d), lambda b, i: (b, 0, 0)),
        pl.BlockSpec((1, d), lambda b, i: (0, 0)),
        pl.BlockSpec(memory_space=pl.ANY),
        pl.BlockSpec(memory_space=pl.ANY),
        pl.BlockSpec(memory_space=pl.ANY),
    ]
    args = [x, mod, g.reshape(1, d), wg, wu, wd]
    if final:
        in_specs.append(pl.BlockSpec((1, d), lambda b, i: (0, 0)))
        args.append(final_g.reshape(1, d))
    if ctx is not None:
        in_specs.append(pl.BlockSpec((1, tm, d), lambda b, i: (0, jnp.where(b < nb - 1, 0, i), 0)))
        args.append(ctx)
    return pl.pallas_call(
        functools.partial(_ffn_kernel, layer=layer, m0=m0, final=final, with_ctx=ctx is not None),
        grid=(nb, nt),
        in_specs=in_specs,
        out_specs=pl.BlockSpec((1, tm, d), lambda b, i: (b, i, 0)),
        out_shape=jax.ShapeDtypeStruct((nb, t, d), F32),
        scratch_shapes=[
            pltpu.VMEM((tm, d), BF16),
            pltpu.VMEM((2, tm, tf), BF16),
            pltpu.VMEM((FFN_SLOTS, d, tf), BF16), pltpu.VMEM((FFN_SLOTS, d, tf), BF16),
            pltpu.VMEM((FFN_SLOTS, tf, d), BF16),
            pltpu.SemaphoreType.DMA((3, FFN_SLOTS)),
        ],
        compiler_params=_cparams("arbitrary", "arbitrary"),
        name="swiglu_half_step",
    )(*args)


def _cast_kernel(w_ref, o_ref):
    cb = o_ref.shape[3]
    for n in range(o_ref.shape[1]):
        o_ref[0, n] = w_ref[0, :, n * cb:(n + 1) * cb].astype(BF16)


def _cast_call(w, layer, col_block):
    _, r, c = w.shape
    nblk = c // col_block
    tr = max(n for n in range(16, r + 1, 16) if r % n == 0 and 4 * c * n <= CAST_BLOCK_BYTES)
    return pl.pallas_call(
        _cast_kernel,
        grid=(r // tr,),
        in_specs=[pl.BlockSpec((1, tr, c), lambda i: (layer, i, 0))],
        out_specs=pl.BlockSpec((1, nblk, tr, col_block), lambda i: (0, 0, i, 0)),
        out_shape=jax.ShapeDtypeStruct((1, nblk, r, col_block), BF16),
        compiler_params=_cparams("parallel"),
        name="weight_cast",
    )(w)


def _call_with_casts(kernel_fn, riders, *, grid, in_specs, out_specs, out_shape, **kw):
    n_in, n_out, n_r = len(in_specs), len(out_specs), len(riders)
    steps = int(np.prod(grid))
    r_in, r_out, r_shape = [], [], []
    for w, layer, cb in riders:
        _, r, c = w.shape
        tr = min(n for n in range(16, r + 1, 16) if r % n == 0 and r // n <= steps)

        def row_block(*g, n_rows=r // tr):
            lin = g[0]
            for gi, n in zip(g[1:], grid[1:]):
                lin = lin * n + gi
            return jnp.minimum(lin, n_rows - 1)

        r_in.append(pl.BlockSpec((1, tr, c), lambda *g, f=row_block, l=layer: (l, f(*g), 0)))
        r_out.append(pl.BlockSpec((1, c // cb, tr, cb), lambda *g, f=row_block: (0, 0, f(*g), 0)))
        r_shape.append(jax.ShapeDtypeStruct((1, c // cb, r, cb), BF16))

    def body(*refs):
        outs_at = n_in + n_r
        for w_ref, o_ref in zip(refs[n_in:outs_at], refs[outs_at + n_out:outs_at + n_out + n_r]):
            _cast_kernel(w_ref, o_ref)
        kernel_fn(*refs[:n_in], *refs[outs_at:outs_at + n_out], *refs[outs_at + n_out + n_r:])

    if riders:
        kw["compiler_params"] = _cparams(*["arbitrary"] * len(grid))
    call = pl.pallas_call(body, grid=grid, in_specs=list(in_specs) + r_in,
                          out_specs=list(out_specs) + r_out, out_shape=list(out_shape) + r_shape, **kw)

    def run(*args):
        res = call(*args, *[w for w, _, _ in riders])
        return list(res[:n_out]), list(res[n_out:])
    return run


def _as_ffn_weights(gate, up, down):
    _, _, dff, d = down.shape
    tf = gate.shape[3]
    return gate, up, down.reshape(1, dff // tf, tf, d)


def _hyb_in_kernel(x_ref, mod_ref, g_ref, wgb_ref, wgc_ref, wu_ref, wq_ref, wk_ref, wvt_ref,
                   qg_ref, kg_ref, cos_ref, sin_ref,
                   gb_ref, z_ref, q_ref, k_ref, vt_ref):
    q_scale = HEAD_DIM ** -0.5 * LOG2E
    norm = lambda rows: _modnorm(x_ref[0, rows, :], g_ref[...], mod_ref[0, 3:4, :],
                                 mod_ref[0, 4:5, :]).astype(BF16)
    chunks = _row_chunks(x_ref.shape[1], PROJ_ROW_CHUNK)
    xn_next = norm(chunks[0])
    for n, rows in enumerate(chunks):
        xn = xn_next
        if n + 1 < len(chunks):
            xn_next = norm(chunks[n + 1])
        gb_ref[0, rows, :] = _dot(xn, wgb_ref[...]).astype(BF16)
        z_ref[0, rows, :] = (_dot(xn, wgc_ref[...]) * _dot(xn, wu_ref[...])).astype(BF16)
        vt_ref[0, :, rows] = _dot_t(wvt_ref[...], xn).astype(BF16)
        cos = cos_ref[0, rows, :]
        sin = sin_ref[0, rows, :]
        q = _dot(xn, wq_ref[...])
        for h in range(q.shape[1] // HEAD_DIM):
            sl = slice(h * HEAD_DIM, (h + 1) * HEAD_DIM)
            q_ref[0, rows, sl] = (_rope(_rms(q[:, sl], qg_ref[...]), cos, sin) * q_scale).astype(BF16)
        k = _dot(xn, wk_ref[...])
        for h in range(k.shape[1] // HEAD_DIM):
            sl = slice(h * HEAD_DIM, (h + 1) * HEAD_DIM)
            k_ref[0, rows, sl] = _rope(_rms(k[:, sl], kg_ref[...]), cos, sin).astype(BF16)


def _hyb_in_call(x, mod, g, wgb, wgc, wu, wq, wk, wvt, qg, kg, cos2, sin2, riders=()):
    nb, t, d = x.shape
    tm = min(512, t)
    cc, nq, nkv = wgb.shape[1], wq.shape[1], wk.shape[1]
    row = lambda n: pl.BlockSpec((1, tm, n), lambda b, i: (b, i, 0))
    table = pl.BlockSpec((1, tm, LANES), lambda b, i: (jnp.where(b == nb - 1, 1, 0), i, 0))
    return _call_with_casts(
        _hyb_in_kernel, riders,
        grid=(nb, t // tm),
        in_specs=[
            row(d),
            pl.BlockSpec((1, N_MOD, d), lambda b, i: (b, 0, 0)),
            _resident((1, d)),
            _resident(wgb.shape), _resident(wgc.shape), _resident(wu.shape),
            _resident(wq.shape), _resident(wk.shape), _resident(wvt.shape),
            _resident((1, HEAD_DIM)), _resident((1, HEAD_DIM)),
            table, table,
        ],
        out_specs=[row(cc), row(cc), row(nq), row(nkv),
                   pl.BlockSpec((1, nkv, tm), lambda b, i: (b, 0, i))],
        out_shape=[jax.ShapeDtypeStruct((nb, t, n), BF16) for n in (cc, cc, nq, nkv)]
        + [jax.ShapeDtypeStruct((nb, nkv, t), BF16)],
        compiler_params=_cparams("parallel", "parallel"),
        name="hyb_in_proj",
    )(x, mod, g.reshape(1, d), wgb, wgc, wu, wq, wk, wvt,
      qg.reshape(1, HEAD_DIM), kg.reshape(1, HEAD_DIM), cos2, sin2)


def _softmax_pv(s_list, kv_list):
    m = s_list[0].max(axis=-1, keepdims=True)
    for s in s_list[1:]:
        m = jnp.maximum(m, s.max(axis=-1, keepdims=True))
    l = 0.0
    o = 0.0
    for s, (_, v) in zip(s_list, kv_list):
        p = jnp.exp2(s - m)
        l = l + p.sum(axis=-1, keepdims=True)
        o = o + _dot(p.astype(BF16), v)
    return o / l


def _attend_chains(chains, kv_list):
    scores = lambda load_q: [_dot_t(load_q(), k) for k, _ in kv_list]
    ahead = [scores(load_q) for load_q, _ in chains[:ATTN_SCORE_LOOKAHEAD]]
    for n, (_, store_o) in enumerate(chains):
        s_list = ahead.pop(0)
        if n + ATTN_SCORE_LOOKAHEAD < len(chains):
            ahead.append(scores(chains[n + ATTN_SCORE_LOOKAHEAD][0]))
        store_o(_softmax_pv(s_list, kv_list).astype(BF16))


def _softmax_pv_t(st_list, kvt_list):
    m = st_list[0].max(axis=0, keepdims=True)
    for s in st_list[1:]:
        m = jnp.maximum(m, s.max(axis=0, keepdims=True))
    l = 0.0
    o = 0.0
    for s, (_, vt) in zip(st_list, kvt_list):
        p = jnp.exp2(s - m)
        l = l + p.sum(axis=0, keepdims=True)
        o = o + _dot(vt, p.astype(BF16))
    return (o / l).T


def _attend_chains_t(chains, kvt_list):
    scores = lambda load_q: [_dot_t(k, load_q()) for k, _ in kvt_list]
    ahead = [scores(load_q) for load_q, _ in chains[:ATTN_SCORE_LOOKAHEAD]]
    for n, (_, store_o) in enumerate(chains):
        st_list = ahead.pop(0)
        if n + ATTN_SCORE_LOOKAHEAD < len(chains):
            ahead.append(scores(chains[n + ATTN_SCORE_LOOKAHEAD][0]))
        store_o(_softmax_pv_t(st_list, kvt_list).astype(BF16))


def _row_chunks(n, c):
    c = min(c, n)
    return [slice(r * c, (r + 1) * c) for r in range(n // c)]


def _gqa_kernel(*refs, has_lat):
    if has_lat:
        q_ref, kc_ref, vtc_ref, kl_ref, vtl_ref, o_ref = refs
    else:
        q_ref, kc_ref, vtc_ref, o_ref = refs
    kvt_list = [(kc_ref[0], vtc_ref[0])]
    if has_lat:
        kvt_list.append((kl_ref[0], vtl_ref[0]))

    def chain(rows, sl):
        def store(o):
            o_ref[0, rows, sl] = o
        return (lambda: q_ref[0, rows, sl]), store

    _attend_chains_t([chain(rows, slice(h * HEAD_DIM, (h + 1) * HEAD_DIM))
                      for h in range(q_ref.shape[2] // HEAD_DIM)
                      for rows in _row_chunks(q_ref.shape[1], GQA_ROW_CHUNK)], kvt_list)


def _gqa_latent_call(q, k, vt, b_lat, tc, riders=()):
    nb, t, nq = q.shape
    gw = nq // GQA_KV_HEADS
    tq = min(1024, t)
    ctx = pl.BlockSpec((1, tc, HEAD_DIM), lambda b, g, i: (b_lat, b, g))
    lat = pl.BlockSpec((1, t, HEAD_DIM), lambda b, g, i: (b, 0, g))
    vt_ctx = pl.BlockSpec((1, HEAD_DIM, tc), lambda b, g, i: (b_lat, g, b))
    vt_lat = pl.BlockSpec((1, HEAD_DIM, t), lambda b, g, i: (b, g, 0))
    qo = pl.BlockSpec((1, tq, gw), lambda b, g, i: (b, i, g))
    return _call_with_casts(
        functools.partial(_gqa_kernel, has_lat=True), riders,
        grid=(b_lat, GQA_KV_HEADS, t // tq),
        in_specs=[qo, ctx, vt_ctx, lat, vt_lat],
        out_specs=[qo],
        out_shape=[jax.ShapeDtypeStruct((b_lat, t, nq), BF16)],
        compiler_params=_cparams("parallel", "parallel", "parallel"),
        name="gqa_latent",
    )(q, k, vt, k, vt)


def _gqa_ctx_call(q, k, vt, b_lat, tc):
    nb, t, nq = q.shape
    gw = nq // GQA_KV_HEADS
    ctx = pl.BlockSpec((1, tc, HEAD_DIM), lambda b, g: (b_lat, b, g))
    vt_ctx = pl.BlockSpec((1, HEAD_DIM, tc), lambda b, g: (b_lat, g, b))
    return pl.pallas_call(
        functools.partial(_gqa_kernel, has_lat=False),
        grid=(t // tc, GQA_KV_HEADS),
        in_specs=[pl.BlockSpec((1, tc, gw), lambda b, g: (b_lat, b, g)), ctx, vt_ctx],
        out_specs=pl.BlockSpec((1, tc, gw), lambda b, g: (0, b, g)),
        out_shape=jax.ShapeDtypeStruct((1, t, nq), BF16),
        compiler_params=_cparams("parallel", "parallel"),
        name="gqa_context",
    )(q, k, vt)


def _hyb_out_kernel(x_ref, mod_ref, gb_ref, z_ref, zp_ref, zn_ref, att_ref, attc_ref, cw_ref, w_ref,
                    o_ref, *, seq_lat, seq_ctx, b_lat):
    b, i = pl.program_id(0), pl.program_id(1)
    tm, cc = z_ref.shape[1], z_ref.shape[2]
    acc = _dot(jnp.where(b == b_lat, attc_ref[0], att_ref[0]), w_ref[cc:, :])
    z = z_ref[0].astype(F32)
    rows = lax.broadcasted_iota(jnp.int32, (tm, cc), 0)
    seq = jnp.where(b == b_lat, seq_ctx, seq_lat)
    pos = (i * tm + rows) & (seq - 1)
    z_prev = jnp.where(rows == 0, zp_ref[0, 7:8, :].astype(F32), pltpu.roll(z, 1, axis=0))
    z_next = jnp.where(rows == tm - 1, zn_ref[0, 0:1, :].astype(F32),
                       pltpu.roll(z, tm - 1, axis=0))
    z_prev = jnp.where(pos == 0, 0.0, z_prev)
    z_next = jnp.where(pos == seq - 1, 0.0, z_next)
    conv = z_prev * cw_ref[0:1, :] + z * cw_ref[1:2, :] + z_next * cw_ref[2:3, :]
    acc = acc + _dot((gb_ref[0].astype(F32) * conv).astype(BF16), w_ref[:cc, :])
    o_ref[0] = x_ref[0] + mod_ref[0, 5:6, :] * acc


def _hyb_out_call(x, mod, gb, z, att, att_c, conv_w, w_out, b_lat, tc, riders=()):
    nb, t, d = x.shape
    cc = gb.shape[2]
    tm = min(512, t)
    nt = t // tm
    hb = tm // 8
    row = lambda n: pl.BlockSpec((1, tm, n), lambda b, i: (b, i, 0))
    nq = att.shape[2]
    att_spec = pl.BlockSpec((1, tm, nq), lambda b, i: (jnp.minimum(b, b_lat - 1),
                                                       jnp.where(b < b_lat, i, nt - 1), 0))
    attc_spec = pl.BlockSpec((1, tm, nq), lambda b, i: (0, jnp.where(b < b_lat, 0, i), 0))
    return _call_with_casts(
        functools.partial(_hyb_out_kernel, seq_lat=t, seq_ctx=tc, b_lat=b_lat), riders,
        grid=(nb, t // tm),
        in_specs=[
            row(d),
            pl.BlockSpec((1, N_MOD, d), lambda b, i: (b, 0, 0)),
            row(cc), row(cc),
            pl.BlockSpec((1, 8, cc), lambda b, i: (b, jnp.maximum(i * hb - 1, 0), 0)),
            pl.BlockSpec((1, 8, cc), lambda b, i: (b, jnp.minimum((i + 1) * hb, t // 8 - 1), 0)),
            att_spec, attc_spec,
            _resident(conv_w.shape),
            _resident(w_out.shape),
        ],
        out_specs=[row(d)],
        out_shape=[jax.ShapeDtypeStruct((nb, t, d), F32)],
        compiler_params=_cparams("parallel", "parallel"),
        name="hyb_out_proj",
    )(x, mod, gb, z, z, z, att, att_c, conv_w, w_out)


def _mla_in_kernel(x_ref, mod_ref, g_ref, wd_ref, qg_ref, kvg_ref, wuq_ref, wuk_ref, wuvt_ref,
                   cos_ref, sin_ref, qn_ref, qr_ref, kn_ref, vt_ref, kr_ref, *, q_rank, kv_rank):
    n_nope = qn_ref.shape[2]
    q_scale = (MLA_NOPE + MLA_ROPE) ** -0.5 * LOG2E
    down = lambda rows: _dot(_modnorm(x_ref[0, rows, :], g_ref[...], mod_ref[0, 3:4, :],
                                      mod_ref[0, 4:5, :]).astype(BF16), wd_ref[...])
    chunks = _row_chunks(x_ref.shape[1], PROJ_ROW_CHUNK)
    d_next = down(chunks[0])
    for n, rows in enumerate(chunks):
        d = d_next
        if n + 1 < len(chunks):
            d_next = down(chunks[n + 1])
        cos = cos_ref[0, rows, :]
        sin = sin_ref[0, rows, :]
        q = _dot(_rms(d[:, :q_rank], qg_ref[...]).astype(BF16), wuq_ref[...])
        qn_ref[0, rows, :] = (q[:, :n_nope] * q_scale).astype(BF16)
        for j in range(qr_ref.shape[2] // LANES):
            qr_ref[0, rows, j * LANES:(j + 1) * LANES] = (
                _rope(q[:, n_nope + j * LANES:n_nope + (j + 1) * LANES], cos, sin) * q_scale
            ).astype(BF16)
        ckv = _rms(d[:, q_rank:q_rank + kv_rank], kvg_ref[...]).astype(BF16)
        kn_ref[0, rows, :] = _dot(ckv, wuk_ref[...]).astype(BF16)
        vt_ref[0, :, rows] = _dot_t(wuvt_ref[...], ckv).astype(BF16)
        kr = _rope(d[:, q_rank + kv_rank:], cos, sin)
        first = (lax.broadcasted_iota(jnp.int32, kr.shape, 1) & (MLA_ROPE // 2)) == 0
        kr_ref[0, rows, :LANES] = jnp.where(first, kr, 0.0).astype(BF16)
        kr_ref[0, rows, LANES:] = jnp.where(first, 0.0, kr).astype(BF16)


def _mla_in_call(x, mod, g, wd, qg, kvg, wuq, wuk, wuvt, cos2, sin2, riders=()):
    nb, t, d = x.shape
    tm = min(512, t)
    q_rank, kv_rank = qg.shape[0], kvg.shape[0]
    n_nope = wuk.shape[1]
    n_rope = wuq.shape[1] - n_nope
    n_v = wuvt.shape[0]
    row = lambda n: pl.BlockSpec((1, tm, n), lambda b, i: (b, i, 0))
    table = pl.BlockSpec((1, tm, LANES), lambda b, i: (jnp.where(b == nb - 1, 1, 0), i, 0))
    out_specs = [row(n_nope), row(n_rope), row(n_nope),
                 pl.BlockSpec((1, n_v, tm), lambda b, i: (b, 0, i)), row(2 * LANES)]
    out_shapes = [(nb, t, n_nope), (nb, t, n_rope), (nb, t, n_nope), (nb, n_v, t), (nb, t, 2 * LANES)]
    return _call_with_casts(
        functools.partial(_mla_in_kernel, q_rank=q_rank, kv_rank=kv_rank), riders,
        grid=(nb, t // tm),
        in_specs=[
            row(d),
            pl.BlockSpec((1, N_MOD, d), lambda b, i: (b, 0, 0)),
            _resident((1, d)),
            _resident(wd.shape), _resident((1, q_rank)), _resident((1, kv_rank)),
            _resident(wuq.shape), _resident(wuk.shape), _resident(wuvt.shape),
            table, table,
        ],
        out_specs=out_specs,
        out_shape=[jax.ShapeDtypeStruct(s, BF16) for s in out_shapes],
        compiler_params=_cparams("parallel", "parallel"),
        name="mla_in_proj",
    )(x, mod, g.reshape(1, d), wd, qg.reshape(1, q_rank), kvg.reshape(1, kv_rank), wuq, wuk, wuvt,
      cos2, sin2)


def _mla_attn_kernel(qn_ref, qr_ref, knc_ref, krc_ref, vtc_ref, knl_ref, krl_ref, vtl_ref,
                     o_ref, k_ref, vt_ref):
    tc = knc_ref.shape[1]

    @pl.when(pl.program_id(2) == 0)
    def _():
        k_ref[:tc, :LANES] = knc_ref[0]
        k_ref[:tc, LANES:] = krc_ref[0]
        k_ref[tc:, :LANES] = knl_ref[0]
        k_ref[tc:, LANES:] = krl_ref[0]
        vt_ref[:, :tc] = vtc_ref[0]
        vt_ref[:, tc:] = vtl_ref[0]

    def chain(rows):
        def store(o):
            o_ref[0, rows, :] = o
        return (lambda: jnp.concatenate([qn_ref[0, rows, :], qr_ref[0, rows, :]], axis=1)), store

    _attend_chains_t([chain(rows) for rows in _row_chunks(qn_ref.shape[1], MLA_ROW_CHUNK)],
                     [(k_ref[...], vt_ref[...])])


def _mla_attn_call(qn, qr, kn, vt, kr, b_lat, tc, riders=()):
    nb, t, n_nope = qn.shape
    heads = n_nope // MLA_NOPE
    tq = min(2048, t)
    qspec = pl.BlockSpec((1, tq, LANES), lambda b, h, i: (b, i, h))
    qrspec = pl.BlockSpec((1, tq, LANES), lambda b, h, i: (b, i, h // 2))
    ctx = lambda f: pl.BlockSpec((1, tc, LANES), lambda b, h, i: (b_lat, b, f(h)))
    lat = lambda f: pl.BlockSpec((1, t, LANES), lambda b, h, i: (b, 0, f(h)))
    vt_ctx = pl.BlockSpec((1, MLA_V, tc), lambda b, h, i: (b_lat, h, b))
    vt_lat = pl.BlockSpec((1, MLA_V, t), lambda b, h, i: (b, h, 0))
    same = lambda h: h
    parity = lambda h: h % 2
    return _call_with_casts(
        _mla_attn_kernel, riders,
        grid=(b_lat, heads, t // tq),
        in_specs=[qspec, qrspec, ctx(same), ctx(parity), vt_ctx, lat(same), lat(parity), vt_lat],
        out_specs=[qspec],
        out_shape=[jax.ShapeDtypeStruct((b_lat, t, n_nope), BF16)],
        scratch_shapes=[pltpu.VMEM((tc + t, 2 * LANES), BF16), pltpu.VMEM((MLA_V, tc + t), BF16)],
        compiler_params=_cparams("parallel", "parallel", "arbitrary"),
        name="mla_attention",
    )(qn, qr, kn, kr, vt, kn, kr, vt)


def _mla_out_kernel(x_ref, mod_ref, att_ref, w_ref, o_ref):
    o_ref[0] = x_ref[0] + mod_ref[0, 5:6, :] * _dot(att_ref[0], w_ref[...])


def _mla_out_call(x, mod, att, w_o, riders=()):
    nb, t, k = att.shape
    d = w_o.shape[1]
    tm = min(512, t)
    row = lambda n: pl.BlockSpec((1, tm, n), lambda b, i: (b, i, 0))
    return _call_with_casts(
        _mla_out_kernel, riders,
        grid=(nb, t // tm),
        in_specs=[row(d), pl.BlockSpec((1, N_MOD, d), lambda b, i: (b, 0, 0)), row(k),
                  _resident(w_o.shape)],
        out_specs=[row(d)],
        out_shape=[jax.ShapeDtypeStruct((nb, t, d), F32)],
        compiler_params=_cparams("parallel", "parallel"),
        name="mla_out_proj",
    )(x, mod, att, w_o)


def _rope_angles(n_tok, dim):
    n_rows = n_tok // GRID_W
    row = jnp.repeat(jnp.arange(n_rows), GRID_W).astype(F32)
    col = jnp.tile(jnp.arange(GRID_W), n_rows).astype(F32)
    half = dim // 2
    inv = 1.0 / (ROPE_THETA ** (jnp.arange(0, half, 2, dtype=F32) / half))
    return jnp.concatenate([row[:, None] * inv, col[:, None] * inv], axis=-1)


def _rope_tables(n_tok, dim):
    ang = _rope_angles(n_tok, dim)
    reps = LANES // dim
    cos = jnp.concatenate([jnp.cos(ang)] * (2 * reps), axis=-1)
    sin = jnp.concatenate([-jnp.sin(ang)] * reps + [jnp.sin(ang)] * reps, axis=-1)
    return (jnp.stack([cos, jnp.ones_like(cos)]), jnp.stack([sin, jnp.zeros_like(sin)]))


def _deinterleave(n):
    return np.concatenate([np.arange(0, n, 2), np.arange(1, n, 2)])


def kernel(x, c, ctx, c_ctx, mod_w, mod_b, norm_ffn1, norm_mix, norm_ffn2,
           ffn1_w_gate, ffn1_w_up, ffn1_w_down, ffn2_w_gate, ffn2_w_up, ffn2_w_down,
           hyb_w_in, hyb_conv_w, hyb_q_norm, hyb_k_norm, hyb_w_out,
           mla_w_down, mla_q_norm, mla_kv_norm, mla_w_uq, mla_w_ukv, mla_w_o, final_norm):
    b_lat, t, d = x.shape
    tc = ctx.shape[1]
    depth = mod_w.shape[0]
    assert depth == 2 and ctx.shape[0] * tc == t and b_lat + 1 <= MOD_ROWS
    assert t & (t - 1) == 0 and tc & (tc - 1) == 0 and t % GRID_W == 0
    nb = b_lat + 1

    cvec = jnp.concatenate([c, c_ctx[None], jnp.zeros((MOD_ROWS - nb, d), F32)], axis=0)
    mods = _mod_call(cvec, mod_w, mod_b).reshape(depth, MOD_ROWS, N_MOD, d)
    cast = lambda w: w.astype(BF16)
    tf = min(FFN_TILE, ffn1_w_gate.shape[2])
    f1 = _as_ffn_weights(_cast_call(ffn1_w_gate, 0, tf), _cast_call(ffn1_w_up, 0, tf),
                         _cast_call(ffn1_w_down, 0, d))

    mod = mods[0]
    xs = _ffn_call(x, nb, mod, norm_ffn1[0], *f1, 0, 0, ctx=ctx.reshape(1, t, d))
    cc = d // 2
    n_q = cc
    n_kv = GQA_KV_HEADS * HEAD_DIM
    w_in = hyb_w_in[0]
    perm = _deinterleave(HEAD_DIM)
    pairs_last = lambda w: w.reshape(d, -1, HEAD_DIM // 2, 2).swapaxes(2, 3).reshape(d, -1)
    wq = pairs_last(w_in[:, 3 * cc:3 * cc + n_q])
    wk = pairs_last(w_in[:, 3 * cc + n_q:3 * cc + n_q + n_kv])
    wvt = w_in[:, 3 * cc + n_q + n_kv:].T
    cos_a, sin_a = _rope_tables(t, HEAD_DIM)
    (gb, z, q, k, v), _ = _hyb_in_call(
        xs, mod, norm_mix[0], cast(w_in[:, :cc]), cast(w_in[:, cc:2 * cc]), cast(w_in[:, 2 * cc:3 * cc]),
        cast(wq), cast(wk), cast(wvt), hyb_q_norm[0][perm], hyb_k_norm[0][perm], cos_a, sin_a)
    (att,), casts = _gqa_latent_call(
        q, k, v, b_lat, tc,
        riders=[(ffn2_w_gate, 0, tf), (ffn1_w_gate, 1, tf), (ffn2_w_up, 0, tf), (ffn1_w_up, 1, tf),
                (ffn2_w_down, 0, d), (ffn1_w_down, 1, d), (hyb_w_out, 0, d)])
    gates, ups, downs = casts[0:2], casts[2:4], casts[4:6]
    att_c = _gqa_ctx_call(q, k, v, b_lat, tc)
    (xs,), _ = _hyb_out_call(xs, mod, gb, z, att, att_c, hyb_conv_w[0], casts[6][0, 0], b_lat, tc)
    xs = _ffn_call(xs, nb, mod, norm_ffn2[0], *_as_ffn_weights(gates[0], ups[0], downs[0]), 0, 6)

    mod = mods[1]
    xs = _ffn_call(xs, nb, mod, norm_ffn1[1], *_as_ffn_weights(gates[1], ups[1], downs[1]), 0, 0)
    heads = d // HEAD_DIM
    q_rank, kv_rank = mla_q_norm.shape[1], mla_kv_norm.shape[1]
    uq = mla_w_uq[0].reshape(q_rank, heads, MLA_NOPE + MLA_ROPE)
    uq_rope = uq[:, :, MLA_NOPE:].reshape(q_rank, heads // 2, 2, MLA_ROPE // 2, 2)
    wuq = jnp.concatenate([uq[:, :, :MLA_NOPE].reshape(q_rank, -1),
                           uq_rope.transpose(0, 1, 4, 2, 3).reshape(q_rank, -1)], axis=1)
    ukv = mla_w_ukv[0].reshape(kv_rank, heads, MLA_NOPE + MLA_V)
    wuk = ukv[:, :, :MLA_NOPE].reshape(kv_rank, -1)
    wuvt = ukv[:, :, MLA_NOPE:].reshape(kv_rank, -1).T
    kr0 = q_rank + kv_rank
    w_kr = mla_w_down[0][:, kr0:].reshape(d, MLA_ROPE // 2, 2)
    wdn = jnp.concatenate([mla_w_down[0][:, :kr0], w_kr[:, :, 0], w_kr[:, :, 0],
                           w_kr[:, :, 1], w_kr[:, :, 1]], axis=1)
    cos_m, sin_m = _rope_tables(t, MLA_ROPE)
    (qn, qr, kn, vt, kr), _ = _mla_in_call(
        xs, mod, norm_mix[1], cast(wdn), mla_q_norm[0], mla_kv_norm[0],
        cast(wuq), cast(wuk), cast(wuvt), cos_m, sin_m)
    (att,), casts = _mla_attn_call(
        qn, qr, kn, vt, kr, b_lat, tc,
        riders=[(ffn2_w_gate, 1, tf), (ffn2_w_up, 1, tf), (ffn2_w_down, 1, d), (mla_w_o, 0, d)])
    (xs,), _ = _mla_out_call(xs, mod, att, casts[3][0, 0])
    return _ffn_call(xs, b_lat, mod, norm_ffn2[1], *_as_ffn_weights(*casts[:3]), 0, 6,
                     final_g=final_norm)
```

```python
import functools

import numpy as np
import jax
import jax.numpy as jnp
from jax import lax
from jax.experimental import pallas as pl
from jax.experimental.pallas import tpu as pltpu

HEAD_DIM = 128
GRID_W = 64
ROPE_THETA = 10000.0
EPS = 1e-6
N_MOD = 9
GQA_KV_HEADS = 2
MLA_NOPE = 128
MLA_ROPE = 64
MLA_V = 128
LANES = 128
VMEM_LIMIT = 60 * 1024 * 1024
MOD_ROWS = 16
FFN_ROW_CHUNKS = 4
FFN_TILE = 512
FFN_SLOTS = 4
PROJ_ROW_CHUNK = 256
GQA_ROW_CHUNK = 512
MLA_ROW_CHUNK = 512
ATTN_SCORE_LOOKAHEAD = 2
LOG2E = 1.4426950408889634
CAST_BLOCK_BYTES = 4 * 1024 * 1024

F32 = jnp.float32
BF16 = jnp.bfloat16


def _cparams(*sem):
    return pltpu.CompilerParams(dimension_semantics=sem, vmem_limit_bytes=VMEM_LIMIT)


def _resident(shape):
    nd = len(shape)
    return pl.BlockSpec(shape, lambda *_: (0,) * nd, pipeline_mode=pl.Buffered(1))


def _rms(x, g):
    return x * lax.rsqrt(jnp.mean(x * x, axis=-1, keepdims=True) + EPS) * g


def _modnorm(x, g, shift, scale):
    return _rms(x, g) * (1.0 + scale) + shift


def _rope(x, cos, sin):
    return x * cos + pltpu.roll(x, LANES // 2, axis=1) * sin


def _dot(a, b):
    return jnp.dot(a, b, preferred_element_type=F32)


def _dot_t(a, b):
    return lax.dot_general(a, b, (((1,), (1,)), ((), ())), preferred_element_type=F32)


def _mod_kernel(c_ref, w_ref, b_ref, o_ref):
    c = c_ref[...]
    s = (c * jax.nn.sigmoid(c)).astype(BF16)
    o_ref[0] = _dot(s, w_ref[0].astype(BF16)) + b_ref[0]


def _mod_call(cvec, mod_w, mod_b):
    depth, d, n = mod_w.shape
    tn = min(1024, n)
    return pl.pallas_call(
        _mod_kernel,
        grid=(depth, n // tn),
        in_specs=[
            pl.BlockSpec((MOD_ROWS, d), lambda l, j: (0, 0)),
            pl.BlockSpec((1, d, tn), lambda l, j: (l, 0, j)),
            pl.BlockSpec((1, 1, tn), lambda l, j: (l, 0, j)),
        ],
        out_specs=pl.BlockSpec((1, MOD_ROWS, tn), lambda l, j: (l, 0, j)),
        out_shape=jax.ShapeDtypeStruct((depth, MOD_ROWS, n), F32),
        compiler_params=_cparams("parallel", "parallel"),
        name="adaln_mod",
    )(cvec, mod_w, mod_b.reshape(depth, 1, n))


def _swiglu_act(xn, wg_ref, wu_ref):
    hg = _dot(xn, wg_ref[...])
    hu = _dot(xn, wu_ref[...])
    return (hg * jax.nn.sigmoid(hg) * hu).astype(BF16)


_GATE_UP, _DOWN = "gate_up", "down"


def _ring_plan(n_ff):
    assert FFN_SLOTS >= 4
    plan = {}
    for j in range(0, n_ff, 2):
        waits = ([(_GATE_UP, t) for t in (j + 1, j + 2) if t < n_ff]
                 + [(_DOWN, t) for t in (j, j + 1) if t < n_ff])
        starts = ([(_GATE_UP, t) for t in (j + 3, j + 4) if t < n_ff]
                  + [(_DOWN, t) for t in (j + 2, j + 3) if t < n_ff])
        plan[j] = (waits, starts, [])
    first = [(_GATE_UP, t) for t in range(min(3, n_ff))] + [(_DOWN, t) for t in range(min(2, n_ff))]
    for kind, tile in first:
        last_user = max(t for t in range(n_ff) if t % FFN_SLOTS == tile % FFN_SLOTS)
        free_from = last_user if kind == _GATE_UP else last_user + 1
        barrier = free_from + free_from % 2
        assert barrier in plan, "no barrier left to prefetch the next grid step's tile"
        plan[barrier][2].append((kind, tile))
    return plan, first


def _ffn_kernel(x_ref, mod_ref, g_ref, wg_hbm, wu_hbm, wd_hbm, *rest, layer, m0, final, with_ctx):
    rest = list(rest)
    fn_ref = rest.pop(0) if final else None
    ctx_ref = rest.pop(0) if with_ctx else None
    o_ref, xn_ref, h_ref, wg_buf, wu_buf, wd_buf, sem = rest
    is_ctx = pl.program_id(0) == pl.num_programs(0) - 1
    n_ff = wg_hbm.shape[1]
    tm = xn_ref.shape[0]
    step = pl.program_id(0) * pl.num_programs(1) + pl.program_id(1)
    not_last_step = step < pl.num_programs(0) * pl.num_programs(1) - 1
    half_gate = 0.5 * mod_ref[0, m0 + 2:m0 + 3, :]

    def copies(kind, blk):
        slot = blk % FFN_SLOTS
        pairs = (((0, wg_hbm, wg_buf), (1, wu_hbm, wu_buf)) if kind == _GATE_UP
                 else ((2, wd_hbm, wd_buf),))
        return [pltpu.make_async_copy(hbm.at[layer, blk], buf.at[slot], sem.at[k, slot])
                for k, hbm, buf in pairs]

    def start(kind, blk):
        for c in copies(kind, blk):
            c.start()

    def wait(kind, blk):
        for c in copies(kind, blk):
            c.wait()

    def x_rows(rows):
        x = x_ref[0, rows, :]
        return jnp.where(is_ctx, ctx_ref[0, rows, :], x) if with_ctx else x

    plan, first = _ring_plan(n_ff)

    @pl.when(step == 0)
    def _():
        for item in first:
            start(*item)

    wait(_GATE_UP, 0)
    rs = tm // FFN_ROW_CHUNKS
    for r in range(FFN_ROW_CHUNKS):
        rows = slice(r * rs, (r + 1) * rs)
        xn = _modnorm(x_rows(rows), g_ref[...], mod_ref[0, m0:m0 + 1, :],
                      mod_ref[0, m0 + 1:m0 + 2, :]).astype(BF16)
        xn_ref[rows, :] = xn
        h_ref[0, rows, :] = _swiglu_act(xn, wg_buf.at[0], wu_buf.at[0])

    for j in range(n_ff):
        if j in plan:
            waits, starts, next_starts = plan[j]
            for item in starts:
                start(*item)
            if next_starts:
                pl.when(not_last_step)(
                    lambda items=next_starts: [start(*item) for item in items] and None)
            for item in waits:
                wait(*item)
        if j + 1 < n_ff:
            nslot = (j + 1) % FFN_SLOTS
            h_ref[(j + 1) % 2] = _swiglu_act(xn_ref[...], wg_buf.at[nslot], wu_buf.at[nslot])
        contrib = half_gate * _dot(h_ref[j % 2], wd_buf[j % FFN_SLOTS])
        if j == 0:
            o_ref[0] = x_rows(slice(None)) + contrib
        else:
            o_ref[0] += contrib

    if final:
        o_ref[0] = _rms(o_ref[0], fn_ref[...])


def _ffn_call(x, nb, mod, g, wg, wu, wd, layer, m0, final_g=None, ctx=None):
    _, t, d = x.shape
    _, n_ff, _, tf = wg.shape
    _ring_plan(n_ff)
    final = final_g is not None
    tm = min(512, t)
    nt = t // tm
    if ctx is None:
        x_spec = pl.BlockSpec((1, tm, d), lambda b, i: (b, i, 0))
    else:
        x_spec = pl.BlockSpec((1, tm, d), lambda b, i: (jnp.minimum(b, nb - 2),
                                                         jnp.where(b < nb - 1, i, nt - 1), 0))
    in_specs = [
        x_spec,
        pl.BlockSpec((1, N_MOD, d), lambda b, i: (b, 0, 0)),
        pl.BlockSpec((1, d), lambda b, i: (0, 0)),
        pl.BlockSpec(memory_space=pl.ANY),
        pl.BlockSpec(memory_space=pl.ANY),
        pl.BlockSpec(memory_space=pl.ANY),
    ]
    args = [x, mod, g.reshape(1, d), wg, wu, wd]
    if final:
        in_specs.append(pl.BlockSpec((1, d), lambda b, i: (0, 0)))
        args.append(final_g.reshape(1, d))
    if ctx is not None:
        in_specs.append(pl.BlockSpec((1, tm, d), lambda b, i: (0, jnp.where(b < nb - 1, 0, i), 0)))
        args.append(ctx)
    return pl.pallas_call(
        functools.partial(_ffn_kernel, layer=layer, m0=m0, final=final, with_ctx=ctx is not None),
        grid=(nb, nt),
        in_specs=in_specs,
        out_specs=pl.BlockSpec((1, tm, d), lambda b, i: (b, i, 0)),
        out_shape=jax.ShapeDtypeStruct((nb, t, d), F32),
        scratch_shapes=[
            pltpu.VMEM((tm, d), BF16),
            pltpu.VMEM((2, tm, tf), BF16),
            pltpu.VMEM((FFN_SLOTS, d, tf), BF16), pltpu.VMEM((FFN_SLOTS, d, tf), BF16),
            pltpu.VMEM((FFN_SLOTS, tf, d), BF16),
            pltpu.SemaphoreType.DMA((3, FFN_SLOTS)),
        ],
        compiler_params=_cparams("arbitrary", "arbitrary"),
        name="swiglu_half_step",
    )(*args)


def _cast_kernel(w_ref, o_ref):
    cb = o_ref.shape[3]
    for n in range(o_ref.shape[1]):
        o_ref[0, n] = w_ref[0, :, n * cb:(n + 1) * cb].astype(BF16)


def _cast_call(w, layer, col_block):
    _, r, c = w.shape
    nblk = c // col_block
    tr = max(n for n in range(16, r + 1, 16) if r % n == 0 and 4 * c * n <= CAST_BLOCK_BYTES)
    return pl.pallas_call(
        _cast_kernel,
        grid=(r // tr,),
        in_specs=[pl.BlockSpec((1, tr, c), lambda i: (layer, i, 0))],
        out_specs=pl.BlockSpec((1, nblk, tr, col_block), lambda i: (0, 0, i, 0)),
        out_shape=jax.ShapeDtypeStruct((1, nblk, r, col_block), BF16),
        compiler_params=_cparams("parallel"),
        name="weight_cast",
    )(w)


def _call_with_casts(kernel_fn, riders, *, grid, in_specs, out_specs, out_shape, **kw):
    n_in, n_out, n_r = len(in_specs), len(out_specs), len(riders)
    steps = int(np.prod(grid))
    r_in, r_out, r_shape = [], [], []
    for w, layer, cb in riders:
        _, r, c = w.shape
        tr = min(n for n in range(16, r + 1, 16) if r % n == 0 and r // n <= steps)

        def row_block(*g, n_rows=r // tr):
            lin = g[0]
            for gi, n in zip(g[1:], grid[1:]):
                lin = lin * n + gi
            return jnp.minimum(lin, n_rows - 1)

        r_in.append(pl.BlockSpec((1, tr, c), lambda *g, f=row_block, l=layer: (l, f(*g), 0)))
        r_out.append(pl.BlockSpec((1, c // cb, tr, cb), lambda *g, f=row_block: (0, 0, f(*g), 0)))
        r_shape.append(jax.ShapeDtypeStruct((1, c // cb, r, cb), BF16))

    def body(*refs):
        outs_at = n_in + n_r
        for w_ref, o_ref in zip(refs[n_in:outs_at], refs[outs_at + n_out:outs_at + n_out + n_r]):
            _cast_kernel(w_ref, o_ref)
        kernel_fn(*refs[:n_in], *refs[outs_at:outs_at + n_out], *refs[outs_at + n_out + n_r:])

    if riders:
        kw["compiler_params"] = _cparams(*["arbitrary"] * len(grid))
    call = pl.pallas_call(body, grid=grid, in_specs=list(in_specs) + r_in,
                          out_specs=list(out_specs) + r_out, out_shape=list(out_shape) + r_shape, **kw)

    def run(*args):
        res = call(*args, *[w for w, _, _ in riders])
        return list(res[:n_out]), list(res[n_out:])
    return run


def _as_ffn_weights(gate, up, down):
    _, _, dff, d = down.shape
    tf = gate.shape[3]
    return gate, up, down.reshape(1, dff // tf, tf, d)


def _hyb_in_kernel(x_ref, mod_ref, g_ref, wgb_ref, wgc_ref, wu_ref, wq_ref, wk_ref, wvt_ref,
                   qg_ref, kg_ref, cos_ref, sin_ref,
                   gb_ref, z_ref, q_ref, k_ref, vt_ref):
    q_scale = HEAD_DIM ** -0.5 * LOG2E
    norm = lambda rows: _modnorm(x_ref[0, rows, :], g_ref[...], mod_ref[0, 3:4, :],
                                 mod_ref[0, 4:5, :]).astype(BF16)
    chunks = _row_chunks(x_ref.shape[1], PROJ_ROW_CHUNK)
    xn_next = norm(chunks[0])
    for n, rows in enumerate(chunks):
        xn = xn_next
        if n + 1 < len(chunks):
            xn_next = norm(chunks[n + 1])
        gb_ref[0, rows, :] = _dot(xn, wgb_ref[...]).astype(BF16)
        z_ref[0, rows, :] = (_dot(xn, wgc_ref[...]) * _dot(xn, wu_ref[...])).astype(BF16)
        vt_ref[0, :, rows] = _dot_t(wvt_ref[...], xn).astype(BF16)
        cos = cos_ref[0, rows, :]
        sin = sin_ref[0, rows, :]
        q = _dot(xn, wq_ref[...])
        for h in range(q.shape[1] // HEAD_DIM):
            sl = slice(h * HEAD_DIM, (h + 1) * HEAD_DIM)
            q_ref[0, rows, sl] = (_rope(_rms(q[:, sl], qg_ref[...]), cos, sin) * q_scale).astype(BF16)
        k = _dot(xn, wk_ref[...])
        for h in range(k.shape[1] // HEAD_DIM):
            sl = slice(h * HEAD_DIM, (h + 1) * HEAD_DIM)
            k_ref[0, rows, sl] = _rope(_rms(k[:, sl], kg_ref[...]), cos, sin).astype(BF16)


def _hyb_in_call(x, mod, g, wgb, wgc, wu, wq, wk, wvt, qg, kg, cos2, sin2, riders=()):
    nb, t, d = x.shape
    tm = min(512, t)
    cc, nq, nkv = wgb.shape[1], wq.shape[1], wk.shape[1]
    row = lambda n: pl.BlockSpec((1, tm, n), lambda b, i: (b, i, 0))
    table = pl.BlockSpec((1, tm, LANES), lambda b, i: (jnp.where(b == nb - 1, 1, 0), i, 0))
    return _call_with_casts(
        _hyb_in_kernel, riders,
        grid=(nb, t // tm),
        in_specs=[
            row(d),
            pl.BlockSpec((1, N_MOD, d), lambda b, i: (b, 0, 0)),
            _resident((1, d)),
            _resident(wgb.shape), _resident(wgc.shape), _resident(wu.shape),
            _resident(wq.shape), _resident(wk.shape), _resident(wvt.shape),
            _resident((1, HEAD_DIM)), _resident((1, HEAD_DIM)),
            table, table,
        ],
        out_specs=[row(cc), row(cc), row(nq), row(nkv),
                   pl.BlockSpec((1, nkv, tm), lambda b, i: (b, 0, i))],
        out_shape=[jax.ShapeDtypeStruct((nb, t, n), BF16) for n in (cc, cc, nq, nkv)]
        + [jax.ShapeDtypeStruct((nb, nkv, t), BF16)],
        compiler_params=_cparams("parallel", "parallel"),
        name="hyb_in_proj",
    )(x, mod, g.reshape(1, d), wgb, wgc, wu, wq, wk, wvt,
      qg.reshape(1, HEAD_DIM), kg.reshape(1, HEAD_DIM), cos2, sin2)


def _softmax_pv_t(st_list, kvt_list):
    m = st_list[0].max(axis=0, keepdims=True)
    for s in st_list[1:]:
        m = jnp.maximum(m, s.max(axis=0, keepdims=True))
    l = 0.0
    o = 0.0
    for s, (_, vt) in zip(st_list, kvt_list):
        p = jnp.exp2(s - m)
        l = l + p.sum(axis=0, keepdims=True)
        o = o + _dot(vt, p.astype(BF16))
    return (o / l).T


def _attend_chains_t(chains, kvt_list):
    scores = lambda load_q: [_dot_t(k, load_q()) for k, _ in kvt_list]
    ahead = [scores(load_q) for load_q, _ in chains[:ATTN_SCORE_LOOKAHEAD]]
    for n, (_, store_o) in enumerate(chains):
        st_list = ahead.pop(0)
        if n + ATTN_SCORE_LOOKAHEAD < len(chains):
            ahead.append(scores(chains[n + ATTN_SCORE_LOOKAHEAD][0]))
        store_o(_softmax_pv_t(st_list, kvt_list).astype(BF16))


def _row_chunks(n, c):
    c = min(c, n)
    return [slice(r * c, (r + 1) * c) for r in range(n // c)]


def _gqa_kernel(*refs, has_lat):
    if has_lat:
        q_ref, kc_ref, vtc_ref, kl_ref, vtl_ref, o_ref = refs
    else:
        q_ref, kc_ref, vtc_ref, o_ref = refs
    kvt_list = [(kc_ref[0], vtc_ref[0])]
    if has_lat:
        kvt_list.append((kl_ref[0], vtl_ref[0]))

    def chain(rows, sl):
        def store(o):
            o_ref[0, rows, sl] = o
        return (lambda: q_ref[0, rows, sl]), store

    _attend_chains_t([chain(rows, slice(h * HEAD_DIM, (h + 1) * HEAD_DIM))
                      for h in range(q_ref.shape[2] // HEAD_DIM)
                      for rows in _row_chunks(q_ref.shape[1], GQA_ROW_CHUNK)], kvt_list)


def _gqa_latent_call(q, k, vt, b_lat, tc, riders=()):
    nb, t, nq = q.shape
    gw = nq // GQA_KV_HEADS
    tq = min(1024, t)
    ctx = pl.BlockSpec((1, tc, HEAD_DIM), lambda b, g, i: (b_lat, b, g))
    lat = pl.BlockSpec((1, t, HEAD_DIM), lambda b, g, i: (b, 0, g))
    vt_ctx = pl.BlockSpec((1, HEAD_DIM, tc), lambda b, g, i: (b_lat, g, b))
    vt_lat = pl.BlockSpec((1, HEAD_DIM, t), lambda b, g, i: (b, g, 0))
    qo = pl.BlockSpec((1, tq, gw), lambda b, g, i: (b, i, g))
    return _call_with_casts(
        functools.partial(_gqa_kernel, has_lat=True), riders,
        grid=(b_lat, GQA_KV_HEADS, t // tq),
        in_specs=[qo, ctx, vt_ctx, lat, vt_lat],
        out_specs=[qo],
        out_shape=[jax.ShapeDtypeStruct((b_lat, t, nq), BF16)],
        compiler_params=_cparams("parallel", "parallel", "parallel"),
        name="gqa_latent",
    )(q, k, vt, k, vt)


def _gqa_ctx_call(q, k, vt, b_lat, tc):
    nb, t, nq = q.shape
    gw = nq // GQA_KV_HEADS
    ctx = pl.BlockSpec((1, tc, HEAD_DIM), lambda b, g: (b_lat, b, g))
    vt_ctx = pl.BlockSpec((1, HEAD_DIM, tc), lambda b, g: (b_lat, g, b))
    return pl.pallas_call(
        functools.partial(_gqa_kernel, has_lat=False),
        grid=(t // tc, GQA_KV_HEADS),
        in_specs=[pl.BlockSpec((1, tc, gw), lambda b, g: (b_lat, b, g)), ctx, vt_ctx],
        out_specs=pl.BlockSpec((1, tc, gw), lambda b, g: (0, b, g)),
        out_shape=jax.ShapeDtypeStruct((1, t, nq), BF16),
        compiler_params=_cparams("parallel", "parallel"),
        name="gqa_context",
    )(q, k, vt)


def _hyb_out_kernel(x_ref, mod_ref, gb_ref, z_ref, zp_ref, zn_ref, att_ref, attc_ref, cw_ref, w_ref,
                    o_ref, *, seq_lat, seq_ctx, b_lat):
    b, i = pl.program_id(0), pl.program_id(1)
    tm, cc = z_ref.shape[1], z_ref.shape[2]
    acc = _dot(jnp.where(b == b_lat, attc_ref[0], att_ref[0]), w_ref[cc:, :])
    z = z_ref[0].astype(F32)
    rows = lax.broadcasted_iota(jnp.int32, (tm, cc), 0)
    seq = jnp.where(b == b_lat, seq_ctx, seq_lat)
    pos = (i * tm + rows) & (seq - 1)
    z_prev = jnp.where(rows == 0, zp_ref[0, 7:8, :].astype(F32), pltpu.roll(z, 1, axis=0))
    z_next = jnp.where(rows == tm - 1, zn_ref[0, 0:1, :].astype(F32),
                       pltpu.roll(z, tm - 1, axis=0))
    z_prev = jnp.where(pos == 0, 0.0, z_prev)
    z_next = jnp.where(pos == seq - 1, 0.0, z_next)
    conv = z_prev * cw_ref[0:1, :] + z * cw_ref[1:2, :] + z_next * cw_ref[2:3, :]
    acc = acc + _dot((gb_ref[0].astype(F32) * conv).astype(BF16), w_ref[:cc, :])
    o_ref[0] = x_ref[0] + mod_ref[0, 5:6, :] * acc


def _hyb_out_call(x, mod, gb, z, att, att_c, conv_w, w_out, b_lat, tc, riders=()):
    nb, t, d = x.shape
    cc = gb.shape[2]
    tm = min(512, t)
    nt = t // tm
    hb = tm // 8
    row = lambda n: pl.BlockSpec((1, tm, n), lambda b, i: (b, i, 0))
    nq = att.shape[2]
    att_spec = pl.BlockSpec((1, tm, nq), lambda b, i: (jnp.minimum(b, b_lat - 1),
                                                       jnp.where(b < b_lat, i, nt - 1), 0))
    attc_spec = pl.BlockSpec((1, tm, nq), lambda b, i: (0, jnp.where(b < b_lat, 0, i), 0))
    return _call_with_casts(
        functools.partial(_hyb_out_kernel, seq_lat=t, seq_ctx=tc, b_lat=b_lat), riders,
        grid=(nb, t // tm),
        in_specs=[
            row(d),
            pl.BlockSpec((1, N_MOD, d), lambda b, i: (b, 0, 0)),
            row(cc), row(cc),
            pl.BlockSpec((1, 8, cc), lambda b, i: (b, jnp.maximum(i * hb - 1, 0), 0)),
            pl.BlockSpec((1, 8, cc), lambda b, i: (b, jnp.minimum((i + 1) * hb, t // 8 - 1), 0)),
            att_spec, attc_spec,
            _resident(conv_w.shape),
            _resident(w_out.shape),
        ],
        out_specs=[row(d)],
        out_shape=[jax.ShapeDtypeStruct((nb, t, d), F32)],
        compiler_params=_cparams("parallel", "parallel"),
        name="hyb_out_proj",
    )(x, mod, gb, z, z, z, att, att_c, conv_w, w_out)


def _mla_in_kernel(x_ref, mod_ref, g_ref, wd_ref, qg_ref, kvg_ref, wuq_ref, wuk_ref, wuvt_ref,
                   cos_ref, sin_ref, qn_ref, qr_ref, kn_ref, vt_ref, kr_ref, *, q_rank, kv_rank):
    n_nope = qn_ref.shape[2]
    q_scale = (MLA_NOPE + MLA_ROPE) ** -0.5 * LOG2E
    down = lambda rows: _dot(_modnorm(x_ref[0, rows, :], g_ref[...], mod_ref[0, 3:4, :],
                                      mod_ref[0, 4:5, :]).astype(BF16), wd_ref[...])
    chunks = _row_chunks(x_ref.shape[1], PROJ_ROW_CHUNK)
    d_next = down(chunks[0])
    for n, rows in enumerate(chunks):
        d = d_next
        if n + 1 < len(chunks):
            d_next = down(chunks[n + 1])
        cos = cos_ref[0, rows, :]
        sin = sin_ref[0, rows, :]
        q = _dot(_rms(d[:, :q_rank], qg_ref[...]).astype(BF16), wuq_ref[...])
        qn_ref[0, rows, :] = (q[:, :n_nope] * q_scale).astype(BF16)
        for j in range(qr_ref.shape[2] // LANES):
            qr_ref[0, rows, j * LANES:(j + 1) * LANES] = (
                _rope(q[:, n_nope + j * LANES:n_nope + (j + 1) * LANES], cos, sin) * q_scale
            ).astype(BF16)
        ckv = _rms(d[:, q_rank:q_rank + kv_rank], kvg_ref[...]).astype(BF16)
        kn_ref[0, rows, :] = _dot(ckv, wuk_ref[...]).astype(BF16)
        vt_ref[0, :, rows] = _dot_t(wuvt_ref[...], ckv).astype(BF16)
        kr = _rope(d[:, q_rank + kv_rank:], cos, sin)
        first = (lax.broadcasted_iota(jnp.int32, kr.shape, 1) & (MLA_ROPE // 2)) == 0
        kr_ref[0, rows, :LANES] = jnp.where(first, kr, 0.0).astype(BF16)
        kr_ref[0, rows, LANES:] = jnp.where(first, 0.0, kr).astype(BF16)


def _mla_in_call(x, mod, g, wd, qg, kvg, wuq, wuk, wuvt, cos2, sin2, riders=()):
    nb, t, d = x.shape
    tm = min(512, t)
    q_rank, kv_rank = qg.shape[0], kvg.shape[0]
    n_nope = wuk.shape[1]
    n_rope = wuq.shape[1] - n_nope
    n_v = wuvt.shape[0]
    row = lambda n: pl.BlockSpec((1, tm, n), lambda b, i: (b, i, 0))
    table = pl.BlockSpec((1, tm, LANES), lambda b, i: (jnp.where(b == nb - 1, 1, 0), i, 0))
    out_specs = [row(n_nope), row(n_rope), row(n_nope),
                 pl.BlockSpec((1, n_v, tm), lambda b, i: (b, 0, i)), row(2 * LANES)]
    out_shapes = [(nb, t, n_nope), (nb, t, n_rope), (nb, t, n_nope), (nb, n_v, t), (nb, t, 2 * LANES)]
    return _call_with_casts(
        functools.partial(_mla_in_kernel, q_rank=q_rank, kv_rank=kv_rank), riders,
        grid=(nb, t // tm),
        in_specs=[
            row(d),
            pl.BlockSpec((1, N_MOD, d), lambda b, i: (b, 0, 0)),
            _resident((1, d)),
            _resident(wd.shape), _resident((1, q_rank)), _resident((1, kv_rank)),
            _resident(wuq.shape), _resident(wuk.shape), _resident(wuvt.shape),
            table, table,
        ],
        out_specs=out_specs,
        out_shape=[jax.ShapeDtypeStruct(s, BF16) for s in out_shapes],
        compiler_params=_cparams("parallel", "parallel"),
        name="mla_in_proj",
    )(x, mod, g.reshape(1, d), wd, qg.reshape(1, q_rank), kvg.reshape(1, kv_rank), wuq, wuk, wuvt,
      cos2, sin2)


def _mla_attn_kernel(qn_ref, qr_ref, knc_ref, krc_ref, vtc_ref, knl_ref, krl_ref, vtl_ref,
                     o_ref, k_ref, vt_ref):
    tc = knc_ref.shape[1]

    @pl.when(pl.program_id(2) == 0)
    def _():
        k_ref[:tc, :LANES] = knc_ref[0]
        k_ref[:tc, LANES:] = krc_ref[0]
        k_ref[tc:, :LANES] = knl_ref[0]
        k_ref[tc:, LANES:] = krl_ref[0]
        vt_ref[:, :tc] = vtc_ref[0]
        vt_ref[:, tc:] = vtl_ref[0]

    def chain(rows):
        def store(o):
            o_ref[0, rows, :] = o
        return (lambda: jnp.concatenate([qn_ref[0, rows, :], qr_ref[0, rows, :]], axis=1)), store

    _attend_chains_t([chain(rows) for rows in _row_chunks(qn_ref.shape[1], MLA_ROW_CHUNK)],
                     [(k_ref[...], vt_ref[...])])


def _mla_attn_call(qn, qr, kn, vt, kr, b_lat, tc, riders=()):
    nb, t, n_nope = qn.shape
    heads = n_nope // MLA_NOPE
    tq = min(2048, t)
    qspec = pl.BlockSpec((1, tq, LANES), lambda b, h, i: (b, i, h))
    qrspec = pl.BlockSpec((1, tq, LANES), lambda b, h, i: (b, i, h // 2))
    ctx = lambda f: pl.BlockSpec((1, tc, LANES), lambda b, h, i: (b_lat, b, f(h)))
    lat = lambda f: pl.BlockSpec((1, t, LANES), lambda b, h, i: (b, 0, f(h)))
    vt_ctx = pl.BlockSpec((1, MLA_V, tc), lambda b, h, i: (b_lat, h, b))
    vt_lat = pl.BlockSpec((1, MLA_V, t), lambda b, h, i: (b, h, 0))
    same = lambda h: h
    parity = lambda h: h % 2
    return _call_with_casts(
        _mla_attn_kernel, riders,
        grid=(b_lat, heads, t // tq),
        in_specs=[qspec, qrspec, ctx(same), ctx(parity), vt_ctx, lat(same), lat(parity), vt_lat],
        out_specs=[qspec],
        out_shape=[jax.ShapeDtypeStruct((b_lat, t, n_nope), BF16)],
        scratch_shapes=[pltpu.VMEM((tc + t, 2 * LANES), BF16), pltpu.VMEM((MLA_V, tc + t), BF16)],
        compiler_params=_cparams("parallel", "parallel", "arbitrary"),
        name="mla_attention",
    )(qn, qr, kn, kr, vt, kn, kr, vt)


def _mla_out_kernel(x_ref, mod_ref, att_ref, w_ref, o_ref):
    o_ref[0] = x_ref[0] + mod_ref[0, 5:6, :] * _dot(att_ref[0], w_ref[...])


def _mla_out_call(x, mod, att, w_o, riders=()):
    nb, t, k = att.shape
    d = w_o.shape[1]
    tm = min(512, t)
    row = lambda n: pl.BlockSpec((1, tm, n), lambda b, i: (b, i, 0))
    return _call_with_casts(
        _mla_out_kernel, riders,
        grid=(nb, t // tm),
        in_specs=[row(d), pl.BlockSpec((1, N_MOD, d), lambda b, i: (b, 0, 0)), row(k),
                  _resident(w_o.shape)],
        out_specs=[row(d)],
        out_shape=[jax.ShapeDtypeStruct((nb, t, d), F32)],
        compiler_params=_cparams("parallel", "parallel"),
        name="mla_out_proj",
    )(x, mod, att, w_o)


def _rope_angles(n_tok, dim):
    n_rows = n_tok // GRID_W
    row = jnp.repeat(jnp.arange(n_rows), GRID_W).astype(F32)
    col = jnp.tile(jnp.arange(GRID_W), n_rows).astype(F32)
    half = dim // 2
    inv = 1.0 / (ROPE_THETA ** (jnp.arange(0, half, 2, dtype=F32) / half))
    return jnp.concatenate([row[:, None] * inv, col[:, None] * inv], axis=-1)


def _rope_tables(n_tok, dim):
    ang = _rope_angles(n_tok, dim)
    reps = LANES // dim
    cos = jnp.concatenate([jnp.cos(ang)] * (2 * reps), axis=-1)
    sin = jnp.concatenate([-jnp.sin(ang)] * reps + [jnp.sin(ang)] * reps, axis=-1)
    return (jnp.stack([cos, jnp.ones_like(cos)]), jnp.stack([sin, jnp.zeros_like(sin)]))


def _deinterleave(n):
    return np.concatenate([np.arange(0, n, 2), np.arange(1, n, 2)])


def kernel(x, c, ctx, c_ctx, mod_w, mod_b, norm_ffn1, norm_mix, norm_ffn2,
           ffn1_w_gate, ffn1_w_up, ffn1_w_down, ffn2_w_gate, ffn2_w_up, ffn2_w_down,
           hyb_w_in, hyb_conv_w, hyb_q_norm, hyb_k_norm, hyb_w_out,
           mla_w_down, mla_q_norm, mla_kv_norm, mla_w_uq, mla_w_ukv, mla_w_o, final_norm):
    b_lat, t, d = x.shape
    tc = ctx.shape[1]
    depth = mod_w.shape[0]
    assert depth == 2 and ctx.shape[0] * tc == t and b_lat + 1 <= MOD_ROWS
    assert t & (t - 1) == 0 and tc & (tc - 1) == 0 and t % GRID_W == 0
    nb = b_lat + 1

    cvec = jnp.concatenate([c, c_ctx[None], jnp.zeros((MOD_ROWS - nb, d), F32)], axis=0)
    mods = _mod_call(cvec, mod_w, mod_b).reshape(depth, MOD_ROWS, N_MOD, d)
    cast = lambda w: w.astype(BF16)
    tf = min(FFN_TILE, ffn1_w_gate.shape[2])
    f1 = _as_ffn_weights(_cast_call(ffn1_w_gate, 0, tf), _cast_call(ffn1_w_up, 0, tf),
                         _cast_call(ffn1_w_down, 0, d))

    mod = mods[0]
    xs = _ffn_call(x, nb, mod, norm_ffn1[0], *f1, 0, 0, ctx=ctx.reshape(1, t, d))
    cc = d // 2
    n_q = cc
    n_kv = GQA_KV_HEADS * HEAD_DIM
    w_in = hyb_w_in[0]
    perm = _deinterleave(HEAD_DIM)
    pairs_last = lambda w: w.reshape(d, -1, HEAD_DIM // 2, 2).swapaxes(2, 3).reshape(d, -1)
    wq = pairs_last(w_in[:, 3 * cc:3 * cc + n_q])
    wk = pairs_last(w_in[:, 3 * cc + n_q:3 * cc + n_q + n_kv])
    wvt = w_in[:, 3 * cc + n_q + n_kv:].T
    cos_a, sin_a = _rope_tables(t, HEAD_DIM)
    (gb, z, q, k, v), _ = _hyb_in_call(
        xs, mod, norm_mix[0], cast(w_in[:, :cc]), cast(w_in[:, cc:2 * cc]), cast(w_in[:, 2 * cc:3 * cc]),
        cast(wq), cast(wk), cast(wvt), hyb_q_norm[0][perm], hyb_k_norm[0][perm], cos_a, sin_a)
    (att,), casts = _gqa_latent_call(
        q, k, v, b_lat, tc,
        riders=[(ffn2_w_gate, 0, tf), (ffn1_w_gate, 1, tf), (ffn2_w_up, 0, tf), (ffn1_w_up, 1, tf),
                (ffn2_w_down, 0, d), (ffn1_w_down, 1, d), (hyb_w_out, 0, d)])
    gates, ups, downs = casts[0:2], casts[2:4], casts[4:6]
    att_c = _gqa_ctx_call(q, k, v, b_lat, tc)
    (xs,), _ = _hyb_out_call(xs, mod, gb, z, att, att_c, hyb_conv_w[0], casts[6][0, 0], b_lat, tc)
    xs = _ffn_call(xs, nb, mod, norm_ffn2[0], *_as_ffn_weights(gates[0], ups[0], downs[0]), 0, 6)

    mod = mods[1]
    xs = _ffn_call(xs, nb, mod, norm_ffn1[1], *_as_ffn_weights(gates[1], ups[1], downs[1]), 0, 0)
    heads = d // HEAD_DIM
    q_rank, kv_rank = mla_q_norm.shape[1], mla_kv_norm.shape[1]
    uq = mla_w_uq[0].reshape(q_rank, heads, MLA_NOPE + MLA_ROPE)
    uq_rope = uq[:, :, MLA_NOPE:].reshape(q_rank, heads // 2, 2, MLA_ROPE // 2, 2)
    wuq = jnp.concatenate([uq[:, :, :MLA_NOPE].reshape(q_rank, -1),
                           uq_rope.transpose(0, 1, 4, 2, 3).reshape(q_rank, -1)], axis=1)
    ukv = mla_w_ukv[0].reshape(kv_rank, heads, MLA_NOPE + MLA_V)
    wuk = ukv[:, :, :MLA_NOPE].reshape(kv_rank, -1)
    wuvt = ukv[:, :, MLA_NOPE:].reshape(kv_rank, -1).T
    kr0 = q_rank + kv_rank
    w_kr = mla_w_down[0][:, kr0:].reshape(d, MLA_ROPE // 2, 2)
    wdn = jnp.concatenate([mla_w_down[0][:, :kr0], w_kr[:, :, 0], w_kr[:, :, 0],
                           w_kr[:, :, 1], w_kr[:, :, 1]], axis=1)
    cos_m, sin_m = _rope_tables(t, MLA_ROPE)
    (qn, qr, kn, vt, kr), _ = _mla_in_call(
        xs, mod, norm_mix[1], cast(wdn), mla_q_norm[0], mla_kv_norm[0],
        cast(wuq), cast(wuk), cast(wuvt), cos_m, sin_m)
    (att,), casts = _mla_attn_call(
        qn, qr, kn, vt, kr, b_lat, tc,
        riders=[(ffn2_w_gate, 1, tf), (ffn2_w_up, 1, tf), (ffn2_w_down, 1, d), (mla_w_o, 0, d)])
    (xs,), _ = _mla_out_call(xs, mod, att, casts[3][0, 0])
    return _ffn_call(xs, b_lat, mod, norm_ffn2[1], *_as_ffn_weights(*casts[:3]), 0, 6,
                     final_g=final_norm)
```
